```python
import jax, jax.numpy as jnp
from jax import lax
import numpy as np

D_MODEL = 1024
BATCH = 8
SEQ = 4096
DEPTH = 2
DEC_BATCH = 4
DEC_SEQ = 8192
PAST_LEN = 128

HEAD_DIM = 64
N_ATTN_HEADS = 8
ATTN_WIDTH = N_ATTN_HEADS * HEAD_DIM
DILATED_PATTERNS = ((128, 1), (512, 4), (2048, 16))
ROT_DIM = HEAD_DIM // 4
ROPE_THETA = 500000.0
N_DN_HEADS = 4
DN_HEAD_DIM = 128
DN_WIDTH = N_DN_HEADS * DN_HEAD_DIM
CHUNK = 64
CONV_K = 5
D_FF = 2816
IN_WIDTH = 3 * ATTN_WIDTH + 4 * DN_WIDTH + 4 * N_DN_HEADS
MIX_WIDTH = ATTN_WIDTH + DN_WIDTH
N_MOD = 9
EPS = 1e-6

kernel_name = "hybrid_dilated_attn_gated_deltanet_encoder"


def _rmsnorm(x, w):
    xf = x.astype(jnp.float32)
    y = xf * lax.rsqrt(jnp.mean(xf * xf, axis=-1, keepdims=True) + EPS)
    return (y * w.astype(jnp.float32)).astype(x.dtype)


def _l2norm(x):
    return x * lax.rsqrt(jnp.sum(x * x, axis=-1, keepdims=True) + EPS)


def _partial_rope(x):
    S = x.shape[1]
    half = ROT_DIM // 2
    inv = ROPE_THETA ** (-jnp.arange(half, dtype=jnp.float32) / half)
    ang = jnp.arange(S, dtype=jnp.float32)[:, None] * inv[None, :]
    cos = jnp.cos(ang)[None, :, None, :]
    sin = jnp.sin(ang)[None, :, None, :]
    xr = x[..., :ROT_DIM].astype(jnp.float32)
    x1, x2 = xr[..., :half], xr[..., half:]
    rot = jnp.concatenate([x1 * cos - x2 * sin, x2 * cos + x1 * sin], axis=-1).astype(x.dtype)
    return jnp.concatenate([rot, x[..., ROT_DIM:]], axis=-1)


def _dilated_window_attn(q, k, v, dilation, radius):
    B, S, H, Dh = q.shape
    L = S // dilation
    G = B * dilation

    def to_sub(t):
        return t.reshape(B, L, dilation, H, Dh).transpose(0, 2, 1, 3, 4).reshape(G, L, H, Dh)

    qs, ks, vs = to_sub(q), to_sub(k), to_sub(v)
    nb = -(-L // radius)
    Lp = nb * radius
    qb = jnp.pad(qs, ((0, 0), (0, Lp - L), (0, 0), (0, 0))).reshape(G, nb, radius, H, Dh)

    def kblocks(t):
        tp = jnp.pad(t, ((0, 0), (radius, Lp - L + radius), (0, 0), (0, 0))).reshape(G, nb + 2, radius, H, Dh)
        return jnp.concatenate([tp[:, :-2], tp[:, 1:-1], tp[:, 2:]], axis=2)

    kb, vb = kblocks(ks), kblocks(vs)
    qpos = np.arange(nb)[:, None] * radius + np.arange(radius)[None, :]
    kpos = np.arange(nb)[:, None] * radius - radius + np.arange(3 * radius)[None, :]
    valid = ((kpos[:, None, :] >= 0) & (kpos[:, None, :] < L)
             & (np.abs(qpos[:, :, None] - kpos[:, None, :]) <= radius))

    scores = jnp.einsum('gnqhd,gnkhd->gnhqk', qb, kb).astype(jnp.float32) * (Dh ** -0.5)
    scores = jnp.where(valid[None, :, None], scores, -1e30)
    m = jnp.max(scores, axis=-1, keepdims=True)
    p = jnp.exp(scores - m)
    denom = jnp.sum(p, axis=-1, keepdims=True)
    o = jnp.einsum('gnhqk,gnkhd->gnhqd', p, vb.astype(jnp.float32)) / denom
    lse = (m + jnp.log(denom))[..., 0]

    o = o.transpose(0, 1, 3, 2, 4).reshape(G, Lp, H, Dh)[:, :L]
    o = o.reshape(B, dilation, L, H, Dh).transpose(0, 2, 1, 3, 4).reshape(B, S, H, Dh)
    lse = lse.transpose(0, 1, 3, 2).reshape(G, Lp, H)[:, :L]
    lse = lse.reshape(B, dilation, L, H).transpose(0, 2, 1, 3).reshape(B, S, H)
    return o, lse


def _short_conv(x, w):
    S = x.shape[1]
    pad = CONV_K // 2
    xp = jnp.pad(x, ((0, 0), (pad, pad), (0, 0)))
    y = xp[:, 0:S] * w[0]
    for j in range(1, CONV_K):
        y = y + xp[:, j:j + S] * w[j]
    return jax.nn.silu(y)


def _gated_delta_chunked(q, k, v, g, beta):
    B, S, H, Dk = q.shape
    Dv = v.shape[-1]
    N = S // CHUNK

    def chunks(t):
        return t.reshape(B, N, CHUNK, H, t.shape[-1]).transpose(1, 0, 3, 2, 4)

    qc, kc, vc = chunks(q), chunks(k), chunks(v)
    gc = jnp.cumsum(g.reshape(B, N, CHUNK, H).transpose(1, 0, 3, 2), axis=-1)
    bc = beta.reshape(B, N, CHUNK, H).transpose(1, 0, 3, 2)
    lower = np.tril(np.ones((CHUNK, CHUNK), dtype=bool))
    strict = np.tril(np.ones((CHUNK, CHUNK), dtype=bool), -1)
    decay = jnp.exp(jnp.where(lower, gc[..., :, None] - gc[..., None, :], -jnp.inf))
    kbeta = kc * bc[..., None]
    A = jnp.where(strict, jnp.einsum('nbhid,nbhjd->nbhij', kbeta, kc) * decay, 0.0)
    eye = jnp.eye(CHUNK, dtype=jnp.float32)
    T = lax.linalg.triangular_solve(eye + A, jnp.broadcast_to(eye, A.shape), left_side=True, lower=True)
    u = T @ (vc * bc[..., None])
    w = T @ (kbeta * jnp.exp(gc)[..., None])
    qk = jnp.einsum('nbhid,nbhjd->nbhij', qc, kc) * decay

    def step(state, xs):
        q_i, k_i, u_i, w_i, qk_i, g_i = xs
        v_new = u_i - w_i @ state
        o = (q_i * jnp.exp(g_i)[..., None]) @ state + qk_i @ v_new
        g_last = g_i[..., -1]
        k_dec = k_i * jnp.exp(g_last[..., None] - g_i)[..., None]
        state = state * jnp.exp(g_last)[..., None, None] + jnp.einsum('bhcd,bhce->bhde', k_dec, v_new)
        return state, o

    state0 = jnp.zeros((B, H, Dk, Dv), jnp.float32)
    _, o = lax.scan(step, state0, (qc, kc, u, w, qk, gc))
    return o.transpose(1, 0, 3, 2, 4).reshape(B, S, H, Dv)


def _mixer(h, w_in, conv_w, a_log, dt_bias, dn_norm, w_out):
    B, S, _ = h.shape
    proj = h @ w_in
    aq, ak, av, dn_qkv, z, a, b = jnp.split(
        proj, [ATTN_WIDTH, 2 * ATTN_WIDTH, 3 * ATTN_WIDTH, 3 * ATTN_WIDTH + 3 * DN_WIDTH,
               3 * ATTN_WIDTH + 4 * DN_WIDTH, 3 * ATTN_WIDTH + 4 * DN_WIDTH + 2 * N_DN_HEADS], axis=-1)

    aq = _partial_rope(aq.reshape(B, S, N_ATTN_HEADS, HEAD_DIM))
    ak = _partial_rope(ak.reshape(B, S, N_ATTN_HEADS, HEAD_DIM))
    av = av.reshape(B, S, N_ATTN_HEADS, HEAD_DIM)
    outs, lses = [], []
    for window, dil in DILATED_PATTERNS:
        o_g, lse_g = _dilated_window_attn(aq, ak, av, dil, window // (2 * dil))
        outs.append(o_g)
        lses.append(lse_g)
    wts = jax.nn.softmax(jnp.stack(lses, axis=0), axis=0)
    attn = jnp.sum(wts[..., None] * jnp.stack(outs, axis=0), axis=0)
    attn = attn.reshape(B, S, ATTN_WIDTH).astype(h.dtype)

    qkv = _short_conv(dn_qkv, conv_w).astype(jnp.float32)
    dq, dk, dv = jnp.split(qkv, 3, axis=-1)
    dq = _l2norm(dq.reshape(B, S, N_DN_HEADS, DN_HEAD_DIM)) * (DN_HEAD_DIM ** -0.5)
    dk = _l2norm(dk.reshape(B, S, N_DN_HEADS, DN_HEAD_DIM))
    dv = dv.reshape(B, S, N_DN_HEADS, DN_HEAD_DIM)
    a = a.astype(jnp.float32).reshape(B, S, 2, N_DN_HEADS)
    g = -jnp.exp(a_log.astype(jnp.float32)) * jax.nn.softplus(a + dt_bias.astype(jnp.float32))
    beta = jax.nn.sigmoid(b.astype(jnp.float32).reshape(B, S, 2, N_DN_HEADS))
    o_f = _gated_delta_chunked(dq, dk, dv, g[:, :, 0], beta[:, :, 0])
    flip = lambda t: jnp.flip(t, axis=1)
    o_b = flip(_gated_delta_chunked(flip(dq), flip(dk), flip(dv), flip(g[:, :, 1]), flip(beta[:, :, 1])))
    o = o_f + o_b
    zf = z.astype(jnp.float32).reshape(B, S, N_DN_HEADS, DN_HEAD_DIM)
    o = o * lax.rsqrt(jnp.mean(o * o, axis=-1, keepdims=True) + EPS) * dn_norm.astype(jnp.float32) * jax.nn.silu(zf)
    dn = o.reshape(B, S, DN_WIDTH).astype(h.dtype)

    return jnp.concatenate([attn, dn], axis=-1) @ w_out


def _swiglu(h, w_gate, w_up, w_down):
    return (jax.nn.silu(h @ w_gate) * (h @ w_up)) @ w_down


def _trunk(x, c, ada_w, ada_b, norm_ffn1, ffn1_w_gate, ffn1_w_up, ffn1_w_down, norm_mix, w_in, conv_w,
           a_log, dt_bias, dn_norm, w_out, norm_ffn2, ffn2_w_gate, ffn2_w_up, ffn2_w_down, norm_final):
    sc = jax.nn.silu(c)
    for l in range(DEPTH):
        mod = (sc @ ada_w[l] + ada_b[l])[:, None, :]
        sh1, sc1, gt1, sh2, sc2, gt2, sh3, sc3, gt3 = jnp.split(mod, N_MOD, axis=-1)
        h = _rmsnorm(x, norm_ffn1[l]) * (1 + sc1) + sh1
        x = x + 0.5 * gt1 * _swiglu(h, ffn1_w_gate[l], ffn1_w_up[l], ffn1_w_down[l])
        h = _rmsnorm(x, norm_mix[l]) * (1 + sc2) + sh2
        x = x + gt2 * _mixer(h, w_in[l], conv_w[l], a_log[l], dt_bias[l], dn_norm[l], w_out[l])
        h = _rmsnorm(x, norm_ffn2[l]) * (1 + sc3) + sh3
        x = x + 0.5 * gt3 * _swiglu(h, ffn2_w_gate[l], ffn2_w_up[l], ffn2_w_down[l])
    return _rmsnorm(x, norm_final)


def setup_inputs(seed: int = 0) -> dict:
    key = jax.random.key(seed)
    ks = jax.random.split(key, 24)
    f32 = jnp.float32
    D, F = D_MODEL, D_FF
    nrm = lambda k, shape, scale: jax.random.normal(k, shape, f32) * scale
    gain = lambda k, shape: 1.0 + 0.02 * jax.random.normal(k, shape, f32)
    dt = jnp.exp(jax.random.uniform(ks[13], (DEPTH, 2, N_DN_HEADS), f32, np.log(1e-3), np.log(1e-1)))
    return {
        "x_prompt": nrm(ks[0], (BATCH, SEQ, D), 1.0),
        "x_sample": nrm(ks[1], (DEC_BATCH, DEC_SEQ, D), 1.0),
        "c_prompt": nrm(ks[2], (BATCH, D), 1.0),
        "c_sample": nrm(ks[3], (DEC_BATCH, D), 1.0),
        "ada_w": nrm(ks[4], (DEPTH, D, N_MOD * D), D ** -0.5),
        "ada_b": nrm(ks[5], (DEPTH, N_MOD * D), 0.02),
        "norm_ffn1": gain(ks[6], (DEPTH, D)),
        "ffn1_w_gate": nrm(ks[7], (DEPTH, D, F), D ** -0.5),
        "ffn1_w_up": nrm(ks[8], (DEPTH, D, F), D ** -0.5),
        "ffn1_w_down": nrm(ks[9], (DEPTH, F, D), F ** -0.5),
        "norm_mix": gain(ks[10], (DEPTH, D)),
        "w_in": nrm(ks[11], (DEPTH, D, IN_WIDTH), D ** -0.5),
        "conv_w": nrm(ks[12], (DEPTH, CONV_K, 3 * DN_WIDTH), CONV_K ** -0.5),
        "a_log": jnp.log(jax.random.uniform(ks[14], (DEPTH, 2, N_DN_HEADS), f32, 1.0, 16.0)),
        "dt_bias": dt + jnp.log(-jnp.expm1(-dt)),
        "dn_norm": gain(ks[15], (DEPTH, DN_HEAD_DIM)),
        "w_out": nrm(ks[16], (DEPTH, MIX_WIDTH, D), MIX_WIDTH ** -0.5),
        "norm_ffn2": gain(ks[17], (DEPTH, D)),
        "ffn2_w_gate": nrm(ks[18], (DEPTH, D, F), D ** -0.5),
        "ffn2_w_up": nrm(ks[19], (DEPTH, D, F), D ** -0.5),
        "ffn2_w_down": nrm(ks[20], (DEPTH, F, D), F ** -0.5),
        "norm_final": gain(ks[21], (D,)),
    }


def reference(x_prompt, x_sample, c_prompt, c_sample, ada_w, ada_b, norm_ffn1, ffn1_w_gate, ffn1_w_up,
              ffn1_w_down, norm_mix, w_in, conv_w, a_log, dt_bias, dn_norm, w_out, norm_ffn2, ffn2_w_gate,
              ffn2_w_up, ffn2_w_down, norm_final):
    y_prompt = _trunk(x_prompt, c_prompt, ada_w, ada_b, norm_ffn1, ffn1_w_gate, ffn1_w_up, ffn1_w_down,
                      norm_mix, w_in, conv_w, a_log, dt_bias, dn_norm, w_out, norm_ffn2, ffn2_w_gate,
                      ffn2_w_up, ffn2_w_down, norm_final)
    y_sample = _trunk(x_sample, c_sample, ada_w, ada_b, norm_ffn1, ffn1_w_gate, ffn1_w_up, ffn1_w_down,
                      norm_mix, w_in, conv_w, a_log, dt_bias, dn_norm, w_out, norm_ffn2, ffn2_w_gate,
                      ffn2_w_up, ffn2_w_down, norm_final)
    return (y_prompt, y_sample)
```

```python
import functools

import jax
import jax.numpy as jnp
from jax import lax
from jax.experimental import pallas as pl
from jax.experimental.pallas import tpu as pltpu

F32 = jnp.float32
BF16 = jnp.bfloat16

D_MODEL = 1024
D_FF = 2816
N_MOD = 9
EPS = 1e-6
N_ATTN_HEADS = 8
HEAD_DIM = 64
ATTN_WIDTH = N_ATTN_HEADS * HEAD_DIM
ROT_HALF = HEAD_DIM // 8
ROPE_THETA = 500000.0
DILATED_PATTERNS = ((128, 1), (512, 4), (2048, 16))
N_DN_HEADS = 4
DN_HEAD_DIM = 128
DN_WIDTH = N_DN_HEADS * DN_HEAD_DIM
CONV_K = 5
N_GATES = 4 * N_DN_HEADS

LANES = 128
SUBLANES = 8
VMEM_LIMIT_BYTES = 56 * 1024 * 1024

FFN_ROWS = 256
PROJ_ROWS = 256
ATTN_Q_ROWS = 128
DN_CHUNK = 128
INV_BASE = 4
PREP_ROWS = 512
MOD_COLS = 1152
HALO_ROWS = 8
NEG_BIG = -1e30


def _cparams(sem):
    return pltpu.CompilerParams(dimension_semantics=sem, vmem_limit_bytes=VMEM_LIMIT_BYTES)


def _silu(x):
    return x * jax.nn.sigmoid(x)


def _modulated_norm(x, norm_w, shift, scale):
    y = x * lax.rsqrt(jnp.mean(x * x, axis=-1, keepdims=True) + EPS) * norm_w
    return y * (1.0 + scale) + shift


def _mod_kernel(c_ref, w_ref, b_ref, o_ref):
    sc = _silu(c_ref[...]).astype(BF16)
    o_ref[0] = jnp.dot(sc, w_ref[0].astype(BF16), preferred_element_type=F32) + b_ref[0]


def _modulation(c_all, ada_w, ada_b):
    depth, d, n = ada_w.shape
    rows = c_all.shape[0]
    return pl.pallas_call(
        _mod_kernel,
        grid=(depth, n // MOD_COLS),
        in_specs=[
            pl.BlockSpec((rows, d), lambda l, j: (0, 0)),
            pl.BlockSpec((1, d, MOD_COLS), lambda l, j: (l, 0, j)),
            pl.BlockSpec((1, 1, MOD_COLS), lambda l, j: (l, 0, j)),
        ],
        out_specs=pl.BlockSpec((1, rows, MOD_COLS), lambda l, j: (l, 0, j)),
        out_shape=jax.ShapeDtypeStruct((depth, rows, n), F32),
        compiler_params=_cparams(("arbitrary", "arbitrary")),
        name="modulation",
    )(c_all, ada_w, ada_b.reshape(depth, 1, n))


def _ffn_kernel(x_ref, mod_ref, nw_ref, wg_ref, wu_ref, wd_ref, *rest, final_norm):
    o_ref = rest[-1]
    x = x_ref[...]
    h = _modulated_norm(x, nw_ref[...], mod_ref[0, 0:1, :], mod_ref[0, 1:2, :]).astype(BF16)
    g = jnp.dot(h, wg_ref[...], preferred_element_type=F32)
    u = jnp.dot(h, wu_ref[...], preferred_element_type=F32)
    a = (_silu(g) * u).astype(BF16)
    y = jnp.dot(a, wd_ref[...], preferred_element_type=F32)
    out = x + (0.5 * mod_ref[0, 2:3, :]) * y
    if final_norm:
        fw_ref = rest[0]
        out = out * lax.rsqrt(jnp.mean(out * out, axis=-1, keepdims=True) + EPS) * fw_ref[...]
    o_ref[...] = out


def _ffn(x, mod, norm_w, wg, wu, wd, blk_rows, final_w=None):
    t, d = x.shape
    f = wg.shape[1]
    tm = FFN_ROWS
    per_blk = blk_rows // tm
    const = lambda i: (0, 0)
    in_specs = [
        pl.BlockSpec((tm, d), lambda i: (i, 0)),
        pl.BlockSpec((1, 3, d), lambda i: (i // per_blk, 0, 0)),
        pl.BlockSpec((1, d), const),
        pl.BlockSpec((d, f), const),
        pl.BlockSpec((d, f), const),
        pl.BlockSpec((f, d), const),
    ]
    args = [x, mod, norm_w.reshape(1, d), wg, wu, wd]
    if final_w is not None:
        in_specs.append(pl.BlockSpec((1, d), const))
        args.append(final_w.reshape(1, d))
    return pl.pallas_call(
        functools.partial(_ffn_kernel, final_norm=final_w is not None),
        grid=(t // tm,),
        in_specs=in_specs,
        out_specs=pl.BlockSpec((tm, d), lambda i: (i, 0)),
        out_shape=jax.ShapeDtypeStruct((t, d), F32),
        compiler_params=_cparams(("arbitrary",)),
        name="ffn",
    )(*args)


def _inproj_kernel(x_ref, mod_ref, nw_ref, w_ref, wg_ref, wgt_ref, gp_ref, gpt_ref,
                   cos_ref, sfw_ref, sbk_ref,
                   q_ref, k_ref, v_ref, dn_ref, z_ref, gb_ref, gbt_ref):
    h = _modulated_norm(x_ref[...], nw_ref[...], mod_ref[0, 0:1, :], mod_ref[0, 1:2, :]).astype(BF16)
    p = jnp.dot(h, w_ref[...], preferred_element_type=F32)
    cos, sfw, sbk = cos_ref[...], sfw_ref[...], sbk_ref[...]

    def rope(xp):
        return xp * cos + pltpu.roll(xp, LANES - ROT_HALF, 1) * sfw + pltpu.roll(xp, ROT_HALF, 1) * sbk

    for j in range(ATTN_WIDTH // LANES):
        sl = slice(j * LANES, (j + 1) * LANES)
        q_ref[:, sl] = (rope(p[:, sl]) * (HEAD_DIM ** -0.5)).astype(BF16)
        k_ref[:, sl] = rope(p[:, ATTN_WIDTH + j * LANES:ATTN_WIDTH + (j + 1) * LANES]).astype(BF16)
    v_ref[...] = p[:, 2 * ATTN_WIDTH:3 * ATTN_WIDTH].astype(BF16)
    o0 = 3 * ATTN_WIDTH
    dn_ref[...] = p[:, o0:o0 + 3 * DN_WIDTH]
    z_ref[...] = p[:, o0 + 3 * DN_WIDTH:o0 + 4 * DN_WIDTH].astype(BF16)

    n_g = 2 * N_DN_HEADS
    ab = jnp.dot(h, wg_ref[...], preferred_element_type=F32)
    is_g = lax.broadcasted_iota(jnp.int32, ab.shape, 1) < n_g
    gp = gp_ref[...]
    gb_ref[...] = jnp.where(is_g, -jnp.exp(gp[0:1, :]) * jax.nn.softplus(ab + gp[1:2, :]), jax.nn.sigmoid(ab))
    abt = lax.dot_general(wgt_ref[...], h, (((1,), (1,)), ((), ())), preferred_element_type=F32)
    is_gt = lax.broadcasted_iota(jnp.int32, abt.shape, 0) < n_g
    gpt = gpt_ref[...]
    gbt_ref[...] = jnp.where(is_gt, -jnp.exp(gpt[:, 0:1]) * jax.nn.softplus(abt + gpt[:, 1:2]), jax.nn.sigmoid(abt))


def _in_proj(x, mod, norm_w, w_main, w_gate, w_gate_t, gate_par, gate_par_t, rope_tabs, blk_rows, seq_lens, t1):
    t, d = x.shape
    tm = PROJ_ROWS
    per_blk = blk_rows // tm
    n_main = w_main.shape[1]
    nb1 = t1 // tm
    s1b, s2b = seq_lens[0] // tm, seq_lens[1] // tm
    const = lambda i: (0, 0)
    row = lambda i: (i, 0)
    pos = lambda i: (jnp.where(i < nb1, i % s1b, (i - nb1) % s2b), 0)
    cos, sfw, sbk = rope_tabs
    outs = pl.pallas_call(
        _inproj_kernel,
        grid=(t // tm,),
        in_specs=[
            pl.BlockSpec((tm, d), row),
            pl.BlockSpec((1, 3, d), lambda i: (i // per_blk, 0, 0)),
            pl.BlockSpec((1, d), const),
            pl.BlockSpec((d, n_main), const),
            pl.BlockSpec((d, LANES), const),
            pl.BlockSpec((N_GATES, d), const),
            pl.BlockSpec((2, LANES), const),
            pl.BlockSpec((N_GATES, 2), const),
            pl.BlockSpec((tm, LANES), pos),
            pl.BlockSpec((tm, LANES), pos),
            pl.BlockSpec((tm, LANES), pos),
        ],
        out_specs=[
            pl.BlockSpec((tm, ATTN_WIDTH), row),
            pl.BlockSpec((tm, ATTN_WIDTH), row),
            pl.BlockSpec((tm, ATTN_WIDTH), row),
            pl.BlockSpec((tm, 3 * DN_WIDTH), row),
            pl.BlockSpec((tm, DN_WIDTH), row),
            pl.BlockSpec((tm, LANES), row),
            pl.BlockSpec((N_GATES, tm), lambda i: (0, i)),
        ],
        out_shape=[
            jax.ShapeDtypeStruct((t, ATTN_WIDTH), BF16),
            jax.ShapeDtypeStruct((t, ATTN_WIDTH), BF16),
            jax.ShapeDtypeStruct((t, ATTN_WIDTH), BF16),
            jax.ShapeDtypeStruct((t, 3 * DN_WIDTH), F32),
            jax.ShapeDtypeStruct((t, DN_WIDTH), BF16),
            jax.ShapeDtypeStruct((t, LANES), F32),
            jax.ShapeDtypeStruct((N_GATES, t), F32),
        ],
        compiler_params=_cparams(("arbitrary",)),
        name="in_proj",
    )(x, mod, norm_w.reshape(1, d), w_main, w_gate, w_gate_t, gate_par, gate_par_t, cos, sfw, sbk)
    return outs


def _rope_tables(max_len):
    half = ROT_HALF
    inv = ROPE_THETA ** (-jnp.arange(half, dtype=F32) / half)
    ang = jnp.arange(max_len, dtype=F32)[:, None] * inv[None, :]
    cos, sin = jnp.cos(ang), jnp.sin(ang)
    ones = jnp.ones((max_len, HEAD_DIM - 2 * half), F32)
    zeros_h = jnp.zeros((max_len, half), F32)
    zeros_r = jnp.zeros((max_len, HEAD_DIM - 2 * half), F32)
    cos_h = jnp.concatenate([cos, cos, ones], axis=1)
    sfw_h = jnp.concatenate([-sin, zeros_h, zeros_r], axis=1)
    sbk_h = jnp.concatenate([zeros_h, sin, zeros_r], axis=1)
    rep = LANES // HEAD_DIM
    return tuple(jnp.tile(a, (1, rep)) for a in (cos_h, sfw_h, sbk_h))


def _attn_kernel(q_ref, kp_ref, km_ref, kn_ref, vp_ref, vm_ref, vn_ref, *rest,
                 first, last, radius, rows1, len1, len2):
    if first:
        o_ref, l_ref = rest
    elif last:
        op_ref, lp_ref, o_ref = rest
    else:
        op_ref, lp_ref, o_ref, l_ref = rest
    tq = q_ref.shape[0]
    tk = tq + 2 * radius
    r0 = pl.program_id(1) * tq
    in_first = r0 < rows1
    seq_len = jnp.where(in_first, len1, len2)
    off = jnp.where(in_first, r0, r0 - rows1)
    first_key = jnp.where((off % seq_len) == 0, radius, 0)
    end_key = jnp.where(((off + tq) % seq_len) == 0, tq + radius, tk)

    qi = lax.broadcasted_iota(jnp.int32, (tq, tk), 0)
    kj = lax.broadcasted_iota(jnp.int32, (tq, tk), 1)
    valid = (jnp.abs(kj - radius - qi) <= radius) & (kj >= first_key) & (kj < end_key)
    bias = jnp.where(valid, 0.0, NEG_BIG).astype(F32)

    lane = lax.broadcasted_iota(jnp.int32, (1, LANES), 1)
    low_half = lane < HEAD_DIM
    nt = (((1,), (1,)), ((), ()))
    for j in range(ATTN_WIDTH // LANES):
        sl = slice(j * LANES, (j + 1) * LANES)
        q2 = q_ref[:, sl]
        kw = jnp.concatenate([kp_ref[:, sl], km_ref[:, sl], kn_ref[:, sl]], axis=0)
        vw = jnp.concatenate([vp_ref[:, sl], vm_ref[:, sl], vn_ref[:, sl]], axis=0)
        halves = []
        for head_lanes in (low_half, jnp.logical_not(low_half)):
            qh = jnp.where(head_lanes, q2, jnp.zeros_like(q2))
            s = lax.dot_general(qh, kw, nt, preferred_element_type=F32) + bias
            m = jnp.max(s, axis=-1, keepdims=True)
            p = jnp.exp(s - m)
            den = jnp.sum(p, axis=-1, keepdims=True)
            o = jnp.dot(p.astype(BF16), vw, preferred_element_type=F32) / den
            halves.append((o, m + jnp.log(den)))
        o_new = jnp.where(low_half, halves[0][0], halves[1][0])
        l_new = jnp.where(low_half, halves[0][1], halves[1][1])
        if not first:
            o_old, l_old = op_ref[:, sl], lp_ref[:, sl]
            mx = jnp.maximum(l_old, l_new)
            w_old, w_new = jnp.exp(l_old - mx), jnp.exp(l_new - mx)
            tot = w_old + w_new
            o_new = (w_old * o_old + w_new * o_new) / tot
            l_new = mx + jnp.log(tot)
        o_ref[:, sl] = o_new.astype(o_ref.dtype)
        if not last:
            l_ref[:, sl] = l_new


def _attn_pattern(q, k, v, prev, dil, radius, first, last, t1, seq_lens):
    t, w = q.shape
    rows = t // dil
    tq = ATTN_Q_ROWS
    nq = rows // tq
    per_q = tq // radius
    n_halo = rows // radius
    view = lambda a: a.reshape(rows, dil * w)
    main = lambda c, i: (i, c)
    before = lambda c, i: (jnp.maximum(i * per_q - 1, 0), c)
    after = lambda c, i: (jnp.minimum((i + 1) * per_q, n_halo - 1), c)
    kv_specs = [pl.BlockSpec((radius, w), before), pl.BlockSpec((tq, w), main), pl.BlockSpec((radius, w), after)]
    in_specs = [pl.BlockSpec((tq, w), main)] + kv_specs + kv_specs
    args = [view(q)] + [view(k)] * 3 + [view(v)] * 3
    if not first:
        in_specs += [pl.BlockSpec((tq, w), main)] * 2
        args += [view(prev[0]), view(prev[1])]
    out_specs = [pl.BlockSpec((tq, w), main)]
    out_shape = [jax.ShapeDtypeStruct((rows, dil * w), BF16 if last else F32)]
    if not last:
        out_specs.append(pl.BlockSpec((tq, w), main))
        out_shape.append(jax.ShapeDtypeStruct((rows, dil * w), F32))
    outs = pl.pallas_call(
        functools.partial(_attn_kernel, first=first, last=last, radius=radius,
                          rows1=t1 // dil, len1=seq_lens[0] // dil, len2=seq_lens[1] // dil),
        grid=(dil, nq),
        in_specs=in_specs,
        out_specs=out_specs,
        out_shape=out_shape,
        compiler_params=_cparams(("arbitrary", "arbitrary")),
        name=f"attn_d{dil}",
    )(*args)
    return [o.reshape(t, w) for o in outs]


def _attention(q, k, v, t1, seq_lens):
    state = None
    n = len(DILATED_PATTERNS)
    for idx, (window, dil) in enumerate(DILATED_PATTERNS):
        state = _attn_pattern(q, k, v, state, dil, window // (2 * dil), idx == 0, idx == n - 1, t1, seq_lens)
    return state[0]


def _dnprep_kernel(xp_ref, xm_ref, xn_ref, w_ref, q_ref, k_ref, v_ref, buf_ref, *, rows1, len1, len2):
    tm = xm_ref.shape[0]
    r0 = pl.program_id(0) * tm
    in_first = r0 < rows1
    seq_len = jnp.where(in_first, len1, len2)
    off = jnp.where(in_first, r0, r0 - rows1)
    keep_prev = jnp.where((off % seq_len) == 0, 0.0, 1.0).astype(F32)
    keep_next = jnp.where(((off + tm) % seq_len) == 0, 0.0, 1.0).astype(F32)
    buf_ref[0:HALO_ROWS, :] = xp_ref[...] * keep_prev
    buf_ref[HALO_ROWS:HALO_ROWS + tm, :] = xm_ref[...]
    buf_ref[HALO_ROWS + tm:, :] = xn_ref[...] * keep_next
    pad = CONV_K // 2
    y = buf_ref[HALO_ROWS - pad:HALO_ROWS - pad + tm, :] * w_ref[0:1, :]
    for j in range(1, CONV_K):
        y = y + buf_ref[HALO_ROWS - pad + j:HALO_ROWS - pad + j + tm, :] * w_ref[j:j + 1, :]
    y = _silu(y)
    for hd in range(N_DN_HEADS):
        sl = slice(hd * DN_HEAD_DIM, (hd + 1) * DN_HEAD_DIM)
        qh = y[:, sl]
        kh = y[:, DN_WIDTH + hd * DN_HEAD_DIM:DN_WIDTH + (hd + 1) * DN_HEAD_DIM]
        qn = qh * lax.rsqrt(jnp.sum(qh * qh, axis=-1, keepdims=True) + EPS) * (DN_HEAD_DIM ** -0.5)
        kn = kh * lax.rsqrt(jnp.sum(kh * kh, axis=-1, keepdims=True) + EPS)
        q_ref[:, sl] = qn.astype(BF16)
        k_ref[:, sl] = kn.astype(BF16)
    v_ref[...] = y[:, 2 * DN_WIDTH:].astype(BF16)


def _dn_prep(pre, conv_w, t1, seq_lens):
    t, w3 = pre.shape
    tm = PREP_ROWS
    per = tm // HALO_ROWS
    n_halo = t // HALO_ROWS
    row = lambda i: (i, 0)
    out_spec = pl.BlockSpec((tm, DN_WIDTH), row)
    out_sds = jax.ShapeDtypeStruct((t, DN_WIDTH), BF16)
    return pl.pallas_call(
        functools.partial(_dnprep_kernel, rows1=t1, len1=seq_lens[0], len2=seq_lens[1]),
        grid=(t // tm,),
        in_specs=[
            pl.BlockSpec((HALO_ROWS, w3), lambda i: (jnp.maximum(i * per - 1, 0), 0)),
            pl.BlockSpec((tm, w3), row),
            pl.BlockSpec((HALO_ROWS, w3), lambda i: (jnp.minimum((i + 1) * per, n_halo - 1), 0)),
            pl.BlockSpec((CONV_K, w3), lambda i: (0, 0)),
        ],
        out_specs=[out_spec, out_spec, out_spec],
        out_shape=[out_sds, out_sds, out_sds],
        scratch_shapes=[pltpu.VMEM((tm + 2 * HALO_ROWS, w3), F32)],
        compiler_params=_cparams(("arbitrary",)),
        name="dn_prep",
    )(pre, pre, pre, conv_w)


def _dn_direction(q, k, v, gram, qk, gc_col, gc_row, beta_col, tot_col, tot_row, s_ref, hd, keep, strict, eye,
                  level_masks):
    c = q.shape[0]
    decay = jnp.where(keep, jnp.exp(jnp.minimum(gc_col - gc_row, 0.0)), 0.0)
    a = jnp.where(strict, beta_col * gram * decay, 0.0)
    dg = (a * level_masks[0]).astype(BF16)
    x = eye - a * level_masks[0]
    x = x + jnp.dot(x.astype(BF16), jnp.dot(dg, dg, preferred_element_type=F32).astype(BF16),
                    preferred_element_type=F32)
    for off_mask in level_masks[1:]:
        xb = x.astype(BF16)
        y = jnp.dot((a * off_mask).astype(BF16), xb, preferred_element_type=F32)
        x = x - jnp.dot(xb, y.astype(BF16), preferred_element_type=F32)
    egc = jnp.exp(gc_col)
    k_beta = k * beta_col
    rhs = jnp.concatenate([v * beta_col, k_beta * egc], axis=1).astype(BF16)
    uw = jnp.dot(x.astype(BF16), rhs, preferred_element_type=F32)
    u, w = uw[:, :DN_HEAD_DIM], uw[:, DN_HEAD_DIM:]
    state = s_ref[hd]
    lhs = jnp.concatenate([w, q * egc], axis=0).astype(BF16)
    ws = jnp.dot(lhs, state.astype(BF16), preferred_element_type=F32)
    v_new = (u - ws[:c]).astype(BF16)
    o = ws[c:] + jnp.dot((qk * decay).astype(BF16), v_new, preferred_element_type=F32)
    k_dec_t = jnp.transpose(k * jnp.exp(tot_col - gc_col)).astype(BF16)
    s_ref[hd] = state * jnp.exp(tot_row) + jnp.dot(k_dec_t, v_new, preferred_element_type=F32)
    return o


def _dn_kernel(qf_ref, kf_ref, vf_ref, gf_ref, gtf_ref, qb_ref, kb_ref, vb_ref, gb_ref, gtb_ref,
               of_ref, ob_ref, sf_ref, sb_ref, *, n_chunks, chunks1, len1, len2):
    c = qf_ref.shape[0]
    i = pl.program_id(0)

    def seq_pos(ci):
        in_first = ci < chunks1
        seq_len = jnp.where(in_first, len1, len2)
        off = jnp.where(in_first, ci, ci - chunks1)
        return off % seq_len, seq_len

    pos_f, _ = seq_pos(i)
    pos_b, len_b = seq_pos(n_chunks - 1 - i)

    @pl.when(pos_f == 0)
    def _():
        sf_ref[...] = jnp.zeros_like(sf_ref)

    @pl.when(pos_b == len_b - 1)
    def _():
        sb_ref[...] = jnp.zeros_like(sb_ref)

    ii = lax.broadcasted_iota(jnp.int32, (c, c), 0)
    jj = lax.broadcasted_iota(jnp.int32, (c, c), 1)
    lower, lower_strict = ii >= jj, ii > jj
    upper, upper_strict = ii <= jj, ii < jj
    eye = jnp.where(ii == jj, 1.0, 0.0).astype(F32)
    tri_l = jnp.where(lower, 1.0, 0.0).astype(F32)
    tri_u = jnp.where(upper, 1.0, 0.0).astype(F32)
    level_masks = [jnp.where(ii // INV_BASE == jj // INV_BASE, 1.0, 0.0).astype(F32)]
    b = INV_BASE
    while b < c:
        same_outer = ii // (2 * b) == jj // (2 * b)
        level_masks.append(jnp.where(same_outer & (ii // b != jj // b), 1.0, 0.0).astype(F32))
        b *= 2
    ones_col = jnp.ones((c, c), F32)
    ones_row = jnp.ones((c, DN_HEAD_DIM), F32)
    hi = lax.Precision.HIGHEST
    nt = (((1,), (1,)), ((), ()))

    def sweep(q_ref, k_ref, v_ref, g_ref, gt_ref, o_ref, s_ref, forward):
        g, gt = g_ref[...], gt_ref[...]
        tri_c, tri_r = (tri_l, tri_u) if forward else (tri_u, tri_l)
        cs_col = jnp.dot(tri_c, g, precision=hi, preferred_element_type=F32)
        cs_row = jnp.dot(gt, tri_r, precision=hi, preferred_element_type=F32)
        tot_col = jnp.dot(ones_col, g, precision=hi, preferred_element_type=F32)
        tot_row = jnp.dot(gt, ones_row, precision=hi, preferred_element_type=F32)
        keep, strict = (lower, lower_strict) if forward else (upper, upper_strict)
        d0 = 0 if forward else N_DN_HEADS
        for hd in range(N_DN_HEADS):
            sl = slice(hd * DN_HEAD_DIM, (hd + 1) * DN_HEAD_DIM)
            qh, kh, vh = q_ref[:, sl], k_ref[:, sl], v_ref[:, sl]
            both = lax.dot_general(jnp.concatenate([qh, kh], axis=0), kh, nt, preferred_element_type=F32)
            gi = d0 + hd
            bi = 2 * N_DN_HEADS + d0 + hd
            o_ref[:, sl] = _dn_direction(
                qh.astype(F32), kh.astype(F32), vh.astype(F32), both[c:], both[:c],
                cs_col[:, gi:gi + 1], cs_row[gi:gi + 1, :], g[:, bi:bi + 1],
                tot_col[:, gi:gi + 1], tot_row[gi:gi + 1, :], s_ref, hd, keep, strict, eye, level_masks)

    sweep(qf_ref, kf_ref, vf_ref, gf_ref, gtf_ref, of_ref, sf_ref, True)
    sweep(qb_ref, kb_ref, vb_ref, gb_ref, gtb_ref, ob_ref, sb_ref, False)


def _deltanet(q, k, v, gates, gates_t, t1, seq_lens):
    t, w = q.shape
    c = DN_CHUNK
    n = t // c
    fwd = lambda i: (i, 0)
    bwd = lambda i: (n - 1 - i, 0)
    fwd_t = lambda i: (0, i)
    bwd_t = lambda i: (0, n - 1 - i)

    def specs(row_map, col_map):
        return [pl.BlockSpec((c, w), row_map)] * 3 + [
            pl.BlockSpec((c, LANES), row_map), pl.BlockSpec((N_GATES, c), col_map)]

    state = pltpu.VMEM((N_DN_HEADS, DN_HEAD_DIM, DN_HEAD_DIM), F32)
    return pl.pallas_call(
        functools.partial(_dn_kernel, n_chunks=n, chunks1=t1 // c, len1=seq_lens[0] // c, len2=seq_lens[1] // c),
        grid=(n,),
        in_specs=specs(fwd, fwd_t) + specs(bwd, bwd_t),
        out_specs=[pl.BlockSpec((c, w), fwd), pl.BlockSpec((c, w), bwd)],
        out_shape=[jax.ShapeDtypeStruct((t, w), F32)] * 2,
        scratch_shapes=[state, state],
        compiler_params=_cparams(("arbitrary",)),
        name="deltanet",
    )(q, k, v, gates, gates_t, q, k, v, gates, gates_t)


def _outproj_kernel(x_ref, mod_ref, attn_ref, of_ref, ob_ref, z_ref, nw_ref, wa_ref, wd_ref, o_ref):
    o = of_ref[...] + ob_ref[...]
    z = z_ref[...].astype(F32)
    nw = nw_ref[...]
    parts = []
    for hd in range(N_DN_HEADS):
        sl = slice(hd * DN_HEAD_DIM, (hd + 1) * DN_HEAD_DIM)
        oh = o[:, sl]
        parts.append(oh * lax.rsqrt(jnp.mean(oh * oh, axis=-1, keepdims=True) + EPS) * nw * _silu(z[:, sl]))
    dn = jnp.concatenate(parts, axis=1).astype(BF16)
    y = jnp.dot(attn_ref[...], wa_ref[...], preferred_element_type=F32)
    y = y + jnp.dot(dn, wd_ref[...], preferred_element_type=F32)
    o_ref[...] = x_ref[...] + mod_ref[0, 2:3, :] * y


def _out_proj(x, mod, attn, o_f, o_b, z, dn_norm, w_attn, w_dn, blk_rows):
    t, d = x.shape
    tm = PROJ_ROWS
    per_blk = blk_rows // tm
    row = lambda i: (i, 0)
    const = lambda i: (0, 0)
    return pl.pallas_call(
        _outproj_kernel,
        grid=(t // tm,),
        in_specs=[
            pl.BlockSpec((tm, d), row),
            pl.BlockSpec((1, 3, d), lambda i: (i // per_blk, 0, 0)),
            pl.BlockSpec((tm, ATTN_WIDTH), row),
            pl.BlockSpec((tm, DN_WIDTH), row),
            pl.BlockSpec((tm, DN_WIDTH), row),
            pl.BlockSpec((tm, DN_WIDTH), row),
            pl.BlockSpec((1, DN_HEAD_DIM), const),
            pl.BlockSpec((ATTN_WIDTH, d), const),
            pl.BlockSpec((DN_WIDTH, d), const),
        ],
        out_specs=pl.BlockSpec((tm, d), row),
        out_shape=jax.ShapeDtypeStruct((t, d), F32),
        compiler_params=_cparams(("arbitrary",)),
        name="out_proj",
    )(x, mod, attn, o_f, o_b, z, dn_norm.reshape(1, DN_HEAD_DIM), w_attn, w_dn)


def kernel(x_prompt, x_sample, c_prompt, c_sample, ada_w, ada_b, norm_ffn1, ffn1_w_gate, ffn1_w_up, ffn1_w_down, norm_mix, w_in, conv_w, a_log, dt_bias, dn_norm, w_out, norm_ffn2, ffn2_w_gate, ffn2_w_up, ffn2_w_down, norm_final):
    b1, s1, d = x_prompt.shape
    b2, s2, _ = x_sample.shape
    depth = ada_w.shape[0]
    t1 = b1 * s1
    seq_lens = (s1, s2)
    assert s2 % s1 == 0 and t1 % s2 == 0, "flat layout needs nested sequence lengths"
    x = jnp.concatenate([x_prompt.reshape(t1, d), x_sample.reshape(b2 * s2, d)], axis=0)

    n_seq = b1 + b2
    c_all = jnp.concatenate([c_prompt, c_sample, jnp.zeros((-n_seq % SUBLANES, d), F32)], axis=0)
    mod = _modulation(c_all, ada_w, ada_b)
    blk_seq = jnp.concatenate([jnp.arange(b1), b1 + jnp.repeat(jnp.arange(b2), s2 // s1)])
    mod = mod[:, blk_seq].reshape(depth, blk_seq.shape[0], N_MOD, d)

    rope_tabs = _rope_tables(max(s1, s2))
    n_main = 3 * ATTN_WIDTH + 4 * DN_WIDTH
    gate_par_t = jnp.stack([a_log.astype(F32).reshape(depth, -1), dt_bias.astype(F32).reshape(depth, -1)], axis=2)
    gate_par_t = jnp.pad(gate_par_t, ((0, 0), (0, N_GATES - gate_par_t.shape[1]), (0, 0)))
    gate_par = jnp.pad(jnp.swapaxes(gate_par_t, 1, 2), ((0, 0), (0, 0), (0, LANES - N_GATES)))

    for l in range(depth):
        bf = lambda a: a.astype(BF16)
        x = _ffn(x, mod[l, :, 0:3], norm_ffn1[l], bf(ffn1_w_gate[l]), bf(ffn1_w_up[l]), bf(ffn1_w_down[l]), s1)
        w_gate_t = bf(w_in[l][:, n_main:]).T
        w_gate = jnp.pad(w_gate_t.T, ((0, 0), (0, LANES - N_GATES)))
        q, k, v, dn_pre, z, gates, gates_t = _in_proj(
            x, mod[l, :, 3:6], norm_mix[l], bf(w_in[l][:, :n_main]), w_gate, w_gate_t,
            gate_par[l], gate_par_t[l], rope_tabs, s1, seq_lens, t1)
        attn = _attention(q, k, v, t1, seq_lens)
        dq, dk, dv = _dn_prep(dn_pre, conv_w[l], t1, seq_lens)
        o_f, o_b = _deltanet(dq, dk, dv, gates, gates_t, t1, seq_lens)
        x = _out_proj(x, mod[l, :, 3:6], attn, o_f, o_b, z, dn_norm[l], bf(w_out[l][:ATTN_WIDTH]),
                      bf(w_out[l][ATTN_WIDTH:]), s1)
        x = _ffn(x, mod[l, :, 6:9], norm_ffn2[l], bf(ffn2_w_gate[l]), bf(ffn2_w_up[l]), bf(ffn2_w_down[l]), s1,
                 final_w=norm_final if l == depth - 1 else None)
    return x[:t1].reshape(b1, s1, d), x[t1:].reshape(b2, s2, d)
```

```python
import functools

import jax
import jax.numpy as jnp
from jax import lax
from jax.experimental import pallas as pl
from jax.experimental.pallas import tpu as pltpu

F32 = jnp.float32
BF16 = jnp.bfloat16

D_MODEL = 1024
D_FF = 2816
N_MOD = 9
EPS = 1e-6
N_ATTN_HEADS = 8
HEAD_DIM = 64
ATTN_WIDTH = N_ATTN_HEADS * HEAD_DIM
ROT_HALF = HEAD_DIM // 8
ROPE_THETA = 500000.0
DILATED_PATTERNS = ((128, 1), (512, 4), (2048, 16))
N_DN_HEADS = 4
DN_HEAD_DIM = 128
DN_WIDTH = N_DN_HEADS * DN_HEAD_DIM
CONV_K = 5
N_GATES = 4 * N_DN_HEADS

LANES = 128
SUBLANES = 8
VMEM_LIMIT_BYTES = 56 * 1024 * 1024

FFN_ROWS = 256
PROJ_ROWS = 256
ATTN_Q_ROWS = 128
DN_CHUNK = 128
INV_BASE = 4
PREP_ROWS = 512
MOD_COLS = 1152
HALO_ROWS = 8
NEG_BIG = -1e30


def _cparams(sem):
    return pltpu.CompilerParams(dimension_semantics=sem, vmem_limit_bytes=VMEM_LIMIT_BYTES)


def _silu(x):
    return x * jax.nn.sigmoid(x)


def _modulated_norm(x, norm_w, shift, scale):
    y = x * lax.rsqrt(jnp.mean(x * x, axis=-1, keepdims=True) + EPS) * norm_w
    return y * (1.0 + scale) + shift


def _mod_kernel(c_ref, w_ref, b_ref, o_ref):
    sc = _silu(c_ref[...]).astype(BF16)
    o_ref[0] = jnp.dot(sc, w_ref[0].astype(BF16), preferred_element_type=F32) + b_ref[0]


def _modulation(c_all, ada_w, ada_b):
    depth, d, n = ada_w.shape
    rows = c_all.shape[0]
    return pl.pallas_call(
        _mod_kernel,
        grid=(depth, n // MOD_COLS),
        in_specs=[
            pl.BlockSpec((rows, d), lambda l, j: (0, 0)),
            pl.BlockSpec((1, d, MOD_COLS), lambda l, j: (l, 0, j)),
            pl.BlockSpec((1, 1, MOD_COLS), lambda l, j: (l, 0, j)),
        ],
        out_specs=pl.BlockSpec((1, rows, MOD_COLS), lambda l, j: (l, 0, j)),
        out_shape=jax.ShapeDtypeStruct((depth, rows, n), F32),
        compiler_params=_cparams(("arbitrary", "arbitrary")),
        name="modulation",
    )(c_all, ada_w, ada_b.reshape(depth, 1, n))


def _ffn_kernel(x_ref, mod_ref, nw_ref, wg_ref, wu_ref, wd_ref, *rest, final_norm):
    o_ref = rest[-1]
    x = x_ref[...]
    h = _modulated_norm(x, nw_ref[...], mod_ref[0, 0:1, :], mod_ref[0, 1:2, :]).astype(BF16)
    g = jnp.dot(h, wg_ref[...], preferred_element_type=F32)
    u = jnp.dot(h, wu_ref[...], preferred_element_type=F32)
    a = (_silu(g) * u).astype(BF16)
    y = jnp.dot(a, wd_ref[...], preferred_element_type=F32)
    out = x + (0.5 * mod_ref[0, 2:3, :]) * y
    if final_norm:
        fw_ref = rest[0]
        out = out * lax.rsqrt(jnp.mean(out * out, axis=-1, keepdims=True) + EPS) * fw_ref[...]
    o_ref[...] = out


def _ffn(x, mod, norm_w, wg, wu, wd, blk_rows, final_w=None):
    t, d = x.shape
    f = wg.shape[1]
    tm = FFN_ROWS
    per_blk = blk_rows // tm
    const = lambda i: (0, 0)
    in_specs = [
        pl.BlockSpec((tm, d), lambda i: (i, 0)),
        pl.BlockSpec((1, 3, d), lambda i: (i // per_blk, 0, 0)),
        pl.BlockSpec((1, d), const),
        pl.BlockSpec((d, f), const),
        pl.BlockSpec((d, f), const),
        pl.BlockSpec((f, d), const),
    ]
    args = [x, mod, norm_w.reshape(1, d), wg, wu, wd]
    if final_w is not None:
        in_specs.append(pl.BlockSpec((1, d), const))
        args.append(final_w.reshape(1, d))
    return pl.pallas_call(
        functools.partial(_ffn_kernel, final_norm=final_w is not None),
        grid=(t // tm,),
        in_specs=in_specs,
        out_specs=pl.BlockSpec((tm, d), lambda i: (i, 0)),
        out_shape=jax.ShapeDtypeStruct((t, d), F32),
        compiler_params=_cparams(("arbitrary",)),
        name="ffn",
    )(*args)


def _inproj_kernel(x_ref, mod_ref, nw_ref, w_ref, wg_ref, wgt_ref, gp_ref, gpt_ref,
                   cos_ref, sfw_ref, sbk_ref,
                   q_ref, k_ref, v_ref, dn_ref, z_ref, gb_ref, gbt_ref):
    h = _modulated_norm(x_ref[...], nw_ref[...], mod_ref[0, 0:1, :], mod_ref[0, 1:2, :]).astype(BF16)
    p = jnp.dot(h, w_ref[...], preferred_element_type=F32)
    cos, sfw, sbk = cos_ref[...], sfw_ref[...], sbk_ref[...]

    def rope(xp):
        return xp * cos + pltpu.roll(xp, LANES - ROT_HALF, 1) * sfw + pltpu.roll(xp, ROT_HALF, 1) * sbk

    for j in range(ATTN_WIDTH // LANES):
        sl = slice(j * LANES, (j + 1) * LANES)
        q_ref[:, sl] = (rope(p[:, sl]) * (HEAD_DIM ** -0.5)).astype(BF16)
        k_ref[:, sl] = rope(p[:, ATTN_WIDTH + j * LANES:ATTN_WIDTH + (j + 1) * LANES]).astype(BF16)
    v_ref[...] = p[:, 2 * ATTN_WIDTH:3 * ATTN_WIDTH].astype(BF16)
    o0 = 3 * ATTN_WIDTH
    dn_ref[...] = p[:, o0:o0 + 3 * DN_WIDTH]
    z_ref[...] = p[:, o0 + 3 * DN_WIDTH:o0 + 4 * DN_WIDTH].astype(BF16)

    n_g = 2 * N_DN_HEADS
    ab = jnp.dot(h, wg_ref[...], preferred_element_type=F32)
    is_g = lax.broadcasted_iota(jnp.int32, ab.shape, 1) < n_g
    gp = gp_ref[...]
    gb_ref[...] = jnp.where(is_g, -jnp.exp(gp[0:1, :]) * jax.nn.softplus(ab + gp[1:2, :]), jax.nn.sigmoid(ab))
    abt = lax.dot_general(wgt_ref[...], h, (((1,), (1,)), ((), ())), preferred_element_type=F32)
    is_gt = lax.broadcasted_iota(jnp.int32, abt.shape, 0) < n_g
    gpt = gpt_ref[...]
    gbt_ref[...] = jnp.where(is_gt, -jnp.exp(gpt[:, 0:1]) * jax.nn.softplus(abt + gpt[:, 1:2]), jax.nn.sigmoid(abt))


def _in_proj(x, mod, norm_w, w_main, w_gate, w_gate_t, gate_par, gate_par_t, rope_tabs, blk_rows, seq_lens, t1):
    t, d = x.shape
    tm = PROJ_ROWS
    per_blk = blk_rows // tm
    n_main = w_main.shape[1]
    nb1 = t1 // tm
    s1b, s2b = seq_lens[0] // tm, seq_lens[1] // tm
    const = lambda i: (0, 0)
    row = lambda i: (i, 0)
    pos = lambda i: (jnp.where(i < nb1, i % s1b, (i - nb1) % s2b), 0)
    cos, sfw, sbk = rope_tabs
    outs = pl.pallas_call(
        _inproj_kernel,
        grid=(t // tm,),
        in_specs=[
            pl.BlockSpec((tm, d), row),
            pl.BlockSpec((1, 3, d), lambda i: (i // per_blk, 0, 0)),
            pl.BlockSpec((1, d), const),
            pl.BlockSpec((d, n_main), const),
            pl.BlockSpec((d, LANES), const),
            pl.BlockSpec((N_GATES, d), const),
            pl.BlockSpec((2, LANES), const),
            pl.BlockSpec((N_GATES, 2), const),
            pl.BlockSpec((tm, LANES), pos),
            pl.BlockSpec((tm, LANES), pos),
            pl.BlockSpec((tm, LANES), pos),
        ],
        out_specs=[
            pl.BlockSpec((tm, ATTN_WIDTH), row),
            pl.BlockSpec((tm, ATTN_WIDTH), row),
            pl.BlockSpec((tm, ATTN_WIDTH), row),
            pl.BlockSpec((tm, 3 * DN_WIDTH), row),
            pl.BlockSpec((tm, DN_WIDTH), row),
            pl.BlockSpec((tm, LANES), row),
            pl.BlockSpec((N_GATES, tm), lambda i: (0, i)),
        ],
        out_shape=[
            jax.ShapeDtypeStruct((t, ATTN_WIDTH), BF16),
            jax.ShapeDtypeStruct((t, ATTN_WIDTH), BF16),
            jax.ShapeDtypeStruct((t, ATTN_WIDTH), BF16),
            jax.ShapeDtypeStruct((t, 3 * DN_WIDTH), F32),
            jax.ShapeDtypeStruct((t, DN_WIDTH), BF16),
            jax.ShapeDtypeStruct((t, LANES), F32),
            jax.ShapeDtypeStruct((N_GATES, t), F32),
        ],
        compiler_params=_cparams(("arbitrary",)),
        name="in_proj",
    )(x, mod, norm_w.reshape(1, d), w_main, w_gate, w_gate_t, gate_par, gate_par_t, cos, sfw, sbk)
    return outs


def _rope_tables(max_len):
    half = ROT_HALF
    inv = ROPE_THETA ** (-jnp.arange(half, dtype=F32) / half)
    ang = jnp.arange(max_len, dtype=F32)[:, None] * inv[None, :]
    cos, sin = jnp.cos(ang), jnp.sin(ang)
    ones = jnp.ones((max_len, HEAD_DIM - 2 * half), F32)
    zeros_h = jnp.zeros((max_len, half), F32)
    zeros_r = jnp.zeros((max_len, HEAD_DIM - 2 * half), F32)
    cos_h = jnp.concatenate([cos, cos, ones], axis=1)
    sfw_h = jnp.concatenate([-sin, zeros_h, zeros_r], axis=1)
    sbk_h = jnp.concatenate([zeros_h, sin, zeros_r], axis=1)
    rep = LANES // HEAD_DIM
    return tuple(jnp.tile(a, (1, rep)) for a in (cos_h, sfw_h, sbk_h))


def _attn_kernel(q_ref, kp_ref, km_ref, kn_ref, vp_ref, vm_ref, vn_ref, *rest,
                 first, last, radius, rows1, len1, len2):
    if first:
        o_ref, l_ref = rest
    elif last:
        op_ref, lp_ref, o_ref = rest
    else:
        op_ref, lp_ref, o_ref, l_ref = rest
    tq = q_ref.shape[0]
    tk = tq + 2 * radius
    r0 = pl.program_id(1) * tq
    in_first = r0 < rows1
    seq_len = jnp.where(in_first, len1, len2)
    off = jnp.where(in_first, r0, r0 - rows1)
    first_key = jnp.where((off % seq_len) == 0, radius, 0)
    end_key = jnp.where(((off + tq) % seq_len) == 0, tq + radius, tk)

    qi = lax.broadcasted_iota(jnp.int32, (tq, tk), 0)
    kj = lax.broadcasted_iota(jnp.int32, (tq, tk), 1)
    valid = (jnp.abs(kj - radius - qi) <= radius) & (kj >= first_key) & (kj < end_key)
    bias = jnp.where(valid, 0.0, NEG_BIG).astype(F32)

    lane = lax.broadcasted_iota(jnp.int32, (1, LANES), 1)
    low_half = lane < HEAD_DIM
    nt = (((1,), (1,)), ((), ()))
    for j in range(ATTN_WIDTH // LANES):
        sl = slice(j * LANES, (j + 1) * LANES)
        q2 = q_ref[:, sl]
        kw = jnp.concatenate([kp_ref[:, sl], km_ref[:, sl], kn_ref[:, sl]], axis=0)
        vw = jnp.concatenate([vp_ref[:, sl], vm_ref[:, sl], vn_ref[:, sl]], axis=0)
        halves = []
        for head_lanes in (low_half, jnp.logical_not(low_half)):
            qh = jnp.where(head_lanes, q2, jnp.zeros_like(q2))
            s = lax.dot_general(qh, kw, nt, preferred_element_type=F32) + bias
            m = jnp.max(s, axis=-1, keepdims=True)
            p = jnp.exp(s - m)
            den = jnp.sum(p, axis=-1, keepdims=True)
            o = jnp.dot(p.astype(BF16), vw, preferred_element_type=F32) / den
            halves.append((o, m + jnp.log(den)))
        o_new = jnp.where(low_half, halves[0][0], halves[1][0])
        l_new = jnp.where(low_half, halves[0][1], halves[1][1])
        if not first:
            o_old, l_old = op_ref[:, sl], lp_ref[:, sl]
            mx = jnp.maximum(l_old, l_new)
            w_old, w_new = jnp.exp(l_old - mx), jnp.exp(l_new - mx)
            tot = w_old + w_new
            o_new = (w_old * o_old + w_new * o_new) / tot
            l_new = mx + jnp.log(tot)
        o_ref[:, sl] = o_new.astype(o_ref.dtype)
        if not last:
            l_ref[:, sl] = l_new


def _attn_pattern(q, k, v, prev, dil, radius, first, last, t1, seq_lens):
    t, w = q.shape
    rows = t // dil
    tq = ATTN_Q_ROWS
    nq = rows // tq
    per_q = tq // radius
    n_halo = rows // radius
    view = lambda a: a.reshape(rows, dil * w)
    main = lambda c, i: (i, c)
    before = lambda c, i: (jnp.maximum(i * per_q - 1, 0), c)
    after = lambda c, i: (jnp.minimum((i + 1) * per_q, n_halo - 1), c)
    kv_specs = [pl.BlockSpec((radius, w), before), pl.BlockSpec((tq, w), main), pl.BlockSpec((radius, w), after)]
    in_specs = [pl.BlockSpec((tq, w), main)] + kv_specs + kv_specs
    args = [view(q)] + [view(k)] * 3 + [view(v)] * 3
    if not first:
        in_specs += [pl.BlockSpec((tq, w), main)] * 2
        args += [view(prev[0]), view(prev[1])]
    out_specs = [pl.BlockSpec((tq, w), main)]
    out_shape = [jax.ShapeDtypeStruct((rows, dil * w), BF16 if last else F32)]
    if not last:
        out_specs.append(pl.BlockSpec((tq, w), main))
        out_shape.append(jax.ShapeDtypeStruct((rows, dil * w), F32))
    outs = pl.pallas_call(
        functools.partial(_attn_kernel, first=first, last=last, radius=radius,
                          rows1=t1 // dil, len1=seq_lens[0] // dil, len2=seq_lens[1] // dil),
        grid=(dil, nq),
        in_specs=in_specs,
        out_specs=out_specs,
        out_shape=out_shape,
        compiler_params=_cparams(("arbitrary", "arbitrary")),
        name=f"attn_d{dil}",
    )(*args)
    return [o.reshape(t, w) for o in outs]


def _attention(q, k, v, t1, seq_lens):
    state = None
    n = len(DILATED_PATTERNS)
    for idx, (window, dil) in enumerate(DILATED_PATTERNS):
        state = _attn_pattern(q, k, v, state, dil, window // (2 * dil), idx == 0, idx == n - 1, t1, seq_lens)
    return state[0]


def _dnprep_kernel(xp_ref, xm_ref, xn_ref, w_ref, q_ref, k_ref, v_ref, buf_ref, *, rows1, len1, len2):
    tm = xm_ref.shape[0]
    r0 = pl.program_id(0) * tm
    in_first = r0 < rows1
    seq_len = jnp.where(in_first, len1, len2)
    off = jnp.where(in_first, r0, r0 - rows1)
    keep_prev = jnp.where((off % seq_len) == 0, 0.0, 1.0).astype(F32)
    keep_next = jnp.where(((off + tm) % seq_len) == 0, 0.0, 1.0).astype(F32)
    buf_ref[0:HALO_ROWS, :] = xp_ref[...] * keep_prev
    buf_ref[HALO_ROWS:HALO_ROWS + tm, :] = xm_ref[...]
    buf_ref[HALO_ROWS + tm:, :] = xn_ref[...] * keep_next
    pad = CONV_K // 2
    y = buf_ref[HALO_ROWS - pad:HALO_ROWS - pad + tm, :] * w_ref[0:1, :]
    for j in range(1, CONV_K):
        y = y + buf_ref[HALO_ROWS - pad + j:HALO_ROWS - pad + j + tm, :] * w_ref[j:j + 1, :]
    y = _silu(y)
    for hd in range(N_DN_HEADS):
        sl = slice(hd * DN_HEAD_DIM, (hd + 1) * DN_HEAD_DIM)
        qh = y[:, sl]
        kh = y[:, DN_WIDTH + hd * DN_HEAD_DIM:DN_WIDTH + (hd + 1) * DN_HEAD_DIM]
        qn = qh * lax.rsqrt(jnp.sum(qh * qh, axis=-1, keepdims=True) + EPS) * (DN_HEAD_DIM ** -0.5)
        kn = kh * lax.rsqrt(jnp.sum(kh * kh, axis=-1, keepdims=True) + EPS)
        q_ref[:, sl] = qn.astype(BF16)
        k_ref[:, sl] = kn.astype(BF16)
    v_ref[...] = y[:, 2 * DN_WIDTH:].astype(BF16)


def _dn_prep(pre, conv_w, t1, seq_lens):
    t, w3 = pre.shape
    tm = PREP_ROWS
    per = tm // HALO_ROWS
    n_halo = t // HALO_ROWS
    row = lambda i: (i, 0)
    out_spec = pl.BlockSpec((tm, DN_WIDTH), row)
    out_sds = jax.ShapeDtypeStruct((t, DN_WIDTH), BF16)
    return pl.pallas_call(
        functools.partial(_dnprep_kernel, rows1=t1, len1=seq_lens[0], len2=seq_lens[1]),
        grid=(t // tm,),
        in_specs=[
            pl.BlockSpec((HALO_ROWS, w3), lambda i: (jnp.maximum(i * per - 1, 0), 0)),
            pl.BlockSpec((tm, w3), row),
            pl.BlockSpec((HALO_ROWS, w3), lambda i: (jnp.minimum((i + 1) * per, n_halo - 1), 0)),
            pl.BlockSpec((CONV_K, w3), lambda i: (0, 0)),
        ],
        out_specs=[out_spec, out_spec, out_spec],
        out_shape=[out_sds, out_sds, out_sds],
        scratch_shapes=[pltpu.VMEM((tm + 2 * HALO_ROWS, w3), F32)],
        compiler_params=_cparams(("arbitrary",)),
        name="dn_prep",
    )(pre, pre, pre, conv_w)


def _bdot(a, b):
    return jnp.dot(a.astype(BF16), b.astype(BF16), preferred_element_type=F32)


def _split3(x):
    hi = x.astype(BF16)
    r1 = x - hi.astype(F32)
    mid = r1.astype(BF16)
    lo = (r1 - mid.astype(F32)).astype(BF16)
    return hi, mid, lo


def _dn_kernel(qf_ref, kf_ref, vf_ref, gf_ref, gtf_ref, qb_ref, kb_ref, vb_ref, gb_ref, gtb_ref,
               of_ref, ob_ref, sf_ref, sb_ref, *, n_chunks, chunks1, len1, len2):
    c = qf_ref.shape[0]
    i = pl.program_id(0)

    def seq_pos(ci):
        in_first = ci < chunks1
        seq_len = jnp.where(in_first, len1, len2)
        off = jnp.where(in_first, ci, ci - chunks1)
        return off % seq_len, seq_len

    pos_f, _ = seq_pos(i)
    pos_b, len_b = seq_pos(n_chunks - 1 - i)

    @pl.when(pos_f == 0)
    def _():
        sf_ref[...] = jnp.zeros_like(sf_ref)

    @pl.when(pos_b == len_b - 1)
    def _():
        sb_ref[...] = jnp.zeros_like(sb_ref)

    ii = lax.broadcasted_iota(jnp.int32, (c, c), 0)
    jj = lax.broadcasted_iota(jnp.int32, (c, c), 1)
    lower, lower_strict = ii >= jj, ii > jj
    upper, upper_strict = ii <= jj, ii < jj
    eye = jnp.where(ii == jj, 1.0, 0.0).astype(F32)
    tri_l = jnp.where(lower, 1.0, 0.0).astype(F32)
    tri_u = jnp.where(upper, 1.0, 0.0).astype(F32)
    level_masks = [jnp.where(ii // INV_BASE == jj // INV_BASE, 1.0, 0.0).astype(F32)]
    b = INV_BASE
    while b < c:
        same_outer = ii // (2 * b) == jj // (2 * b)
        level_masks.append(jnp.where(same_outer & (ii // b != jj // b), 1.0, 0.0).astype(F32))
        b *= 2
    ones_cc = jnp.ones((c, c), BF16)
    ones_cd = jnp.ones((c, DN_HEAD_DIM), BF16)
    nt = (((1,), (1,)), ((), ()))

    chains = []
    for forward, (q_ref, k_ref, v_ref, g_ref, gt_ref, o_ref, s_ref) in (
            (True, (qf_ref, kf_ref, vf_ref, gf_ref, gtf_ref, of_ref, sf_ref)),
            (False, (qb_ref, kb_ref, vb_ref, gb_ref, gtb_ref, ob_ref, sb_ref))):
        g, gt = g_ref[...], gt_ref[...]
        tri_c, tri_r = (tri_l, tri_u) if forward else (tri_u, tri_l)
        col_lhs = jnp.concatenate([tri_c.astype(BF16), ones_cc], axis=0)
        col = sum(jnp.dot(col_lhs, piece, preferred_element_type=F32) for piece in _split3(g))
        row_rhs = jnp.concatenate([tri_r.astype(BF16), ones_cd], axis=1)
        row = sum(jnp.dot(piece, row_rhs, preferred_element_type=F32) for piece in _split3(gt))
        cs_col, tot_col = col[:c], col[c:]
        cs_row, tot_row = row[:, :c], row[:, c:]
        keep, strict = (lower, lower_strict) if forward else (upper, upper_strict)
        d0 = 0 if forward else N_DN_HEADS
        for hd in range(N_DN_HEADS):
            gi = d0 + hd
            bi = 2 * N_DN_HEADS + d0 + hd
            chains.append(dict(
                sl=slice(hd * DN_HEAD_DIM, (hd + 1) * DN_HEAD_DIM), hd=hd, keep=keep, strict=strict,
                q_ref=q_ref, k_ref=k_ref, v_ref=v_ref, o_ref=o_ref, s_ref=s_ref,
                gc_col=cs_col[:, gi:gi + 1], gc_row=cs_row[gi:gi + 1, :], beta=g[:, bi:bi + 1],
                tot_col=tot_col[:, gi:gi + 1], tot_row=tot_row[gi:gi + 1, :]))

    for ch in chains:
        qh, kh = ch["q_ref"][:, ch["sl"]], ch["k_ref"][:, ch["sl"]]
        both = lax.dot_general(jnp.concatenate([qh, kh], axis=0), kh, nt, preferred_element_type=F32)
        decay = jnp.where(ch["keep"], jnp.exp(jnp.minimum(ch["gc_col"] - ch["gc_row"], 0.0)), 0.0)
        ch["a"] = jnp.where(ch["strict"], ch["beta"] * both[c:] * decay, 0.0)
        ch["qk"] = (both[:c] * decay).astype(BF16)
    for ch in chains:
        dg = ch["a"] * level_masks[0]
        ch["x"] = eye - dg
        ch["y"] = _bdot(dg, dg)
    for ch in chains:
        ch["x"] = ch["x"] + _bdot(ch["x"], ch["y"])
    for off_mask in level_masks[1:]:
        for ch in chains:
            ch["y"] = _bdot(ch["a"] * off_mask, ch["x"])
        for ch in chains:
            ch["x"] = ch["x"] - _bdot(ch["x"], ch["y"])
    for ch in chains:
        kh = ch["k_ref"][:, ch["sl"]].astype(F32)
        vh = ch["v_ref"][:, ch["sl"]].astype(F32)
        egc = jnp.exp(ch["gc_col"])
        rhs = jnp.concatenate([vh * ch["beta"], kh * (ch["beta"] * egc)], axis=1)
        ch["uw"] = _bdot(ch["x"], rhs)
        ch["egc"] = egc
    for ch in chains:
        qh = ch["q_ref"][:, ch["sl"]].astype(F32)
        lhs = jnp.concatenate([ch["uw"][:, DN_HEAD_DIM:], qh * ch["egc"]], axis=0)
        ch["state"] = ch["s_ref"][ch["hd"]]
        ch["ws"] = _bdot(lhs, ch["state"])
    for ch in chains:
        v_new = (ch["uw"][:, :DN_HEAD_DIM] - ch["ws"][:c]).astype(BF16)
        ch["o_ref"][:, ch["sl"]] = ch["ws"][c:] + jnp.dot(ch["qk"], v_new, preferred_element_type=F32)
        ch["v_new"] = v_new
    for ch in chains:
        kh = ch["k_ref"][:, ch["sl"]].astype(F32)
        k_dec_t = jnp.transpose(kh * jnp.exp(ch["tot_col"] - ch["gc_col"])).astype(BF16)
        ch["s_ref"][ch["hd"]] = ch["state"] * jnp.exp(ch["tot_row"]) + jnp.dot(
            k_dec_t, ch["v_new"], preferred_element_type=F32)


def _deltanet(q, k, v, gates, gates_t, t1, seq_lens):
    t, w = q.shape
    c = DN_CHUNK
    n = t // c
    fwd = lambda i: (i, 0)
    bwd = lambda i: (n - 1 - i, 0)
    fwd_t = lambda i: (0, i)
    bwd_t = lambda i: (0, n - 1 - i)

    def specs(row_map, col_map):
        return [pl.BlockSpec((c, w), row_map)] * 3 + [
            pl.BlockSpec((c, LANES), row_map), pl.BlockSpec((N_GATES, c), col_map)]

    state = pltpu.VMEM((N_DN_HEADS, DN_HEAD_DIM, DN_HEAD_DIM), F32)
    return pl.pallas_call(
        functools.partial(_dn_kernel, n_chunks=n, chunks1=t1 // c, len1=seq_lens[0] // c, len2=seq_lens[1] // c),
        grid=(n,),
        in_specs=specs(fwd, fwd_t) + specs(bwd, bwd_t),
        out_specs=[pl.BlockSpec((c, w), fwd), pl.BlockSpec((c, w), bwd)],
        out_shape=[jax.ShapeDtypeStruct((t, w), F32)] * 2,
        scratch_shapes=[state, state],
        compiler_params=_cparams(("arbitrary",)),
        name="deltanet",
    )(q, k, v, gates, gates_t, q, k, v, gates, gates_t)


def _outproj_kernel(x_ref, mod_ref, attn_ref, of_ref, ob_ref, z_ref, nw_ref, wa_ref, wd_ref, o_ref):
    o = of_ref[...] + ob_ref[...]
    z = z_ref[...].astype(F32)
    nw = nw_ref[...]
    parts = []
    for hd in range(N_DN_HEADS):
        sl = slice(hd * DN_HEAD_DIM, (hd + 1) * DN_HEAD_DIM)
        oh = o[:, sl]
        parts.append(oh * lax.rsqrt(jnp.mean(oh * oh, axis=-1, keepdims=True) + EPS) * nw * _silu(z[:, sl]))
    dn = jnp.concatenate(parts, axis=1).astype(BF16)
    y = jnp.dot(attn_ref[...], wa_ref[...], preferred_element_type=F32)
    y = y + jnp.dot(dn, wd_ref[...], preferred_element_type=F32)
    o_ref[...] = x_ref[...] + mod_ref[0, 2:3, :] * y


def _out_proj(x, mod, attn, o_f, o_b, z, dn_norm, w_attn, w_dn, blk_rows):
    t, d = x.shape
    tm = PROJ_ROWS
    per_blk = blk_rows // tm
    row = lambda i: (i, 0)
    const = lambda i: (0, 0)
    return pl.pallas_call(
        _outproj_kernel,
        grid=(t // tm,),
        in_specs=[
            pl.BlockSpec((tm, d), row),
            pl.BlockSpec((1, 3, d), lambda i: (i // per_blk, 0, 0)),
            pl.BlockSpec((tm, ATTN_WIDTH), row),
            pl.BlockSpec((tm, DN_WIDTH), row),
            pl.BlockSpec((tm, DN_WIDTH), row),
            pl.BlockSpec((tm, DN_WIDTH), row),
            pl.BlockSpec((1, DN_HEAD_DIM), const),
            pl.BlockSpec((ATTN_WIDTH, d), const),
            pl.BlockSpec((DN_WIDTH, d), const),
        ],
        out_specs=pl.BlockSpec((tm, d), row),
        out_shape=jax.ShapeDtypeStruct((t, d), F32),
        compiler_params=_cparams(("arbitrary",)),
        name="out_proj",
    )(x, mod, attn, o_f, o_b, z, dn_norm.reshape(1, DN_HEAD_DIM), w_attn, w_dn)


def kernel(x_prompt, x_sample, c_prompt, c_sample, ada_w, ada_b, norm_ffn1, ffn1_w_gate, ffn1_w_up, ffn1_w_down, norm_mix, w_in, conv_w, a_log, dt_bias, dn_norm, w_out, norm_ffn2, ffn2_w_gate, ffn2_w_up, ffn2_w_down, norm_final):
    b1, s1, d = x_prompt.shape
    b2, s2, _ = x_sample.shape
    depth = ada_w.shape[0]
    t1 = b1 * s1
    seq_lens = (s1, s2)
    assert s2 % s1 == 0 and t1 % s2 == 0, "flat layout needs nested sequence lengths"
    x = jnp.concatenate([x_prompt.reshape(t1, d), x_sample.reshape(b2 * s2, d)], axis=0)

    n_seq = b1 + b2
    c_all = jnp.concatenate([c_prompt, c_sample, jnp.zeros((-n_seq % SUBLANES, d), F32)], axis=0)
    mod = _modulation(c_all, ada_w, ada_b)
    blk_seq = jnp.concatenate([jnp.arange(b1), b1 + jnp.repeat(jnp.arange(b2), s2 // s1)])
    mod = mod[:, blk_seq].reshape(depth, blk_seq.shape[0], N_MOD, d)

    rope_tabs = _rope_tables(max(s1, s2))
    n_main = 3 * ATTN_WIDTH + 4 * DN_WIDTH
    gate_par_t = jnp.stack([a_log.astype(F32).reshape(depth, -1), dt_bias.astype(F32).reshape(depth, -1)], axis=2)
    gate_par_t = jnp.pad(gate_par_t, ((0, 0), (0, N_GATES - gate_par_t.shape[1]), (0, 0)))
    gate_par = jnp.pad(jnp.swapaxes(gate_par_t, 1, 2), ((0, 0), (0, 0), (0, LANES - N_GATES)))

    for l in range(depth):
        bf = lambda a: a.astype(BF16)
        x = _ffn(x, mod[l, :, 0:3], norm_ffn1[l], bf(ffn1_w_gate[l]), bf(ffn1_w_up[l]), bf(ffn1_w_down[l]), s1)
        w_gate_t = bf(w_in[l][:, n_main:]).T
        w_gate = jnp.pad(w_gate_t.T, ((0, 0), (0, LANES - N_GATES)))
        q, k, v, dn_pre, z, gates, gates_t = _in_proj(
            x, mod[l, :, 3:6], norm_mix[l], bf(w_in[l][:, :n_main]), w_gate, w_gate_t,
            gate_par[l], gate_par_t[l], rope_tabs, s1, seq_lens, t1)
        attn = _attention(q, k, v, t1, seq_lens)
        dq, dk, dv = _dn_prep(dn_pre, conv_w[l], t1, seq_lens)
        o_f, o_b = _deltanet(dq, dk, dv, gates, gates_t, t1, seq_lens)
        x = _out_proj(x, mod[l, :, 3:6], attn, o_f, o_b, z, dn_norm[l], bf(w_out[l][:ATTN_WIDTH]),
                      bf(w_out[l][ATTN_WIDTH:]), s1)
        x = _ffn(x, mod[l, :, 6:9], norm_ffn2[l], bf(ffn2_w_gate[l]), bf(ffn2_w_up[l]), bf(ffn2_w_down[l]), s1,
                 final_w=norm_final if l == depth - 1 else None)
    return x[:t1].reshape(b1, s1, d), x[t1:].reshape(b2, s2, d)
```

```python
import functools

import jax
import jax.numpy as jnp
from jax import lax
from jax.experimental import pallas as pl
from jax.experimental.pallas import tpu as pltpu

F32 = jnp.float32
BF16 = jnp.bfloat16

D_MODEL = 1024
D_FF = 2816
N_MOD = 9
EPS = 1e-6
N_ATTN_HEADS = 8
HEAD_DIM = 64
ATTN_WIDTH = N_ATTN_HEADS * HEAD_DIM
ROT_HALF = HEAD_DIM // 8
ROPE_THETA = 500000.0
DILATED_PATTERNS = ((128, 1), (512, 4), (2048, 16))
N_DN_HEADS = 4
DN_HEAD_DIM = 128
DN_WIDTH = N_DN_HEADS * DN_HEAD_DIM
CONV_K = 5
N_GATES = 4 * N_DN_HEADS

LANES = 128
SUBLANES = 8
VMEM_LIMIT_BYTES = 56 * 1024 * 1024

FFN_ROWS = 256
PROJ_ROWS = 256
ATTN_Q_ROWS = {1: 128, 4: 64, 16: 64}
DN_CHUNK = 128
INV_BASE = 4
PREP_ROWS = 512
MOD_COLS = 1152
HALO_ROWS = 8
NEG_BIG = -1e30


def _cparams(sem):
    return pltpu.CompilerParams(dimension_semantics=sem, vmem_limit_bytes=VMEM_LIMIT_BYTES)


def _silu(x):
    return x * jax.nn.sigmoid(x)


def _modulated_norm(x, norm_w, shift, scale):
    y = x * lax.rsqrt(jnp.mean(x * x, axis=-1, keepdims=True) + EPS) * norm_w
    return y * (1.0 + scale) + shift


def _mod_kernel(c_ref, w_ref, b_ref, o_ref):
    sc = _silu(c_ref[...]).astype(BF16)
    o_ref[0] = jnp.dot(sc, w_ref[0].astype(BF16), preferred_element_type=F32) + b_ref[0]


def _modulation(c_all, ada_w, ada_b):
    depth, d, n = ada_w.shape
    rows = c_all.shape[0]
    return pl.pallas_call(
        _mod_kernel,
        grid=(depth, n // MOD_COLS),
        in_specs=[
            pl.BlockSpec((rows, d), lambda l, j: (0, 0)),
            pl.BlockSpec((1, d, MOD_COLS), lambda l, j: (l, 0, j)),
            pl.BlockSpec((1, 1, MOD_COLS), lambda l, j: (l, 0, j)),
        ],
        out_specs=pl.BlockSpec((1, rows, MOD_COLS), lambda l, j: (l, 0, j)),
        out_shape=jax.ShapeDtypeStruct((depth, rows, n), F32),
        compiler_params=_cparams(("arbitrary", "arbitrary")),
        name="modulation",
    )(c_all, ada_w, ada_b.reshape(depth, 1, n))


def _ffn_kernel(x_ref, mod_ref, nw_ref, wg_ref, wu_ref, wd_ref, *rest, final_norm):
    o_ref = rest[-1]
    x = x_ref[...]
    h = _modulated_norm(x, nw_ref[...], mod_ref[0, 0:1, :], mod_ref[0, 1:2, :]).astype(BF16)
    g = jnp.dot(h, wg_ref[...], preferred_element_type=F32)
    u = jnp.dot(h, wu_ref[...], preferred_element_type=F32)
    a = (_silu(g) * u).astype(BF16)
    y = jnp.dot(a, wd_ref[...], preferred_element_type=F32)
    out = x + (0.5 * mod_ref[0, 2:3, :]) * y
    if final_norm:
        fw_ref = rest[0]
        out = out * lax.rsqrt(jnp.mean(out * out, axis=-1, keepdims=True) + EPS) * fw_ref[...]
    o_ref[...] = out


def _ffn(x, mod, norm_w, wg, wu, wd, blk_rows, final_w=None):
    t, d = x.shape
    f = wg.shape[1]
    tm = FFN_ROWS
    per_blk = blk_rows // tm
    const = lambda i: (0, 0)
    in_specs = [
        pl.BlockSpec((tm, d), lambda i: (i, 0)),
        pl.BlockSpec((1, 3, d), lambda i: (i // per_blk, 0, 0)),
        pl.BlockSpec((1, d), const),
        pl.BlockSpec((d, f), const),
        pl.BlockSpec((d, f), const),
        pl.BlockSpec((f, d), const),
    ]
    args = [x, mod, norm_w.reshape(1, d), wg, wu, wd]
    if final_w is not None:
        in_specs.append(pl.BlockSpec((1, d), const))
        args.append(final_w.reshape(1, d))
    return pl.pallas_call(
        functools.partial(_ffn_kernel, final_norm=final_w is not None),
        grid=(t // tm,),
        in_specs=in_specs,
        out_specs=pl.BlockSpec((tm, d), lambda i: (i, 0)),
        out_shape=jax.ShapeDtypeStruct((t, d), F32),
        compiler_params=_cparams(("arbitrary",)),
        name="ffn",
    )(*args)


def _inproj_kernel(x_ref, mod_ref, nw_ref, w_ref, wg_ref, wgt_ref, gp_ref, gpt_ref,
                   cos_ref, sfw_ref, sbk_ref, *rest):
    n_views = len(DILATED_PATTERNS)
    qkv_refs = [rest[3 * n:3 * n + 3] for n in range(n_views)]
    dn_ref, z_ref, gb_ref, gbt_ref, stage_ref = rest[3 * n_views:]
    tm = x_ref.shape[0]
    h = _modulated_norm(x_ref[...], nw_ref[...], mod_ref[0, 0:1, :], mod_ref[0, 1:2, :]).astype(BF16)
    p = jnp.dot(h, w_ref[...], preferred_element_type=F32)
    cos, sfw, sbk = cos_ref[...], sfw_ref[...], sbk_ref[...]

    def rope(xp):
        return xp * cos + pltpu.roll(xp, LANES - ROT_HALF, 1) * sfw + pltpu.roll(xp, ROT_HALF, 1) * sbk

    n_tiles = ATTN_WIDTH // LANES
    for a in range(3):
        for j in range(n_tiles):
            col = p[:, a * ATTN_WIDTH + j * LANES:a * ATTN_WIDTH + (j + 1) * LANES]
            if a == 0:
                col = rope(col) * (HEAD_DIM ** -0.5)
            elif a == 1:
                col = rope(col)
            stage_ref[a * n_tiles + j] = col
    for n, (_, dil) in enumerate(DILATED_PATTERNS):
        rows = tm // dil
        for a in range(3):
            for j in range(n_tiles):
                for c in range(dil):
                    lo = c * ATTN_WIDTH + j * LANES
                    src = stage_ref[a * n_tiles + j, pl.ds(c, rows, stride=dil), :] if dil > 1 else stage_ref[a * n_tiles + j]
                    qkv_refs[n][a][:, lo:lo + LANES] = src.astype(BF16)
    o0 = 3 * ATTN_WIDTH
    dn_ref[...] = p[:, o0:o0 + 3 * DN_WIDTH]
    z_ref[...] = p[:, o0 + 3 * DN_WIDTH:o0 + 4 * DN_WIDTH].astype(BF16)

    n_g = 2 * N_DN_HEADS
    ab = jnp.dot(h, wg_ref[...], preferred_element_type=F32)
    is_g = lax.broadcasted_iota(jnp.int32, ab.shape, 1) < n_g
    gp = gp_ref[...]
    gb_ref[...] = jnp.where(is_g, -jnp.exp(gp[0:1, :]) * jax.nn.softplus(ab + gp[1:2, :]), jax.nn.sigmoid(ab))
    abt = lax.dot_general(wgt_ref[...], h, (((1,), (1,)), ((), ())), preferred_element_type=F32)
    is_gt = lax.broadcasted_iota(jnp.int32, abt.shape, 0) < n_g
    gpt = gpt_ref[...]
    gbt_ref[...] = jnp.where(is_gt, -jnp.exp(gpt[:, 0:1]) * jax.nn.softplus(abt + gpt[:, 1:2]), jax.nn.sigmoid(abt))


def _in_proj(x, mod, norm_w, w_main, w_gate, w_gate_t, gate_par, gate_par_t, rope_tabs, blk_rows, seq_lens, t1):
    t, d = x.shape
    tm = PROJ_ROWS
    per_blk = blk_rows // tm
    n_main = w_main.shape[1]
    nb1 = t1 // tm
    s1b, s2b = seq_lens[0] // tm, seq_lens[1] // tm
    const = lambda i: (0, 0)
    row = lambda i: (i, 0)
    pos = lambda i: (jnp.where(i < nb1, i % s1b, (i - nb1) % s2b), 0)
    cos, sfw, sbk = rope_tabs
    view_specs, view_shapes = [], []
    for _, dil in DILATED_PATTERNS:
        view_specs += [pl.BlockSpec((tm // dil, dil * ATTN_WIDTH), row)] * 3
        view_shapes += [jax.ShapeDtypeStruct((t // dil, dil * ATTN_WIDTH), BF16)] * 3
    outs = pl.pallas_call(
        _inproj_kernel,
        grid=(t // tm,),
        in_specs=[
            pl.BlockSpec((tm, d), row),
            pl.BlockSpec((1, 3, d), lambda i: (i // per_blk, 0, 0)),
            pl.BlockSpec((1, d), const),
            pl.BlockSpec((d, n_main), const),
            pl.BlockSpec((d, LANES), const),
            pl.BlockSpec((N_GATES, d), const),
            pl.BlockSpec((2, LANES), const),
            pl.BlockSpec((N_GATES, 2), const),
            pl.BlockSpec((tm, LANES), pos),
            pl.BlockSpec((tm, LANES), pos),
            pl.BlockSpec((tm, LANES), pos),
        ],
        out_specs=view_specs + [
            pl.BlockSpec((tm, 3 * DN_WIDTH), row),
            pl.BlockSpec((tm, DN_WIDTH), row),
            pl.BlockSpec((tm, LANES), row),
            pl.BlockSpec((N_GATES, tm), lambda i: (0, i)),
        ],
        out_shape=view_shapes + [
            jax.ShapeDtypeStruct((t, 3 * DN_WIDTH), F32),
            jax.ShapeDtypeStruct((t, DN_WIDTH), BF16),
            jax.ShapeDtypeStruct((t, LANES), F32),
            jax.ShapeDtypeStruct((N_GATES, t), F32),
        ],
        scratch_shapes=[pltpu.VMEM((3 * ATTN_WIDTH // LANES, tm, LANES), F32)],
        compiler_params=_cparams(("arbitrary",)),
        name="in_proj",
    )(x, mod, norm_w.reshape(1, d), w_main, w_gate, w_gate_t, gate_par, gate_par_t, cos, sfw, sbk)
    n_qkv = 3 * len(DILATED_PATTERNS)
    qkv_views = [outs[3 * n:3 * n + 3] for n in range(len(DILATED_PATTERNS))]
    return qkv_views, outs[n_qkv:]


def _rope_tables(max_len):
    half = ROT_HALF
    inv = ROPE_THETA ** (-jnp.arange(half, dtype=F32) / half)
    ang = jnp.arange(max_len, dtype=F32)[:, None] * inv[None, :]
    cos, sin = jnp.cos(ang), jnp.sin(ang)
    ones = jnp.ones((max_len, HEAD_DIM - 2 * half), F32)
    zeros_h = jnp.zeros((max_len, half), F32)
    zeros_r = jnp.zeros((max_len, HEAD_DIM - 2 * half), F32)
    cos_h = jnp.concatenate([cos, cos, ones], axis=1)
    sfw_h = jnp.concatenate([-sin, zeros_h, zeros_r], axis=1)
    sbk_h = jnp.concatenate([zeros_h, sin, zeros_r], axis=1)
    rep = LANES // HEAD_DIM
    return tuple(jnp.tile(a, (1, rep)) for a in (cos_h, sfw_h, sbk_h))


def _attn_kernel(q_ref, kp_ref, km_ref, kn_ref, vp_ref, vm_ref, vn_ref, *rest,
                 dil, next_dil, first, radius, rows1, len1, len2):
    last = next_dil is None
    if not first:
        accp_ref, mlp_ref = rest[:2]
        rest = rest[2:]
    if last:
        o_ref, acc_scr, nat_scr = rest
    else:
        acc_out_ref, ml_out_ref, acc_scr, ml_scr = rest
    tq = q_ref.shape[0]
    tk = tq + 2 * radius
    n_tiles = ATTN_WIDTH // LANES
    c = pl.program_id(1)
    r0 = pl.program_id(0) * tq
    in_first = r0 < rows1
    seq_len = jnp.where(in_first, len1, len2)
    off = jnp.where(in_first, r0, r0 - rows1)
    first_key = jnp.where((off % seq_len) == 0, radius, 0)
    end_key = jnp.where(((off + tq) % seq_len) == 0, tq + radius, tk)

    qi = lax.broadcasted_iota(jnp.int32, (tq, tk), 0)
    kj = lax.broadcasted_iota(jnp.int32, (tq, tk), 1)
    valid = (jnp.abs(kj - radius - qi) <= radius) & (kj >= first_key) & (kj < end_key)
    bias = jnp.where(valid, 0.0, NEG_BIG).astype(F32)

    lane = lax.broadcasted_iota(jnp.int32, (1, LANES), 1)
    low_half = lane < HEAD_DIM
    nt = (((1,), (1,)), ((), ()))
    heads = []
    for j in range(n_tiles):
        sl = slice(j * LANES, (j + 1) * LANES)
        q2 = q_ref[:, sl]
        kw = jnp.concatenate([kp_ref[:, sl], km_ref[:, sl], kn_ref[:, sl]], axis=0)
        vw = jnp.concatenate([vp_ref[:, sl], vm_ref[:, sl], vn_ref[:, sl]], axis=0)
        for head_lanes in (low_half, jnp.logical_not(low_half)):
            qh = jnp.where(head_lanes, q2, jnp.zeros_like(q2))
            heads.append(dict(s=lax.dot_general(qh, kw, nt, preferred_element_type=F32) + bias, vw=vw))
    for h, hd in enumerate(heads):
        m_cur = jnp.max(hd["s"], axis=-1, keepdims=True)
        if first:
            hd["m"] = m_cur
        else:
            m_old = mlp_ref[:, h:h + 1]
            hd["m"] = jnp.maximum(m_old, m_cur)
            hd["alpha"] = jnp.exp(m_old - hd["m"])
    for h, hd in enumerate(heads):
        p = jnp.exp(hd["s"] - hd["m"])
        hd["l"] = jnp.sum(p, axis=-1, keepdims=True)
        if not first:
            hd["l"] = hd["alpha"] * mlp_ref[:, N_ATTN_HEADS + h:N_ATTN_HEADS + h + 1] + hd["l"]
        hd["p"] = p.astype(BF16)
    for hd in heads:
        hd["pv"] = jnp.dot(hd["p"], hd["vw"], preferred_element_type=F32)
    for j in range(n_tiles):
        a, b = heads[2 * j], heads[2 * j + 1]
        acc = jnp.where(low_half, a["pv"], b["pv"])
        if not first:
            acc = accp_ref[:, j * LANES:(j + 1) * LANES] * jnp.where(low_half, a["alpha"], b["alpha"]) + acc
        if last:
            acc = acc / jnp.where(low_half, a["l"], b["l"])
        acc_scr[c, j] = acc
    if not last:
        ml = jnp.zeros((tq, LANES), F32)
        for h, hd in enumerate(heads):
            ml = jnp.where(lane == h, hd["m"], ml)
            ml = jnp.where(lane == N_ATTN_HEADS + h, hd["l"], ml)
        ml_scr[c] = ml

    @pl.when(c == dil - 1)
    def _():
        if last:
            for cc in range(dil):
                for j in range(n_tiles):
                    nat_scr[j, pl.ds(cc, tq, stride=dil), :] = acc_scr[cc, j]
            for j in range(n_tiles):
                o_ref[:, j * LANES:(j + 1) * LANES] = nat_scr[j].astype(o_ref.dtype)
        else:
            ratio = next_dil // dil
            rows_out = tq // ratio
            for cc in range(dil):
                for m in range(ratio):
                    cb = m * dil + cc
                    rows = pl.ds(m, rows_out, stride=ratio)
                    for j in range(n_tiles):
                        lo = cb * ATTN_WIDTH + j * LANES
                        acc_out_ref[:, lo:lo + LANES] = acc_scr[cc, j, rows, :]
                    ml_out_ref[:, cb * LANES:(cb + 1) * LANES] = ml_scr[cc, rows, :]


def _attn_pattern(qkv, prev, dil, next_dil, radius, t, t1, seq_lens):
    q, k, v = qkv
    w = ATTN_WIDTH
    first, last = prev is None, next_dil is None
    rows = t // dil
    tq = ATTN_Q_ROWS[dil]
    per_q = tq // radius
    n_halo = rows // radius
    n_tiles = w // LANES
    main = lambda i, c: (i, c)
    before = lambda i, c: (jnp.maximum(i * per_q - 1, 0), c)
    after = lambda i, c: (jnp.minimum((i + 1) * per_q, n_halo - 1), c)
    whole = lambda i, c: (i, 0)
    kv_specs = [pl.BlockSpec((radius, w), before), pl.BlockSpec((tq, w), main), pl.BlockSpec((radius, w), after)]
    in_specs = [pl.BlockSpec((tq, w), main)] + kv_specs + kv_specs
    args = [q, k, k, k, v, v, v]
    if not first:
        in_specs += [pl.BlockSpec((tq, w), main), pl.BlockSpec((tq, LANES), main)]
        args += list(prev)
    scratch = [pltpu.VMEM((dil, n_tiles, tq, LANES), F32)]
    if last:
        out_specs = [pl.BlockSpec((tq * dil, w), whole)]
        out_shape = [jax.ShapeDtypeStruct((t, w), BF16)]
        scratch.append(pltpu.VMEM((n_tiles, tq * dil, LANES), F32))
    else:
        rows_out = tq * dil // next_dil
        out_specs = [pl.BlockSpec((rows_out, next_dil * w), whole), pl.BlockSpec((rows_out, next_dil * LANES), whole)]
        out_shape = [jax.ShapeDtypeStruct((t // next_dil, next_dil * w), F32),
                     jax.ShapeDtypeStruct((t // next_dil, next_dil * LANES), F32)]
        scratch.append(pltpu.VMEM((dil, tq, LANES), F32))
    return pl.pallas_call(
        functools.partial(_attn_kernel, dil=dil, next_dil=next_dil, first=first, radius=radius,
                          rows1=t1 // dil, len1=seq_lens[0] // dil, len2=seq_lens[1] // dil),
        grid=(rows // tq, dil),
        in_specs=in_specs,
        out_specs=out_specs,
        out_shape=out_shape,
        scratch_shapes=scratch,
        compiler_params=_cparams(("arbitrary", "arbitrary")),
        name=f"attn_d{dil}",
    )(*args)


def _attention(qkv_views, t, t1, seq_lens):
    state = None
    n = len(DILATED_PATTERNS)
    for idx, (window, dil) in enumerate(DILATED_PATTERNS):
        next_dil = DILATED_PATTERNS[idx + 1][1] if idx + 1 < n else None
        state = _attn_pattern(qkv_views[idx], state, dil, next_dil, window // (2 * dil), t, t1, seq_lens)
    return state[0]


def _dnprep_kernel(xp_ref, xm_ref, xn_ref, w_ref, q_ref, k_ref, v_ref, buf_ref, *, rows1, len1, len2):
    tm = xm_ref.shape[0]
    r0 = pl.program_id(0) * tm
    in_first = r0 < rows1
    seq_len = jnp.where(in_first, len1, len2)
    off = jnp.where(in_first, r0, r0 - rows1)
    keep_prev = jnp.where((off % seq_len) == 0, 0.0, 1.0).astype(F32)
    keep_next = jnp.where(((off + tm) % seq_len) == 0, 0.0, 1.0).astype(F32)
    buf_ref[0:HALO_ROWS, :] = xp_ref[...] * keep_prev
    buf_ref[HALO_ROWS:HALO_ROWS + tm, :] = xm_ref[...]
    buf_ref[HALO_ROWS + tm:, :] = xn_ref[...] * keep_next
    pad = CONV_K // 2
    y = buf_ref[HALO_ROWS - pad:HALO_ROWS - pad + tm, :] * w_ref[0:1, :]
    for j in range(1, CONV_K):
        y = y + buf_ref[HALO_ROWS - pad + j:HALO_ROWS - pad + j + tm, :] * w_ref[j:j + 1, :]
    y = _silu(y)
    for hd in range(N_DN_HEADS):
        sl = slice(hd * DN_HEAD_DIM, (hd + 1) * DN_HEAD_DIM)
        qh = y[:, sl]
        kh = y[:, DN_WIDTH + hd * DN_HEAD_DIM:DN_WIDTH + (hd + 1) * DN_HEAD_DIM]
        qn = qh * lax.rsqrt(jnp.sum(qh * qh, axis=-1, keepdims=True) + EPS) * (DN_HEAD_DIM ** -0.5)
        kn = kh * lax.rsqrt(jnp.sum(kh * kh, axis=-1, keepdims=True) + EPS)
        q_ref[:, sl] = qn.astype(BF16)
        k_ref[:, sl] = kn.astype(BF16)
    v_ref[...] = y[:, 2 * DN_WIDTH:].astype(BF16)


def _dn_prep(pre, conv_w, t1, seq_lens):
    t, w3 = pre.shape
    tm = PREP_ROWS
    per = tm // HALO_ROWS
    n_halo = t // HALO_ROWS
    row = lambda i: (i, 0)
    out_spec = pl.BlockSpec((tm, DN_WIDTH), row)
    out_sds = jax.ShapeDtypeStruct((t, DN_WIDTH), BF16)
    return pl.pallas_call(
        functools.partial(_dnprep_kernel, rows1=t1, len1=seq_lens[0], len2=seq_lens[1]),
        grid=(t // tm,),
        in_specs=[
            pl.BlockSpec((HALO_ROWS, w3), lambda i: (jnp.maximum(i * per - 1, 0), 0)),
            pl.BlockSpec((tm, w3), row),
            pl.BlockSpec((HALO_ROWS, w3), lambda i: (jnp.minimum((i + 1) * per, n_halo - 1), 0)),
            pl.BlockSpec((CONV_K, w3), lambda i: (0, 0)),
        ],
        out_specs=[out_spec, out_spec, out_spec],
        out_shape=[out_sds, out_sds, out_sds],
        scratch_shapes=[pltpu.VMEM((tm + 2 * HALO_ROWS, w3), F32)],
        compiler_params=_cparams(("arbitrary",)),
        name="dn_prep",
    )(pre, pre, pre, conv_w)


def _bdot(a, b):
    return jnp.dot(a.astype(BF16), b.astype(BF16), preferred_element_type=F32)


def _split3(x):
    hi = x.astype(BF16)
    r1 = x - hi.astype(F32)
    mid = r1.astype(BF16)
    lo = (r1 - mid.astype(F32)).astype(BF16)
    return hi, mid, lo


def _dn_kernel(qf_ref, kf_ref, vf_ref, gf_ref, gtf_ref, qb_ref, kb_ref, vb_ref, gb_ref, gtb_ref,
               of_ref, ob_ref, sf_ref, sb_ref, *, n_chunks, chunks1, len1, len2):
    c = qf_ref.shape[0]
    i = pl.program_id(0)

    def seq_pos(ci):
        in_first = ci < chunks1
        seq_len = jnp.where(in_first, len1, len2)
        off = jnp.where(in_first, ci, ci - chunks1)
        return off % seq_len, seq_len

    pos_f, _ = seq_pos(i)
    pos_b, len_b = seq_pos(n_chunks - 1 - i)

    @pl.when(pos_f == 0)
    def _():
        sf_ref[...] = jnp.zeros_like(sf_ref)

    @pl.when(pos_b == len_b - 1)
    def _():
        sb_ref[...] = jnp.zeros_like(sb_ref)

    ii = lax.broadcasted_iota(jnp.int32, (c, c), 0)
    jj = lax.broadcasted_iota(jnp.int32, (c, c), 1)
    lower, lower_strict = ii >= jj, ii > jj
    upper, upper_strict = ii <= jj, ii < jj
    eye = jnp.where(ii == jj, 1.0, 0.0).astype(F32)
    tri_l = jnp.where(lower, 1.0, 0.0).astype(F32)
    tri_u = jnp.where(upper, 1.0, 0.0).astype(F32)
    level_masks = [jnp.where(ii // INV_BASE == jj // INV_BASE, 1.0, 0.0).astype(F32)]
    b = INV_BASE
    while b < c:
        same_outer = ii // (2 * b) == jj // (2 * b)
        level_masks.append(jnp.where(same_outer & (ii // b != jj // b), 1.0, 0.0).astype(F32))
        b *= 2
    ones_cc = jnp.ones((c, c), BF16)
    ones_cd = jnp.ones((c, DN_HEAD_DIM), BF16)
    nt = (((1,), (1,)), ((), ()))

    chains = []
    for forward, (q_ref, k_ref, v_ref, g_ref, gt_ref, o_ref, s_ref) in (
            (True, (qf_ref, kf_ref, vf_ref, gf_ref, gtf_ref, of_ref, sf_ref)),
            (False, (qb_ref, kb_ref, vb_ref, gb_ref, gtb_ref, ob_ref, sb_ref))):
        g, gt = g_ref[...], gt_ref[...]
        tri_c, tri_r = (tri_l, tri_u) if forward else (tri_u, tri_l)
        col_lhs = jnp.concatenate([tri_c.astype(BF16), ones_cc], axis=0)
        col = sum(jnp.dot(col_lhs, piece, preferred_element_type=F32) for piece in _split3(g))
        row_rhs = jnp.concatenate([tri_r.astype(BF16), ones_cd], axis=1)
        row = sum(jnp.dot(piece, row_rhs, preferred_element_type=F32) for piece in _split3(gt))
        cs_col, tot_col = col[:c], col[c:]
        cs_row, tot_row = row[:, :c], row[:, c:]
        keep, strict = (lower, lower_strict) if forward else (upper, upper_strict)
        d0 = 0 if forward else N_DN_HEADS
        for hd in range(N_DN_HEADS):
            gi = d0 + hd
            bi = 2 * N_DN_HEADS + d0 + hd
            chains.append(dict(
                sl=slice(hd * DN_HEAD_DIM, (hd + 1) * DN_HEAD_DIM), hd=hd, keep=keep, strict=strict,
                q_ref=q_ref, k_ref=k_ref, v_ref=v_ref, o_ref=o_ref, s_ref=s_ref,
                gc_col=cs_col[:, gi:gi + 1], gc_row=cs_row[gi:gi + 1, :], beta=g[:, bi:bi + 1],
                tot_col=tot_col[:, gi:gi + 1], tot_row=tot_row[gi:gi + 1, :]))

    for ch in chains:
        qh, kh = ch["q_ref"][:, ch["sl"]], ch["k_ref"][:, ch["sl"]]
        both = lax.dot_general(jnp.concatenate([qh, kh], axis=0), kh, nt, preferred_element_type=F32)
        decay = jnp.where(ch["keep"], jnp.exp(jnp.minimum(ch["gc_col"] - ch["gc_row"], 0.0)), 0.0)
        ch["a"] = jnp.where(ch["strict"], ch["beta"] * both[c:] * decay, 0.0)
        ch["qk"] = (both[:c] * decay).astype(BF16)
    for ch in chains:
        dg = ch["a"] * level_masks[0]
        ch["x"] = eye - dg
        ch["y"] = _bdot(dg, dg)
    for ch in chains:
        ch["x"] = ch["x"] + _bdot(ch["x"], ch["y"])
    for off_mask in level_masks[1:]:
        for ch in chains:
            ch["y"] = _bdot(ch["a"] * off_mask, ch["x"])
        for ch in chains:
            ch["x"] = ch["x"] - _bdot(ch["x"], ch["y"])
    for ch in chains:
        kh = ch["k_ref"][:, ch["sl"]].astype(F32)
        vh = ch["v_ref"][:, ch["sl"]].astype(F32)
        egc = jnp.exp(ch["gc_col"])
        rhs = jnp.concatenate([vh * ch["beta"], kh * (ch["beta"] * egc)], axis=1)
        ch["uw"] = _bdot(ch["x"], rhs)
        ch["egc"] = egc
    for ch in chains:
        qh = ch["q_ref"][:, ch["sl"]].astype(F32)
        lhs = jnp.concatenate([ch["uw"][:, DN_HEAD_DIM:], qh * ch["egc"]], axis=0)
        ch["state"] = ch["s_ref"][ch["hd"]]
        ch["ws"] = _bdot(lhs, ch["state"])
    for ch in chains:
        v_new = (ch["uw"][:, :DN_HEAD_DIM] - ch["ws"][:c]).astype(BF16)
        ch["o_ref"][:, ch["sl"]] = ch["ws"][c:] + jnp.dot(ch["qk"], v_new, preferred_element_type=F32)
        ch["v_new"] = v_new
    for ch in chains:
        kh = ch["k_ref"][:, ch["sl"]].astype(F32)
        k_dec_t = jnp.transpose(kh * jnp.exp(ch["tot_col"] - ch["gc_col"])).astype(BF16)
        ch["s_ref"][ch["hd"]] = ch["state"] * jnp.exp(ch["tot_row"]) + jnp.dot(
            k_dec_t, ch["v_new"], preferred_element_type=F32)


def _deltanet(q, k, v, gates, gates_t, t1, seq_lens):
    t, w = q.shape
    c = DN_CHUNK
    n = t // c
    fwd = lambda i: (i, 0)
    bwd = lambda i: (n - 1 - i, 0)
    fwd_t = lambda i: (0, i)
    bwd_t = lambda i: (0, n - 1 - i)

    def specs(row_map, col_map):
        return [pl.BlockSpec((c, w), row_map)] * 3 + [
            pl.BlockSpec((c, LANES), row_map), pl.BlockSpec((N_GATES, c), col_map)]

    state = pltpu.VMEM((N_DN_HEADS, DN_HEAD_DIM, DN_HEAD_DIM), F32)
    return pl.pallas_call(
        functools.partial(_dn_kernel, n_chunks=n, chunks1=t1 // c, len1=seq_lens[0] // c, len2=seq_lens[1] // c),
        grid=(n,),
        in_specs=specs(fwd, fwd_t) + specs(bwd, bwd_t),
        out_specs=[pl.BlockSpec((c, w), fwd), pl.BlockSpec((c, w), bwd)],
        out_shape=[jax.ShapeDtypeStruct((t, w), F32)] * 2,
        scratch_shapes=[state, state],
        compiler_params=_cparams(("arbitrary",)),
        name="deltanet",
    )(q, k, v, gates, gates_t, q, k, v, gates, gates_t)


def _outproj_kernel(x_ref, mod_ref, attn_ref, of_ref, ob_ref, z_ref, nw_ref, wa_ref, wd_ref, o_ref):
    o = of_ref[...] + ob_ref[...]
    z = z_ref[...].astype(F32)
    nw = nw_ref[...]
    parts = []
    for hd in range(N_DN_HEADS):
        sl = slice(hd * DN_HEAD_DIM, (hd + 1) * DN_HEAD_DIM)
        oh = o[:, sl]
        parts.append(oh * lax.rsqrt(jnp.mean(oh * oh, axis=-1, keepdims=True) + EPS) * nw * _silu(z[:, sl]))
    dn = jnp.concatenate(parts, axis=1).astype(BF16)
    y = jnp.dot(attn_ref[...], wa_ref[...], preferred_element_type=F32)
    y = y + jnp.dot(dn, wd_ref[...], preferred_element_type=F32)
    o_ref[...] = x_ref[...] + mod_ref[0, 2:3, :] * y


def _out_proj(x, mod, attn, o_f, o_b, z, dn_norm, w_attn, w_dn, blk_rows):
    t, d = x.shape
    tm = PROJ_ROWS
    per_blk = blk_rows // tm
    row = lambda i: (i, 0)
    const = lambda i: (0, 0)
    return pl.pallas_call(
        _outproj_kernel,
        grid=(t // tm,),
        in_specs=[
            pl.BlockSpec((tm, d), row),
            pl.BlockSpec((1, 3, d), lambda i: (i // per_blk, 0, 0)),
            pl.BlockSpec((tm, ATTN_WIDTH), row),
            pl.BlockSpec((tm, DN_WIDTH), row),
            pl.BlockSpec((tm, DN_WIDTH), row),
            pl.BlockSpec((tm, DN_WIDTH), row),
            pl.BlockSpec((1, DN_HEAD_DIM), const),
            pl.BlockSpec((ATTN_WIDTH, d), const),
            pl.BlockSpec((DN_WIDTH, d), const),
        ],
        out_specs=pl.BlockSpec((tm, d), row),
        out_shape=jax.ShapeDtypeStruct((t, d), F32),
        compiler_params=_cparams(("arbitrary",)),
        name="out_proj",
    )(x, mod, attn, o_f, o_b, z, dn_norm.reshape(1, DN_HEAD_DIM), w_attn, w_dn)


def kernel(x_prompt, x_sample, c_prompt, c_sample, ada_w, ada_b, norm_ffn1, ffn1_w_gate, ffn1_w_up, ffn1_w_down, norm_mix, w_in, conv_w, a_log, dt_bias, dn_norm, w_out, norm_ffn2, ffn2_w_gate, ffn2_w_up, ffn2_w_down, norm_final):
    b1, s1, d = x_prompt.shape
    b2, s2, _ = x_sample.shape
    depth = ada_w.shape[0]
    t1 = b1 * s1
    seq_lens = (s1, s2)
    assert s2 % s1 == 0 and t1 % s2 == 0, "flat layout needs nested sequence lengths"
    x = jnp.concatenate([x_prompt.reshape(t1, d), x_sample.reshape(b2 * s2, d)], axis=0)

    n_seq = b1 + b2
    c_all = jnp.concatenate([c_prompt, c_sample, jnp.zeros((-n_seq % SUBLANES, d), F32)], axis=0)
    mod = _modulation(c_all, ada_w, ada_b)
    blk_seq = jnp.concatenate([jnp.arange(b1), b1 + jnp.repeat(jnp.arange(b2), s2 // s1)])
    mod = mod[:, blk_seq].reshape(depth, blk_seq.shape[0], N_MOD, d)

    rope_tabs = _rope_tables(max(s1, s2))
    n_main = 3 * ATTN_WIDTH + 4 * DN_WIDTH
    gate_par_t = jnp.stack([a_log.astype(F32).reshape(depth, -1), dt_bias.astype(F32).reshape(depth, -1)], axis=2)
    gate_par_t = jnp.pad(gate_par_t, ((0, 0), (0, N_GATES - gate_par_t.shape[1]), (0, 0)))
    gate_par = jnp.pad(jnp.swapaxes(gate_par_t, 1, 2), ((0, 0), (0, 0), (0, LANES - N_GATES)))

    for l in range(depth):
        bf = lambda a: a.astype(BF16)
        x = _ffn(x, mod[l, :, 0:3], norm_ffn1[l], bf(ffn1_w_gate[l]), bf(ffn1_w_up[l]), bf(ffn1_w_down[l]), s1)
        w_gate_t = bf(w_in[l][:, n_main:]).T
        w_gate = jnp.pad(w_gate_t.T, ((0, 0), (0, LANES - N_GATES)))
        qkv_views, (dn_pre, z, gates, gates_t) = _in_proj(
            x, mod[l, :, 3:6], norm_mix[l], bf(w_in[l][:, :n_main]), w_gate, w_gate_t,
            gate_par[l], gate_par_t[l], rope_tabs, s1, seq_lens, t1)
        attn = _attention(qkv_views, x.shape[0], t1, seq_lens)
        dq, dk, dv = _dn_prep(dn_pre, conv_w[l], t1, seq_lens)
        o_f, o_b = _deltanet(dq, dk, dv, gates, gates_t, t1, seq_lens)
        x = _out_proj(x, mod[l, :, 3:6], attn, o_f, o_b, z, dn_norm[l], bf(w_out[l][:ATTN_WIDTH]),
                      bf(w_out[l][ATTN_WIDTH:]), s1)
        x = _ffn(x, mod[l, :, 6:9], norm_ffn2[l], bf(ffn2_w_gate[l]), bf(ffn2_w_up[l]), bf(ffn2_w_down[l]), s1,
                 final_w=norm_final if l == depth - 1 else None)
    return x[:t1].reshape(b1, s1, d), x[t1:].reshape(b2, s2, d)
```

```python
import functools

import jax
import jax.numpy as jnp
from jax import lax
from jax.experimental import pallas as pl
from jax.experimental.pallas import tpu as pltpu

F32 = jnp.float32
BF16 = jnp.bfloat16

D_MODEL = 1024
D_FF = 2816
N_MOD = 9
EPS = 1e-6
N_ATTN_HEADS = 8
HEAD_DIM = 64
ATTN_WIDTH = N_ATTN_HEADS * HEAD_DIM
ROT_HALF = HEAD_DIM // 8
ROPE_THETA = 500000.0
DILATED_PATTERNS = ((128, 1), (512, 4), (2048, 16))
N_DN_HEADS = 4
DN_HEAD_DIM = 128
DN_WIDTH = N_DN_HEADS * DN_HEAD_DIM
CONV_K = 5
N_GATES = 4 * N_DN_HEADS

LANES = 128
SUBLANES = 8
VMEM_LIMIT_BYTES = 56 * 1024 * 1024

FFN_ROWS = 256
PROJ_ROWS = 256
ATTN_TILE = {1: (128, 2, 1), 4: (128, 1, 4), 16: (128, 1, 4)}
DN_CHUNK = 128
INV_BASE = 4
PREP_ROWS = 512
MOD_COLS = 1152
HALO_ROWS = 8
NEG_BIG = -1e30


def _cparams(sem):
    return pltpu.CompilerParams(dimension_semantics=sem, vmem_limit_bytes=VMEM_LIMIT_BYTES)


def _silu(x):
    return x * jax.nn.sigmoid(x)


def _modulated_norm(x, norm_w, shift, scale):
    y = x * lax.rsqrt(jnp.mean(x * x, axis=-1, keepdims=True) + EPS) * norm_w
    return y * (1.0 + scale) + shift


def _mod_kernel(c_ref, w_ref, b_ref, o_ref):
    sc = _silu(c_ref[...]).astype(BF16)
    o_ref[0] = jnp.dot(sc, w_ref[0].astype(BF16), preferred_element_type=F32) + b_ref[0]


def _modulation(c_all, ada_w, ada_b):
    depth, d, n = ada_w.shape
    rows = c_all.shape[0]
    return pl.pallas_call(
        _mod_kernel,
        grid=(depth, n // MOD_COLS),
        in_specs=[
            pl.BlockSpec((rows, d), lambda l, j: (0, 0)),
            pl.BlockSpec((1, d, MOD_COLS), lambda l, j: (l, 0, j)),
            pl.BlockSpec((1, 1, MOD_COLS), lambda l, j: (l, 0, j)),
        ],
        out_specs=pl.BlockSpec((1, rows, MOD_COLS), lambda l, j: (l, 0, j)),
        out_shape=jax.ShapeDtypeStruct((depth, rows, n), F32),
        compiler_params=_cparams(("arbitrary", "arbitrary")),
        name="modulation",
    )(c_all, ada_w, ada_b.reshape(depth, 1, n))


def _ffn_kernel(*refs, n_in, n_out, final_norm, rows1):
    x_refs, refs = refs[:n_in], refs[n_in:]
    mod_ref, nw_ref, wg_ref, wu_ref, wd_ref = refs[:5]
    refs = refs[5:]
    if final_norm:
        fw_ref, refs = refs[0], refs[1:]
    o_refs, refs = refs[:n_out], refs[n_out:]
    tm = x_refs[0].shape[0]
    in_first = pl.program_id(0) < rows1 // tm
    if n_in == 2:
        stage_ref = refs[0]

        @pl.when(in_first)
        def _():
            stage_ref[...] = x_refs[0][...]

        @pl.when(jnp.logical_not(in_first))
        def _():
            stage_ref[...] = x_refs[1][...]

        x = stage_ref[...]
    else:
        x = x_refs[0][...]
    h = _modulated_norm(x, nw_ref[...], mod_ref[0, 0:1, :], mod_ref[0, 1:2, :]).astype(BF16)
    g = jnp.dot(h, wg_ref[...], preferred_element_type=F32)
    u = jnp.dot(h, wu_ref[...], preferred_element_type=F32)
    a = (_silu(g) * u).astype(BF16)
    y = jnp.dot(a, wd_ref[...], preferred_element_type=F32)
    out = x + (0.5 * mod_ref[0, 2:3, :]) * y
    if final_norm:
        out = out * lax.rsqrt(jnp.mean(out * out, axis=-1, keepdims=True) + EPS) * fw_ref[...]
    if n_out == 2:
        @pl.when(in_first)
        def _():
            o_refs[0][...] = out

        @pl.when(jnp.logical_not(in_first))
        def _():
            o_refs[1][...] = out
    else:
        o_refs[0][...] = out


def _ffn(xs, mod, norm_w, wg, wu, wd, blk_rows, rows1, final_w=None, split_out=False):
    d = xs[0].shape[1]
    t = sum(x.shape[0] for x in xs)
    f = wg.shape[1]
    tm = FFN_ROWS
    per_blk = blk_rows // tm
    nb1 = rows1 // tm
    const = lambda i: (0, 0)
    row = lambda i: (i, 0)
    group1 = lambda i: (jnp.minimum(i, nb1 - 1), 0)
    group2 = lambda i: (jnp.maximum(i - nb1, 0), 0)
    x_specs = [pl.BlockSpec((tm, d), row)] if len(xs) == 1 else [pl.BlockSpec((tm, d), group1), pl.BlockSpec((tm, d), group2)]
    in_specs = x_specs + [
        pl.BlockSpec((1, 3, d), lambda i: (i // per_blk, 0, 0)),
        pl.BlockSpec((1, d), const),
        pl.BlockSpec((d, f), const),
        pl.BlockSpec((d, f), const),
        pl.BlockSpec((f, d), const),
    ]
    args = list(xs) + [mod, norm_w.reshape(1, d), wg, wu, wd]
    if final_w is not None:
        in_specs.append(pl.BlockSpec((1, d), const))
        args.append(final_w.reshape(1, d))
    if split_out:
        out_specs = [pl.BlockSpec((tm, d), group1), pl.BlockSpec((tm, d), group2)]
        out_shape = [jax.ShapeDtypeStruct((rows1, d), F32), jax.ShapeDtypeStruct((t - rows1, d), F32)]
    else:
        out_specs = [pl.BlockSpec((tm, d), row)]
        out_shape = [jax.ShapeDtypeStruct((t, d), F32)]
    outs = pl.pallas_call(
        functools.partial(_ffn_kernel, n_in=len(xs), n_out=len(out_specs), final_norm=final_w is not None,
                          rows1=rows1),
        grid=(t // tm,),
        in_specs=in_specs,
        out_specs=out_specs,
        out_shape=out_shape,
        scratch_shapes=[pltpu.VMEM((tm, d), F32)] if len(xs) == 2 else [],
        compiler_params=_cparams(("arbitrary",)),
        name="ffn",
    )(*args)
    return outs if split_out else outs[0]


def _inproj_kernel(x_ref, mod_ref, nw_ref, w_ref, wg_ref, wgt_ref, gp_ref, gpt_ref,
                   cos_ref, sfw_ref, sbk_ref, *rest):
    n_views = len(DILATED_PATTERNS)
    qkv_refs = [rest[3 * n:3 * n + 3] for n in range(n_views)]
    dn_ref, z_ref, gb_ref, gbt_ref, stage_ref = rest[3 * n_views:]
    tm = x_ref.shape[0]
    h = _modulated_norm(x_ref[...], nw_ref[...], mod_ref[0, 0:1, :], mod_ref[0, 1:2, :]).astype(BF16)
    p = jnp.dot(h, w_ref[...], preferred_element_type=F32)
    cos, sfw, sbk = cos_ref[...], sfw_ref[...], sbk_ref[...]

    def rope(xp):
        return xp * cos + pltpu.roll(xp, LANES - ROT_HALF, 1) * sfw + pltpu.roll(xp, ROT_HALF, 1) * sbk

    n_tiles = ATTN_WIDTH // LANES
    for a in range(3):
        for j in range(n_tiles):
            col = p[:, a * ATTN_WIDTH + j * LANES:a * ATTN_WIDTH + (j + 1) * LANES]
            if a == 0:
                col = rope(col) * (HEAD_DIM ** -0.5)
            elif a == 1:
                col = rope(col)
            stage_ref[a * n_tiles + j] = col
    for n, (_, dil) in enumerate(DILATED_PATTERNS):
        rows = tm // dil
        for a in range(3):
            for j in range(n_tiles):
                for c in range(dil):
                    lo = c * ATTN_WIDTH + j * LANES
                    src = stage_ref[a * n_tiles + j, pl.ds(c, rows, stride=dil), :] if dil > 1 else stage_ref[a * n_tiles + j]
                    qkv_refs[n][a][:, lo:lo + LANES] = src.astype(BF16)
    o0 = 3 * ATTN_WIDTH
    dn_ref[...] = p[:, o0:o0 + 3 * DN_WIDTH]
    z_ref[...] = p[:, o0 + 3 * DN_WIDTH:o0 + 4 * DN_WIDTH].astype(BF16)

    n_g = 2 * N_DN_HEADS
    ab = jnp.dot(h, wg_ref[...], preferred_element_type=F32)
    is_g = lax.broadcasted_iota(jnp.int32, ab.shape, 1) < n_g
    gp = gp_ref[...]
    gb_ref[...] = jnp.where(is_g, -jnp.exp(gp[0:1, :]) * jax.nn.softplus(ab + gp[1:2, :]), jax.nn.sigmoid(ab))
    abt = lax.dot_general(wgt_ref[...], h, (((1,), (1,)), ((), ())), preferred_element_type=F32)
    is_gt = lax.broadcasted_iota(jnp.int32, abt.shape, 0) < n_g
    gpt = gpt_ref[...]
    gbt_ref[...] = jnp.where(is_gt, -jnp.exp(gpt[:, 0:1]) * jax.nn.softplus(abt + gpt[:, 1:2]), jax.nn.sigmoid(abt))


def _in_proj(x, mod, norm_w, w_main, w_gate, w_gate_t, gate_par, gate_par_t, rope_tabs, blk_rows, seq_lens, t1):
    t, d = x.shape
    tm = PROJ_ROWS
    per_blk = blk_rows // tm
    n_main = w_main.shape[1]
    nb1 = t1 // tm
    s1b, s2b = seq_lens[0] // tm, seq_lens[1] // tm
    const = lambda i: (0, 0)
    row = lambda i: (i, 0)
    pos = lambda i: (jnp.where(i < nb1, i % s1b, (i - nb1) % s2b), 0)
    cos, sfw, sbk = rope_tabs
    view_specs, view_shapes = [], []
    for _, dil in DILATED_PATTERNS:
        view_specs += [pl.BlockSpec((tm // dil, dil * ATTN_WIDTH), row)] * 3
        view_shapes += [jax.ShapeDtypeStruct((t // dil, dil * ATTN_WIDTH), BF16)] * 3
    outs = pl.pallas_call(
        _inproj_kernel,
        grid=(t // tm,),
        in_specs=[
            pl.BlockSpec((tm, d), row),
            pl.BlockSpec((1, 3, d), lambda i: (i // per_blk, 0, 0)),
            pl.BlockSpec((1, d), const),
            pl.BlockSpec((d, n_main), const),
            pl.BlockSpec((d, LANES), const),
            pl.BlockSpec((N_GATES, d), const),
            pl.BlockSpec((2, LANES), const),
            pl.BlockSpec((N_GATES, 2), const),
            pl.BlockSpec((tm, LANES), pos),
            pl.BlockSpec((tm, LANES), pos),
            pl.BlockSpec((tm, LANES), pos),
        ],
        out_specs=view_specs + [
            pl.BlockSpec((tm, 3 * DN_WIDTH), row),
            pl.BlockSpec((tm, DN_WIDTH), row),
            pl.BlockSpec((tm, LANES), row),
            pl.BlockSpec((N_GATES, tm), lambda i: (0, i)),
        ],
        out_shape=view_shapes + [
            jax.ShapeDtypeStruct((t, 3 * DN_WIDTH), F32),
            jax.ShapeDtypeStruct((t, DN_WIDTH), BF16),
            jax.ShapeDtypeStruct((t, LANES), F32),
            jax.ShapeDtypeStruct((N_GATES, t), F32),
        ],
        scratch_shapes=[pltpu.VMEM((3 * ATTN_WIDTH // LANES, tm, LANES), F32)],
        compiler_params=_cparams(("arbitrary",)),
        name="in_proj",
    )(x, mod, norm_w.reshape(1, d), w_main, w_gate, w_gate_t, gate_par, gate_par_t, cos, sfw, sbk)
    n_qkv = 3 * len(DILATED_PATTERNS)
    qkv_views = [outs[3 * n:3 * n + 3] for n in range(len(DILATED_PATTERNS))]
    return qkv_views, outs[n_qkv:]


def _rope_tables(max_len):
    half = ROT_HALF
    inv = ROPE_THETA ** (-jnp.arange(half, dtype=F32) / half)
    ang = jnp.arange(max_len, dtype=F32)[:, None] * inv[None, :]
    cos, sin = jnp.cos(ang), jnp.sin(ang)
    ones = jnp.ones((max_len, HEAD_DIM - 2 * half), F32)
    zeros_h = jnp.zeros((max_len, half), F32)
    zeros_r = jnp.zeros((max_len, HEAD_DIM - 2 * half), F32)
    cos_h = jnp.concatenate([cos, cos, ones], axis=1)
    sfw_h = jnp.concatenate([-sin, zeros_h, zeros_r], axis=1)
    sbk_h = jnp.concatenate([zeros_h, sin, zeros_r], axis=1)
    rep = LANES // HEAD_DIM
    return tuple(jnp.tile(a, (1, rep)) for a in (cos_h, sfw_h, sbk_h))


def _attn_kernel(q_ref, kp_ref, km_ref, kn_ref, vp_ref, vm_ref, vn_ref, *rest,
                 dil, next_dil, first, radius, tq, rows1, len1, len2):
    last = next_dil is None
    if not first:
        accp_ref, mlp_ref = rest[:2]
        rest = rest[2:]
    if last:
        o_ref, acc_scr, nat_scr = rest
    else:
        acc_out_ref, ml_out_ref, acc_scr, ml_scr = rest
    tb = q_ref.shape[0]
    n_sub = tb // tq
    tk = tq + 2 * radius
    n_tiles = ATTN_WIDTH // LANES
    n_res = q_ref.shape[1] // ATTN_WIDTH
    c0 = pl.program_id(1) * n_res
    r0 = pl.program_id(0) * tb
    in_first = r0 < rows1
    seq_len = jnp.where(in_first, len1, len2)
    off = jnp.where(in_first, r0, r0 - rows1)
    first_key = jnp.where((off % seq_len) == 0, radius, 0)
    end_key = jnp.where(((off + tb) % seq_len) == 0, tq + radius, tk)

    qi = lax.broadcasted_iota(jnp.int32, (tq, tk), 0)
    kj = lax.broadcasted_iota(jnp.int32, (tq, tk), 1)
    in_band = jnp.abs(kj - radius - qi) <= radius
    biases = []
    for sub in range(n_sub):
        valid = in_band
        if sub == 0:
            valid = valid & (kj >= first_key)
        if sub == n_sub - 1:
            valid = valid & (kj < end_key)
        bias = jnp.where(valid, 0.0, NEG_BIG).astype(F32)
        biases.append(jnp.concatenate([bias, bias], axis=0))

    lane = lax.broadcasted_iota(jnp.int32, (1, LANES), 1)
    low_half = lane < HEAD_DIM
    nt = (((1,), (1,)), ((), ()))

    def pair_col(ref, rows, col):
        return jnp.concatenate([ref[rows, col:col + 1], ref[rows, col + 1:col + 2]], axis=0)

    pairs = []
    for r in range(n_res):
        for j in range(n_tiles):
            sl = slice(r * ATTN_WIDTH + j * LANES, r * ATTN_WIDTH + (j + 1) * LANES)
            k_all = jnp.concatenate([kp_ref[:, sl], km_ref[:, sl], kn_ref[:, sl]], axis=0)
            v_all = jnp.concatenate([vp_ref[:, sl], vm_ref[:, sl], vn_ref[:, sl]], axis=0)
            for sub in range(n_sub):
                rows = slice(sub * tq, (sub + 1) * tq)
                q2 = q_ref[rows, sl]
                zero = jnp.zeros_like(q2)
                qs = jnp.concatenate([jnp.where(low_half, q2, zero), jnp.where(low_half, zero, q2)], axis=0)
                kw = k_all[sub * tq:sub * tq + tk]
                pairs.append(dict(r=r, j=j, sub=sub, sl=sl, rows=rows, vw=v_all[sub * tq:sub * tq + tk],
                                  stats=r * LANES + 2 * j,
                                  s=lax.dot_general(qs, kw, nt, preferred_element_type=F32) + biases[sub]))
    for pr in pairs:
        m_cur = jnp.max(pr["s"], axis=-1, keepdims=True)
        if first:
            pr["m"] = m_cur
        else:
            m_old = pair_col(mlp_ref, pr["rows"], pr["stats"])
            pr["m"] = jnp.maximum(m_old, m_cur)
            pr["alpha"] = jnp.exp(m_old - pr["m"])
    ones = jnp.ones((tk, LANES), BF16)
    for pr in pairs:
        pr["p"] = jnp.exp(pr["s"] - pr["m"]).astype(BF16)
    for pr in pairs:
        pr["pv"] = jnp.dot(pr["p"], pr["vw"], preferred_element_type=F32)
        pr["l"] = jnp.dot(pr["p"], ones, preferred_element_type=F32)
        if not first:
            pr["l"] = pr["alpha"] * pair_col(mlp_ref, pr["rows"], pr["stats"] + N_ATTN_HEADS) + pr["l"]
    for pr in pairs:
        acc = jnp.where(low_half, pr["pv"][:tq], pr["pv"][tq:])
        if not first:
            acc = accp_ref[pr["rows"], pr["sl"]] * jnp.where(low_half, pr["alpha"][:tq], pr["alpha"][tq:]) + acc
        if last:
            acc = acc / jnp.where(low_half, pr["l"][:tq], pr["l"][tq:])
        acc_scr[c0 + pr["r"], pr["j"], pr["rows"], :] = acc
    if not last:
        for r in range(n_res):
            for sub in range(n_sub):
                ml = jnp.zeros((tq, LANES), F32)
                for pr in pairs:
                    if pr["r"] != r or pr["sub"] != sub:
                        continue
                    for half, rows in enumerate((slice(0, tq), slice(tq, 2 * tq))):
                        h = 2 * pr["j"] + half
                        ml = jnp.where(lane == h, pr["m"][rows], ml)
                        ml = jnp.where(lane == N_ATTN_HEADS + h, pr["l"][rows], ml)
                ml_scr[c0 + r, sub * tq:(sub + 1) * tq, :] = ml

    @pl.when(c0 + n_res == dil)
    def _():
        if last:
            for cc in range(dil):
                for j in range(n_tiles):
                    nat_scr[j, pl.ds(cc, tb, stride=dil), :] = acc_scr[cc, j]
            for j in range(n_tiles):
                o_ref[:, j * LANES:(j + 1) * LANES] = nat_scr[j].astype(o_ref.dtype)
        else:
            ratio = next_dil // dil
            rows_out = tb // ratio
            for cc in range(dil):
                for m in range(ratio):
                    cb = m * dil + cc
                    rows = pl.ds(m, rows_out, stride=ratio)
                    for j in range(n_tiles):
                        lo = cb * ATTN_WIDTH + j * LANES
                        acc_out_ref[:, lo:lo + LANES] = acc_scr[cc, j, rows, :]
                    ml_out_ref[:, cb * LANES:(cb + 1) * LANES] = ml_scr[cc, rows, :]


def _attn_pattern(qkv, prev, dil, next_dil, radius, t, t1, seq_lens):
    q, k, v = qkv
    w = ATTN_WIDTH
    first, last = prev is None, next_dil is None
    rows = t // dil
    tq, n_sub, n_res = ATTN_TILE[dil]
    tb = tq * n_sub
    per_q = tb // radius
    n_halo = rows // radius
    n_tiles = w // LANES
    wr = n_res * w
    main = lambda i, c: (i, c)
    before = lambda i, c: (jnp.maximum(i * per_q - 1, 0), c)
    after = lambda i, c: (jnp.minimum((i + 1) * per_q, n_halo - 1), c)
    whole = lambda i, c: (i, 0)
    kv_specs = [pl.BlockSpec((radius, wr), before), pl.BlockSpec((tb, wr), main), pl.BlockSpec((radius, wr), after)]
    in_specs = [pl.BlockSpec((tb, wr), main)] + kv_specs + kv_specs
    args = [q, k, k, k, v, v, v]
    if not first:
        in_specs += [pl.BlockSpec((tb, wr), main), pl.BlockSpec((tb, n_res * LANES), main)]
        args += list(prev)
    scratch = [pltpu.VMEM((dil, n_tiles, tb, LANES), F32)]
    if last:
        out_specs = [pl.BlockSpec((tb * dil, w), whole)]
        out_shape = [jax.ShapeDtypeStruct((t, w), BF16)]
        scratch.append(pltpu.VMEM((n_tiles, tb * dil, LANES), F32))
    else:
        rows_out = tb * dil // next_dil
        out_specs = [pl.BlockSpec((rows_out, next_dil * w), whole), pl.BlockSpec((rows_out, next_dil * LANES), whole)]
        out_shape = [jax.ShapeDtypeStruct((t // next_dil, next_dil * w), F32),
                     jax.ShapeDtypeStruct((t // next_dil, next_dil * LANES), F32)]
        scratch.append(pltpu.VMEM((dil, tb, LANES), F32))
    return pl.pallas_call(
        functools.partial(_attn_kernel, dil=dil, next_dil=next_dil, first=first, radius=radius, tq=tq,
                          rows1=t1 // dil, len1=seq_lens[0] // dil, len2=seq_lens[1] // dil),
        grid=(rows // tb, dil // n_res),
        in_specs=in_specs,
        out_specs=out_specs,
        out_shape=out_shape,
        scratch_shapes=scratch,
        compiler_params=_cparams(("arbitrary", "arbitrary")),
        name=f"attn_d{dil}",
    )(*args)


def _attention(qkv_views, t, t1, seq_lens):
    state = None
    n = len(DILATED_PATTERNS)
    for idx, (window, dil) in enumerate(DILATED_PATTERNS):
        next_dil = DILATED_PATTERNS[idx + 1][1] if idx + 1 < n else None
        state = _attn_pattern(qkv_views[idx], state, dil, next_dil, window // (2 * dil), t, t1, seq_lens)
    return state[0]


def _dnprep_kernel(xp_ref, xm_ref, xn_ref, w_ref, q_ref, k_ref, v_ref, buf_ref, *, rows1, len1, len2):
    tm = xm_ref.shape[0]
    r0 = pl.program_id(0) * tm
    in_first = r0 < rows1
    seq_len = jnp.where(in_first, len1, len2)
    off = jnp.where(in_first, r0, r0 - rows1)
    keep_prev = jnp.where((off % seq_len) == 0, 0.0, 1.0).astype(F32)
    keep_next = jnp.where(((off + tm) % seq_len) == 0, 0.0, 1.0).astype(F32)
    buf_ref[0:HALO_ROWS, :] = xp_ref[...] * keep_prev
    buf_ref[HALO_ROWS:HALO_ROWS + tm, :] = xm_ref[...]
    buf_ref[HALO_ROWS + tm:, :] = xn_ref[...] * keep_next
    pad = CONV_K // 2
    y = buf_ref[HALO_ROWS - pad:HALO_ROWS - pad + tm, :] * w_ref[0:1, :]
    for j in range(1, CONV_K):
        y = y + buf_ref[HALO_ROWS - pad + j:HALO_ROWS - pad + j + tm, :] * w_ref[j:j + 1, :]
    y = _silu(y)
    for hd in range(N_DN_HEADS):
        sl = slice(hd * DN_HEAD_DIM, (hd + 1) * DN_HEAD_DIM)
        qh = y[:, sl]
        kh = y[:, DN_WIDTH + hd * DN_HEAD_DIM:DN_WIDTH + (hd + 1) * DN_HEAD_DIM]
        qn = qh * lax.rsqrt(jnp.sum(qh * qh, axis=-1, keepdims=True) + EPS) * (DN_HEAD_DIM ** -0.5)
        kn = kh * lax.rsqrt(jnp.sum(kh * kh, axis=-1, keepdims=True) + EPS)
        q_ref[:, sl] = qn.astype(BF16)
        k_ref[:, sl] = kn.astype(BF16)
    v_ref[...] = y[:, 2 * DN_WIDTH:].astype(BF16)


def _dn_prep(pre, conv_w, t1, seq_lens):
    t, w3 = pre.shape
    tm = PREP_ROWS
    per = tm // HALO_ROWS
    n_halo = t // HALO_ROWS
    row = lambda i: (i, 0)
    out_spec = pl.BlockSpec((tm, DN_WIDTH), row)
    out_sds = jax.ShapeDtypeStruct((t, DN_WIDTH), BF16)
    return pl.pallas_call(
        functools.partial(_dnprep_kernel, rows1=t1, len1=seq_lens[0], len2=seq_lens[1]),
        grid=(t // tm,),
        in_specs=[
            pl.BlockSpec((HALO_ROWS, w3), lambda i: (jnp.maximum(i * per - 1, 0), 0)),
            pl.BlockSpec((tm, w3), row),
            pl.BlockSpec((HALO_ROWS, w3), lambda i: (jnp.minimum((i + 1) * per, n_halo - 1), 0)),
            pl.BlockSpec((CONV_K, w3), lambda i: (0, 0)),
        ],
        out_specs=[out_spec, out_spec, out_spec],
        out_shape=[out_sds, out_sds, out_sds],
        scratch_shapes=[pltpu.VMEM((tm + 2 * HALO_ROWS, w3), F32)],
        compiler_params=_cparams(("arbitrary",)),
        name="dn_prep",
    )(pre, pre, pre, conv_w)


def _bdot(a, b):
    return jnp.dot(a.astype(BF16), b.astype(BF16), preferred_element_type=F32)


def _split3(x):
    hi = x.astype(BF16)
    r1 = x - hi.astype(F32)
    mid = r1.astype(BF16)
    lo = (r1 - mid.astype(F32)).astype(BF16)
    return hi, mid, lo


def _dn_kernel(qf_ref, kf_ref, vf_ref, gf_ref, gtf_ref, qb_ref, kb_ref, vb_ref, gb_ref, gtb_ref,
               of_ref, ob_ref, sf_ref, sb_ref, *, n_chunks, chunks1, len1, len2):
    c = qf_ref.shape[0]
    i = pl.program_id(0)

    def seq_pos(ci):
        in_first = ci < chunks1
        seq_len = jnp.where(in_first, len1, len2)
        off = jnp.where(in_first, ci, ci - chunks1)
        return off % seq_len, seq_len

    pos_f, _ = seq_pos(i)
    pos_b, len_b = seq_pos(n_chunks - 1 - i)

    @pl.when(pos_f == 0)
    def _():
        sf_ref[...] = jnp.zeros_like(sf_ref)

    @pl.when(pos_b == len_b - 1)
    def _():
        sb_ref[...] = jnp.zeros_like(sb_ref)

    ii = lax.broadcasted_iota(jnp.int32, (c, c), 0)
    jj = lax.broadcasted_iota(jnp.int32, (c, c), 1)
    lower, lower_strict = ii >= jj, ii > jj
    upper, upper_strict = ii <= jj, ii < jj
    eye = jnp.where(ii == jj, 1.0, 0.0).astype(F32)
    tri_l = jnp.where(lower, 1.0, 0.0).astype(F32)
    tri_u = jnp.where(upper, 1.0, 0.0).astype(F32)
    level_masks = [jnp.where(ii // INV_BASE == jj // INV_BASE, 1.0, 0.0).astype(F32)]
    b = INV_BASE
    while b < c:
        same_outer = ii // (2 * b) == jj // (2 * b)
        level_masks.append(jnp.where(same_outer & (ii // b != jj // b), 1.0, 0.0).astype(F32))
        b *= 2
    ones_cc = jnp.ones((c, c), BF16)
    ones_cd = jnp.ones((c, DN_HEAD_DIM), BF16)
    nt = (((1,), (1,)), ((), ()))

    chains = []
    for forward, (q_ref, k_ref, v_ref, g_ref, gt_ref, o_ref, s_ref) in (
            (True, (qf_ref, kf_ref, vf_ref, gf_ref, gtf_ref, of_ref, sf_ref)),
            (False, (qb_ref, kb_ref, vb_ref, gb_ref, gtb_ref, ob_ref, sb_ref))):
        g, gt = g_ref[...], gt_ref[...]
        tri_c, tri_r = (tri_l, tri_u) if forward else (tri_u, tri_l)
        col_lhs = jnp.concatenate([tri_c.astype(BF16), ones_cc], axis=0)
        col = sum(jnp.dot(col_lhs, piece, preferred_element_type=F32) for piece in _split3(g))
        row_rhs = jnp.concatenate([tri_r.astype(BF16), ones_cd], axis=1)
        row = sum(jnp.dot(piece, row_rhs, preferred_element_type=F32) for piece in _split3(gt))
        cs_col, tot_col = col[:c], col[c:]
        cs_row, tot_row = row[:, :c], row[:, c:]
        keep, strict = (lower, lower_strict) if forward else (upper, upper_strict)
        d0 = 0 if forward else N_DN_HEADS
        for hd in range(N_DN_HEADS):
            gi = d0 + hd
            bi = 2 * N_DN_HEADS + d0 + hd
            chains.append(dict(
                sl=slice(hd * DN_HEAD_DIM, (hd + 1) * DN_HEAD_DIM), hd=hd, keep=keep, strict=strict,
                q_ref=q_ref, k_ref=k_ref, v_ref=v_ref, o_ref=o_ref, s_ref=s_ref,
                gc_col=cs_col[:, gi:gi + 1], gc_row=cs_row[gi:gi + 1, :], beta=g[:, bi:bi + 1],
                tot_col=tot_col[:, gi:gi + 1], tot_row=tot_row[gi:gi + 1, :]))

    for ch in chains:
        qh, kh = ch["q_ref"][:, ch["sl"]], ch["k_ref"][:, ch["sl"]]
        both = lax.dot_general(jnp.concatenate([qh, kh], axis=0), kh, nt, preferred_element_type=F32)
        decay = jnp.where(ch["keep"], jnp.exp(jnp.minimum(ch["gc_col"] - ch["gc_row"], 0.0)), 0.0)
        ch["a"] = jnp.where(ch["strict"], ch["beta"] * both[c:] * decay, 0.0)
        ch["qk"] = (both[:c] * decay).astype(BF16)
    for ch in chains:
        dg = ch["a"] * level_masks[0]
        ch["x"] = eye - dg
        ch["y"] = _bdot(dg, dg)
    for ch in chains:
        ch["x"] = ch["x"] + _bdot(ch["x"], ch["y"])
    for off_mask in level_masks[1:]:
        for ch in chains:
            ch["y"] = _bdot(ch["a"] * off_mask, ch["x"])
        for ch in chains:
            ch["x"] = ch["x"] - _bdot(ch["x"], ch["y"])
    for ch in chains:
        kh = ch["k_ref"][:, ch["sl"]].astype(F32)
        vh = ch["v_ref"][:, ch["sl"]].astype(F32)
        egc = jnp.exp(ch["gc_col"])
        rhs = jnp.concatenate([vh * ch["beta"], kh * (ch["beta"] * egc)], axis=1)
        ch["uw"] = _bdot(ch["x"], rhs)
        ch["egc"] = egc
    for ch in chains:
        qh = ch["q_ref"][:, ch["sl"]].astype(F32)
        lhs = jnp.concatenate([ch["uw"][:, DN_HEAD_DIM:], qh * ch["egc"]], axis=0)
        ch["state"] = ch["s_ref"][ch["hd"]]
        ch["ws"] = _bdot(lhs, ch["state"])
    for ch in chains:
        v_new = (ch["uw"][:, :DN_HEAD_DIM] - ch["ws"][:c]).astype(BF16)
        ch["o_ref"][:, ch["sl"]] = ch["ws"][c:] + jnp.dot(ch["qk"], v_new, preferred_element_type=F32)
        ch["v_new"] = v_new
    for ch in chains:
        kh = ch["k_ref"][:, ch["sl"]].astype(F32)
        k_dec_t = jnp.transpose(kh * jnp.exp(ch["tot_col"] - ch["gc_col"])).astype(BF16)
        ch["s_ref"][ch["hd"]] = ch["state"] * jnp.exp(ch["tot_row"]) + jnp.dot(
            k_dec_t, ch["v_new"], preferred_element_type=F32)


def _deltanet(q, k, v, gates, gates_t, t1, seq_lens):
    t, w = q.shape
    c = DN_CHUNK
    n = t // c
    fwd = lambda i: (i, 0)
    bwd = lambda i: (n - 1 - i, 0)
    fwd_t = lambda i: (0, i)
    bwd_t = lambda i: (0, n - 1 - i)

    def specs(row_map, col_map):
        return [pl.BlockSpec((c, w), row_map)] * 3 + [
            pl.BlockSpec((c, LANES), row_map), pl.BlockSpec((N_GATES, c), col_map)]

    state = pltpu.VMEM((N_DN_HEADS, DN_HEAD_DIM, DN_HEAD_DIM), F32)
    return pl.pallas_call(
        functools.partial(_dn_kernel, n_chunks=n, chunks1=t1 // c, len1=seq_lens[0] // c, len2=seq_lens[1] // c),
        grid=(n,),
        in_specs=specs(fwd, fwd_t) + specs(bwd, bwd_t),
        out_specs=[pl.BlockSpec((c, w), fwd), pl.BlockSpec((c, w), bwd)],
        out_shape=[jax.ShapeDtypeStruct((t, w), F32)] * 2,
        scratch_shapes=[state, state],
        compiler_params=_cparams(("arbitrary",)),
        name="deltanet",
    )(q, k, v, gates, gates_t, q, k, v, gates, gates_t)


def _outproj_kernel(x_ref, mod_ref, attn_ref, of_ref, ob_ref, z_ref, nw_ref, wa_ref, wd_ref, o_ref):
    o = of_ref[...] + ob_ref[...]
    z = z_ref[...].astype(F32)
    nw = nw_ref[...]
    parts = []
    for hd in range(N_DN_HEADS):
        sl = slice(hd * DN_HEAD_DIM, (hd + 1) * DN_HEAD_DIM)
        oh = o[:, sl]
        parts.append(oh * lax.rsqrt(jnp.mean(oh * oh, axis=-1, keepdims=True) + EPS) * nw * _silu(z[:, sl]))
    dn = jnp.concatenate(parts, axis=1).astype(BF16)
    y = jnp.dot(attn_ref[...], wa_ref[...], preferred_element_type=F32)
    y = y + jnp.dot(dn, wd_ref[...], preferred_element_type=F32)
    o_ref[...] = x_ref[...] + mod_ref[0, 2:3, :] * y


def _out_proj(x, mod, attn, o_f, o_b, z, dn_norm, w_attn, w_dn, blk_rows):
    t, d = x.shape
    tm = PROJ_ROWS
    per_blk = blk_rows // tm
    row = lambda i: (i, 0)
    const = lambda i: (0, 0)
    return pl.pallas_call(
        _outproj_kernel,
        grid=(t // tm,),
        in_specs=[
            pl.BlockSpec((tm, d), row),
            pl.BlockSpec((1, 3, d), lambda i: (i // per_blk, 0, 0)),
            pl.BlockSpec((tm, ATTN_WIDTH), row),
            pl.BlockSpec((tm, DN_WIDTH), row),
            pl.BlockSpec((tm, DN_WIDTH), row),
            pl.BlockSpec((tm, DN_WIDTH), row),
            pl.BlockSpec((1, DN_HEAD_DIM), const),
            pl.BlockSpec((ATTN_WIDTH, d), const),
            pl.BlockSpec((DN_WIDTH, d), const),
        ],
        out_specs=pl.BlockSpec((tm, d), row),
        out_shape=jax.ShapeDtypeStruct((t, d), F32),
        compiler_params=_cparams(("arbitrary",)),
        name="out_proj",
    )(x, mod, attn, o_f, o_b, z, dn_norm.reshape(1, DN_HEAD_DIM), w_attn, w_dn)


def kernel(x_prompt, x_sample, c_prompt, c_sample, ada_w, ada_b, norm_ffn1, ffn1_w_gate, ffn1_w_up, ffn1_w_down, norm_mix, w_in, conv_w, a_log, dt_bias, dn_norm, w_out, norm_ffn2, ffn2_w_gate, ffn2_w_up, ffn2_w_down, norm_final):
    b1, s1, d = x_prompt.shape
    b2, s2, _ = x_sample.shape
    depth = ada_w.shape[0]
    t1 = b1 * s1
    seq_lens = (s1, s2)
    assert s2 % s1 == 0 and t1 % s2 == 0, "flat layout needs nested sequence lengths"
    t = t1 + b2 * s2
    xs = [x_prompt.reshape(t1, d), x_sample.reshape(b2 * s2, d)]

    n_seq = b1 + b2
    c_all = jnp.concatenate([c_prompt, c_sample, jnp.zeros((-n_seq % SUBLANES, d), F32)], axis=0)
    mod = _modulation(c_all, ada_w, ada_b)
    blk_seq = jnp.concatenate([jnp.arange(b1), b1 + jnp.repeat(jnp.arange(b2), s2 // s1)])
    mod = mod[:, blk_seq].reshape(depth, blk_seq.shape[0], N_MOD, d)

    rope_tabs = _rope_tables(max(s1, s2))
    n_main = 3 * ATTN_WIDTH + 4 * DN_WIDTH
    gate_par_t = jnp.stack([a_log.astype(F32).reshape(depth, -1), dt_bias.astype(F32).reshape(depth, -1)], axis=2)
    gate_par_t = jnp.pad(gate_par_t, ((0, 0), (0, N_GATES - gate_par_t.shape[1]), (0, 0)))
    gate_par = jnp.pad(jnp.swapaxes(gate_par_t, 1, 2), ((0, 0), (0, 0), (0, LANES - N_GATES)))

    for l in range(depth):
        bf = lambda a: a.astype(BF16)
        x = _ffn(xs if l == 0 else [x], mod[l, :, 0:3], norm_ffn1[l], bf(ffn1_w_gate[l]), bf(ffn1_w_up[l]),
                 bf(ffn1_w_down[l]), s1, t1)
        w_gate_t = bf(w_in[l][:, n_main:]).T
        w_gate = jnp.pad(w_gate_t.T, ((0, 0), (0, LANES - N_GATES)))
        qkv_views, (dn_pre, z, gates, gates_t) = _in_proj(
            x, mod[l, :, 3:6], norm_mix[l], bf(w_in[l][:, :n_main]), w_gate, w_gate_t,
            gate_par[l], gate_par_t[l], rope_tabs, s1, seq_lens, t1)
        attn = _attention(qkv_views, t, t1, seq_lens)
        dq, dk, dv = _dn_prep(dn_pre, conv_w[l], t1, seq_lens)
        o_f, o_b = _deltanet(dq, dk, dv, gates, gates_t, t1, seq_lens)
        x = _out_proj(x, mod[l, :, 3:6], attn, o_f, o_b, z, dn_norm[l], bf(w_out[l][:ATTN_WIDTH]),
                      bf(w_out[l][ATTN_WIDTH:]), s1)
        is_last = l == depth - 1
        x = _ffn([x], mod[l, :, 6:9], norm_ffn2[l], bf(ffn2_w_gate[l]), bf(ffn2_w_up[l]), bf(ffn2_w_down[l]), s1,
                 t1, final_w=norm_final if is_last else None, split_out=is_last)
    return x[0].reshape(b1, s1, d), x[1].reshape(b2, s2, d)
```

```python
import functools

import jax
import jax.numpy as jnp
from jax import lax
from jax.experimental import pallas as pl
from jax.experimental.pallas import tpu as pltpu

F32 = jnp.float32
BF16 = jnp.bfloat16

D_MODEL = 1024
D_FF = 2816
N_MOD = 9
EPS = 1e-6
N_ATTN_HEADS = 8
HEAD_DIM = 64
ATTN_WIDTH = N_ATTN_HEADS * HEAD_DIM
ROT_HALF = HEAD_DIM // 8
ROPE_THETA = 500000.0
DILATED_PATTERNS = ((128, 1), (512, 4), (2048, 16))
N_DN_HEADS = 4
DN_HEAD_DIM = 128
DN_WIDTH = N_DN_HEADS * DN_HEAD_DIM
CONV_K = 5
N_GATES = 4 * N_DN_HEADS

LANES = 128
SUBLANES = 8
VMEM_LIMIT_BYTES = 56 * 1024 * 1024

FFN_ROWS = 512
PROJ_ROWS = 512
ATTN_TILE = {1: (128, 2, 1), 4: (128, 1, 4), 16: (128, 1, 4)}
DN_CHUNK = 128
DN_CHUNKS_PER_STEP = 2
INV_BASE = 4
PREP_ROWS = 512
MOD_COLS = 1152
HALO_ROWS = 8
NEG_BIG = -1e30


def _cparams(sem):
    return pltpu.CompilerParams(dimension_semantics=sem, vmem_limit_bytes=VMEM_LIMIT_BYTES)


def _resident(shape):
    return pl.BlockSpec(shape, lambda *_: (0, 0), pipeline_mode=pl.Buffered(1))


def _silu(x):
    return x * jax.nn.sigmoid(x)


def _modulated_norm(x, norm_w, shift, scale):
    y = x * lax.rsqrt(jnp.mean(x * x, axis=-1, keepdims=True) + EPS) * norm_w
    return y * (1.0 + scale) + shift


def _mod_kernel(c_ref, w_ref, b_ref, o_ref):
    sc = _silu(c_ref[...]).astype(BF16)
    o_ref[0] = jnp.dot(sc, w_ref[0].astype(BF16), preferred_element_type=F32) + b_ref[0]


def _modulation(c_all, ada_w, ada_b):
    depth, d, n = ada_w.shape
    rows = c_all.shape[0]
    return pl.pallas_call(
        _mod_kernel,
        grid=(depth, n // MOD_COLS),
        in_specs=[
            pl.BlockSpec((rows, d), lambda l, j: (0, 0)),
            pl.BlockSpec((1, d, MOD_COLS), lambda l, j: (l, 0, j)),
            pl.BlockSpec((1, 1, MOD_COLS), lambda l, j: (l, 0, j)),
        ],
        out_specs=pl.BlockSpec((1, rows, MOD_COLS), lambda l, j: (l, 0, j)),
        out_shape=jax.ShapeDtypeStruct((depth, rows, n), F32),
        compiler_params=_cparams(("arbitrary", "arbitrary")),
        name="modulation",
    )(c_all, ada_w, ada_b.reshape(depth, 1, n))


def _ffn_kernel(*refs, n_in, n_out, final_norm, rows1):
    x_refs, refs = refs[:n_in], refs[n_in:]
    mod_ref, nw_ref, wg_ref, wu_ref, wd_ref = refs[:5]
    refs = refs[5:]
    if final_norm:
        fw_ref, refs = refs[0], refs[1:]
    o_refs, refs = refs[:n_out], refs[n_out:]
    tm = x_refs[0].shape[0]
    in_first = pl.program_id(0) < rows1 // tm
    if n_in == 2:
        stage_ref = refs[0]

        @pl.when(in_first)
        def _():
            stage_ref[...] = x_refs[0][...]

        @pl.when(jnp.logical_not(in_first))
        def _():
            stage_ref[...] = x_refs[1][...]

        x = stage_ref[...]
    else:
        x = x_refs[0][...]
    h = _modulated_norm(x, nw_ref[...], mod_ref[0, 0:1, :], mod_ref[0, 1:2, :]).astype(BF16)
    g = jnp.dot(h, wg_ref[...], preferred_element_type=F32)
    u = jnp.dot(h, wu_ref[...], preferred_element_type=F32)
    a = (_silu(g) * u).astype(BF16)
    y = jnp.dot(a, wd_ref[...], preferred_element_type=F32)
    out = x + (0.5 * mod_ref[0, 2:3, :]) * y
    if final_norm:
        out = out * lax.rsqrt(jnp.mean(out * out, axis=-1, keepdims=True) + EPS) * fw_ref[...]
    if n_out == 2:
        @pl.when(in_first)
        def _():
            o_refs[0][...] = out

        @pl.when(jnp.logical_not(in_first))
        def _():
            o_refs[1][...] = out
    else:
        o_refs[0][...] = out


def _ffn(xs, mod, norm_w, wg, wu, wd, blk_rows, rows1, final_w=None, split_out=False):
    d = xs[0].shape[1]
    t = sum(x.shape[0] for x in xs)
    f = wg.shape[1]
    tm = FFN_ROWS
    per_blk = blk_rows // tm
    nb1 = rows1 // tm
    const = lambda i: (0, 0)
    row = lambda i: (i, 0)
    group1 = lambda i: (jnp.minimum(i, nb1 - 1), 0)
    group2 = lambda i: (jnp.maximum(i - nb1, 0), 0)
    x_specs = [pl.BlockSpec((tm, d), row)] if len(xs) == 1 else [pl.BlockSpec((tm, d), group1), pl.BlockSpec((tm, d), group2)]
    in_specs = x_specs + [
        pl.BlockSpec((1, 3, d), lambda i: (i // per_blk, 0, 0)),
        pl.BlockSpec((1, d), const),
        _resident((d, f)),
        _resident((d, f)),
        _resident((f, d)),
    ]
    args = list(xs) + [mod, norm_w.reshape(1, d), wg, wu, wd]
    if final_w is not None:
        in_specs.append(pl.BlockSpec((1, d), const))
        args.append(final_w.reshape(1, d))
    if split_out:
        out_specs = [pl.BlockSpec((tm, d), group1), pl.BlockSpec((tm, d), group2)]
        out_shape = [jax.ShapeDtypeStruct((rows1, d), F32), jax.ShapeDtypeStruct((t - rows1, d), F32)]
    else:
        out_specs = [pl.BlockSpec((tm, d), row)]
        out_shape = [jax.ShapeDtypeStruct((t, d), F32)]
    outs = pl.pallas_call(
        functools.partial(_ffn_kernel, n_in=len(xs), n_out=len(out_specs), final_norm=final_w is not None,
                          rows1=rows1),
        grid=(t // tm,),
        in_specs=in_specs,
        out_specs=out_specs,
        out_shape=out_shape,
        scratch_shapes=[pltpu.VMEM((tm, d), F32)] if len(xs) == 2 else [],
        compiler_params=_cparams(("arbitrary",)),
        name="ffn",
    )(*args)
    return outs if split_out else outs[0]


def _inproj_kernel(x_ref, mod_ref, nw_ref, w_ref, wg_ref, wgt_ref, gp_ref, gpt_ref,
                   cos_ref, sfw_ref, sbk_ref, *rest):
    n_views = len(DILATED_PATTERNS)
    qkv_refs = [rest[3 * n:3 * n + 3] for n in range(n_views)]
    dn_ref, z_ref, gb_ref, gbt_ref, stage_ref = rest[3 * n_views:]
    tm = x_ref.shape[0]
    h = _modulated_norm(x_ref[...], nw_ref[...], mod_ref[0, 0:1, :], mod_ref[0, 1:2, :]).astype(BF16)
    p = jnp.dot(h, w_ref[...], preferred_element_type=F32)
    cos, sfw, sbk = cos_ref[...], sfw_ref[...], sbk_ref[...]

    def rope(xp):
        return xp * cos + pltpu.roll(xp, LANES - ROT_HALF, 1) * sfw + pltpu.roll(xp, ROT_HALF, 1) * sbk

    n_tiles = ATTN_WIDTH // LANES
    for a in range(3):
        for j in range(n_tiles):
            col = p[:, a * ATTN_WIDTH + j * LANES:a * ATTN_WIDTH + (j + 1) * LANES]
            if a == 0:
                col = rope(col) * (HEAD_DIM ** -0.5)
            elif a == 1:
                col = rope(col)
            stage_ref[a * n_tiles + j] = col
    for n, (_, dil) in enumerate(DILATED_PATTERNS):
        rows = tm // dil
        for a in range(3):
            for j in range(n_tiles):
                for c in range(dil):
                    lo = c * ATTN_WIDTH + j * LANES
                    src = stage_ref[a * n_tiles + j, pl.ds(c, rows, stride=dil), :] if dil > 1 else stage_ref[a * n_tiles + j]
                    qkv_refs[n][a][:, lo:lo + LANES] = src.astype(BF16)
    o0 = 3 * ATTN_WIDTH
    dn_ref[...] = p[:, o0:o0 + 3 * DN_WIDTH]
    z_ref[...] = p[:, o0 + 3 * DN_WIDTH:o0 + 4 * DN_WIDTH].astype(BF16)

    n_g = 2 * N_DN_HEADS
    ab = jnp.dot(h, wg_ref[...], preferred_element_type=F32)
    is_g = lax.broadcasted_iota(jnp.int32, ab.shape, 1) < n_g
    gp = gp_ref[...]
    gb_ref[...] = jnp.where(is_g, -jnp.exp(gp[0:1, :]) * jax.nn.softplus(ab + gp[1:2, :]), jax.nn.sigmoid(ab))
    abt = lax.dot_general(wgt_ref[...], h, (((1,), (1,)), ((), ())), preferred_element_type=F32)
    is_gt = lax.broadcasted_iota(jnp.int32, abt.shape, 0) < n_g
    gpt = gpt_ref[...]
    gbt_ref[...] = jnp.where(is_gt, -jnp.exp(gpt[:, 0:1]) * jax.nn.softplus(abt + gpt[:, 1:2]), jax.nn.sigmoid(abt))


def _in_proj(x, mod, norm_w, w_main, w_gate, w_gate_t, gate_par, gate_par_t, rope_tabs, blk_rows, seq_lens, t1):
    t, d = x.shape
    tm = PROJ_ROWS
    per_blk = blk_rows // tm
    n_main = w_main.shape[1]
    nb1 = t1 // tm
    s1b, s2b = seq_lens[0] // tm, seq_lens[1] // tm
    const = lambda i: (0, 0)
    row = lambda i: (i, 0)
    pos = lambda i: (jnp.where(i < nb1, i % s1b, (i - nb1) % s2b), 0)
    cos, sfw, sbk = rope_tabs
    view_specs, view_shapes = [], []
    for _, dil in DILATED_PATTERNS:
        view_specs += [pl.BlockSpec((tm // dil, dil * ATTN_WIDTH), row)] * 3
        view_shapes += [jax.ShapeDtypeStruct((t // dil, dil * ATTN_WIDTH), BF16)] * 3
    outs = pl.pallas_call(
        _inproj_kernel,
        grid=(t // tm,),
        in_specs=[
            pl.BlockSpec((tm, d), row),
            pl.BlockSpec((1, 3, d), lambda i: (i // per_blk, 0, 0)),
            pl.BlockSpec((1, d), const),
            _resident((d, n_main)),
            pl.BlockSpec((d, LANES), const),
            pl.BlockSpec((N_GATES, d), const),
            pl.BlockSpec((2, LANES), const),
            pl.BlockSpec((N_GATES, 2), const),
            pl.BlockSpec((tm, LANES), pos),
            pl.BlockSpec((tm, LANES), pos),
            pl.BlockSpec((tm, LANES), pos),
        ],
        out_specs=view_specs + [
            pl.BlockSpec((tm, 3 * DN_WIDTH), row),
            pl.BlockSpec((tm, DN_WIDTH), row),
            pl.BlockSpec((tm, LANES), row),
            pl.BlockSpec((N_GATES, tm), lambda i: (0, i)),
        ],
        out_shape=view_shapes + [
            jax.ShapeDtypeStruct((t, 3 * DN_WIDTH), F32),
            jax.ShapeDtypeStruct((t, DN_WIDTH), BF16),
            jax.ShapeDtypeStruct((t, LANES), F32),
            jax.ShapeDtypeStruct((N_GATES, t), F32),
        ],
        scratch_shapes=[pltpu.VMEM((3 * ATTN_WIDTH // LANES, tm, LANES), F32)],
        compiler_params=_cparams(("arbitrary",)),
        name="in_proj",
    )(x, mod, norm_w.reshape(1, d), w_main, w_gate, w_gate_t, gate_par, gate_par_t, cos, sfw, sbk)
    n_qkv = 3 * len(DILATED_PATTERNS)
    qkv_views = [outs[3 * n:3 * n + 3] for n in range(len(DILATED_PATTERNS))]
    return qkv_views, outs[n_qkv:]


def _rope_tables(max_len):
    half = ROT_HALF
    inv = ROPE_THETA ** (-jnp.arange(half, dtype=F32) / half)
    ang = jnp.arange(max_len, dtype=F32)[:, None] * inv[None, :]
    cos, sin = jnp.cos(ang), jnp.sin(ang)
    ones = jnp.ones((max_len, HEAD_DIM - 2 * half), F32)
    zeros_h = jnp.zeros((max_len, half), F32)
    zeros_r = jnp.zeros((max_len, HEAD_DIM - 2 * half), F32)
    cos_h = jnp.concatenate([cos, cos, ones], axis=1)
    sfw_h = jnp.concatenate([-sin, zeros_h, zeros_r], axis=1)
    sbk_h = jnp.concatenate([zeros_h, sin, zeros_r], axis=1)
    rep = LANES // HEAD_DIM
    return tuple(jnp.tile(a, (1, rep)) for a in (cos_h, sfw_h, sbk_h))


def _attn_kernel(q_ref, kp_ref, km_ref, kn_ref, vp_ref, vm_ref, vn_ref, *rest,
                 dil, next_dil, first, radius, tq, rows1, len1, len2):
    last = next_dil is None
    if not first:
        accp_ref, mlp_ref = rest[:2]
        rest = rest[2:]
    if last:
        o_ref, acc_scr, nat_scr = rest
    else:
        acc_out_ref, ml_out_ref, acc_scr, ml_scr = rest
    tb = q_ref.shape[0]
    n_sub = tb // tq
    tk = tq + 2 * radius
    n_tiles = ATTN_WIDTH // LANES
    n_res = q_ref.shape[1] // ATTN_WIDTH
    c0 = pl.program_id(1) * n_res
    r0 = pl.program_id(0) * tb
    in_first = r0 < rows1
    seq_len = jnp.where(in_first, len1, len2)
    off = jnp.where(in_first, r0, r0 - rows1)
    first_key = jnp.where((off % seq_len) == 0, radius, 0)
    end_key = jnp.where(((off + tb) % seq_len) == 0, tq + radius, tk)

    qi = lax.broadcasted_iota(jnp.int32, (tq, tk), 0)
    kj = lax.broadcasted_iota(jnp.int32, (tq, tk), 1)
    in_band = jnp.abs(kj - radius - qi) <= radius
    biases = []
    for sub in range(n_sub):
        valid = in_band
        if sub == 0:
            valid = valid & (kj >= first_key)
        if sub == n_sub - 1:
            valid = valid & (kj < end_key)
        bias = jnp.where(valid, 0.0, NEG_BIG).astype(F32)
        biases.append(jnp.concatenate([bias, bias], axis=0))

    lane = lax.broadcasted_iota(jnp.int32, (1, LANES), 1)
    low_half = lane < HEAD_DIM
    nt = (((1,), (1,)), ((), ()))

    def pair_col(ref, rows, col):
        return jnp.concatenate([ref[rows, col:col + 1], ref[rows, col + 1:col + 2]], axis=0)

    pairs = []
    for r in range(n_res):
        for j in range(n_tiles):
            sl = slice(r * ATTN_WIDTH + j * LANES, r * ATTN_WIDTH + (j + 1) * LANES)
            k_all = jnp.concatenate([kp_ref[:, sl], km_ref[:, sl], kn_ref[:, sl]], axis=0)
            v_all = jnp.concatenate([vp_ref[:, sl], vm_ref[:, sl], vn_ref[:, sl]], axis=0)
            for sub in range(n_sub):
                rows = slice(sub * tq, (sub + 1) * tq)
                q2 = q_ref[rows, sl]
                zero = jnp.zeros_like(q2)
                qs = jnp.concatenate([jnp.where(low_half, q2, zero), jnp.where(low_half, zero, q2)], axis=0)
                kw = k_all[sub * tq:sub * tq + tk]
                pairs.append(dict(r=r, j=j, sub=sub, sl=sl, rows=rows, vw=v_all[sub * tq:sub * tq + tk],
                                  stats=r * LANES + 2 * j,
                                  s=lax.dot_general(qs, kw, nt, preferred_element_type=F32) + biases[sub]))
    for pr in pairs:
        m_cur = jnp.max(pr["s"], axis=-1, keepdims=True)
        if first:
            pr["m"] = m_cur
        else:
            m_old = pair_col(mlp_ref, pr["rows"], pr["stats"])
            pr["m"] = jnp.maximum(m_old, m_cur)
            pr["alpha"] = jnp.exp(m_old - pr["m"])
    ones = jnp.ones((tk, LANES), BF16)
    for pr in pairs:
        pr["p"] = jnp.exp(pr["s"] - pr["m"]).astype(BF16)
    for pr in pairs:
        pr["pv"] = jnp.dot(pr["p"], pr["vw"], preferred_element_type=F32)
        pr["l"] = jnp.dot(pr["p"], ones, preferred_element_type=F32)
        if not first:
            pr["l"] = pr["alpha"] * pair_col(mlp_ref, pr["rows"], pr["stats"] + N_ATTN_HEADS) + pr["l"]
    for pr in pairs:
        acc = jnp.where(low_half, pr["pv"][:tq], pr["pv"][tq:])
        if not first:
            acc = accp_ref[pr["rows"], pr["sl"]] * jnp.where(low_half, pr["alpha"][:tq], pr["alpha"][tq:]) + acc
        if last:
            acc = acc / jnp.where(low_half, pr["l"][:tq], pr["l"][tq:])
        acc_scr[c0 + pr["r"], pr["j"], pr["rows"], :] = acc
    if not last:
        for r in range(n_res):
            for sub in range(n_sub):
                ml = jnp.zeros((tq, LANES), F32)
                for pr in pairs:
                    if pr["r"] != r or pr["sub"] != sub:
                        continue
                    for half, rows in enumerate((slice(0, tq), slice(tq, 2 * tq))):
                        h = 2 * pr["j"] + half
                        ml = jnp.where(lane == h, pr["m"][rows], ml)
                        ml = jnp.where(lane == N_ATTN_HEADS + h, pr["l"][rows], ml)
                ml_scr[c0 + r, sub * tq:(sub + 1) * tq, :] = ml

    @pl.when(c0 + n_res == dil)
    def _():
        if last:
            for cc in range(dil):
                for j in range(n_tiles):
                    nat_scr[j, pl.ds(cc, tb, stride=dil), :] = acc_scr[cc, j]
            for j in range(n_tiles):
                o_ref[:, j * LANES:(j + 1) * LANES] = nat_scr[j].astype(o_ref.dtype)
        else:
            ratio = next_dil // dil
            rows_out = tb // ratio
            for cc in range(dil):
                for m in range(ratio):
                    cb = m * dil + cc
                    rows = pl.ds(m, rows_out, stride=ratio)
                    for j in range(n_tiles):
                        lo = cb * ATTN_WIDTH + j * LANES
                        acc_out_ref[:, lo:lo + LANES] = acc_scr[cc, j, rows, :]
                    ml_out_ref[:, cb * LANES:(cb + 1) * LANES] = ml_scr[cc, rows, :]


def _attn_pattern(qkv, prev, dil, next_dil, radius, t, t1, seq_lens):
    q, k, v = qkv
    w = ATTN_WIDTH
    first, last = prev is None, next_dil is None
    rows = t // dil
    tq, n_sub, n_res = ATTN_TILE[dil]
    tb = tq * n_sub
    per_q = tb // radius
    n_halo = rows // radius
    n_tiles = w // LANES
    wr = n_res * w
    main = lambda i, c: (i, c)
    before = lambda i, c: (jnp.maximum(i * per_q - 1, 0), c)
    after = lambda i, c: (jnp.minimum((i + 1) * per_q, n_halo - 1), c)
    whole = lambda i, c: (i, 0)
    kv_specs = [pl.BlockSpec((radius, wr), before), pl.BlockSpec((tb, wr), main), pl.BlockSpec((radius, wr), after)]
    in_specs = [pl.BlockSpec((tb, wr), main)] + kv_specs + kv_specs
    args = [q, k, k, k, v, v, v]
    if not first:
        in_specs += [pl.BlockSpec((tb, wr), main), pl.BlockSpec((tb, n_res * LANES), main)]
        args += list(prev)
    scratch = [pltpu.VMEM((dil, n_tiles, tb, LANES), F32)]
    if last:
        out_specs = [pl.BlockSpec((tb * dil, w), whole)]
        out_shape = [jax.ShapeDtypeStruct((t, w), BF16)]
        scratch.append(pltpu.VMEM((n_tiles, tb * dil, LANES), F32))
    else:
        rows_out = tb * dil // next_dil
        out_specs = [pl.BlockSpec((rows_out, next_dil * w), whole), pl.BlockSpec((rows_out, next_dil * LANES), whole)]
        out_shape = [jax.ShapeDtypeStruct((t // next_dil, next_dil * w), F32),
                     jax.ShapeDtypeStruct((t // next_dil, next_dil * LANES), F32)]
        scratch.append(pltpu.VMEM((dil, tb, LANES), F32))
    return pl.pallas_call(
        functools.partial(_attn_kernel, dil=dil, next_dil=next_dil, first=first, radius=radius, tq=tq,
                          rows1=t1 // dil, len1=seq_lens[0] // dil, len2=seq_lens[1] // dil),
        grid=(rows // tb, dil // n_res),
        in_specs=in_specs,
        out_specs=out_specs,
        out_shape=out_shape,
        scratch_shapes=scratch,
        compiler_params=_cparams(("arbitrary", "arbitrary")),
        name=f"attn_d{dil}",
    )(*args)


def _attention(qkv_views, t, t1, seq_lens):
    state = None
    n = len(DILATED_PATTERNS)
    for idx, (window, dil) in enumerate(DILATED_PATTERNS):
        next_dil = DILATED_PATTERNS[idx + 1][1] if idx + 1 < n else None
        state = _attn_pattern(qkv_views[idx], state, dil, next_dil, window // (2 * dil), t, t1, seq_lens)
    return state[0]


def _dnprep_kernel(xp_ref, xm_ref, xn_ref, w_ref, q_ref, k_ref, v_ref, buf_ref, *, rows1, len1, len2):
    tm = xm_ref.shape[0]
    r0 = pl.program_id(0) * tm
    in_first = r0 < rows1
    seq_len = jnp.where(in_first, len1, len2)
    off = jnp.where(in_first, r0, r0 - rows1)
    keep_prev = jnp.where((off % seq_len) == 0, 0.0, 1.0).astype(F32)
    keep_next = jnp.where(((off + tm) % seq_len) == 0, 0.0, 1.0).astype(F32)
    buf_ref[0:HALO_ROWS, :] = xp_ref[...] * keep_prev
    buf_ref[HALO_ROWS:HALO_ROWS + tm, :] = xm_ref[...]
    buf_ref[HALO_ROWS + tm:, :] = xn_ref[...] * keep_next
    pad = CONV_K // 2
    y = buf_ref[HALO_ROWS - pad:HALO_ROWS - pad + tm, :] * w_ref[0:1, :]
    for j in range(1, CONV_K):
        y = y + buf_ref[HALO_ROWS - pad + j:HALO_ROWS - pad + j + tm, :] * w_ref[j:j + 1, :]
    y = _silu(y)
    for hd in range(N_DN_HEADS):
        sl = slice(hd * DN_HEAD_DIM, (hd + 1) * DN_HEAD_DIM)
        qh = y[:, sl]
        kh = y[:, DN_WIDTH + hd * DN_HEAD_DIM:DN_WIDTH + (hd + 1) * DN_HEAD_DIM]
        qn = qh * lax.rsqrt(jnp.sum(qh * qh, axis=-1, keepdims=True) + EPS) * (DN_HEAD_DIM ** -0.5)
        kn = kh * lax.rsqrt(jnp.sum(kh * kh, axis=-1, keepdims=True) + EPS)
        q_ref[:, sl] = qn.astype(BF16)
        k_ref[:, sl] = kn.astype(BF16)
    v_ref[...] = y[:, 2 * DN_WIDTH:].astype(BF16)


def _dn_prep(pre, conv_w, t1, seq_lens):
    t, w3 = pre.shape
    tm = PREP_ROWS
    per = tm // HALO_ROWS
    n_halo = t // HALO_ROWS
    row = lambda i: (i, 0)
    out_spec = pl.BlockSpec((tm, DN_WIDTH), row)
    out_sds = jax.ShapeDtypeStruct((t, DN_WIDTH), BF16)
    return pl.pallas_call(
        functools.partial(_dnprep_kernel, rows1=t1, len1=seq_lens[0], len2=seq_lens[1]),
        grid=(t // tm,),
        in_specs=[
            pl.BlockSpec((HALO_ROWS, w3), lambda i: (jnp.maximum(i * per - 1, 0), 0)),
            pl.BlockSpec((tm, w3), row),
            pl.BlockSpec((HALO_ROWS, w3), lambda i: (jnp.minimum((i + 1) * per, n_halo - 1), 0)),
            pl.BlockSpec((CONV_K, w3), lambda i: (0, 0)),
        ],
        out_specs=[out_spec, out_spec, out_spec],
        out_shape=[out_sds, out_sds, out_sds],
        scratch_shapes=[pltpu.VMEM((tm + 2 * HALO_ROWS, w3), F32)],
        compiler_params=_cparams(("arbitrary",)),
        name="dn_prep",
    )(pre, pre, pre, conv_w)


def _bdot(a, b):
    return jnp.dot(a.astype(BF16), b.astype(BF16), preferred_element_type=F32)


def _split3(x):
    hi = x.astype(BF16)
    r1 = x - hi.astype(F32)
    mid = r1.astype(BF16)
    lo = (r1 - mid.astype(F32)).astype(BF16)
    return hi, mid, lo


def _dn_kernel(qf_ref, kf_ref, vf_ref, gf_ref, gtf_ref, qb_ref, kb_ref, vb_ref, gb_ref, gtb_ref,
               of_ref, ob_ref, sf_ref, sb_ref, *, c, n_chunks, chunks1, len1, len2):
    n_sub = qf_ref.shape[0] // c
    i = pl.program_id(0)

    def seq_pos(ci):
        in_first = ci < chunks1
        seq_len = jnp.where(in_first, len1, len2)
        off = jnp.where(in_first, ci, ci - chunks1)
        return off % seq_len, seq_len

    pos_f, _ = seq_pos(i * n_sub)
    pos_b, len_b = seq_pos(n_chunks - 1 - i * n_sub)

    @pl.when(pos_f == 0)
    def _():
        sf_ref[...] = jnp.zeros_like(sf_ref)

    @pl.when(pos_b == len_b - 1)
    def _():
        sb_ref[...] = jnp.zeros_like(sb_ref)

    ii = lax.broadcasted_iota(jnp.int32, (c, c), 0)
    jj = lax.broadcasted_iota(jnp.int32, (c, c), 1)
    lower, lower_strict = ii >= jj, ii > jj
    upper, upper_strict = ii <= jj, ii < jj
    eye = jnp.where(ii == jj, 1.0, 0.0).astype(F32)
    tri_l = jnp.where(lower, 1.0, 0.0).astype(F32)
    tri_u = jnp.where(upper, 1.0, 0.0).astype(F32)
    level_masks = [jnp.where(ii // INV_BASE == jj // INV_BASE, 1.0, 0.0).astype(F32)]
    b = INV_BASE
    while b < c:
        same_outer = ii // (2 * b) == jj // (2 * b)
        level_masks.append(jnp.where(same_outer & (ii // b != jj // b), 1.0, 0.0).astype(F32))
        b *= 2
    ones_cc = jnp.ones((c, c), BF16)
    ones_cd = jnp.ones((c, DN_HEAD_DIM), BF16)
    nt = (((1,), (1,)), ((), ()))

    chains = []
    for forward, (q_ref, k_ref, v_ref, g_ref, gt_ref, o_ref, s_ref) in (
            (True, (qf_ref, kf_ref, vf_ref, gf_ref, gtf_ref, of_ref, sf_ref)),
            (False, (qb_ref, kb_ref, vb_ref, gb_ref, gtb_ref, ob_ref, sb_ref))):
        tri_c, tri_r = (tri_l, tri_u) if forward else (tri_u, tri_l)
        col_lhs = jnp.concatenate([tri_c.astype(BF16), ones_cc], axis=0)
        row_rhs = jnp.concatenate([tri_r.astype(BF16), ones_cd], axis=1)
        keep, strict = (lower, lower_strict) if forward else (upper, upper_strict)
        d0 = 0 if forward else N_DN_HEADS
        for order in range(n_sub):
            sub = order if forward else n_sub - 1 - order
            rows = slice(sub * c, (sub + 1) * c)
            g, gt = g_ref[rows, :], gt_ref[:, rows]
            col = sum(jnp.dot(col_lhs, piece, preferred_element_type=F32) for piece in _split3(g))
            row = sum(jnp.dot(piece, row_rhs, preferred_element_type=F32) for piece in _split3(gt))
            cs_col, tot_col = col[:c], col[c:]
            cs_row, tot_row = row[:, :c], row[:, c:]
            for hd in range(N_DN_HEADS):
                gi = d0 + hd
                bi = 2 * N_DN_HEADS + d0 + hd
                sl = slice(hd * DN_HEAD_DIM, (hd + 1) * DN_HEAD_DIM)
                chains.append(dict(
                    order=order, rows=rows, sl=sl, hd=hd, keep=keep, strict=strict, o_ref=o_ref, s_ref=s_ref,
                    q=q_ref[rows, sl], k=k_ref[rows, sl], v=v_ref[rows, sl],
                    gc_col=cs_col[:, gi:gi + 1], gc_row=cs_row[gi:gi + 1, :], beta=g[:, bi:bi + 1],
                    tot_col=tot_col[:, gi:gi + 1], tot_row=tot_row[gi:gi + 1, :]))

    for ch in chains:
        qh, kh = ch["q"], ch["k"]
        both = lax.dot_general(jnp.concatenate([qh, kh], axis=0), kh, nt, preferred_element_type=F32)
        decay = jnp.where(ch["keep"], jnp.exp(jnp.minimum(ch["gc_col"] - ch["gc_row"], 0.0)), 0.0)
        ch["a"] = jnp.where(ch["strict"], ch["beta"] * both[c:] * decay, 0.0)
        ch["qk"] = (both[:c] * decay).astype(BF16)
    for ch in chains:
        dg = ch["a"] * level_masks[0]
        ch["x"] = eye - dg
        ch["y"] = _bdot(dg, dg)
    for ch in chains:
        ch["x"] = ch["x"] + _bdot(ch["x"], ch["y"])
    for off_mask in level_masks[1:]:
        for ch in chains:
            ch["y"] = _bdot(ch["a"] * off_mask, ch["x"])
        for ch in chains:
            ch["x"] = ch["x"] - _bdot(ch["x"], ch["y"])
    for ch in chains:
        kh = ch["k"].astype(F32)
        egc = jnp.exp(ch["gc_col"])
        rhs = jnp.concatenate([ch["v"].astype(F32) * ch["beta"], kh * (ch["beta"] * egc)], axis=1)
        ch["uw"] = _bdot(ch["x"], rhs)
        ch["lhs_q"] = ch["q"].astype(F32) * egc
        ch["k_dec_t"] = jnp.transpose(kh * jnp.exp(ch["tot_col"] - ch["gc_col"])).astype(BF16)
    states = {}
    for order in range(n_sub):
        group = [ch for ch in chains if ch["order"] == order]
        for ch in group:
            key = (id(ch["s_ref"]), ch["hd"])
            ch["state"] = states[key] if order else ch["s_ref"][ch["hd"]]
            lhs = jnp.concatenate([ch["uw"][:, DN_HEAD_DIM:], ch["lhs_q"]], axis=0)
            ch["ws"] = _bdot(lhs, ch["state"])
        for ch in group:
            v_new = (ch["uw"][:, :DN_HEAD_DIM] - ch["ws"][:c]).astype(BF16)
            ch["o_ref"][ch["rows"], ch["sl"]] = ch["ws"][c:] + jnp.dot(ch["qk"], v_new, preferred_element_type=F32)
            ch["v_new"] = v_new
        for ch in group:
            states[(id(ch["s_ref"]), ch["hd"])] = ch["state"] * jnp.exp(ch["tot_row"]) + jnp.dot(
                ch["k_dec_t"], ch["v_new"], preferred_element_type=F32)
    for ch in chains:
        if ch["order"] == n_sub - 1:
            ch["s_ref"][ch["hd"]] = states[(id(ch["s_ref"]), ch["hd"])]


def _deltanet(q, k, v, gates, gates_t, t1, seq_lens):
    t, w = q.shape
    c = DN_CHUNK
    rows = c * DN_CHUNKS_PER_STEP
    n = t // rows
    fwd = lambda i: (i, 0)
    bwd = lambda i: (n - 1 - i, 0)
    fwd_t = lambda i: (0, i)
    bwd_t = lambda i: (0, n - 1 - i)

    def specs(row_map, col_map):
        return [pl.BlockSpec((rows, w), row_map)] * 3 + [
            pl.BlockSpec((rows, LANES), row_map), pl.BlockSpec((N_GATES, rows), col_map)]

    state = pltpu.VMEM((N_DN_HEADS, DN_HEAD_DIM, DN_HEAD_DIM), F32)
    return pl.pallas_call(
        functools.partial(_dn_kernel, c=c, n_chunks=t // c, chunks1=t1 // c, len1=seq_lens[0] // c,
                          len2=seq_lens[1] // c),
        grid=(n,),
        in_specs=specs(fwd, fwd_t) + specs(bwd, bwd_t),
        out_specs=[pl.BlockSpec((rows, w), fwd), pl.BlockSpec((rows, w), bwd)],
        out_shape=[jax.ShapeDtypeStruct((t, w), F32)] * 2,
        scratch_shapes=[state, state],
        compiler_params=_cparams(("arbitrary",)),
        name="deltanet",
    )(q, k, v, gates, gates_t, q, k, v, gates, gates_t)


def _outproj_kernel(x_ref, mod_ref, attn_ref, of_ref, ob_ref, z_ref, nw_ref, wa_ref, wd_ref, o_ref):
    o = of_ref[...] + ob_ref[...]
    z = z_ref[...].astype(F32)
    nw = nw_ref[...]
    parts = []
    for hd in range(N_DN_HEADS):
        sl = slice(hd * DN_HEAD_DIM, (hd + 1) * DN_HEAD_DIM)
        oh = o[:, sl]
        parts.append(oh * lax.rsqrt(jnp.mean(oh * oh, axis=-1, keepdims=True) + EPS) * nw * _silu(z[:, sl]))
    dn = jnp.concatenate(parts, axis=1).astype(BF16)
    y = jnp.dot(attn_ref[...], wa_ref[...], preferred_element_type=F32)
    y = y + jnp.dot(dn, wd_ref[...], preferred_element_type=F32)
    o_ref[...] = x_ref[...] + mod_ref[0, 2:3, :] * y


def _out_proj(x, mod, attn, o_f, o_b, z, dn_norm, w_attn, w_dn, blk_rows):
    t, d = x.shape
    tm = PROJ_ROWS
    per_blk = blk_rows // tm
    row = lambda i: (i, 0)
    const = lambda i: (0, 0)
    return pl.pallas_call(
        _outproj_kernel,
        grid=(t // tm,),
        in_specs=[
            pl.BlockSpec((tm, d), row),
            pl.BlockSpec((1, 3, d), lambda i: (i // per_blk, 0, 0)),
            pl.BlockSpec((tm, ATTN_WIDTH), row),
            pl.BlockSpec((tm, DN_WIDTH), row),
            pl.BlockSpec((tm, DN_WIDTH), row),
            pl.BlockSpec((tm, DN_WIDTH), row),
            pl.BlockSpec((1, DN_HEAD_DIM), const),
            _resident((ATTN_WIDTH, d)),
            _resident((DN_WIDTH, d)),
        ],
        out_specs=pl.BlockSpec((tm, d), row),
        out_shape=jax.ShapeDtypeStruct((t, d), F32),
        compiler_params=_cparams(("arbitrary",)),
        name="out_proj",
    )(x, mod, attn, o_f, o_b, z, dn_norm.reshape(1, DN_HEAD_DIM), w_attn, w_dn)


def kernel(x_prompt, x_sample, c_prompt, c_sample, ada_w, ada_b, norm_ffn1, ffn1_w_gate, ffn1_w_up, ffn1_w_down, norm_mix, w_in, conv_w, a_log, dt_bias, dn_norm, w_out, norm_ffn2, ffn2_w_gate, ffn2_w_up, ffn2_w_down, norm_final):
    b1, s1, d = x_prompt.shape
    b2, s2, _ = x_sample.shape
    depth = ada_w.shape[0]
    t1 = b1 * s1
    seq_lens = (s1, s2)
    assert s2 % s1 == 0 and t1 % s2 == 0, "flat layout needs nested sequence lengths"
    t = t1 + b2 * s2
    xs = [x_prompt.reshape(t1, d), x_sample.reshape(b2 * s2, d)]

    n_seq = b1 + b2
    c_all = jnp.concatenate([c_prompt, c_sample, jnp.zeros((-n_seq % SUBLANES, d), F32)], axis=0)
    mod = _modulation(c_all, ada_w, ada_b)
    blk_seq = jnp.concatenate([jnp.arange(b1), b1 + jnp.repeat(jnp.arange(b2), s2 // s1)])
    mod = mod[:, blk_seq].reshape(depth, blk_seq.shape[0], N_MOD, d)

    rope_tabs = _rope_tables(max(s1, s2))
    n_main = 3 * ATTN_WIDTH + 4 * DN_WIDTH
    gate_par_t = jnp.stack([a_log.astype(F32).reshape(depth, -1), dt_bias.astype(F32).reshape(depth, -1)], axis=2)
    gate_par_t = jnp.pad(gate_par_t, ((0, 0), (0, N_GATES - gate_par_t.shape[1]), (0, 0)))
    gate_par = jnp.pad(jnp.swapaxes(gate_par_t, 1, 2), ((0, 0), (0, 0), (0, LANES - N_GATES)))

    for l in range(depth):
        bf = lambda a: a.astype(BF16)
        x = _ffn(xs if l == 0 else [x], mod[l, :, 0:3], norm_ffn1[l], bf(ffn1_w_gate[l]), bf(ffn1_w_up[l]),
                 bf(ffn1_w_down[l]), s1, t1)
        w_gate_t = bf(w_in[l][:, n_main:]).T
        w_gate = jnp.pad(w_gate_t.T, ((0, 0), (0, LANES - N_GATES)))
        qkv_views, (dn_pre, z, gates, gates_t) = _in_proj(
            x, mod[l, :, 3:6], norm_mix[l], bf(w_in[l][:, :n_main]), w_gate, w_gate_t,
            gate_par[l], gate_par_t[l], rope_tabs, s1, seq_lens, t1)
        attn = _attention(qkv_views, t, t1, seq_lens)
        dq, dk, dv = _dn_prep(dn_pre, conv_w[l], t1, seq_lens)
        o_f, o_b = _deltanet(dq, dk, dv, gates, gates_t, t1, seq_lens)
        x = _out_proj(x, mod[l, :, 3:6], attn, o_f, o_b, z, dn_norm[l], bf(w_out[l][:ATTN_WIDTH]),
                      bf(w_out[l][ATTN_WIDTH:]), s1)
        is_last = l == depth - 1
        x = _ffn([x], mod[l, :, 6:9], norm_ffn2[l], bf(ffn2_w_gate[l]), bf(ffn2_w_up[l]), bf(ffn2_w_down[l]), s1,
                 t1, final_w=norm_final if is_last else None, split_out=is_last)
    return x[0].reshape(b1, s1, d), x[1].reshape(b2, s2, d)
```

```python
import functools

import jax
import jax.numpy as jnp
from jax import lax
from jax.experimental import pallas as pl
from jax.experimental.pallas import tpu as pltpu

F32 = jnp.float32
BF16 = jnp.bfloat16

D_MODEL = 1024
D_FF = 2816
N_MOD = 9
EPS = 1e-6
N_ATTN_HEADS = 8
HEAD_DIM = 64
ATTN_WIDTH = N_ATTN_HEADS * HEAD_DIM
ROT_HALF = HEAD_DIM // 8
ROPE_THETA = 500000.0
DILATED_PATTERNS = ((128, 1), (512, 4), (2048, 16))
N_DN_HEADS = 4
DN_HEAD_DIM = 128
DN_WIDTH = N_DN_HEADS * DN_HEAD_DIM
CONV_K = 5
N_GATES = 4 * N_DN_HEADS

LANES = 128
SUBLANES = 8
VMEM_LIMIT_BYTES = 56 * 1024 * 1024

FFN_ROWS = 512
PROJ_ROWS = 512
ATTN_TILE = {1: (128, 2, 1), 4: (128, 1, 4), 16: (128, 1, 4)}
DN_CHUNK = 128
DN_CHUNKS_PER_STEP = 2
INV_BASE = 4
MOD_COLS = 1152
HALO_ROWS = 8
NEG_BIG = -1e30


def _cparams(sem):
    return pltpu.CompilerParams(dimension_semantics=sem, vmem_limit_bytes=VMEM_LIMIT_BYTES)


def _resident(shape):
    return pl.BlockSpec(shape, lambda *_: (0, 0), pipeline_mode=pl.Buffered(1))


def _silu(x):
    return x * jax.nn.sigmoid(x)


def _modulated_norm(x, norm_w, shift, scale):
    y = x * lax.rsqrt(jnp.mean(x * x, axis=-1, keepdims=True) + EPS) * norm_w
    return y * (1.0 + scale) + shift


def _mod_kernel(c_ref, w_ref, b_ref, o_ref):
    sc = _silu(c_ref[...]).astype(BF16)
    o_ref[0] = jnp.dot(sc, w_ref[0].astype(BF16), preferred_element_type=F32) + b_ref[0]


def _modulation(c_all, ada_w, ada_b):
    depth, d, n = ada_w.shape
    rows = c_all.shape[0]
    return pl.pallas_call(
        _mod_kernel,
        grid=(depth, n // MOD_COLS),
        in_specs=[
            pl.BlockSpec((rows, d), lambda l, j: (0, 0)),
            pl.BlockSpec((1, d, MOD_COLS), lambda l, j: (l, 0, j)),
            pl.BlockSpec((1, 1, MOD_COLS), lambda l, j: (l, 0, j)),
        ],
        out_specs=pl.BlockSpec((1, rows, MOD_COLS), lambda l, j: (l, 0, j)),
        out_shape=jax.ShapeDtypeStruct((depth, rows, n), F32),
        compiler_params=_cparams(("arbitrary", "arbitrary")),
        name="modulation",
    )(c_all, ada_w, ada_b.reshape(depth, 1, n))


def _mixer_residual(x, mod_ref, attn_ref, of_ref, ob_ref, z_ref, nw_ref, wa_ref, wd_ref):
    o = of_ref[...] + ob_ref[...]
    z = z_ref[...].astype(F32)
    nw = nw_ref[...]
    parts = []
    for hd in range(N_DN_HEADS):
        sl = slice(hd * DN_HEAD_DIM, (hd + 1) * DN_HEAD_DIM)
        oh = o[:, sl]
        parts.append(oh * lax.rsqrt(jnp.mean(oh * oh, axis=-1, keepdims=True) + EPS) * nw * _silu(z[:, sl]))
    dn = jnp.concatenate(parts, axis=1).astype(BF16)
    y = jnp.dot(attn_ref[...], wa_ref[...], preferred_element_type=F32)
    y = y + jnp.dot(dn, wd_ref[...], preferred_element_type=F32)
    return x + mod_ref[0, 2:3, :] * y


N_MIXER_REFS = 8


def _ffn_kernel(*refs, n_in, n_out, with_mixer, final_norm, rows1):
    x_refs, refs = refs[:n_in], refs[n_in:]
    if with_mixer:
        mixer_refs, refs = refs[:N_MIXER_REFS], refs[N_MIXER_REFS:]
    mod_ref, nw_ref, wg_ref, wu_ref, wd_ref = refs[:5]
    refs = refs[5:]
    if final_norm:
        fw_ref, refs = refs[0], refs[1:]
    o_refs, refs = refs[:n_out], refs[n_out:]
    tm = x_refs[0].shape[0]
    in_first = pl.program_id(0) < rows1 // tm
    if n_in == 2:
        stage_ref = refs[0]

        @pl.when(in_first)
        def _():
            stage_ref[...] = x_refs[0][...]

        @pl.when(jnp.logical_not(in_first))
        def _():
            stage_ref[...] = x_refs[1][...]

        x = stage_ref[...]
    else:
        x = x_refs[0][...]
    if with_mixer:
        x = _mixer_residual(x, *mixer_refs)
    h = _modulated_norm(x, nw_ref[...], mod_ref[0, 0:1, :], mod_ref[0, 1:2, :]).astype(BF16)
    g = jnp.dot(h, wg_ref[...], preferred_element_type=F32)
    u = jnp.dot(h, wu_ref[...], preferred_element_type=F32)
    a = (_silu(g) * u).astype(BF16)
    y = jnp.dot(a, wd_ref[...], preferred_element_type=F32)
    out = x + (0.5 * mod_ref[0, 2:3, :]) * y
    if final_norm:
        out = out * lax.rsqrt(jnp.mean(out * out, axis=-1, keepdims=True) + EPS) * fw_ref[...]
    if n_out == 2:
        @pl.when(in_first)
        def _():
            o_refs[0][...] = out

        @pl.when(jnp.logical_not(in_first))
        def _():
            o_refs[1][...] = out
    else:
        o_refs[0][...] = out


def _ffn(xs, mod, norm_w, wg, wu, wd, blk_rows, rows1, final_w=None, split_out=False, mixer=None):
    d = xs[0].shape[1]
    t = sum(x.shape[0] for x in xs)
    f = wg.shape[1]
    tm = FFN_ROWS
    per_blk = blk_rows // tm
    nb1 = rows1 // tm
    const = lambda i: (0, 0)
    row = lambda i: (i, 0)
    blk = lambda i: (i // per_blk, 0, 0)
    group1 = lambda i: (jnp.minimum(i, nb1 - 1), 0)
    group2 = lambda i: (jnp.maximum(i - nb1, 0), 0)
    in_specs = [pl.BlockSpec((tm, d), row)] if len(xs) == 1 else [pl.BlockSpec((tm, d), group1), pl.BlockSpec((tm, d), group2)]
    args = list(xs)
    if mixer is not None:
        m_mod, attn, o_f, o_b, z, dn_norm, w_attn, w_dn = mixer
        in_specs += [
            pl.BlockSpec((1, 3, d), blk),
            pl.BlockSpec((tm, ATTN_WIDTH), row),
            pl.BlockSpec((tm, DN_WIDTH), row),
            pl.BlockSpec((tm, DN_WIDTH), row),
            pl.BlockSpec((tm, DN_WIDTH), row),
            pl.BlockSpec((1, DN_HEAD_DIM), const),
            _resident((ATTN_WIDTH, d)),
            _resident((DN_WIDTH, d)),
        ]
        args += [m_mod, attn, o_f, o_b, z, dn_norm.reshape(1, DN_HEAD_DIM), w_attn, w_dn]
    in_specs += [
        pl.BlockSpec((1, 3, d), blk),
        pl.BlockSpec((1, d), const),
        _resident((d, f)),
        _resident((d, f)),
        _resident((f, d)),
    ]
    args += [mod, norm_w.reshape(1, d), wg, wu, wd]
    if final_w is not None:
        in_specs.append(pl.BlockSpec((1, d), const))
        args.append(final_w.reshape(1, d))
    if split_out:
        out_specs = [pl.BlockSpec((tm, d), group1), pl.BlockSpec((tm, d), group2)]
        out_shape = [jax.ShapeDtypeStruct((rows1, d), F32), jax.ShapeDtypeStruct((t - rows1, d), F32)]
    else:
        out_specs = [pl.BlockSpec((tm, d), row)]
        out_shape = [jax.ShapeDtypeStruct((t, d), F32)]
    outs = pl.pallas_call(
        functools.partial(_ffn_kernel, n_in=len(xs), n_out=len(out_specs), with_mixer=mixer is not None,
                          final_norm=final_w is not None, rows1=rows1),
        grid=(t // tm,),
        in_specs=in_specs,
        out_specs=out_specs,
        out_shape=out_shape,
        scratch_shapes=[pltpu.VMEM((tm, d), F32)] if len(xs) == 2 else [],
        compiler_params=_cparams(("arbitrary",)),
        name="ffn",
    )(*args)
    return outs if split_out else outs[0]


def _inproj_kernel(xp_ref, x_ref, xn_ref, mod_ref, nw_ref, w_ref, wg_ref, wgt_ref, gp_ref, gpt_ref,
                   cos_ref, sfw_ref, sbk_ref, cw_ref, *rest, rows1, len1, len2):
    n_views = len(DILATED_PATTERNS)
    qkv_refs = [rest[3 * n:3 * n + 3] for n in range(n_views)]
    dq_ref, dk_ref, dv_ref, z_ref, gb_ref, gbt_ref, stage_ref, buf_ref = rest[3 * n_views:]
    tm = x_ref.shape[0]
    nw, shift, scale = nw_ref[...], mod_ref[0, 0:1, :], mod_ref[0, 1:2, :]
    h_main = _modulated_norm(x_ref[...], nw, shift, scale)
    h_ext = jnp.concatenate([_modulated_norm(xp_ref[...], nw, shift, scale), h_main,
                             _modulated_norm(xn_ref[...], nw, shift, scale)], axis=0).astype(BF16)
    h = h_main.astype(BF16)
    cos, sfw, sbk = cos_ref[...], sfw_ref[...], sbk_ref[...]

    def rope(xp):
        return xp * cos + pltpu.roll(xp, LANES - ROT_HALF, 1) * sfw + pltpu.roll(xp, ROT_HALF, 1) * sbk

    n_tiles = ATTN_WIDTH // LANES
    for a in range(3):
        pa = jnp.dot(h, w_ref[:, a * ATTN_WIDTH:(a + 1) * ATTN_WIDTH], preferred_element_type=F32)
        for j in range(n_tiles):
            col = pa[:, j * LANES:(j + 1) * LANES]
            if a == 0:
                col = rope(col) * (HEAD_DIM ** -0.5)
            elif a == 1:
                col = rope(col)
            stage_ref[a * n_tiles + j] = col
        for n, (_, dil) in enumerate(DILATED_PATTERNS):
            rows = tm // dil
            for j in range(n_tiles):
                for c in range(dil):
                    lo = c * ATTN_WIDTH + j * LANES
                    src = stage_ref[a * n_tiles + j, pl.ds(c, rows, stride=dil), :] if dil > 1 else stage_ref[a * n_tiles + j]
                    qkv_refs[n][a][:, lo:lo + LANES] = src.astype(BF16)
    o0 = 3 * ATTN_WIDTH
    z_ref[...] = jnp.dot(h, w_ref[:, o0 + 3 * DN_WIDTH:o0 + 4 * DN_WIDTH], preferred_element_type=F32).astype(BF16)

    r0 = pl.program_id(0) * tm
    in_first = r0 < rows1
    seq_len = jnp.where(in_first, len1, len2)
    off = jnp.where(in_first, r0, r0 - rows1)
    keep_prev = jnp.where((off % seq_len) == 0, 0.0, 1.0).astype(F32)
    keep_next = jnp.where(((off + tm) % seq_len) == 0, 0.0, 1.0).astype(F32)
    pad = CONV_K // 2
    for a, out_ref in enumerate((dq_ref, dk_ref, dv_ref)):
        cols = slice(a * DN_WIDTH, (a + 1) * DN_WIDTH)
        pe = jnp.dot(h_ext, w_ref[:, o0 + a * DN_WIDTH:o0 + (a + 1) * DN_WIDTH], preferred_element_type=F32)
        buf_ref[0:HALO_ROWS, cols] = pe[0:HALO_ROWS] * keep_prev
        buf_ref[HALO_ROWS:HALO_ROWS + tm, cols] = pe[HALO_ROWS:HALO_ROWS + tm]
        buf_ref[HALO_ROWS + tm:, cols] = pe[HALO_ROWS + tm:] * keep_next
        y = buf_ref[HALO_ROWS - pad:HALO_ROWS - pad + tm, cols] * cw_ref[0:1, cols]
        for j in range(1, CONV_K):
            y = y + buf_ref[HALO_ROWS - pad + j:HALO_ROWS - pad + j + tm, cols] * cw_ref[j:j + 1, cols]
        y = _silu(y)
        for hd in range(N_DN_HEADS):
            sl = slice(hd * DN_HEAD_DIM, (hd + 1) * DN_HEAD_DIM)
            yh = y[:, sl]
            if a == 0:
                yh = yh * lax.rsqrt(jnp.sum(yh * yh, axis=-1, keepdims=True) + EPS) * (DN_HEAD_DIM ** -0.5)
            elif a == 1:
                yh = yh * lax.rsqrt(jnp.sum(yh * yh, axis=-1, keepdims=True) + EPS)
            out_ref[:, sl] = yh.astype(BF16)

    n_g = 2 * N_DN_HEADS
    ab = jnp.dot(h, wg_ref[...], preferred_element_type=F32)
    is_g = lax.broadcasted_iota(jnp.int32, ab.shape, 1) < n_g
    gp = gp_ref[...]
    gb_ref[...] = jnp.where(is_g, -jnp.exp(gp[0:1, :]) * jax.nn.softplus(ab + gp[1:2, :]), jax.nn.sigmoid(ab))
    abt = lax.dot_general(wgt_ref[...], h, (((1,), (1,)), ((), ())), preferred_element_type=F32)
    is_gt = lax.broadcasted_iota(jnp.int32, abt.shape, 0) < n_g
    gpt = gpt_ref[...]
    gbt_ref[...] = jnp.where(is_gt, -jnp.exp(gpt[:, 0:1]) * jax.nn.softplus(abt + gpt[:, 1:2]), jax.nn.sigmoid(abt))


def _in_proj(x, mod, norm_w, w_main, w_gate, w_gate_t, gate_par, gate_par_t, rope_tabs, conv_w, blk_rows, seq_lens,
             t1):
    t, d = x.shape
    tm = PROJ_ROWS
    per_blk = blk_rows // tm
    n_main = w_main.shape[1]
    nb1 = t1 // tm
    s1b, s2b = seq_lens[0] // tm, seq_lens[1] // tm
    per_halo = tm // HALO_ROWS
    n_halo = t // HALO_ROWS
    const = lambda i: (0, 0)
    row = lambda i: (i, 0)
    pos = lambda i: (jnp.where(i < nb1, i % s1b, (i - nb1) % s2b), 0)
    cos, sfw, sbk = rope_tabs
    view_specs, view_shapes = [], []
    for _, dil in DILATED_PATTERNS:
        view_specs += [pl.BlockSpec((tm // dil, dil * ATTN_WIDTH), row)] * 3
        view_shapes += [jax.ShapeDtypeStruct((t // dil, dil * ATTN_WIDTH), BF16)] * 3
    outs = pl.pallas_call(
        functools.partial(_inproj_kernel, rows1=t1, len1=seq_lens[0], len2=seq_lens[1]),
        grid=(t // tm,),
        in_specs=[
            pl.BlockSpec((HALO_ROWS, d), lambda i: (jnp.maximum(i * per_halo - 1, 0), 0)),
            pl.BlockSpec((tm, d), row),
            pl.BlockSpec((HALO_ROWS, d), lambda i: (jnp.minimum((i + 1) * per_halo, n_halo - 1), 0)),
            pl.BlockSpec((1, 3, d), lambda i: (i // per_blk, 0, 0)),
            pl.BlockSpec((1, d), const),
            _resident((d, n_main)),
            pl.BlockSpec((d, LANES), const),
            pl.BlockSpec((N_GATES, d), const),
            pl.BlockSpec((2, LANES), const),
            pl.BlockSpec((N_GATES, 2), const),
            pl.BlockSpec((tm, LANES), pos),
            pl.BlockSpec((tm, LANES), pos),
            pl.BlockSpec((tm, LANES), pos),
            pl.BlockSpec((CONV_K, 3 * DN_WIDTH), const),
        ],
        out_specs=view_specs + [pl.BlockSpec((tm, DN_WIDTH), row)] * 4 + [
            pl.BlockSpec((tm, LANES), row),
            pl.BlockSpec((N_GATES, tm), lambda i: (0, i)),
        ],
        out_shape=view_shapes + [jax.ShapeDtypeStruct((t, DN_WIDTH), BF16)] * 4 + [
            jax.ShapeDtypeStruct((t, LANES), F32),
            jax.ShapeDtypeStruct((N_GATES, t), F32),
        ],
        scratch_shapes=[pltpu.VMEM((3 * ATTN_WIDTH // LANES, tm, LANES), F32),
                        pltpu.VMEM((tm + 2 * HALO_ROWS, 3 * DN_WIDTH), F32)],
        compiler_params=_cparams(("arbitrary",)),
        name="in_proj",
    )(x, x, x, mod, norm_w.reshape(1, d), w_main, w_gate, w_gate_t, gate_par, gate_par_t, cos, sfw, sbk, conv_w)
    n_qkv = 3 * len(DILATED_PATTERNS)
    qkv_views = [outs[3 * n:3 * n + 3] for n in range(len(DILATED_PATTERNS))]
    return qkv_views, outs[n_qkv:]


def _rope_tables(max_len):
    half = ROT_HALF
    inv = ROPE_THETA ** (-jnp.arange(half, dtype=F32) / half)
    ang = jnp.arange(max_len, dtype=F32)[:, None] * inv[None, :]
    cos, sin = jnp.cos(ang), jnp.sin(ang)
    ones = jnp.ones((max_len, HEAD_DIM - 2 * half), F32)
    zeros_h = jnp.zeros((max_len, half), F32)
    zeros_r = jnp.zeros((max_len, HEAD_DIM - 2 * half), F32)
    cos_h = jnp.concatenate([cos, cos, ones], axis=1)
    sfw_h = jnp.concatenate([-sin, zeros_h, zeros_r], axis=1)
    sbk_h = jnp.concatenate([zeros_h, sin, zeros_r], axis=1)
    rep = LANES // HEAD_DIM
    return tuple(jnp.tile(a, (1, rep)) for a in (cos_h, sfw_h, sbk_h))


def _attn_kernel(q_ref, kp_ref, km_ref, kn_ref, vp_ref, vm_ref, vn_ref, *rest,
                 dil, next_dil, first, radius, tq, rows1, len1, len2):
    last = next_dil is None
    if not first:
        accp_ref, mlp_ref = rest[:2]
        rest = rest[2:]
    if last:
        o_ref, acc_scr, nat_scr = rest
    else:
        acc_out_ref, ml_out_ref, acc_scr, ml_scr = rest
    tb = q_ref.shape[0]
    n_sub = tb // tq
    tk = tq + 2 * radius
    n_tiles = ATTN_WIDTH // LANES
    n_res = q_ref.shape[1] // ATTN_WIDTH
    c0 = pl.program_id(1) * n_res
    r0 = pl.program_id(0) * tb
    in_first = r0 < rows1
    seq_len = jnp.where(in_first, len1, len2)
    off = jnp.where(in_first, r0, r0 - rows1)
    first_key = jnp.where((off % seq_len) == 0, radius, 0)
    end_key = jnp.where(((off + tb) % seq_len) == 0, tq + radius, tk)

    qi = lax.broadcasted_iota(jnp.int32, (tq, tk), 0)
    kj = lax.broadcasted_iota(jnp.int32, (tq, tk), 1)
    in_band = jnp.abs(kj - radius - qi) <= radius
    biases = []
    for sub in range(n_sub):
        valid = in_band
        if sub == 0:
            valid = valid & (kj >= first_key)
        if sub == n_sub - 1:
            valid = valid & (kj < end_key)
        bias = jnp.where(valid, 0.0, NEG_BIG).astype(F32)
        biases.append(jnp.concatenate([bias, bias], axis=0))

    lane = lax.broadcasted_iota(jnp.int32, (1, LANES), 1)
    low_half = lane < HEAD_DIM
    nt = (((1,), (1,)), ((), ()))

    def pair_col(ref, rows, col):
        return jnp.concatenate([ref[rows, col:col + 1], ref[rows, col + 1:col + 2]], axis=0)

    pairs = []
    for r in range(n_res):
        for j in range(n_tiles):
            sl = slice(r * ATTN_WIDTH + j * LANES, r * ATTN_WIDTH + (j + 1) * LANES)
            k_all = jnp.concatenate([kp_ref[:, sl], km_ref[:, sl], kn_ref[:, sl]], axis=0)
            v_all = jnp.concatenate([vp_ref[:, sl], vm_ref[:, sl], vn_ref[:, sl]], axis=0)
            for sub in range(n_sub):
                rows = slice(sub * tq, (sub + 1) * tq)
                q2 = q_ref[rows, sl]
                zero = jnp.zeros_like(q2)
                qs = jnp.concatenate([jnp.where(low_half, q2, zero), jnp.where(low_half, zero, q2)], axis=0)
                kw = k_all[sub * tq:sub * tq + tk]
                pairs.append(dict(r=r, j=j, sub=sub, sl=sl, rows=rows, vw=v_all[sub * tq:sub * tq + tk],
                                  stats=r * LANES + 2 * j,
                                  s=lax.dot_general(qs, kw, nt, preferred_element_type=F32) + biases[sub]))
    for pr in pairs:
        m_cur = jnp.max(pr["s"], axis=-1, keepdims=True)
        if first:
            pr["m"] = m_cur
        else:
            m_old = pair_col(mlp_ref, pr["rows"], pr["stats"])
            pr["m"] = jnp.maximum(m_old, m_cur)
            pr["alpha"] = jnp.exp(m_old - pr["m"])
    ones = jnp.ones((tk, LANES), BF16)
    for pr in pairs:
        pr["p"] = jnp.exp(pr["s"] - pr["m"]).astype(BF16)
    for pr in pairs:
        pr["pv"] = jnp.dot(pr["p"], pr["vw"], preferred_element_type=F32)
        pr["l"] = jnp.dot(pr["p"], ones, preferred_element_type=F32)
        if not first:
            pr["l"] = pr["alpha"] * pair_col(mlp_ref, pr["rows"], pr["stats"] + N_ATTN_HEADS) + pr["l"]
    for pr in pairs:
        acc = jnp.where(low_half, pr["pv"][:tq], pr["pv"][tq:])
        if not first:
            acc = accp_ref[pr["rows"], pr["sl"]] * jnp.where(low_half, pr["alpha"][:tq], pr["alpha"][tq:]) + acc
        if last:
            acc = acc / jnp.where(low_half, pr["l"][:tq], pr["l"][tq:])
        acc_scr[c0 + pr["r"], pr["j"], pr["rows"], :] = acc
    if not last:
        for r in range(n_res):
            for sub in range(n_sub):
                ml = jnp.zeros((tq, LANES), F32)
                for pr in pairs:
                    if pr["r"] != r or pr["sub"] != sub:
                        continue
                    for half, rows in enumerate((slice(0, tq), slice(tq, 2 * tq))):
                        h = 2 * pr["j"] + half
                        ml = jnp.where(lane == h, pr["m"][rows], ml)
                        ml = jnp.where(lane == N_ATTN_HEADS + h, pr["l"][rows], ml)
                ml_scr[c0 + r, sub * tq:(sub + 1) * tq, :] = ml

    @pl.when(c0 + n_res == dil)
    def _():
        if last:
            for cc in range(dil):
                for j in range(n_tiles):
                    nat_scr[j, pl.ds(cc, tb, stride=dil), :] = acc_scr[cc, j]
            for j in range(n_tiles):
                o_ref[:, j * LANES:(j + 1) * LANES] = nat_scr[j].astype(o_ref.dtype)
        else:
            ratio = next_dil // dil
            rows_out = tb // ratio
            for cc in range(dil):
                for m in range(ratio):
                    cb = m * dil + cc
                    rows = pl.ds(m, rows_out, stride=ratio)
                    for j in range(n_tiles):
                        lo = cb * ATTN_WIDTH + j * LANES
                        acc_out_ref[:, lo:lo + LANES] = acc_scr[cc, j, rows, :]
                    ml_out_ref[:, cb * LANES:(cb + 1) * LANES] = ml_scr[cc, rows, :]


def _attn_pattern(qkv, prev, dil, next_dil, radius, t, t1, seq_lens):
    q, k, v = qkv
    w = ATTN_WIDTH
    first, last = prev is None, next_dil is None
    rows = t // dil
    tq, n_sub, n_res = ATTN_TILE[dil]
    tb = tq * n_sub
    per_q = tb // radius
    n_halo = rows // radius
    n_tiles = w // LANES
    wr = n_res * w
    main = lambda i, c: (i, c)
    before = lambda i, c: (jnp.maximum(i * per_q - 1, 0), c)
    after = lambda i, c: (jnp.minimum((i + 1) * per_q, n_halo - 1), c)
    whole = lambda i, c: (i, 0)
    kv_specs = [pl.BlockSpec((radius, wr), before), pl.BlockSpec((tb, wr), main), pl.BlockSpec((radius, wr), after)]
    in_specs = [pl.BlockSpec((tb, wr), main)] + kv_specs + kv_specs
    args = [q, k, k, k, v, v, v]
    if not first:
        in_specs += [pl.BlockSpec((tb, wr), main), pl.BlockSpec((tb, n_res * LANES), main)]
        args += list(prev)
    scratch = [pltpu.VMEM((dil, n_tiles, tb, LANES), F32)]
    if last:
        out_specs = [pl.BlockSpec((tb * dil, w), whole)]
        out_shape = [jax.ShapeDtypeStruct((t, w), BF16)]
        scratch.append(pltpu.VMEM((n_tiles, tb * dil, LANES), F32))
    else:
        rows_out = tb * dil // next_dil
        out_specs = [pl.BlockSpec((rows_out, next_dil * w), whole), pl.BlockSpec((rows_out, next_dil * LANES), whole)]
        out_shape = [jax.ShapeDtypeStruct((t // next_dil, next_dil * w), F32),
                     jax.ShapeDtypeStruct((t // next_dil, next_dil * LANES), F32)]
        scratch.append(pltpu.VMEM((dil, tb, LANES), F32))
    return pl.pallas_call(
        functools.partial(_attn_kernel, dil=dil, next_dil=next_dil, first=first, radius=radius, tq=tq,
                          rows1=t1 // dil, len1=seq_lens[0] // dil, len2=seq_lens[1] // dil),
        grid=(rows // tb, dil // n_res),
        in_specs=in_specs,
        out_specs=out_specs,
        out_shape=out_shape,
        scratch_shapes=scratch,
        compiler_params=_cparams(("arbitrary", "arbitrary")),
        name=f"attn_d{dil}",
    )(*args)


def _attention(qkv_views, t, t1, seq_lens):
    state = None
    n = len(DILATED_PATTERNS)
    for idx, (window, dil) in enumerate(DILATED_PATTERNS):
        next_dil = DILATED_PATTERNS[idx + 1][1] if idx + 1 < n else None
        state = _attn_pattern(qkv_views[idx], state, dil, next_dil, window // (2 * dil), t, t1, seq_lens)
    return state[0]


def _bdot(a, b):
    return jnp.dot(a.astype(BF16), b.astype(BF16), preferred_element_type=F32)


def _split3(x):
    hi = x.astype(BF16)
    r1 = x - hi.astype(F32)
    mid = r1.astype(BF16)
    lo = (r1 - mid.astype(F32)).astype(BF16)
    return hi, mid, lo


def _dn_kernel(qf_ref, kf_ref, vf_ref, gf_ref, gtf_ref, qb_ref, kb_ref, vb_ref, gb_ref, gtb_ref,
               of_ref, ob_ref, sf_ref, sb_ref, *, c, n_chunks, chunks1, len1, len2):
    n_sub = qf_ref.shape[0] // c
    i = pl.program_id(0)

    def seq_pos(ci):
        in_first = ci < chunks1
        seq_len = jnp.where(in_first, len1, len2)
        off = jnp.where(in_first, ci, ci - chunks1)
        return off % seq_len, seq_len

    pos_f, _ = seq_pos(i * n_sub)
    pos_b, len_b = seq_pos(n_chunks - 1 - i * n_sub)

    @pl.when(pos_f == 0)
    def _():
        sf_ref[...] = jnp.zeros_like(sf_ref)

    @pl.when(pos_b == len_b - 1)
    def _():
        sb_ref[...] = jnp.zeros_like(sb_ref)

    ii = lax.broadcasted_iota(jnp.int32, (c, c), 0)
    jj = lax.broadcasted_iota(jnp.int32, (c, c), 1)
    lower, lower_strict = ii >= jj, ii > jj
    upper, upper_strict = ii <= jj, ii < jj
    eye = jnp.where(ii == jj, 1.0, 0.0).astype(F32)
    tri_l = jnp.where(lower, 1.0, 0.0).astype(F32)
    tri_u = jnp.where(upper, 1.0, 0.0).astype(F32)
    level_masks = [jnp.where(ii // INV_BASE == jj // INV_BASE, 1.0, 0.0).astype(F32)]
    b = INV_BASE
    while b < c:
        same_outer = ii // (2 * b) == jj // (2 * b)
        level_masks.append(jnp.where(same_outer & (ii // b != jj // b), 1.0, 0.0).astype(F32))
        b *= 2
    ones_cc = jnp.ones((c, c), BF16)
    ones_cd = jnp.ones((c, DN_HEAD_DIM), BF16)
    nt = (((1,), (1,)), ((), ()))

    chains = []
    for forward, (q_ref, k_ref, v_ref, g_ref, gt_ref, o_ref, s_ref) in (
            (True, (qf_ref, kf_ref, vf_ref, gf_ref, gtf_ref, of_ref, sf_ref)),
            (False, (qb_ref, kb_ref, vb_ref, gb_ref, gtb_ref, ob_ref, sb_ref))):
        tri_c, tri_r = (tri_l, tri_u) if forward else (tri_u, tri_l)
        col_lhs = jnp.concatenate([tri_c.astype(BF16), ones_cc], axis=0)
        row_rhs = jnp.concatenate([tri_r.astype(BF16), ones_cd], axis=1)
        keep, strict = (lower, lower_strict) if forward else (upper, upper_strict)
        d0 = 0 if forward else N_DN_HEADS
        for order in range(n_sub):
            sub = order if forward else n_sub - 1 - order
            rows = slice(sub * c, (sub + 1) * c)
            g, gt = g_ref[rows, :], gt_ref[:, rows]
            col = sum(jnp.dot(col_lhs, piece, preferred_element_type=F32) for piece in _split3(g))
            row = sum(jnp.dot(piece, row_rhs, preferred_element_type=F32) for piece in _split3(gt))
            cs_col, tot_col = col[:c], col[c:]
            cs_row, tot_row = row[:, :c], row[:, c:]
            for hd in range(N_DN_HEADS):
                gi = d0 + hd
                bi = 2 * N_DN_HEADS + d0 + hd
                sl = slice(hd * DN_HEAD_DIM, (hd + 1) * DN_HEAD_DIM)
                chains.append(dict(
                    order=order, rows=rows, sl=sl, hd=hd, keep=keep, strict=strict, o_ref=o_ref, s_ref=s_ref,
                    q=q_ref[rows, sl], k=k_ref[rows, sl], v=v_ref[rows, sl],
                    gc_col=cs_col[:, gi:gi + 1], gc_row=cs_row[gi:gi + 1, :], beta=g[:, bi:bi + 1],
                    tot_col=tot_col[:, gi:gi + 1], tot_row=tot_row[gi:gi + 1, :]))

    for ch in chains:
        qh, kh = ch["q"], ch["k"]
        both = lax.dot_general(jnp.concatenate([qh, kh], axis=0), kh, nt, preferred_element_type=F32)
        decay = jnp.where(ch["keep"], jnp.exp(jnp.minimum(ch["gc_col"] - ch["gc_row"], 0.0)), 0.0)
        ch["a"] = jnp.where(ch["strict"], ch["beta"] * both[c:] * decay, 0.0)
        ch["qk"] = (both[:c] * decay).astype(BF16)
    for ch in chains:
        dg = ch["a"] * level_masks[0]
        ch["x"] = eye - dg
        ch["y"] = _bdot(dg, dg)
    for ch in chains:
        ch["x"] = ch["x"] + _bdot(ch["x"], ch["y"])
    for off_mask in level_masks[1:]:
        for ch in chains:
            ch["y"] = _bdot(ch["a"] * off_mask, ch["x"])
        for ch in chains:
            ch["x"] = ch["x"] - _bdot(ch["x"], ch["y"])
    for ch in chains:
        kh = ch["k"].astype(F32)
        egc = jnp.exp(ch["gc_col"])
        rhs = jnp.concatenate([ch["v"].astype(F32) * ch["beta"], kh * (ch["beta"] * egc)], axis=1)
        ch["uw"] = _bdot(ch["x"], rhs)
        ch["lhs_q"] = ch["q"].astype(F32) * egc
        ch["k_dec_t"] = jnp.transpose(kh * jnp.exp(ch["tot_col"] - ch["gc_col"])).astype(BF16)
    states = {}
    for order in range(n_sub):
        group = [ch for ch in chains if ch["order"] == order]
        for ch in group:
            key = (id(ch["s_ref"]), ch["hd"])
            ch["state"] = states[key] if order else ch["s_ref"][ch["hd"]]
            lhs = jnp.concatenate([ch["uw"][:, DN_HEAD_DIM:], ch["lhs_q"]], axis=0)
            ch["ws"] = _bdot(lhs, ch["state"])
        for ch in group:
            v_new = (ch["uw"][:, :DN_HEAD_DIM] - ch["ws"][:c]).astype(BF16)
            ch["o_ref"][ch["rows"], ch["sl"]] = ch["ws"][c:] + jnp.dot(ch["qk"], v_new, preferred_element_type=F32)
            ch["v_new"] = v_new
        for ch in group:
            states[(id(ch["s_ref"]), ch["hd"])] = ch["state"] * jnp.exp(ch["tot_row"]) + jnp.dot(
                ch["k_dec_t"], ch["v_new"], preferred_element_type=F32)
    for ch in chains:
        if ch["order"] == n_sub - 1:
            ch["s_ref"][ch["hd"]] = states[(id(ch["s_ref"]), ch["hd"])]


def _deltanet(q, k, v, gates, gates_t, t1, seq_lens):
    t, w = q.shape
    c = DN_CHUNK
    rows = c * DN_CHUNKS_PER_STEP
    n = t // rows
    fwd = lambda i: (i, 0)
    bwd = lambda i: (n - 1 - i, 0)
    fwd_t = lambda i: (0, i)
    bwd_t = lambda i: (0, n - 1 - i)

    def specs(row_map, col_map):
        return [pl.BlockSpec((rows, w), row_map)] * 3 + [
            pl.BlockSpec((rows, LANES), row_map), pl.BlockSpec((N_GATES, rows), col_map)]

    state = pltpu.VMEM((N_DN_HEADS, DN_HEAD_DIM, DN_HEAD_DIM), F32)
    return pl.pallas_call(
        functools.partial(_dn_kernel, c=c, n_chunks=t // c, chunks1=t1 // c, len1=seq_lens[0] // c,
                          len2=seq_lens[1] // c),
        grid=(n,),
        in_specs=specs(fwd, fwd_t) + specs(bwd, bwd_t),
        out_specs=[pl.BlockSpec((rows, w), fwd), pl.BlockSpec((rows, w), bwd)],
        out_shape=[jax.ShapeDtypeStruct((t, w), F32)] * 2,
        scratch_shapes=[state, state],
        compiler_params=_cparams(("arbitrary",)),
        name="deltanet",
    )(q, k, v, gates, gates_t, q, k, v, gates, gates_t)


def kernel(x_prompt, x_sample, c_prompt, c_sample, ada_w, ada_b, norm_ffn1, ffn1_w_gate, ffn1_w_up, ffn1_w_down, norm_mix, w_in, conv_w, a_log, dt_bias, dn_norm, w_out, norm_ffn2, ffn2_w_gate, ffn2_w_up, ffn2_w_down, norm_final):
    b1, s1, d = x_prompt.shape
    b2, s2, _ = x_sample.shape
    depth = ada_w.shape[0]
    t1 = b1 * s1
    seq_lens = (s1, s2)
    assert s2 % s1 == 0 and t1 % s2 == 0, "flat layout needs nested sequence lengths"
    t = t1 + b2 * s2
    xs = [x_prompt.reshape(t1, d), x_sample.reshape(b2 * s2, d)]

    n_seq = b1 + b2
    c_all = jnp.concatenate([c_prompt, c_sample, jnp.zeros((-n_seq % SUBLANES, d), F32)], axis=0)
    mod = _modulation(c_all, ada_w, ada_b)
    blk_seq = jnp.concatenate([jnp.arange(b1), b1 + jnp.repeat(jnp.arange(b2), s2 // s1)])
    mod = mod[:, blk_seq].reshape(depth, blk_seq.shape[0], N_MOD, d)

    rope_tabs = _rope_tables(max(s1, s2))
    n_main = 3 * ATTN_WIDTH + 4 * DN_WIDTH
    gate_par_t = jnp.stack([a_log.astype(F32).reshape(depth, -1), dt_bias.astype(F32).reshape(depth, -1)], axis=2)
    gate_par_t = jnp.pad(gate_par_t, ((0, 0), (0, N_GATES - gate_par_t.shape[1]), (0, 0)))
    gate_par = jnp.pad(jnp.swapaxes(gate_par_t, 1, 2), ((0, 0), (0, 0), (0, LANES - N_GATES)))

    for l in range(depth):
        bf = lambda a: a.astype(BF16)
        x = _ffn(xs if l == 0 else [x], mod[l, :, 0:3], norm_ffn1[l], bf(ffn1_w_gate[l]), bf(ffn1_w_up[l]),
                 bf(ffn1_w_down[l]), s1, t1)
        w_gate_t = bf(w_in[l][:, n_main:]).T
        w_gate = jnp.pad(w_gate_t.T, ((0, 0), (0, LANES - N_GATES)))
        qkv_views, (dq, dk, dv, z, gates, gates_t) = _in_proj(
            x, mod[l, :, 3:6], norm_mix[l], bf(w_in[l][:, :n_main]), w_gate, w_gate_t,
            gate_par[l], gate_par_t[l], rope_tabs, conv_w[l], s1, seq_lens, t1)
        attn = _attention(qkv_views, t, t1, seq_lens)
        o_f, o_b = _deltanet(dq, dk, dv, gates, gates_t, t1, seq_lens)
        mixer = (mod[l, :, 3:6], attn, o_f, o_b, z, dn_norm[l], bf(w_out[l][:ATTN_WIDTH]), bf(w_out[l][ATTN_WIDTH:]))
        is_last = l == depth - 1
        x = _ffn([x], mod[l, :, 6:9], norm_ffn2[l], bf(ffn2_w_gate[l]), bf(ffn2_w_up[l]), bf(ffn2_w_down[l]), s1,
                 t1, final_w=norm_final if is_last else None, split_out=is_last, mixer=mixer)
    return x[0].reshape(b1, s1, d), x[1].reshape(b2, s2, d)
```

```python
import functools

import jax
import jax.numpy as jnp
from jax import lax
from jax.experimental import pallas as pl
from jax.experimental.pallas import tpu as pltpu

F32 = jnp.float32
BF16 = jnp.bfloat16

D_MODEL = 1024
D_FF = 2816
N_MOD = 9
EPS = 1e-6
N_ATTN_HEADS = 8
HEAD_DIM = 64
ATTN_WIDTH = N_ATTN_HEADS * HEAD_DIM
ROT_HALF = HEAD_DIM // 8
ROPE_THETA = 500000.0
DILATED_PATTERNS = ((128, 1), (512, 4), (2048, 16))
N_DN_HEADS = 4
DN_HEAD_DIM = 128
DN_WIDTH = N_DN_HEADS * DN_HEAD_DIM
CONV_K = 5
N_GATES = 4 * N_DN_HEADS

LANES = 128
SUBLANES = 8
VMEM_LIMIT_BYTES = 56 * 1024 * 1024

FFN_ROWS = 512
PROJ_ROWS = 512
ATTN_TILE = {1: (128, 2, 1), 4: (128, 1, 4), 16: (128, 1, 4)}
DN_CHUNK = 128
DN_CHUNKS_PER_STEP = 2
INV_BASE = 4
MOD_COLS = 1152
HALO_ROWS = 8
NEG_BIG = -1e30


def _cparams(sem):
    return pltpu.CompilerParams(dimension_semantics=sem, vmem_limit_bytes=VMEM_LIMIT_BYTES)


def _resident(shape):
    return pl.BlockSpec(shape, lambda *_: (0, 0), pipeline_mode=pl.Buffered(1))


def _silu(x):
    return x * jax.nn.sigmoid(x)


def _modulated_norm(x, norm_w, shift, scale):
    y = x * lax.rsqrt(jnp.mean(x * x, axis=-1, keepdims=True) + EPS) * norm_w
    return y * (1.0 + scale) + shift


def _mod_kernel(c_ref, w_ref, b_ref, o_ref):
    sc = _silu(c_ref[...]).astype(BF16)
    o_ref[0] = jnp.dot(sc, w_ref[0].astype(BF16), preferred_element_type=F32) + b_ref[0]


def _modulation(c_all, ada_w, ada_b):
    depth, d, n = ada_w.shape
    rows = c_all.shape[0]
    return pl.pallas_call(
        _mod_kernel,
        grid=(depth, n // MOD_COLS),
        in_specs=[
            pl.BlockSpec((rows, d), lambda l, j: (0, 0)),
            pl.BlockSpec((1, d, MOD_COLS), lambda l, j: (l, 0, j)),
            pl.BlockSpec((1, 1, MOD_COLS), lambda l, j: (l, 0, j)),
        ],
        out_specs=pl.BlockSpec((1, rows, MOD_COLS), lambda l, j: (l, 0, j)),
        out_shape=jax.ShapeDtypeStruct((depth, rows, n), F32),
        compiler_params=_cparams(("arbitrary", "arbitrary")),
        name="modulation",
    )(c_all, ada_w, ada_b.reshape(depth, 1, n))


def _mixer_residual(x, mod_ref, attn_ref, of_ref, ob_ref, z_ref, nw_ref, wa_ref, wd_ref):
    o = of_ref[...] + ob_ref[...]
    z = z_ref[...].astype(F32)
    nw = nw_ref[...]
    parts = []
    for hd in range(N_DN_HEADS):
        sl = slice(hd * DN_HEAD_DIM, (hd + 1) * DN_HEAD_DIM)
        oh = o[:, sl]
        parts.append(oh * lax.rsqrt(jnp.mean(oh * oh, axis=-1, keepdims=True) + EPS) * nw * _silu(z[:, sl]))
    dn = jnp.concatenate(parts, axis=1).astype(BF16)
    y = jnp.dot(attn_ref[...], wa_ref[...], preferred_element_type=F32)
    y = y + jnp.dot(dn, wd_ref[...], preferred_element_type=F32)
    return x + mod_ref[0, 2:3, :] * y


N_MIXER_REFS = 8


def _ffn_kernel(*refs, n_in, n_out, with_mixer, final_norm, rows1):
    x_refs, refs = refs[:n_in], refs[n_in:]
    if with_mixer:
        mixer_refs, refs = refs[:N_MIXER_REFS], refs[N_MIXER_REFS:]
    mod_ref, nw_ref, wg_ref, wu_ref, wd_ref = refs[:5]
    refs = refs[5:]
    if final_norm:
        fw_ref, refs = refs[0], refs[1:]
    o_refs, refs = refs[:n_out], refs[n_out:]
    tm = x_refs[0].shape[0]
    in_first = pl.program_id(0) < rows1 // tm
    if n_in == 2:
        stage_ref = refs[0]

        @pl.when(in_first)
        def _():
            stage_ref[...] = x_refs[0][...]

        @pl.when(jnp.logical_not(in_first))
        def _():
            stage_ref[...] = x_refs[1][...]

        x = stage_ref[...]
    else:
        x = x_refs[0][...]
    if with_mixer:
        x = _mixer_residual(x, *mixer_refs)
    h = _modulated_norm(x, nw_ref[...], mod_ref[0, 0:1, :], mod_ref[0, 1:2, :]).astype(BF16)
    g = jnp.dot(h, wg_ref[...], preferred_element_type=F32)
    u = jnp.dot(h, wu_ref[...], preferred_element_type=F32)
    a = (_silu(g) * u).astype(BF16)
    y = jnp.dot(a, wd_ref[...], preferred_element_type=F32)
    out = x + (0.5 * mod_ref[0, 2:3, :]) * y
    if final_norm:
        out = out * lax.rsqrt(jnp.mean(out * out, axis=-1, keepdims=True) + EPS) * fw_ref[...]
    if n_out == 2:
        @pl.when(in_first)
        def _():
            o_refs[0][...] = out

        @pl.when(jnp.logical_not(in_first))
        def _():
            o_refs[1][...] = out
    else:
        o_refs[0][...] = out


def _ffn(xs, mod, norm_w, wg, wu, wd, blk_rows, rows1, final_w=None, split_out=False, mixer=None):
    d = xs[0].shape[1]
    t = sum(x.shape[0] for x in xs)
    f = wg.shape[1]
    tm = FFN_ROWS
    per_blk = blk_rows // tm
    nb1 = rows1 // tm
    const = lambda i: (0, 0)
    row = lambda i: (i, 0)
    blk = lambda i: (i // per_blk, 0, 0)
    group1 = lambda i: (jnp.minimum(i, nb1 - 1), 0)
    group2 = lambda i: (jnp.maximum(i - nb1, 0), 0)
    in_specs = [pl.BlockSpec((tm, d), row)] if len(xs) == 1 else [pl.BlockSpec((tm, d), group1), pl.BlockSpec((tm, d), group2)]
    args = list(xs)
    if mixer is not None:
        m_mod, attn, o_f, o_b, z, dn_norm, w_attn, w_dn = mixer
        in_specs += [
            pl.BlockSpec((1, 3, d), blk),
            pl.BlockSpec((tm, ATTN_WIDTH), row),
            pl.BlockSpec((tm, DN_WIDTH), row),
            pl.BlockSpec((tm, DN_WIDTH), row),
            pl.BlockSpec((tm, DN_WIDTH), row),
            pl.BlockSpec((1, DN_HEAD_DIM), const),
            _resident((ATTN_WIDTH, d)),
            _resident((DN_WIDTH, d)),
        ]
        args += [m_mod, attn, o_f, o_b, z, dn_norm.reshape(1, DN_HEAD_DIM), w_attn, w_dn]
    in_specs += [
        pl.BlockSpec((1, 3, d), blk),
        pl.BlockSpec((1, d), const),
        _resident((d, f)),
        _resident((d, f)),
        _resident((f, d)),
    ]
    args += [mod, norm_w.reshape(1, d), wg, wu, wd]
    if final_w is not None:
        in_specs.append(pl.BlockSpec((1, d), const))
        args.append(final_w.reshape(1, d))
    if split_out:
        out_specs = [pl.BlockSpec((tm, d), group1), pl.BlockSpec((tm, d), group2)]
        out_shape = [jax.ShapeDtypeStruct((rows1, d), F32), jax.ShapeDtypeStruct((t - rows1, d), F32)]
    else:
        out_specs = [pl.BlockSpec((tm, d), row)]
        out_shape = [jax.ShapeDtypeStruct((t, d), F32)]
    outs = pl.pallas_call(
        functools.partial(_ffn_kernel, n_in=len(xs), n_out=len(out_specs), with_mixer=mixer is not None,
                          final_norm=final_w is not None, rows1=rows1),
        grid=(t // tm,),
        in_specs=in_specs,
        out_specs=out_specs,
        out_shape=out_shape,
        scratch_shapes=[pltpu.VMEM((tm, d), F32)] if len(xs) == 2 else [],
        compiler_params=_cparams(("arbitrary",)),
        name="ffn",
    )(*args)
    return outs if split_out else outs[0]


def _inproj_kernel(xp_ref, x_ref, xn_ref, mod_ref, nw_ref, w_ref, wg_ref, wgt_ref, gp_ref, gpt_ref,
                   cos_ref, sfw_ref, sbk_ref, cw_ref, *rest, rows1, len1, len2):
    n_views = len(DILATED_PATTERNS)
    qkv_refs = [rest[3 * n:3 * n + 3] for n in range(n_views)]
    dq_ref, dk_ref, dv_ref, z_ref, gb_ref, gbt_ref, stage_ref, buf_ref = rest[3 * n_views:]
    tm = x_ref.shape[0]
    nw, shift, scale = nw_ref[...], mod_ref[0, 0:1, :], mod_ref[0, 1:2, :]
    h_main = _modulated_norm(x_ref[...], nw, shift, scale)
    h_ext = jnp.concatenate([_modulated_norm(xp_ref[...], nw, shift, scale), h_main,
                             _modulated_norm(xn_ref[...], nw, shift, scale)], axis=0).astype(BF16)
    h = h_main.astype(BF16)
    cos, sfw, sbk = cos_ref[...], sfw_ref[...], sbk_ref[...]

    def rope(xp):
        return xp * cos + pltpu.roll(xp, LANES - ROT_HALF, 1) * sfw + pltpu.roll(xp, ROT_HALF, 1) * sbk

    n_tiles = ATTN_WIDTH // LANES
    o0 = 3 * ATTN_WIDTH

    def attn_epilogue(a, pa):
        for j in range(n_tiles):
            col = pa[:, j * LANES:(j + 1) * LANES]
            if a == 0:
                col = rope(col) * (HEAD_DIM ** -0.5)
            elif a == 1:
                col = rope(col)
            stage_ref[a * n_tiles + j] = col
        for n, (_, dil) in enumerate(DILATED_PATTERNS):
            rows = tm // dil
            for j in range(n_tiles):
                for c in range(dil):
                    lo = c * ATTN_WIDTH + j * LANES
                    src = stage_ref[a * n_tiles + j, pl.ds(c, rows, stride=dil), :] if dil > 1 else stage_ref[a * n_tiles + j]
                    qkv_refs[n][a][:, lo:lo + LANES] = src.astype(BF16)

    r0 = pl.program_id(0) * tm
    in_first = r0 < rows1
    seq_len = jnp.where(in_first, len1, len2)
    off = jnp.where(in_first, r0, r0 - rows1)
    keep_prev = jnp.where((off % seq_len) == 0, 0.0, 1.0).astype(F32)
    keep_next = jnp.where(((off + tm) % seq_len) == 0, 0.0, 1.0).astype(F32)
    pad = CONV_K // 2

    def dn_epilogue(a, pe):
        out_ref = (dq_ref, dk_ref, dv_ref)[a]
        cols = slice(a * DN_WIDTH, (a + 1) * DN_WIDTH)
        buf_ref[0:HALO_ROWS, cols] = pe[0:HALO_ROWS] * keep_prev
        buf_ref[HALO_ROWS:HALO_ROWS + tm, cols] = pe[HALO_ROWS:HALO_ROWS + tm]
        buf_ref[HALO_ROWS + tm:, cols] = pe[HALO_ROWS + tm:] * keep_next
        y = buf_ref[HALO_ROWS - pad:HALO_ROWS - pad + tm, cols] * cw_ref[0:1, cols]
        for j in range(1, CONV_K):
            y = y + buf_ref[HALO_ROWS - pad + j:HALO_ROWS - pad + j + tm, cols] * cw_ref[j:j + 1, cols]
        y = _silu(y)
        for hd in range(N_DN_HEADS):
            sl = slice(hd * DN_HEAD_DIM, (hd + 1) * DN_HEAD_DIM)
            yh = y[:, sl]
            if a == 0:
                yh = yh * lax.rsqrt(jnp.sum(yh * yh, axis=-1, keepdims=True) + EPS) * (DN_HEAD_DIM ** -0.5)
            elif a == 1:
                yh = yh * lax.rsqrt(jnp.sum(yh * yh, axis=-1, keepdims=True) + EPS)
            out_ref[:, sl] = yh.astype(BF16)

    def z_epilogue(_, pz):
        z_ref[...] = pz.astype(BF16)

    groups = [(h, a * ATTN_WIDTH, ATTN_WIDTH, attn_epilogue, a) for a in range(3)]
    groups += [(h_ext, o0 + a * DN_WIDTH, DN_WIDTH, dn_epilogue, a) for a in range(3)]
    groups.append((h, o0 + 3 * DN_WIDTH, DN_WIDTH, z_epilogue, None))

    def project(group):
        lhs, lo, width = group[:3]
        return jnp.dot(lhs, w_ref[:, lo:lo + width], preferred_element_type=F32)

    ahead = project(groups[0])
    for n, group in enumerate(groups):
        current = ahead
        if n + 1 < len(groups):
            ahead = project(groups[n + 1])
        group[3](group[4], current)

    n_g = 2 * N_DN_HEADS
    ab = jnp.dot(h, wg_ref[...], preferred_element_type=F32)
    is_g = lax.broadcasted_iota(jnp.int32, ab.shape, 1) < n_g
    gp = gp_ref[...]
    gb_ref[...] = jnp.where(is_g, -jnp.exp(gp[0:1, :]) * jax.nn.softplus(ab + gp[1:2, :]), jax.nn.sigmoid(ab))
    abt = lax.dot_general(wgt_ref[...], h, (((1,), (1,)), ((), ())), preferred_element_type=F32)
    is_gt = lax.broadcasted_iota(jnp.int32, abt.shape, 0) < n_g
    gpt = gpt_ref[...]
    gbt_ref[...] = jnp.where(is_gt, -jnp.exp(gpt[:, 0:1]) * jax.nn.softplus(abt + gpt[:, 1:2]), jax.nn.sigmoid(abt))


def _in_proj(x, mod, norm_w, w_main, w_gate, w_gate_t, gate_par, gate_par_t, rope_tabs, conv_w, blk_rows, seq_lens,
             t1):
    t, d = x.shape
    tm = PROJ_ROWS
    per_blk = blk_rows // tm
    n_main = w_main.shape[1]
    nb1 = t1 // tm
    s1b, s2b = seq_lens[0] // tm, seq_lens[1] // tm
    per_halo = tm // HALO_ROWS
    n_halo = t // HALO_ROWS
    const = lambda i: (0, 0)
    row = lambda i: (i, 0)
    pos = lambda i: (jnp.where(i < nb1, i % s1b, (i - nb1) % s2b), 0)
    cos, sfw, sbk = rope_tabs
    view_specs, view_shapes = [], []
    for _, dil in DILATED_PATTERNS:
        view_specs += [pl.BlockSpec((tm // dil, dil * ATTN_WIDTH), row)] * 3
        view_shapes += [jax.ShapeDtypeStruct((t // dil, dil * ATTN_WIDTH), BF16)] * 3
    outs = pl.pallas_call(
        functools.partial(_inproj_kernel, rows1=t1, len1=seq_lens[0], len2=seq_lens[1]),
        grid=(t // tm,),
        in_specs=[
            pl.BlockSpec((HALO_ROWS, d), lambda i: (jnp.maximum(i * per_halo - 1, 0), 0)),
            pl.BlockSpec((tm, d), row),
            pl.BlockSpec((HALO_ROWS, d), lambda i: (jnp.minimum((i + 1) * per_halo, n_halo - 1), 0)),
            pl.BlockSpec((1, 3, d), lambda i: (i // per_blk, 0, 0)),
            pl.BlockSpec((1, d), const),
            _resident((d, n_main)),
            pl.BlockSpec((d, LANES), const),
            pl.BlockSpec((N_GATES, d), const),
            pl.BlockSpec((2, LANES), const),
            pl.BlockSpec((N_GATES, 2), const),
            pl.BlockSpec((tm, LANES), pos),
            pl.BlockSpec((tm, LANES), pos),
            pl.BlockSpec((tm, LANES), pos),
            pl.BlockSpec((CONV_K, 3 * DN_WIDTH), const),
        ],
        out_specs=view_specs + [pl.BlockSpec((tm, DN_WIDTH), row)] * 4 + [
            pl.BlockSpec((tm, LANES), row),
            pl.BlockSpec((N_GATES, tm), lambda i: (0, i)),
        ],
        out_shape=view_shapes + [jax.ShapeDtypeStruct((t, DN_WIDTH), BF16)] * 4 + [
            jax.ShapeDtypeStruct((t, LANES), F32),
            jax.ShapeDtypeStruct((N_GATES, t), F32),
        ],
        scratch_shapes=[pltpu.VMEM((3 * ATTN_WIDTH // LANES, tm, LANES), F32),
                        pltpu.VMEM((tm + 2 * HALO_ROWS, 3 * DN_WIDTH), F32)],
        compiler_params=_cparams(("arbitrary",)),
        name="in_proj",
    )(x, x, x, mod, norm_w.reshape(1, d), w_main, w_gate, w_gate_t, gate_par, gate_par_t, cos, sfw, sbk, conv_w)
    n_qkv = 3 * len(DILATED_PATTERNS)
    qkv_views = [outs[3 * n:3 * n + 3] for n in range(len(DILATED_PATTERNS))]
    return qkv_views, outs[n_qkv:]


def _rope_tables(max_len):
    half = ROT_HALF
    inv = ROPE_THETA ** (-jnp.arange(half, dtype=F32) / half)
    ang = jnp.arange(max_len, dtype=F32)[:, None] * inv[None, :]
    cos, sin = jnp.cos(ang), jnp.sin(ang)
    ones = jnp.ones((max_len, HEAD_DIM - 2 * half), F32)
    zeros_h = jnp.zeros((max_len, half), F32)
    zeros_r = jnp.zeros((max_len, HEAD_DIM - 2 * half), F32)
    cos_h = jnp.concatenate([cos, cos, ones], axis=1)
    sfw_h = jnp.concatenate([-sin, zeros_h, zeros_r], axis=1)
    sbk_h = jnp.concatenate([zeros_h, sin, zeros_r], axis=1)
    rep = LANES // HEAD_DIM
    return tuple(jnp.tile(a, (1, rep)) for a in (cos_h, sfw_h, sbk_h))


def _attn_kernel(q_ref, kp_ref, km_ref, kn_ref, vp_ref, vm_ref, vn_ref, *rest,
                 dil, next_dil, first, radius, tq, rows1, len1, len2):
    last = next_dil is None
    if not first:
        op_ref, wp_ref = rest[:2]
        rest = rest[2:]
    if last:
        o_ref, acc_scr, nat_scr = rest
    else:
        acc_out_ref, ml_out_ref, acc_scr, ml_scr = rest
    tb = q_ref.shape[0]
    n_sub = tb // tq
    tk = tq + 2 * radius
    n_tiles = ATTN_WIDTH // LANES
    n_res = q_ref.shape[1] // ATTN_WIDTH
    c0 = pl.program_id(1) * n_res
    r0 = pl.program_id(0) * tb
    in_first = r0 < rows1
    seq_len = jnp.where(in_first, len1, len2)
    off = jnp.where(in_first, r0, r0 - rows1)
    first_key = jnp.where((off % seq_len) == 0, radius, 0)
    end_key = jnp.where(((off + tb) % seq_len) == 0, tq + radius, tk)

    qi = lax.broadcasted_iota(jnp.int32, (tq, tk), 0)
    kj = lax.broadcasted_iota(jnp.int32, (tq, tk), 1)
    in_band = jnp.abs(kj - radius - qi) <= radius
    biases = []
    for sub in range(n_sub):
        valid = in_band
        if sub == 0:
            valid = valid & (kj >= first_key)
        if sub == n_sub - 1:
            valid = valid & (kj < end_key)
        bias = jnp.where(valid, 0.0, NEG_BIG).astype(F32)
        biases.append(jnp.concatenate([bias, bias], axis=0))

    lane = lax.broadcasted_iota(jnp.int32, (1, LANES), 1)
    low_half = lane < HEAD_DIM
    nt = (((1,), (1,)), ((), ()))

    def unstack(x):
        return jnp.where(low_half, x[:tq], x[tq:])

    if not first:
        e_row = lax.broadcasted_iota(jnp.int32, (LANES, ATTN_WIDTH), 0)
        e_col = lax.broadcasted_iota(jnp.int32, (LANES, ATTN_WIDTH), 1)
        spread = jnp.where(e_row == e_col // LANES + HEAD_DIM * ((e_col % LANES) // HEAD_DIM), 1.0, 0.0).astype(BF16)
        lse_old = {}
        for r in range(n_res):
            for sub in range(n_sub):
                wc = wp_ref[sub * tq:(sub + 1) * tq, r * LANES:(r + 1) * LANES]
                hi = wc.astype(BF16)
                lo = (wc - hi.astype(F32)).astype(BF16)
                ex = jnp.dot(jnp.concatenate([hi, lo], axis=0), spread, preferred_element_type=F32)
                lse_old[(r, sub)] = ex[:tq] + ex[tq:]

    pairs = []
    for r in range(n_res):
        for j in range(n_tiles):
            sl = slice(r * ATTN_WIDTH + j * LANES, r * ATTN_WIDTH + (j + 1) * LANES)
            k_all = jnp.concatenate([kp_ref[:, sl], km_ref[:, sl], kn_ref[:, sl]], axis=0)
            v_all = jnp.concatenate([vp_ref[:, sl], vm_ref[:, sl], vn_ref[:, sl]], axis=0)
            for sub in range(n_sub):
                rows = slice(sub * tq, (sub + 1) * tq)
                q2 = q_ref[rows, sl]
                zero = jnp.zeros_like(q2)
                qs = jnp.concatenate([jnp.where(low_half, q2, zero), jnp.where(low_half, zero, q2)], axis=0)
                kw = k_all[sub * tq:sub * tq + tk]
                pairs.append(dict(r=r, j=j, sub=sub, sl=sl, rows=rows, vw=v_all[sub * tq:sub * tq + tk],
                                  s=lax.dot_general(qs, kw, nt, preferred_element_type=F32) + biases[sub]))
    ones = jnp.ones((tk, LANES), BF16)
    for pr in pairs:
        m_rows = jnp.broadcast_to(jnp.max(pr["s"], axis=-1, keepdims=True), (2 * tq, LANES))
        pr["m"] = unstack(m_rows)
        pr["p"] = jnp.exp(pr["s"] - jnp.concatenate([m_rows] * (tk // LANES), axis=1)).astype(BF16)
    for pr in pairs:
        pr["pv"] = jnp.dot(pr["p"], pr["vw"], preferred_element_type=F32)
        pr["l"] = jnp.dot(pr["p"], ones, preferred_element_type=F32)
    for pr in pairs:
        den = unstack(pr["l"])
        o = unstack(pr["pv"]) / den
        lse = pr["m"] + jnp.log(den)
        if not first:
            lse_prev = lse_old[(pr["r"], pr["sub"])][:, pr["j"] * LANES:(pr["j"] + 1) * LANES]
            top = jnp.maximum(lse_prev, lse)
            w_prev, w_cur = jnp.exp(lse_prev - top), jnp.exp(lse - top)
            tot = w_prev + w_cur
            o = (w_prev * op_ref[pr["rows"], pr["sl"]] + w_cur * o) / tot
            lse = top + jnp.log(tot)
        pr["lse"] = lse
        acc_scr[c0 + pr["r"], pr["j"], pr["rows"], :] = o
    if not last:
        for r in range(n_res):
            for sub in range(n_sub):
                wc = jnp.zeros((tq, LANES), F32)
                for pr in pairs:
                    if pr["r"] == r and pr["sub"] == sub:
                        keep = jnp.logical_or(lane == pr["j"], lane == HEAD_DIM + pr["j"])
                        wc = jnp.where(keep, pr["lse"], wc)
                ml_scr[c0 + r, sub * tq:(sub + 1) * tq, :] = wc

    @pl.when(c0 + n_res == dil)
    def _():
        if last:
            for cc in range(dil):
                for j in range(n_tiles):
                    nat_scr[j, pl.ds(cc, tb, stride=dil), :] = acc_scr[cc, j]
            for j in range(n_tiles):
                o_ref[:, j * LANES:(j + 1) * LANES] = nat_scr[j].astype(o_ref.dtype)
        else:
            ratio = next_dil // dil
            rows_out = tb // ratio
            for cc in range(dil):
                for m in range(ratio):
                    cb = m * dil + cc
                    rows = pl.ds(m, rows_out, stride=ratio)
                    for j in range(n_tiles):
                        lo = cb * ATTN_WIDTH + j * LANES
                        acc_out_ref[:, lo:lo + LANES] = acc_scr[cc, j, rows, :]
                    ml_out_ref[:, cb * LANES:(cb + 1) * LANES] = ml_scr[cc, rows, :]


def _attn_pattern(qkv, prev, dil, next_dil, radius, t, t1, seq_lens):
    q, k, v = qkv
    w = ATTN_WIDTH
    first, last = prev is None, next_dil is None
    rows = t // dil
    tq, n_sub, n_res = ATTN_TILE[dil]
    tb = tq * n_sub
    per_q = tb // radius
    n_halo = rows // radius
    n_tiles = w // LANES
    wr = n_res * w
    main = lambda i, c: (i, c)
    before = lambda i, c: (jnp.maximum(i * per_q - 1, 0), c)
    after = lambda i, c: (jnp.minimum((i + 1) * per_q, n_halo - 1), c)
    whole = lambda i, c: (i, 0)
    kv_specs = [pl.BlockSpec((radius, wr), before), pl.BlockSpec((tb, wr), main), pl.BlockSpec((radius, wr), after)]
    in_specs = [pl.BlockSpec((tb, wr), main)] + kv_specs + kv_specs
    args = [q, k, k, k, v, v, v]
    if not first:
        in_specs += [pl.BlockSpec((tb, wr), main), pl.BlockSpec((tb, n_res * LANES), main)]
        args += list(prev)
    scratch = [pltpu.VMEM((dil, n_tiles, tb, LANES), F32)]
    if last:
        out_specs = [pl.BlockSpec((tb * dil, w), whole)]
        out_shape = [jax.ShapeDtypeStruct((t, w), BF16)]
        scratch.append(pltpu.VMEM((n_tiles, tb * dil, LANES), F32))
    else:
        rows_out = tb * dil // next_dil
        out_specs = [pl.BlockSpec((rows_out, next_dil * w), whole), pl.BlockSpec((rows_out, next_dil * LANES), whole)]
        out_shape = [jax.ShapeDtypeStruct((t // next_dil, next_dil * w), F32),
                     jax.ShapeDtypeStruct((t // next_dil, next_dil * LANES), F32)]
        scratch.append(pltpu.VMEM((dil, tb, LANES), F32))
    return pl.pallas_call(
        functools.partial(_attn_kernel, dil=dil, next_dil=next_dil, first=first, radius=radius, tq=tq,
                          rows1=t1 // dil, len1=seq_lens[0] // dil, len2=seq_lens[1] // dil),
        grid=(rows // tb, dil // n_res),
        in_specs=in_specs,
        out_specs=out_specs,
        out_shape=out_shape,
        scratch_shapes=scratch,
        compiler_params=_cparams(("arbitrary", "arbitrary")),
        name=f"attn_d{dil}",
    )(*args)


def _attention(qkv_views, t, t1, seq_lens):
    state = None
    n = len(DILATED_PATTERNS)
    for idx, (window, dil) in enumerate(DILATED_PATTERNS):
        next_dil = DILATED_PATTERNS[idx + 1][1] if idx + 1 < n else None
        state = _attn_pattern(qkv_views[idx], state, dil, next_dil, window // (2 * dil), t, t1, seq_lens)
    return state[0]


def _bdot(a, b):
    return jnp.dot(a.astype(BF16), b.astype(BF16), preferred_element_type=F32)


def _split3(x):
    hi = x.astype(BF16)
    r1 = x - hi.astype(F32)
    mid = r1.astype(BF16)
    lo = (r1 - mid.astype(F32)).astype(BF16)
    return hi, mid, lo


def _dn_kernel(qf_ref, kf_ref, vf_ref, gf_ref, gtf_ref, qb_ref, kb_ref, vb_ref, gb_ref, gtb_ref,
               of_ref, ob_ref, sf_ref, sb_ref, *, c, n_chunks, chunks1, len1, len2):
    n_sub = qf_ref.shape[0] // c
    i = pl.program_id(0)

    def seq_pos(ci):
        in_first = ci < chunks1
        seq_len = jnp.where(in_first, len1, len2)
        off = jnp.where(in_first, ci, ci - chunks1)
        return off % seq_len, seq_len

    pos_f, _ = seq_pos(i * n_sub)
    pos_b, len_b = seq_pos(n_chunks - 1 - i * n_sub)

    @pl.when(pos_f == 0)
    def _():
        sf_ref[...] = jnp.zeros_like(sf_ref)

    @pl.when(pos_b == len_b - 1)
    def _():
        sb_ref[...] = jnp.zeros_like(sb_ref)

    ii = lax.broadcasted_iota(jnp.int32, (c, c), 0)
    jj = lax.broadcasted_iota(jnp.int32, (c, c), 1)
    lower, lower_strict = ii >= jj, ii > jj
    upper, upper_strict = ii <= jj, ii < jj
    eye = jnp.where(ii == jj, 1.0, 0.0).astype(F32)
    tri_l = jnp.where(lower, 1.0, 0.0).astype(F32)
    tri_u = jnp.where(upper, 1.0, 0.0).astype(F32)
    level_masks = [jnp.where(ii // INV_BASE == jj // INV_BASE, 1.0, 0.0).astype(F32)]
    b = INV_BASE
    while b < c:
        same_outer = ii // (2 * b) == jj // (2 * b)
        level_masks.append(jnp.where(same_outer & (ii // b != jj // b), 1.0, 0.0).astype(F32))
        b *= 2
    ones_cc = jnp.ones((c, c), BF16)
    ones_cd = jnp.ones((c, DN_HEAD_DIM), BF16)
    nt = (((1,), (1,)), ((), ()))

    chains = []
    for forward, (q_ref, k_ref, v_ref, g_ref, gt_ref, o_ref, s_ref) in (
            (True, (qf_ref, kf_ref, vf_ref, gf_ref, gtf_ref, of_ref, sf_ref)),
            (False, (qb_ref, kb_ref, vb_ref, gb_ref, gtb_ref, ob_ref, sb_ref))):
        tri_c, tri_r = (tri_l, tri_u) if forward else (tri_u, tri_l)
        col_lhs = jnp.concatenate([tri_c.astype(BF16), ones_cc], axis=0)
        row_rhs = jnp.concatenate([tri_r.astype(BF16), ones_cd], axis=1)
        keep, strict = (lower, lower_strict) if forward else (upper, upper_strict)
        d0 = 0 if forward else N_DN_HEADS
        for order in range(n_sub):
            sub = order if forward else n_sub - 1 - order
            rows = slice(sub * c, (sub + 1) * c)
            g, gt = g_ref[rows, :], gt_ref[:, rows]
            col = sum(jnp.dot(col_lhs, piece, preferred_element_type=F32) for piece in _split3(g))
            row = sum(jnp.dot(piece, row_rhs, preferred_element_type=F32) for piece in _split3(gt))
            cs_col, tot_col = col[:c], col[c:]
            cs_row, tot_row = row[:, :c], row[:, c:]
            for hd in range(N_DN_HEADS):
                gi = d0 + hd
                bi = 2 * N_DN_HEADS + d0 + hd
                sl = slice(hd * DN_HEAD_DIM, (hd + 1) * DN_HEAD_DIM)
                chains.append(dict(
                    order=order, rows=rows, sl=sl, hd=hd, keep=keep, strict=strict, o_ref=o_ref, s_ref=s_ref,
                    q=q_ref[rows, sl], k=k_ref[rows, sl], v=v_ref[rows, sl],
                    gc_col=cs_col[:, gi:gi + 1], gc_row=cs_row[gi:gi + 1, :], beta=g[:, bi:bi + 1],
                    tot_col=tot_col[:, gi:gi + 1], tot_row=tot_row[gi:gi + 1, :]))

    for ch in chains:
        qh, kh = ch["q"], ch["k"]
        both = lax.dot_general(jnp.concatenate([qh, kh], axis=0), kh, nt, preferred_element_type=F32)
        decay = jnp.where(ch["keep"], jnp.exp(jnp.minimum(ch["gc_col"] - ch["gc_row"], 0.0)), 0.0)
        ch["a"] = jnp.where(ch["strict"], ch["beta"] * both[c:] * decay, 0.0)
        ch["qk"] = (both[:c] * decay).astype(BF16)
    for ch in chains:
        dg = ch["a"] * level_masks[0]
        ch["x"] = eye - dg
        ch["y"] = _bdot(dg, dg)
    for ch in chains:
        ch["x"] = ch["x"] + _bdot(ch["x"], ch["y"])
    for off_mask in level_masks[1:]:
        for ch in chains:
            ch["y"] = _bdot(ch["a"] * off_mask, ch["x"])
        for ch in chains:
            ch["x"] = ch["x"] - _bdot(ch["x"], ch["y"])
    for ch in chains:
        kh = ch["k"].astype(F32)
        egc = jnp.exp(ch["gc_col"])
        rhs = jnp.concatenate([ch["v"].astype(F32) * ch["beta"], kh * (ch["beta"] * egc)], axis=1)
        ch["uw"] = _bdot(ch["x"], rhs)
        ch["lhs_q"] = ch["q"].astype(F32) * egc
        ch["k_dec_t"] = jnp.transpose(kh * jnp.exp(ch["tot_col"] - ch["gc_col"])).astype(BF16)
    states = {}
    for order in range(n_sub):
        group = [ch for ch in chains if ch["order"] == order]
        for ch in group:
            key = (id(ch["s_ref"]), ch["hd"])
            ch["state"] = states[key] if order else ch["s_ref"][ch["hd"]]
            lhs = jnp.concatenate([ch["uw"][:, DN_HEAD_DIM:], ch["lhs_q"]], axis=0)
            ch["ws"] = _bdot(lhs, ch["state"])
        for ch in group:
            v_new = (ch["uw"][:, :DN_HEAD_DIM] - ch["ws"][:c]).astype(BF16)
            ch["o_ref"][ch["rows"], ch["sl"]] = ch["ws"][c:] + jnp.dot(ch["qk"], v_new, preferred_element_type=F32)
            ch["v_new"] = v_new
        for ch in group:
            states[(id(ch["s_ref"]), ch["hd"])] = ch["state"] * jnp.exp(ch["tot_row"]) + jnp.dot(
                ch["k_dec_t"], ch["v_new"], preferred_element_type=F32)
    for ch in chains:
        if ch["order"] == n_sub - 1:
            ch["s_ref"][ch["hd"]] = states[(id(ch["s_ref"]), ch["hd"])]


def _deltanet(q, k, v, gates, gates_t, t1, seq_lens):
    t, w = q.shape
    c = DN_CHUNK
    rows = c * DN_CHUNKS_PER_STEP
    n = t // rows
    fwd = lambda i: (i, 0)
    bwd = lambda i: (n - 1 - i, 0)
    fwd_t = lambda i: (0, i)
    bwd_t = lambda i: (0, n - 1 - i)

    def specs(row_map, col_map):
        return [pl.BlockSpec((rows, w), row_map)] * 3 + [
            pl.BlockSpec((rows, LANES), row_map), pl.BlockSpec((N_GATES, rows), col_map)]

    state = pltpu.VMEM((N_DN_HEADS, DN_HEAD_DIM, DN_HEAD_DIM), F32)
    return pl.pallas_call(
        functools.partial(_dn_kernel, c=c, n_chunks=t // c, chunks1=t1 // c, len1=seq_lens[0] // c,
                          len2=seq_lens[1] // c),
        grid=(n,),
        in_specs=specs(fwd, fwd_t) + specs(bwd, bwd_t),
        out_specs=[pl.BlockSpec((rows, w), fwd), pl.BlockSpec((rows, w), bwd)],
        out_shape=[jax.ShapeDtypeStruct((t, w), F32)] * 2,
        scratch_shapes=[state, state],
        compiler_params=_cparams(("arbitrary",)),
        name="deltanet",
    )(q, k, v, gates, gates_t, q, k, v, gates, gates_t)


def kernel(x_prompt, x_sample, c_prompt, c_sample, ada_w, ada_b, norm_ffn1, ffn1_w_gate, ffn1_w_up, ffn1_w_down, norm_mix, w_in, conv_w, a_log, dt_bias, dn_norm, w_out, norm_ffn2, ffn2_w_gate, ffn2_w_up, ffn2_w_down, norm_final):
    b1, s1, d = x_prompt.shape
    b2, s2, _ = x_sample.shape
    depth = ada_w.shape[0]
    t1 = b1 * s1
    seq_lens = (s1, s2)
    assert s2 % s1 == 0 and t1 % s2 == 0, "flat layout needs nested sequence lengths"
    t = t1 + b2 * s2
    xs = [x_prompt.reshape(t1, d), x_sample.reshape(b2 * s2, d)]

    n_seq = b1 + b2
    c_all = jnp.concatenate([c_prompt, c_sample, jnp.zeros((-n_seq % SUBLANES, d), F32)], axis=0)
    mod = _modulation(c_all, ada_w, ada_b)
    blk_seq = jnp.concatenate([jnp.arange(b1), b1 + jnp.repeat(jnp.arange(b2), s2 // s1)])
    mod = mod[:, blk_seq].reshape(depth, blk_seq.shape[0], N_MOD, d)

    rope_tabs = _rope_tables(max(s1, s2))
    n_main = 3 * ATTN_WIDTH + 4 * DN_WIDTH
    gate_par_t = jnp.stack([a_log.astype(F32).reshape(depth, -1), dt_bias.astype(F32).reshape(depth, -1)], axis=2)
    gate_par_t = jnp.pad(gate_par_t, ((0, 0), (0, N_GATES - gate_par_t.shape[1]), (0, 0)))
    gate_par = jnp.pad(jnp.swapaxes(gate_par_t, 1, 2), ((0, 0), (0, 0), (0, LANES - N_GATES)))

    for l in range(depth):
        bf = lambda a: a.astype(BF16)
        x = _ffn(xs if l == 0 else [x], mod[l, :, 0:3], norm_ffn1[l], bf(ffn1_w_gate[l]), bf(ffn1_w_up[l]),
                 bf(ffn1_w_down[l]), s1, t1)
        w_gate_t = bf(w_in[l][:, n_main:]).T
        w_gate = jnp.pad(w_gate_t.T, ((0, 0), (0, LANES - N_GATES)))
        qkv_views, (dq, dk, dv, z, gates, gates_t) = _in_proj(
            x, mod[l, :, 3:6], norm_mix[l], bf(w_in[l][:, :n_main]), w_gate, w_gate_t,
            gate_par[l], gate_par_t[l], rope_tabs, conv_w[l], s1, seq_lens, t1)
        attn = _attention(qkv_views, t, t1, seq_lens)
        o_f, o_b = _deltanet(dq, dk, dv, gates, gates_t, t1, seq_lens)
        mixer = (mod[l, :, 3:6], attn, o_f, o_b, z, dn_norm[l], bf(w_out[l][:ATTN_WIDTH]), bf(w_out[l][ATTN_WIDTH:]))
        is_last = l == depth - 1
        x = _ffn([x], mod[l, :, 6:9], norm_ffn2[l], bf(ffn2_w_gate[l]), bf(ffn2_w_up[l]), bf(ffn2_w_down[l]), s1,
                 t1, final_w=norm_final if is_last else None, split_out=is_last, mixer=mixer)
    return x[0].reshape(b1, s1, d), x[1].reshape(b2, s2, d)
```

```python
import functools

import jax
import jax.numpy as jnp
from jax import lax
from jax.experimental import pallas as pl
from jax.experimental.pallas import tpu as pltpu

F32 = jnp.float32
BF16 = jnp.bfloat16

D_MODEL = 1024
D_FF = 2816
N_MOD = 9
EPS = 1e-6
N_ATTN_HEADS = 8
HEAD_DIM = 64
ATTN_WIDTH = N_ATTN_HEADS * HEAD_DIM
ROT_HALF = HEAD_DIM // 8
ROPE_THETA = 500000.0
DILATED_PATTERNS = ((128, 1), (512, 4), (2048, 16))
N_DN_HEADS = 4
DN_HEAD_DIM = 128
DN_WIDTH = N_DN_HEADS * DN_HEAD_DIM
CONV_K = 5
N_GATES = 4 * N_DN_HEADS

LANES = 128
SUBLANES = 8
VMEM_LIMIT_BYTES = 56 * 1024 * 1024

FFN_ROWS = 512
PROJ_ROWS = 512
ATTN_TILE = {1: (128, 4, 1), 4: (128, 1, 4), 16: (128, 1, 4)}
DN_CHUNK = 128
DN_CHUNKS_PER_STEP = 2
INV_BASE = 4
MOD_COLS = 1152
HALO_ROWS = 8
NEG_BIG = -1e30


def _cparams(sem):
    return pltpu.CompilerParams(dimension_semantics=sem, vmem_limit_bytes=VMEM_LIMIT_BYTES)


def _resident(shape):
    return pl.BlockSpec(shape, lambda *_: (0, 0), pipeline_mode=pl.Buffered(1))


def _silu(x):
    return x * jax.nn.sigmoid(x)


def _modulated_norm(x, norm_w, shift, scale):
    y = x * lax.rsqrt(jnp.mean(x * x, axis=-1, keepdims=True) + EPS) * norm_w
    return y * (1.0 + scale) + shift


def _mod_kernel(c_ref, w_ref, b_ref, o_ref):
    sc = _silu(c_ref[...]).astype(BF16)
    o_ref[0] = jnp.dot(sc, w_ref[0].astype(BF16), preferred_element_type=F32) + b_ref[0]


def _modulation(c_all, ada_w, ada_b):
    depth, d, n = ada_w.shape
    rows = c_all.shape[0]
    return pl.pallas_call(
        _mod_kernel,
        grid=(depth, n // MOD_COLS),
        in_specs=[
            pl.BlockSpec((rows, d), lambda l, j: (0, 0)),
            pl.BlockSpec((1, d, MOD_COLS), lambda l, j: (l, 0, j)),
            pl.BlockSpec((1, 1, MOD_COLS), lambda l, j: (l, 0, j)),
        ],
        out_specs=pl.BlockSpec((1, rows, MOD_COLS), lambda l, j: (l, 0, j)),
        out_shape=jax.ShapeDtypeStruct((depth, rows, n), F32),
        compiler_params=_cparams(("arbitrary", "arbitrary")),
        name="modulation",
    )(c_all, ada_w, ada_b.reshape(depth, 1, n))


def _mixer_residual(x, mod_ref, attn_ref, of_ref, ob_ref, z_ref, nw_ref, wa_ref, wd_ref):
    o = of_ref[...] + ob_ref[...]
    z = z_ref[...].astype(F32)
    nw = nw_ref[...]
    parts = []
    for hd in range(N_DN_HEADS):
        sl = slice(hd * DN_HEAD_DIM, (hd + 1) * DN_HEAD_DIM)
        oh = o[:, sl]
        parts.append(oh * lax.rsqrt(jnp.mean(oh * oh, axis=-1, keepdims=True) + EPS) * nw * _silu(z[:, sl]))
    dn = jnp.concatenate(parts, axis=1).astype(BF16)
    y = jnp.dot(attn_ref[...], wa_ref[...], preferred_element_type=F32)
    y = y + jnp.dot(dn, wd_ref[...], preferred_element_type=F32)
    return x + mod_ref[0, 2:3, :] * y


N_MIXER_REFS = 8


def _ffn_kernel(*refs, n_in, n_out, with_mixer, final_norm, rows1):
    x_refs, refs = refs[:n_in], refs[n_in:]
    if with_mixer:
        mixer_refs, refs = refs[:N_MIXER_REFS], refs[N_MIXER_REFS:]
    mod_ref, nw_ref, wg_ref, wu_ref, wd_ref = refs[:5]
    refs = refs[5:]
    if final_norm:
        fw_ref, refs = refs[0], refs[1:]
    o_refs, refs = refs[:n_out], refs[n_out:]
    tm = x_refs[0].shape[0]
    in_first = pl.program_id(0) < rows1 // tm
    if n_in == 2:
        stage_ref = refs[0]

        @pl.when(in_first)
        def _():
            stage_ref[...] = x_refs[0][...]

        @pl.when(jnp.logical_not(in_first))
        def _():
            stage_ref[...] = x_refs[1][...]

        x = stage_ref[...]
    else:
        x = x_refs[0][...]
    if with_mixer:
        x = _mixer_residual(x, *mixer_refs)
    h = _modulated_norm(x, nw_ref[...], mod_ref[0, 0:1, :], mod_ref[0, 1:2, :]).astype(BF16)
    g = jnp.dot(h, wg_ref[...], preferred_element_type=F32)
    u = jnp.dot(h, wu_ref[...], preferred_element_type=F32)
    a = (_silu(g) * u).astype(BF16)
    y = jnp.dot(a, wd_ref[...], preferred_element_type=F32)
    out = x + (0.5 * mod_ref[0, 2:3, :]) * y
    if final_norm:
        out = out * lax.rsqrt(jnp.mean(out * out, axis=-1, keepdims=True) + EPS) * fw_ref[...]
    if n_out == 2:
        @pl.when(in_first)
        def _():
            o_refs[0][...] = out

        @pl.when(jnp.logical_not(in_first))
        def _():
            o_refs[1][...] = out
    else:
        o_refs[0][...] = out


def _ffn(xs, mod, norm_w, wg, wu, wd, blk_rows, rows1, final_w=None, split_out=False, mixer=None):
    d = xs[0].shape[1]
    t = sum(x.shape[0] for x in xs)
    f = wg.shape[1]
    tm = FFN_ROWS
    per_blk = blk_rows // tm
    nb1 = rows1 // tm
    const = lambda i: (0, 0)
    row = lambda i: (i, 0)
    blk = lambda i: (i // per_blk, 0, 0)
    group1 = lambda i: (jnp.minimum(i, nb1 - 1), 0)
    group2 = lambda i: (jnp.maximum(i - nb1, 0), 0)
    in_specs = [pl.BlockSpec((tm, d), row)] if len(xs) == 1 else [pl.BlockSpec((tm, d), group1), pl.BlockSpec((tm, d), group2)]
    args = list(xs)
    if mixer is not None:
        m_mod, attn, o_f, o_b, z, dn_norm, w_attn, w_dn = mixer
        in_specs += [
            pl.BlockSpec((1, 3, d), blk),
            pl.BlockSpec((tm, ATTN_WIDTH), row),
            pl.BlockSpec((tm, DN_WIDTH), row),
            pl.BlockSpec((tm, DN_WIDTH), row),
            pl.BlockSpec((tm, DN_WIDTH), row),
            pl.BlockSpec((1, DN_HEAD_DIM), const),
            _resident((ATTN_WIDTH, d)),
            _resident((DN_WIDTH, d)),
        ]
        args += [m_mod, attn, o_f, o_b, z, dn_norm.reshape(1, DN_HEAD_DIM), w_attn, w_dn]
    in_specs += [
        pl.BlockSpec((1, 3, d), blk),
        pl.BlockSpec((1, d), const),
        _resident((d, f)),
        _resident((d, f)),
        _resident((f, d)),
    ]
    args += [mod, norm_w.reshape(1, d), wg, wu, wd]
    if final_w is not None:
        in_specs.append(pl.BlockSpec((1, d), const))
        args.append(final_w.reshape(1, d))
    if split_out:
        out_specs = [pl.BlockSpec((tm, d), group1), pl.BlockSpec((tm, d), group2)]
        out_shape = [jax.ShapeDtypeStruct((rows1, d), F32), jax.ShapeDtypeStruct((t - rows1, d), F32)]
    else:
        out_specs = [pl.BlockSpec((tm, d), row)]
        out_shape = [jax.ShapeDtypeStruct((t, d), F32)]
    outs = pl.pallas_call(
        functools.partial(_ffn_kernel, n_in=len(xs), n_out=len(out_specs), with_mixer=mixer is not None,
                          final_norm=final_w is not None, rows1=rows1),
        grid=(t // tm,),
        in_specs=in_specs,
        out_specs=out_specs,
        out_shape=out_shape,
        scratch_shapes=[pltpu.VMEM((tm, d), F32)] if len(xs) == 2 else [],
        compiler_params=_cparams(("arbitrary",)),
        name="ffn",
    )(*args)
    return outs if split_out else outs[0]


def _inproj_kernel(xp_ref, x_ref, xn_ref, mod_ref, nw_ref, w_ref, wg_ref, wgt_ref, gp_ref, gpt_ref,
                   cos_ref, sfw_ref, sbk_ref, cw_ref, *rest, rows1, len1, len2):
    n_views = len(DILATED_PATTERNS)
    qkv_refs = [rest[3 * n:3 * n + 3] for n in range(n_views)]
    dq_ref, dk_ref, dv_ref, z_ref, gb_ref, gbt_ref, stage_ref = rest[3 * n_views:]
    tm = x_ref.shape[0]
    nw, shift, scale = nw_ref[...], mod_ref[0, 0:1, :], mod_ref[0, 1:2, :]
    h_main = _modulated_norm(x_ref[...], nw, shift, scale)
    h_ext = jnp.concatenate([_modulated_norm(xp_ref[...], nw, shift, scale), h_main,
                             _modulated_norm(xn_ref[...], nw, shift, scale)], axis=0).astype(BF16)
    h = h_main.astype(BF16)
    cos, sfw, sbk = cos_ref[...], sfw_ref[...], sbk_ref[...]

    def rope(xp):
        return xp * cos + pltpu.roll(xp, LANES - ROT_HALF, 1) * sfw + pltpu.roll(xp, ROT_HALF, 1) * sbk

    n_tiles = ATTN_WIDTH // LANES
    o0 = 3 * ATTN_WIDTH

    def attn_epilogue(a, pa):
        for j in range(n_tiles):
            col = pa[:, j * LANES:(j + 1) * LANES]
            if a == 0:
                col = rope(col) * (HEAD_DIM ** -0.5)
            elif a == 1:
                col = rope(col)
            stage_ref[a * n_tiles + j] = col
        for n, (_, dil) in enumerate(DILATED_PATTERNS):
            rows = tm // dil
            for j in range(n_tiles):
                for c in range(dil):
                    lo = c * ATTN_WIDTH + j * LANES
                    src = stage_ref[a * n_tiles + j, pl.ds(c, rows, stride=dil), :] if dil > 1 else stage_ref[a * n_tiles + j]
                    qkv_refs[n][a][:, lo:lo + LANES] = src.astype(BF16)

    r0 = pl.program_id(0) * tm
    in_first = r0 < rows1
    seq_len = jnp.where(in_first, len1, len2)
    off = jnp.where(in_first, r0, r0 - rows1)
    keep_prev = jnp.where((off % seq_len) == 0, 0.0, 1.0).astype(F32)
    keep_next = jnp.where(((off + tm) % seq_len) == 0, 0.0, 1.0).astype(F32)
    pad = CONV_K // 2
    n_ext = tm + 2 * HALO_ROWS

    def dn_epilogue(a, pe):
        out_ref = (dq_ref, dk_ref, dv_ref)[a]
        cols = slice(a * DN_WIDTH, (a + 1) * DN_WIDTH)
        pe = jnp.concatenate([pe[0:HALO_ROWS] * keep_prev, pe[HALO_ROWS:HALO_ROWS + tm],
                              pe[HALO_ROWS + tm:] * keep_next], axis=0)
        taps = [pe * cw_ref[j:j + 1, cols] for j in range(CONV_K)]
        before, after = taps[0], taps[CONV_K - 1]
        for j in range(1, pad):
            before = taps[j] + pltpu.roll(before, 1, 0)
            after = taps[CONV_K - 1 - j] + pltpu.roll(after, n_ext - 1, 0)
        y = taps[pad] + pltpu.roll(before, 1, 0) + pltpu.roll(after, n_ext - 1, 0)
        y = _silu(y[HALO_ROWS:HALO_ROWS + tm])
        for hd in range(N_DN_HEADS):
            sl = slice(hd * DN_HEAD_DIM, (hd + 1) * DN_HEAD_DIM)
            yh = y[:, sl]
            if a == 0:
                yh = yh * lax.rsqrt(jnp.sum(yh * yh, axis=-1, keepdims=True) + EPS) * (DN_HEAD_DIM ** -0.5)
            elif a == 1:
                yh = yh * lax.rsqrt(jnp.sum(yh * yh, axis=-1, keepdims=True) + EPS)
            out_ref[:, sl] = yh.astype(BF16)

    def z_epilogue(_, pz):
        z_ref[...] = pz.astype(BF16)

    groups = [(h, a * ATTN_WIDTH, ATTN_WIDTH, attn_epilogue, a) for a in range(3)]
    groups += [(h_ext, o0 + a * DN_WIDTH, DN_WIDTH, dn_epilogue, a) for a in range(3)]
    groups.append((h, o0 + 3 * DN_WIDTH, DN_WIDTH, z_epilogue, None))

    for lhs, lo, width, epilogue, arg in groups:
        epilogue(arg, jnp.dot(lhs, w_ref[:, lo:lo + width], preferred_element_type=F32))

    n_g = 2 * N_DN_HEADS
    ab = jnp.dot(h, wg_ref[...], preferred_element_type=F32)
    is_g = lax.broadcasted_iota(jnp.int32, ab.shape, 1) < n_g
    gp = gp_ref[...]
    gb_ref[...] = jnp.where(is_g, -jnp.exp(gp[0:1, :]) * jax.nn.softplus(ab + gp[1:2, :]), jax.nn.sigmoid(ab))
    abt = lax.dot_general(wgt_ref[...], h, (((1,), (1,)), ((), ())), preferred_element_type=F32)
    is_gt = lax.broadcasted_iota(jnp.int32, abt.shape, 0) < n_g
    gpt = gpt_ref[...]
    gbt_ref[...] = jnp.where(is_gt, -jnp.exp(gpt[:, 0:1]) * jax.nn.softplus(abt + gpt[:, 1:2]), jax.nn.sigmoid(abt))


def _in_proj(x, mod, norm_w, w_main, w_gate, w_gate_t, gate_par, gate_par_t, rope_tabs, conv_w, blk_rows, seq_lens,
             t1):
    t, d = x.shape
    tm = PROJ_ROWS
    per_blk = blk_rows // tm
    n_main = w_main.shape[1]
    nb1 = t1 // tm
    s1b, s2b = seq_lens[0] // tm, seq_lens[1] // tm
    per_halo = tm // HALO_ROWS
    n_halo = t // HALO_ROWS
    const = lambda i: (0, 0)
    row = lambda i: (i, 0)
    pos = lambda i: (jnp.where(i < nb1, i % s1b, (i - nb1) % s2b), 0)
    cos, sfw, sbk = rope_tabs
    view_specs, view_shapes = [], []
    for _, dil in DILATED_PATTERNS:
        view_specs += [pl.BlockSpec((tm // dil, dil * ATTN_WIDTH), row)] * 3
        view_shapes += [jax.ShapeDtypeStruct((t // dil, dil * ATTN_WIDTH), BF16)] * 3
    outs = pl.pallas_call(
        functools.partial(_inproj_kernel, rows1=t1, len1=seq_lens[0], len2=seq_lens[1]),
        grid=(t // tm,),
        in_specs=[
            pl.BlockSpec((HALO_ROWS, d), lambda i: (jnp.maximum(i * per_halo - 1, 0), 0)),
            pl.BlockSpec((tm, d), row),
            pl.BlockSpec((HALO_ROWS, d), lambda i: (jnp.minimum((i + 1) * per_halo, n_halo - 1), 0)),
            pl.BlockSpec((1, 3, d), lambda i: (i // per_blk, 0, 0)),
            pl.BlockSpec((1, d), const),
            _resident((d, n_main)),
            pl.BlockSpec((d, LANES), const),
            pl.BlockSpec((N_GATES, d), const),
            pl.BlockSpec((2, LANES), const),
            pl.BlockSpec((N_GATES, 2), const),
            pl.BlockSpec((tm, LANES), pos),
            pl.BlockSpec((tm, LANES), pos),
            pl.BlockSpec((tm, LANES), pos),
            pl.BlockSpec((CONV_K, 3 * DN_WIDTH), const),
        ],
        out_specs=view_specs + [pl.BlockSpec((tm, DN_WIDTH), row)] * 4 + [
            pl.BlockSpec((tm, LANES), row),
            pl.BlockSpec((N_GATES, tm), lambda i: (0, i)),
        ],
        out_shape=view_shapes + [jax.ShapeDtypeStruct((t, DN_WIDTH), BF16)] * 4 + [
            jax.ShapeDtypeStruct((t, LANES), F32),
            jax.ShapeDtypeStruct((N_GATES, t), F32),
        ],
        scratch_shapes=[pltpu.VMEM((3 * ATTN_WIDTH // LANES, tm, LANES), F32)],
        compiler_params=_cparams(("arbitrary",)),
        name="in_proj",
    )(x, x, x, mod, norm_w.reshape(1, d), w_main, w_gate, w_gate_t, gate_par, gate_par_t, cos, sfw, sbk, conv_w)
    n_qkv = 3 * len(DILATED_PATTERNS)
    qkv_views = [outs[3 * n:3 * n + 3] for n in range(len(DILATED_PATTERNS))]
    return qkv_views, outs[n_qkv:]


def _rope_tables(max_len):
    half = ROT_HALF
    inv = ROPE_THETA ** (-jnp.arange(half, dtype=F32) / half)
    ang = jnp.arange(max_len, dtype=F32)[:, None] * inv[None, :]
    cos, sin = jnp.cos(ang), jnp.sin(ang)
    ones = jnp.ones((max_len, HEAD_DIM - 2 * half), F32)
    zeros_h = jnp.zeros((max_len, half), F32)
    zeros_r = jnp.zeros((max_len, HEAD_DIM - 2 * half), F32)
    cos_h = jnp.concatenate([cos, cos, ones], axis=1)
    sfw_h = jnp.concatenate([-sin, zeros_h, zeros_r], axis=1)
    sbk_h = jnp.concatenate([zeros_h, sin, zeros_r], axis=1)
    rep = LANES // HEAD_DIM
    return tuple(jnp.tile(a, (1, rep)) for a in (cos_h, sfw_h, sbk_h))


def _attn_kernel(q_ref, kp_ref, km_ref, kn_ref, vp_ref, vm_ref, vn_ref, *rest,
                 dil, next_dil, first, radius, tq, rows1, len1, len2):
    last = next_dil is None
    if not first:
        op_ref, wp_ref = rest[:2]
        rest = rest[2:]
    if last:
        o_ref, acc_scr, nat_scr = rest
    else:
        acc_out_ref, ml_out_ref, acc_scr, ml_scr = rest
    tb = q_ref.shape[0]
    n_sub = tb // tq
    tk = tq + 2 * radius
    n_tiles = ATTN_WIDTH // LANES
    n_res = q_ref.shape[1] // ATTN_WIDTH
    c0 = pl.program_id(1) * n_res
    r0 = pl.program_id(0) * tb
    in_first = r0 < rows1
    seq_len = jnp.where(in_first, len1, len2)
    off = jnp.where(in_first, r0, r0 - rows1)
    first_key = jnp.where((off % seq_len) == 0, radius, 0)
    end_key = jnp.where(((off + tb) % seq_len) == 0, tq + radius, tk)

    qi = lax.broadcasted_iota(jnp.int32, (tq, tk), 0)
    kj = lax.broadcasted_iota(jnp.int32, (tq, tk), 1)
    in_band = jnp.abs(kj - radius - qi) <= radius
    biases = []
    for sub in range(n_sub):
        valid = in_band
        if sub == 0:
            valid = valid & (kj >= first_key)
        if sub == n_sub - 1:
            valid = valid & (kj < end_key)
        bias = jnp.where(valid, 0.0, NEG_BIG).astype(F32)
        biases.append(jnp.concatenate([bias, bias], axis=0))

    lane = lax.broadcasted_iota(jnp.int32, (1, LANES), 1)
    low_half = lane < HEAD_DIM
    nt = (((1,), (1,)), ((), ()))

    def unstack(x):
        return jnp.where(low_half, x[:tq], x[tq:])

    if not first:
        e_row = lax.broadcasted_iota(jnp.int32, (LANES, ATTN_WIDTH), 0)
        e_col = lax.broadcasted_iota(jnp.int32, (LANES, ATTN_WIDTH), 1)
        spread = jnp.where(e_row == e_col // LANES + HEAD_DIM * ((e_col % LANES) // HEAD_DIM), 1.0, 0.0).astype(BF16)
        lse_old = {}
        for r in range(n_res):
            for sub in range(n_sub):
                wc = wp_ref[sub * tq:(sub + 1) * tq, r * LANES:(r + 1) * LANES]
                hi = wc.astype(BF16)
                lo = (wc - hi.astype(F32)).astype(BF16)
                ex = jnp.dot(jnp.concatenate([hi, lo], axis=0), spread, preferred_element_type=F32)
                lse_old[(r, sub)] = ex[:tq] + ex[tq:]

    pairs = []
    for r in range(n_res):
        for j in range(n_tiles):
            sl = slice(r * ATTN_WIDTH + j * LANES, r * ATTN_WIDTH + (j + 1) * LANES)
            k_all = jnp.concatenate([kp_ref[:, sl], km_ref[:, sl], kn_ref[:, sl]], axis=0)
            v_all = jnp.concatenate([vp_ref[:, sl], vm_ref[:, sl], vn_ref[:, sl]], axis=0)
            for sub in range(n_sub):
                rows = slice(sub * tq, (sub + 1) * tq)
                q2 = q_ref[rows, sl]
                zero = jnp.zeros_like(q2)
                qs = jnp.concatenate([jnp.where(low_half, q2, zero), jnp.where(low_half, zero, q2)], axis=0)
                kw = k_all[sub * tq:sub * tq + tk]
                pairs.append(dict(r=r, j=j, sub=sub, sl=sl, rows=rows, vw=v_all[sub * tq:sub * tq + tk],
                                  s=lax.dot_general(qs, kw, nt, preferred_element_type=F32) + biases[sub]))
    ones = jnp.ones((tk, LANES), BF16)
    for pr in pairs:
        m_rows = jnp.broadcast_to(jnp.max(pr["s"], axis=-1, keepdims=True), (2 * tq, LANES))
        pr["m"] = unstack(m_rows)
        pr["p"] = jnp.exp(pr["s"] - jnp.concatenate([m_rows] * (tk // LANES), axis=1)).astype(BF16)
    for pr in pairs:
        pr["pv"] = jnp.dot(pr["p"], pr["vw"], preferred_element_type=F32)
        pr["l"] = jnp.dot(pr["p"], ones, preferred_element_type=F32)
    for pr in pairs:
        den = unstack(pr["l"])
        o = unstack(pr["pv"]) / den
        lse = pr["m"] + jnp.log(den)
        if not first:
            lse_prev = lse_old[(pr["r"], pr["sub"])][:, pr["j"] * LANES:(pr["j"] + 1) * LANES]
            top = jnp.maximum(lse_prev, lse)
            w_prev, w_cur = jnp.exp(lse_prev - top), jnp.exp(lse - top)
            tot = w_prev + w_cur
            o = (w_prev * op_ref[pr["rows"], pr["sl"]] + w_cur * o) / tot
            lse = top + jnp.log(tot)
        pr["lse"] = lse
        acc_scr[c0 + pr["r"], pr["j"], pr["rows"], :] = o
    if not last:
        for r in range(n_res):
            for sub in range(n_sub):
                wc = jnp.zeros((tq, LANES), F32)
                for pr in pairs:
                    if pr["r"] == r and pr["sub"] == sub:
                        keep = jnp.logical_or(lane == pr["j"], lane == HEAD_DIM + pr["j"])
                        wc = jnp.where(keep, pr["lse"], wc)
                ml_scr[c0 + r, sub * tq:(sub + 1) * tq, :] = wc

    @pl.when(c0 + n_res == dil)
    def _():
        if last:
            for cc in range(dil):
                for j in range(n_tiles):
                    nat_scr[j, pl.ds(cc, tb, stride=dil), :] = acc_scr[cc, j]
            for j in range(n_tiles):
                o_ref[:, j * LANES:(j + 1) * LANES] = nat_scr[j].astype(o_ref.dtype)
        else:
            ratio = next_dil // dil
            rows_out = tb // ratio
            for cc in range(dil):
                for m in range(ratio):
                    cb = m * dil + cc
                    rows = pl.ds(m, rows_out, stride=ratio)
                    for j in range(n_tiles):
                        lo = cb * ATTN_WIDTH + j * LANES
                        acc_out_ref[:, lo:lo + LANES] = acc_scr[cc, j, rows, :]
                    ml_out_ref[:, cb * LANES:(cb + 1) * LANES] = ml_scr[cc, rows, :]


def _attn_pattern(qkv, prev, dil, next_dil, radius, t, t1, seq_lens):
    q, k, v = qkv
    w = ATTN_WIDTH
    first, last = prev is None, next_dil is None
    rows = t // dil
    tq, n_sub, n_res = ATTN_TILE[dil]
    tb = tq * n_sub
    per_q = tb // radius
    n_halo = rows // radius
    n_tiles = w // LANES
    wr = n_res * w
    main = lambda i, c: (i, c)
    before = lambda i, c: (jnp.maximum(i * per_q - 1, 0), c)
    after = lambda i, c: (jnp.minimum((i + 1) * per_q, n_halo - 1), c)
    whole = lambda i, c: (i, 0)
    kv_specs = [pl.BlockSpec((radius, wr), before), pl.BlockSpec((tb, wr), main), pl.BlockSpec((radius, wr), after)]
    in_specs = [pl.BlockSpec((tb, wr), main)] + kv_specs + kv_specs
    args = [q, k, k, k, v, v, v]
    if not first:
        in_specs += [pl.BlockSpec((tb, wr), main), pl.BlockSpec((tb, n_res * LANES), main)]
        args += list(prev)
    scratch = [pltpu.VMEM((dil, n_tiles, tb, LANES), F32)]
    if last:
        out_specs = [pl.BlockSpec((tb * dil, w), whole)]
        out_shape = [jax.ShapeDtypeStruct((t, w), BF16)]
        scratch.append(pltpu.VMEM((n_tiles, tb * dil, LANES), F32))
    else:
        rows_out = tb * dil // next_dil
        out_specs = [pl.BlockSpec((rows_out, next_dil * w), whole), pl.BlockSpec((rows_out, next_dil * LANES), whole)]
        out_shape = [jax.ShapeDtypeStruct((t // next_dil, next_dil * w), F32),
                     jax.ShapeDtypeStruct((t // next_dil, next_dil * LANES), F32)]
        scratch.append(pltpu.VMEM((dil, tb, LANES), F32))
    return pl.pallas_call(
        functools.partial(_attn_kernel, dil=dil, next_dil=next_dil, first=first, radius=radius, tq=tq,
                          rows1=t1 // dil, len1=seq_lens[0] // dil, len2=seq_lens[1] // dil),
        grid=(rows // tb, dil // n_res),
        in_specs=in_specs,
        out_specs=out_specs,
        out_shape=out_shape,
        scratch_shapes=scratch,
        compiler_params=_cparams(("arbitrary", "arbitrary")),
        name=f"attn_d{dil}",
    )(*args)


def _attention(qkv_views, t, t1, seq_lens):
    state = None
    n = len(DILATED_PATTERNS)
    for idx, (window, dil) in enumerate(DILATED_PATTERNS):
        next_dil = DILATED_PATTERNS[idx + 1][1] if idx + 1 < n else None
        state = _attn_pattern(qkv_views[idx], state, dil, next_dil, window // (2 * dil), t, t1, seq_lens)
    return state[0]


def _bdot(a, b):
    return jnp.dot(a.astype(BF16), b.astype(BF16), preferred_element_type=F32)


def _split3(x):
    hi = x.astype(BF16)
    r1 = x - hi.astype(F32)
    mid = r1.astype(BF16)
    lo = (r1 - mid.astype(F32)).astype(BF16)
    return hi, mid, lo


def _dn_kernel(qf_ref, kf_ref, vf_ref, gf_ref, gtf_ref, qb_ref, kb_ref, vb_ref, gb_ref, gtb_ref,
               of_ref, ob_ref, sf_ref, sb_ref, *, c, n_chunks, chunks1, len1, len2):
    n_sub = qf_ref.shape[0] // c
    i = pl.program_id(0)

    def seq_pos(ci):
        in_first = ci < chunks1
        seq_len = jnp.where(in_first, len1, len2)
        off = jnp.where(in_first, ci, ci - chunks1)
        return off % seq_len, seq_len

    pos_f, _ = seq_pos(i * n_sub)
    pos_b, len_b = seq_pos(n_chunks - 1 - i * n_sub)

    @pl.when(pos_f == 0)
    def _():
        sf_ref[...] = jnp.zeros_like(sf_ref)

    @pl.when(pos_b == len_b - 1)
    def _():
        sb_ref[...] = jnp.zeros_like(sb_ref)

    ii = lax.broadcasted_iota(jnp.int32, (c, c), 0)
    jj = lax.broadcasted_iota(jnp.int32, (c, c), 1)
    lower, lower_strict = ii >= jj, ii > jj
    upper, upper_strict = ii <= jj, ii < jj
    eye = jnp.where(ii == jj, 1.0, 0.0).astype(F32)
    tri_l = jnp.where(lower, 1.0, 0.0).astype(F32)
    tri_u = jnp.where(upper, 1.0, 0.0).astype(F32)
    level_masks = [jnp.where(ii // INV_BASE == jj // INV_BASE, 1.0, 0.0).astype(F32)]
    b = INV_BASE
    while b < c:
        same_outer = ii // (2 * b) == jj // (2 * b)
        level_masks.append(jnp.where(same_outer & (ii // b != jj // b), 1.0, 0.0).astype(F32))
        b *= 2
    ones_cc = jnp.ones((c, c), BF16)
    ones_cd = jnp.ones((c, DN_HEAD_DIM), BF16)
    nt = (((1,), (1,)), ((), ()))

    chains = []
    for forward, (q_ref, k_ref, v_ref, g_ref, gt_ref, o_ref, s_ref) in (
            (True, (qf_ref, kf_ref, vf_ref, gf_ref, gtf_ref, of_ref, sf_ref)),
            (False, (qb_ref, kb_ref, vb_ref, gb_ref, gtb_ref, ob_ref, sb_ref))):
        tri_c, tri_r = (tri_l, tri_u) if forward else (tri_u, tri_l)
        col_lhs = jnp.concatenate([tri_c.astype(BF16), ones_cc], axis=0)
        row_rhs = jnp.concatenate([tri_r.astype(BF16), ones_cd], axis=1)
        keep, strict = (lower, lower_strict) if forward else (upper, upper_strict)
        d0 = 0 if forward else N_DN_HEADS
        for order in range(n_sub):
            sub = order if forward else n_sub - 1 - order
            rows = slice(sub * c, (sub + 1) * c)
            g, gt = g_ref[rows, :], gt_ref[:, rows]
            col = sum(jnp.dot(col_lhs, piece, preferred_element_type=F32) for piece in _split3(g))
            row = sum(jnp.dot(piece, row_rhs, preferred_element_type=F32) for piece in _split3(gt))
            cs_col, tot_col = col[:c], col[c:]
            cs_row, tot_row = row[:, :c], row[:, c:]
            for hd in range(N_DN_HEADS):
                gi = d0 + hd
                bi = 2 * N_DN_HEADS + d0 + hd
                sl = slice(hd * DN_HEAD_DIM, (hd + 1) * DN_HEAD_DIM)
                chains.append(dict(
                    order=order, rows=rows, sl=sl, hd=hd, keep=keep, strict=strict, o_ref=o_ref, s_ref=s_ref,
                    q=q_ref[rows, sl], k=k_ref[rows, sl], v=v_ref[rows, sl],
                    gc_col=cs_col[:, gi:gi + 1], gc_row=cs_row[gi:gi + 1, :], beta=g[:, bi:bi + 1],
                    tot_col=tot_col[:, gi:gi + 1], tot_row=tot_row[gi:gi + 1, :]))

    for ch in chains:
        qh, kh = ch["q"], ch["k"]
        both = lax.dot_general(jnp.concatenate([qh, kh], axis=0), kh, nt, preferred_element_type=F32)
        decay = jnp.where(ch["keep"], jnp.exp(jnp.minimum(ch["gc_col"] - ch["gc_row"], 0.0)), 0.0)
        ch["a"] = jnp.where(ch["strict"], ch["beta"] * both[c:] * decay, 0.0)
        ch["qk"] = (both[:c] * decay).astype(BF16)
    for ch in chains:
        dg = ch["a"] * level_masks[0]
        ch["x"] = eye - dg
        ch["y"] = _bdot(dg, dg)
    for ch in chains:
        ch["x"] = ch["x"] + _bdot(ch["x"], ch["y"])
    for off_mask in level_masks[1:]:
        for ch in chains:
            ch["y"] = _bdot(ch["a"] * off_mask, ch["x"])
        for ch in chains:
            ch["x"] = ch["x"] - _bdot(ch["x"], ch["y"])
    for ch in chains:
        kh = ch["k"].astype(F32)
        egc = jnp.exp(ch["gc_col"])
        rhs = jnp.concatenate([ch["v"].astype(F32) * ch["beta"], kh * (ch["beta"] * egc)], axis=1)
        ch["uw"] = _bdot(ch["x"], rhs)
        ch["lhs_q"] = ch["q"].astype(F32) * egc
        ch["k_dec_t"] = jnp.transpose(kh * jnp.exp(ch["tot_col"] - ch["gc_col"])).astype(BF16)
    states = {}
    for order in range(n_sub):
        group = [ch for ch in chains if ch["order"] == order]
        for ch in group:
            key = (id(ch["s_ref"]), ch["hd"])
            ch["state"] = states[key] if order else ch["s_ref"][ch["hd"]]
            lhs = jnp.concatenate([ch["uw"][:, DN_HEAD_DIM:], ch["lhs_q"]], axis=0)
            ch["ws"] = _bdot(lhs, ch["state"])
        for ch in group:
            v_new = (ch["uw"][:, :DN_HEAD_DIM] - ch["ws"][:c]).astype(BF16)
            ch["o_ref"][ch["rows"], ch["sl"]] = ch["ws"][c:] + jnp.dot(ch["qk"], v_new, preferred_element_type=F32)
            ch["v_new"] = v_new
        for ch in group:
            states[(id(ch["s_ref"]), ch["hd"])] = ch["state"] * jnp.exp(ch["tot_row"]) + jnp.dot(
                ch["k_dec_t"], ch["v_new"], preferred_element_type=F32)
    for ch in chains:
        if ch["order"] == n_sub - 1:
            ch["s_ref"][ch["hd"]] = states[(id(ch["s_ref"]), ch["hd"])]


def _deltanet(q, k, v, gates, gates_t, t1, seq_lens):
    t, w = q.shape
    c = DN_CHUNK
    rows = c * DN_CHUNKS_PER_STEP
    n = t // rows
    fwd = lambda i: (i, 0)
    bwd = lambda i: (n - 1 - i, 0)
    fwd_t = lambda i: (0, i)
    bwd_t = lambda i: (0, n - 1 - i)

    def specs(row_map, col_map):
        return [pl.BlockSpec((rows, w), row_map)] * 3 + [
            pl.BlockSpec((rows, LANES), row_map), pl.BlockSpec((N_GATES, rows), col_map)]

    state = pltpu.VMEM((N_DN_HEADS, DN_HEAD_DIM, DN_HEAD_DIM), F32)
    return pl.pallas_call(
        functools.partial(_dn_kernel, c=c, n_chunks=t // c, chunks1=t1 // c, len1=seq_lens[0] // c,
                          len2=seq_lens[1] // c),
        grid=(n,),
        in_specs=specs(fwd, fwd_t) + specs(bwd, bwd_t),
        out_specs=[pl.BlockSpec((rows, w), fwd), pl.BlockSpec((rows, w), bwd)],
        out_shape=[jax.ShapeDtypeStruct((t, w), F32)] * 2,
        scratch_shapes=[state, state],
        compiler_params=_cparams(("arbitrary",)),
        name="deltanet",
    )(q, k, v, gates, gates_t, q, k, v, gates, gates_t)


def kernel(x_prompt, x_sample, c_prompt, c_sample, ada_w, ada_b, norm_ffn1, ffn1_w_gate, ffn1_w_up, ffn1_w_down, norm_mix, w_in, conv_w, a_log, dt_bias, dn_norm, w_out, norm_ffn2, ffn2_w_gate, ffn2_w_up, ffn2_w_down, norm_final):
    b1, s1, d = x_prompt.shape
    b2, s2, _ = x_sample.shape
    depth = ada_w.shape[0]
    t1 = b1 * s1
    seq_lens = (s1, s2)
    assert s2 % s1 == 0 and t1 % s2 == 0, "flat layout needs nested sequence lengths"
    t = t1 + b2 * s2
    xs = [x_prompt.reshape(t1, d), x_sample.reshape(b2 * s2, d)]

    n_seq = b1 + b2
    c_all = jnp.concatenate([c_prompt, c_sample, jnp.zeros((-n_seq % SUBLANES, d), F32)], axis=0)
    mod = _modulation(c_all, ada_w, ada_b)
    blk_seq = jnp.concatenate([jnp.arange(b1), b1 + jnp.repeat(jnp.arange(b2), s2 // s1)])
    mod = mod[:, blk_seq].reshape(depth, blk_seq.shape[0], N_MOD, d)

    rope_tabs = _rope_tables(max(s1, s2))
    n_main = 3 * ATTN_WIDTH + 4 * DN_WIDTH
    gate_par_t = jnp.stack([a_log.astype(F32).reshape(depth, -1), dt_bias.astype(F32).reshape(depth, -1)], axis=2)
    gate_par_t = jnp.pad(gate_par_t, ((0, 0), (0, N_GATES - gate_par_t.shape[1]), (0, 0)))
    gate_par = jnp.pad(jnp.swapaxes(gate_par_t, 1, 2), ((0, 0), (0, 0), (0, LANES - N_GATES)))

    for l in range(depth):
        bf = lambda a: a.astype(BF16)
        x = _ffn(xs if l == 0 else [x], mod[l, :, 0:3], norm_ffn1[l], bf(ffn1_w_gate[l]), bf(ffn1_w_up[l]),
                 bf(ffn1_w_down[l]), s1, t1)
        w_gate_t = bf(w_in[l][:, n_main:]).T
        w_gate = jnp.pad(w_gate_t.T, ((0, 0), (0, LANES - N_GATES)))
        qkv_views, (dq, dk, dv, z, gates, gates_t) = _in_proj(
            x, mod[l, :, 3:6], norm_mix[l], bf(w_in[l][:, :n_main]), w_gate, w_gate_t,
            gate_par[l], gate_par_t[l], rope_tabs, conv_w[l], s1, seq_lens, t1)
        attn = _attention(qkv_views, t, t1, seq_lens)
        o_f, o_b = _deltanet(dq, dk, dv, gates, gates_t, t1, seq_lens)
        mixer = (mod[l, :, 3:6], attn, o_f, o_b, z, dn_norm[l], bf(w_out[l][:ATTN_WIDTH]), bf(w_out[l][ATTN_WIDTH:]))
        is_last = l == depth - 1
        x = _ffn([x], mod[l, :, 6:9], norm_ffn2[l], bf(ffn2_w_gate[l]), bf(ffn2_w_up[l]), bf(ffn2_w_down[l]), s1,
                 t1, final_w=norm_final if is_last else None, split_out=is_last, mixer=mixer)
    return x[0].reshape(b1, s1, d), x[1].reshape(b2, s2, d)
```

```python
import functools

import jax
import jax.numpy as jnp
from jax import lax
from jax.experimental import pallas as pl
from jax.experimental.pallas import tpu as pltpu

F32 = jnp.float32
BF16 = jnp.bfloat16

D_MODEL = 1024
D_FF = 2816
N_MOD = 9
EPS = 1e-6
N_ATTN_HEADS = 8
HEAD_DIM = 64
ATTN_WIDTH = N_ATTN_HEADS * HEAD_DIM
ROT_HALF = HEAD_DIM // 8
ROPE_THETA = 500000.0
ATTN_Q_SCALE = HEAD_DIM ** -0.5 * 1.4426950408889634
DILATED_PATTERNS = ((128, 1), (512, 4), (2048, 16))
N_DN_HEADS = 4
DN_HEAD_DIM = 128
DN_WIDTH = N_DN_HEADS * DN_HEAD_DIM
CONV_K = 5
N_GATES = 4 * N_DN_HEADS

LANES = 128
SUBLANES = 8
VMEM_LIMIT_BYTES = 56 * 1024 * 1024

FFN_ROWS = 512
PROJ_ROWS = 512
ATTN_TILE = {1: (128, 4, 1), 4: (128, 1, 4), 16: (128, 1, 4)}
DN_CHUNK = 128
DN_CHUNKS_PER_STEP = 2
INV_BASE = 4
MOD_COLS = 1152
HALO_ROWS = 8
NEG_BIG = -1e30


def _cparams(sem):
    return pltpu.CompilerParams(dimension_semantics=sem, vmem_limit_bytes=VMEM_LIMIT_BYTES)


def _resident(shape):
    return pl.BlockSpec(shape, lambda *_: (0, 0), pipeline_mode=pl.Buffered(1))


def _silu(x):
    return x * jax.nn.sigmoid(x)


def _modulated_norm(x, norm_w, shift, scale):
    y = x * lax.rsqrt(jnp.mean(x * x, axis=-1, keepdims=True) + EPS) * norm_w
    return y * (1.0 + scale) + shift


def _mod_kernel(c_ref, w_ref, b_ref, o_ref):
    sc = _silu(c_ref[...]).astype(BF16)
    o_ref[0] = jnp.dot(sc, w_ref[0].astype(BF16), preferred_element_type=F32) + b_ref[0]


def _modulation(c_all, ada_w, ada_b):
    depth, d, n = ada_w.shape
    rows = c_all.shape[0]
    return pl.pallas_call(
        _mod_kernel,
        grid=(depth, n // MOD_COLS),
        in_specs=[
            pl.BlockSpec((rows, d), lambda l, j: (0, 0)),
            pl.BlockSpec((1, d, MOD_COLS), lambda l, j: (l, 0, j)),
            pl.BlockSpec((1, 1, MOD_COLS), lambda l, j: (l, 0, j)),
        ],
        out_specs=pl.BlockSpec((1, rows, MOD_COLS), lambda l, j: (l, 0, j)),
        out_shape=jax.ShapeDtypeStruct((depth, rows, n), F32),
        compiler_params=_cparams(("arbitrary", "arbitrary")),
        name="modulation",
    )(c_all, ada_w, ada_b.reshape(depth, 1, n))


def _mixer_residual(x, mod_ref, attn_ref, of_ref, ob_ref, z_ref, nw_ref, wa_ref, wd_ref):
    o = of_ref[...] + ob_ref[...]
    z = z_ref[...].astype(F32)
    nw = nw_ref[...]
    parts = []
    for hd in range(N_DN_HEADS):
        sl = slice(hd * DN_HEAD_DIM, (hd + 1) * DN_HEAD_DIM)
        oh = o[:, sl]
        parts.append(oh * lax.rsqrt(jnp.mean(oh * oh, axis=-1, keepdims=True) + EPS) * nw * _silu(z[:, sl]))
    dn = jnp.concatenate(parts, axis=1).astype(BF16)
    y = jnp.dot(attn_ref[...], wa_ref[...], preferred_element_type=F32)
    y = y + jnp.dot(dn, wd_ref[...], preferred_element_type=F32)
    return x + mod_ref[0, 2:3, :] * y


N_MIXER_REFS = 8


def _ffn_kernel(*refs, n_in, n_out, with_mixer, final_norm, rows1):
    x_refs, refs = refs[:n_in], refs[n_in:]
    if with_mixer:
        mixer_refs, refs = refs[:N_MIXER_REFS], refs[N_MIXER_REFS:]
    mod_ref, nw_ref, wg_ref, wu_ref, wd_ref = refs[:5]
    refs = refs[5:]
    if final_norm:
        fw_ref, refs = refs[0], refs[1:]
    o_refs, refs = refs[:n_out], refs[n_out:]
    tm = x_refs[0].shape[0]
    in_first = pl.program_id(0) < rows1 // tm
    if n_in == 2:
        stage_ref = refs[0]

        @pl.when(in_first)
        def _():
            stage_ref[...] = x_refs[0][...]

        @pl.when(jnp.logical_not(in_first))
        def _():
            stage_ref[...] = x_refs[1][...]

        x = stage_ref[...]
    else:
        x = x_refs[0][...]
    if with_mixer:
        x = _mixer_residual(x, *mixer_refs)
    h = _modulated_norm(x, nw_ref[...], mod_ref[0, 0:1, :], mod_ref[0, 1:2, :]).astype(BF16)
    g = jnp.dot(h, wg_ref[...], preferred_element_type=F32)
    u = jnp.dot(h, wu_ref[...], preferred_element_type=F32)
    a = (_silu(g) * u).astype(BF16)
    y = jnp.dot(a, wd_ref[...], preferred_element_type=F32)
    out = x + (0.5 * mod_ref[0, 2:3, :]) * y
    if final_norm:
        out = out * lax.rsqrt(jnp.mean(out * out, axis=-1, keepdims=True) + EPS) * fw_ref[...]
    if n_out == 2:
        @pl.when(in_first)
        def _():
            o_refs[0][...] = out

        @pl.when(jnp.logical_not(in_first))
        def _():
            o_refs[1][...] = out
    else:
        o_refs[0][...] = out


def _ffn(xs, mod, norm_w, wg, wu, wd, blk_rows, rows1, final_w=None, split_out=False, mixer=None):
    d = xs[0].shape[1]
    t = sum(x.shape[0] for x in xs)
    f = wg.shape[1]
    tm = FFN_ROWS
    per_blk = blk_rows // tm
    nb1 = rows1 // tm
    const = lambda i: (0, 0)
    row = lambda i: (i, 0)
    blk = lambda i: (i // per_blk, 0, 0)
    group1 = lambda i: (jnp.minimum(i, nb1 - 1), 0)
    group2 = lambda i: (jnp.maximum(i - nb1, 0), 0)
    in_specs = [pl.BlockSpec((tm, d), row)] if len(xs) == 1 else [pl.BlockSpec((tm, d), group1), pl.BlockSpec((tm, d), group2)]
    args = list(xs)
    if mixer is not None:
        m_mod, attn, o_f, o_b, z, dn_norm, w_attn, w_dn = mixer
        in_specs += [
            pl.BlockSpec((1, 3, d), blk),
            pl.BlockSpec((tm, ATTN_WIDTH), row),
            pl.BlockSpec((tm, DN_WIDTH), row),
            pl.BlockSpec((tm, DN_WIDTH), row),
            pl.BlockSpec((tm, DN_WIDTH), row),
            pl.BlockSpec((1, DN_HEAD_DIM), const),
            _resident((ATTN_WIDTH, d)),
            _resident((DN_WIDTH, d)),
        ]
        args += [m_mod, attn, o_f, o_b, z, dn_norm.reshape(1, DN_HEAD_DIM), w_attn, w_dn]
    in_specs += [
        pl.BlockSpec((1, 3, d), blk),
        pl.BlockSpec((1, d), const),
        _resident((d, f)),
        _resident((d, f)),
        _resident((f, d)),
    ]
    args += [mod, norm_w.reshape(1, d), wg, wu, wd]
    if final_w is not None:
        in_specs.append(pl.BlockSpec((1, d), const))
        args.append(final_w.reshape(1, d))
    if split_out:
        out_specs = [pl.BlockSpec((tm, d), group1), pl.BlockSpec((tm, d), group2)]
        out_shape = [jax.ShapeDtypeStruct((rows1, d), F32), jax.ShapeDtypeStruct((t - rows1, d), F32)]
    else:
        out_specs = [pl.BlockSpec((tm, d), row)]
        out_shape = [jax.ShapeDtypeStruct((t, d), F32)]
    outs = pl.pallas_call(
        functools.partial(_ffn_kernel, n_in=len(xs), n_out=len(out_specs), with_mixer=mixer is not None,
                          final_norm=final_w is not None, rows1=rows1),
        grid=(t // tm,),
        in_specs=in_specs,
        out_specs=out_specs,
        out_shape=out_shape,
        scratch_shapes=[pltpu.VMEM((tm, d), F32)] if len(xs) == 2 else [],
        compiler_params=_cparams(("arbitrary",)),
        name="ffn",
    )(*args)
    return outs if split_out else outs[0]


def _inproj_kernel(xp_ref, x_ref, xn_ref, mod_ref, nw_ref, w_ref, wgt_ref, gpt_ref,
                   cos_ref, sfw_ref, sbk_ref, cw_ref, *rest, rows1, len1, len2):
    n_views = len(DILATED_PATTERNS)
    qkv_refs = [rest[3 * n:3 * n + 3] for n in range(n_views)]
    dq_ref, dk_ref, dv_ref, z_ref, gbt_ref, stage_ref = rest[3 * n_views:]
    tm = x_ref.shape[0]
    nw, shift, scale = nw_ref[...], mod_ref[0, 0:1, :], mod_ref[0, 1:2, :]
    h_main = _modulated_norm(x_ref[...], nw, shift, scale)
    h_ext = jnp.concatenate([_modulated_norm(xp_ref[...], nw, shift, scale), h_main,
                             _modulated_norm(xn_ref[...], nw, shift, scale)], axis=0).astype(BF16)
    h = h_main.astype(BF16)
    cos, sfw, sbk = cos_ref[...], sfw_ref[...], sbk_ref[...]

    def rope(xp):
        return xp * cos + pltpu.roll(xp, LANES - ROT_HALF, 1) * sfw + pltpu.roll(xp, ROT_HALF, 1) * sbk

    n_tiles = ATTN_WIDTH // LANES
    o0 = 3 * ATTN_WIDTH

    def attn_epilogue(a, pa):
        for j in range(n_tiles):
            col = pa[:, j * LANES:(j + 1) * LANES]
            if a == 0:
                col = rope(col) * ATTN_Q_SCALE
            elif a == 1:
                col = rope(col)
            stage_ref[a * n_tiles + j] = col
        for n, (_, dil) in enumerate(DILATED_PATTERNS):
            rows = tm // dil
            for j in range(n_tiles):
                for c in range(dil):
                    lo = c * ATTN_WIDTH + j * LANES
                    src = stage_ref[a * n_tiles + j, pl.ds(c, rows, stride=dil), :] if dil > 1 else stage_ref[a * n_tiles + j]
                    qkv_refs[n][a][:, lo:lo + LANES] = src.astype(BF16)

    r0 = pl.program_id(0) * tm
    in_first = r0 < rows1
    seq_len = jnp.where(in_first, len1, len2)
    off = jnp.where(in_first, r0, r0 - rows1)
    keep_prev = jnp.where((off % seq_len) == 0, 0.0, 1.0).astype(F32)
    keep_next = jnp.where(((off + tm) % seq_len) == 0, 0.0, 1.0).astype(F32)
    pad = CONV_K // 2
    n_ext = tm + 2 * HALO_ROWS

    def dn_epilogue(a, pe):
        out_ref = (dq_ref, dk_ref, dv_ref)[a]
        cols = slice(a * DN_WIDTH, (a + 1) * DN_WIDTH)
        pe = jnp.concatenate([pe[0:HALO_ROWS] * keep_prev, pe[HALO_ROWS:HALO_ROWS + tm],
                              pe[HALO_ROWS + tm:] * keep_next], axis=0)
        taps = [pe * cw_ref[j:j + 1, cols] for j in range(CONV_K)]
        before, after = taps[0], taps[CONV_K - 1]
        for j in range(1, pad):
            before = taps[j] + pltpu.roll(before, 1, 0)
            after = taps[CONV_K - 1 - j] + pltpu.roll(after, n_ext - 1, 0)
        y = taps[pad] + pltpu.roll(before, 1, 0) + pltpu.roll(after, n_ext - 1, 0)
        y = _silu(y[HALO_ROWS:HALO_ROWS + tm])
        for hd in range(N_DN_HEADS):
            sl = slice(hd * DN_HEAD_DIM, (hd + 1) * DN_HEAD_DIM)
            yh = y[:, sl]
            if a == 0:
                yh = yh * lax.rsqrt(jnp.sum(yh * yh, axis=-1, keepdims=True) + EPS) * (DN_HEAD_DIM ** -0.5)
            elif a == 1:
                yh = yh * lax.rsqrt(jnp.sum(yh * yh, axis=-1, keepdims=True) + EPS)
            out_ref[:, sl] = yh.astype(BF16)

    def z_epilogue(_, pz):
        z_ref[...] = pz.astype(BF16)

    groups = [(h, a * ATTN_WIDTH, ATTN_WIDTH, attn_epilogue, a) for a in range(3)]
    groups += [(h_ext, o0 + a * DN_WIDTH, DN_WIDTH, dn_epilogue, a) for a in range(3)]
    groups.append((h, o0 + 3 * DN_WIDTH, DN_WIDTH, z_epilogue, None))

    for lhs, lo, width, epilogue, arg in groups:
        epilogue(arg, jnp.dot(lhs, w_ref[:, lo:lo + width], preferred_element_type=F32))

    abt = lax.dot_general(wgt_ref[...], h, (((1,), (1,)), ((), ())), preferred_element_type=F32)
    is_gt = lax.broadcasted_iota(jnp.int32, abt.shape, 0) < 2 * N_DN_HEADS
    gpt = gpt_ref[...]
    gbt_ref[...] = jnp.where(is_gt, -jnp.exp(gpt[:, 0:1]) * jax.nn.softplus(abt + gpt[:, 1:2]), jax.nn.sigmoid(abt))


def _in_proj(x, mod, norm_w, w_main, w_gate_t, gate_par_t, rope_tabs, conv_w, blk_rows, seq_lens, t1):
    t, d = x.shape
    tm = PROJ_ROWS
    per_blk = blk_rows // tm
    n_main = w_main.shape[1]
    nb1 = t1 // tm
    s1b, s2b = seq_lens[0] // tm, seq_lens[1] // tm
    per_halo = tm // HALO_ROWS
    n_halo = t // HALO_ROWS
    const = lambda i: (0, 0)
    row = lambda i: (i, 0)
    pos = lambda i: (jnp.where(i < nb1, i % s1b, (i - nb1) % s2b), 0)
    cos, sfw, sbk = rope_tabs
    view_specs, view_shapes = [], []
    for _, dil in DILATED_PATTERNS:
        view_specs += [pl.BlockSpec((tm // dil, dil * ATTN_WIDTH), row)] * 3
        view_shapes += [jax.ShapeDtypeStruct((t // dil, dil * ATTN_WIDTH), BF16)] * 3
    outs = pl.pallas_call(
        functools.partial(_inproj_kernel, rows1=t1, len1=seq_lens[0], len2=seq_lens[1]),
        grid=(t // tm,),
        in_specs=[
            pl.BlockSpec((HALO_ROWS, d), lambda i: (jnp.maximum(i * per_halo - 1, 0), 0)),
            pl.BlockSpec((tm, d), row),
            pl.BlockSpec((HALO_ROWS, d), lambda i: (jnp.minimum((i + 1) * per_halo, n_halo - 1), 0)),
            pl.BlockSpec((1, 3, d), lambda i: (i // per_blk, 0, 0)),
            pl.BlockSpec((1, d), const),
            _resident((d, n_main)),
            pl.BlockSpec((N_GATES, d), const),
            pl.BlockSpec((N_GATES, 2), const),
            pl.BlockSpec((tm, LANES), pos),
            pl.BlockSpec((tm, LANES), pos),
            pl.BlockSpec((tm, LANES), pos),
            pl.BlockSpec((CONV_K, 3 * DN_WIDTH), const),
        ],
        out_specs=view_specs + [pl.BlockSpec((tm, DN_WIDTH), row)] * 4 + [
            pl.BlockSpec((N_GATES, tm), lambda i: (0, i)),
        ],
        out_shape=view_shapes + [jax.ShapeDtypeStruct((t, DN_WIDTH), BF16)] * 4 + [
            jax.ShapeDtypeStruct((N_GATES, t), F32),
        ],
        scratch_shapes=[pltpu.VMEM((3 * ATTN_WIDTH // LANES, tm, LANES), F32)],
        compiler_params=_cparams(("arbitrary",)),
        name="in_proj",
    )(x, x, x, mod, norm_w.reshape(1, d), w_main, w_gate_t, gate_par_t, cos, sfw, sbk, conv_w)
    n_qkv = 3 * len(DILATED_PATTERNS)
    qkv_views = [outs[3 * n:3 * n + 3] for n in range(len(DILATED_PATTERNS))]
    return qkv_views, outs[n_qkv:]


def _rope_tables(max_len):
    half = ROT_HALF
    inv = ROPE_THETA ** (-jnp.arange(half, dtype=F32) / half)
    ang = jnp.arange(max_len, dtype=F32)[:, None] * inv[None, :]
    cos, sin = jnp.cos(ang), jnp.sin(ang)
    ones = jnp.ones((max_len, HEAD_DIM - 2 * half), F32)
    zeros_h = jnp.zeros((max_len, half), F32)
    zeros_r = jnp.zeros((max_len, HEAD_DIM - 2 * half), F32)
    cos_h = jnp.concatenate([cos, cos, ones], axis=1)
    sfw_h = jnp.concatenate([-sin, zeros_h, zeros_r], axis=1)
    sbk_h = jnp.concatenate([zeros_h, sin, zeros_r], axis=1)
    rep = LANES // HEAD_DIM
    return tuple(jnp.tile(a, (1, rep)) for a in (cos_h, sfw_h, sbk_h))


def _attn_kernel(q_ref, kp_ref, km_ref, kn_ref, vp_ref, vm_ref, vn_ref, *rest,
                 dil, next_dil, first, radius, tq, rows1, len1, len2):
    last = next_dil is None
    if not first:
        op_ref, wp_ref = rest[:2]
        rest = rest[2:]
    if last:
        o_ref, acc_scr, nat_scr = rest
    else:
        acc_out_ref, ml_out_ref, acc_scr, ml_scr = rest
    tb = q_ref.shape[0]
    n_sub = tb // tq
    tk = tq + 2 * radius
    n_tiles = ATTN_WIDTH // LANES
    n_res = q_ref.shape[1] // ATTN_WIDTH
    c0 = pl.program_id(1) * n_res
    r0 = pl.program_id(0) * tb
    in_first = r0 < rows1
    seq_len = jnp.where(in_first, len1, len2)
    off = jnp.where(in_first, r0, r0 - rows1)
    first_key = jnp.where((off % seq_len) == 0, radius, 0)
    end_key = jnp.where(((off + tb) % seq_len) == 0, tq + radius, tk)

    qi = lax.broadcasted_iota(jnp.int32, (tq, tk), 0)
    kj = lax.broadcasted_iota(jnp.int32, (tq, tk), 1)
    in_band = jnp.abs(kj - radius - qi) <= radius
    biases = []
    for sub in range(n_sub):
        valid = in_band
        if sub == 0:
            valid = valid & (kj >= first_key)
        if sub == n_sub - 1:
            valid = valid & (kj < end_key)
        bias = jnp.where(valid, 0.0, NEG_BIG).astype(F32)
        biases.append(jnp.concatenate([bias, bias], axis=0))

    lane = lax.broadcasted_iota(jnp.int32, (1, LANES), 1)
    low_half = lane < HEAD_DIM
    nt = (((1,), (1,)), ((), ()))

    def unstack(x):
        return jnp.where(low_half, x[:tq], x[tq:])

    if not first:
        e_row = lax.broadcasted_iota(jnp.int32, (LANES, ATTN_WIDTH), 0)
        e_col = lax.broadcasted_iota(jnp.int32, (LANES, ATTN_WIDTH), 1)
        spread = jnp.where(e_row == e_col // LANES + HEAD_DIM * ((e_col % LANES) // HEAD_DIM), 1.0, 0.0).astype(BF16)
        lse_old = {}
        for r in range(n_res):
            for sub in range(n_sub):
                wc = wp_ref[sub * tq:(sub + 1) * tq, r * LANES:(r + 1) * LANES]
                hi = wc.astype(BF16)
                lo = (wc - hi.astype(F32)).astype(BF16)
                ex = jnp.dot(jnp.concatenate([hi, lo], axis=0), spread, preferred_element_type=F32)
                lse_old[(r, sub)] = ex[:tq] + ex[tq:]

    pairs = []
    for r in range(n_res):
        for j in range(n_tiles):
            sl = slice(r * ATTN_WIDTH + j * LANES, r * ATTN_WIDTH + (j + 1) * LANES)
            k_all = jnp.concatenate([kp_ref[:, sl], km_ref[:, sl], kn_ref[:, sl]], axis=0)
            v_all = jnp.concatenate([vp_ref[:, sl], vm_ref[:, sl], vn_ref[:, sl]], axis=0)
            for sub in range(n_sub):
                rows = slice(sub * tq, (sub + 1) * tq)
                q2 = q_ref[rows, sl]
                zero = jnp.zeros_like(q2)
                qs = jnp.concatenate([jnp.where(low_half, q2, zero), jnp.where(low_half, zero, q2)], axis=0)
                kw = k_all[sub * tq:sub * tq + tk]
                pairs.append(dict(r=r, j=j, sub=sub, sl=sl, rows=rows, vw=v_all[sub * tq:sub * tq + tk],
                                  s=lax.dot_general(qs, kw, nt, preferred_element_type=F32) + biases[sub]))
    ones = jnp.ones((tk, LANES), BF16)
    for pr in pairs:
        m_rows = jnp.broadcast_to(jnp.max(pr["s"], axis=-1, keepdims=True), (2 * tq, LANES))
        pr["m"] = unstack(m_rows)
        pr["p"] = jnp.exp2(pr["s"] - jnp.concatenate([m_rows] * (tk // LANES), axis=1)).astype(BF16)
    for pr in pairs:
        pr["pv"] = jnp.dot(pr["p"], pr["vw"], preferred_element_type=F32)
        pr["l"] = jnp.dot(pr["p"], ones, preferred_element_type=F32)
    for pr in pairs:
        den = unstack(pr["l"])
        o = unstack(pr["pv"]) / den
        lse = pr["m"] + jnp.log2(den)
        if not first:
            lse_prev = lse_old[(pr["r"], pr["sub"])][:, pr["j"] * LANES:(pr["j"] + 1) * LANES]
            top = jnp.maximum(lse_prev, lse)
            w_prev, w_cur = jnp.exp2(lse_prev - top), jnp.exp2(lse - top)
            tot = w_prev + w_cur
            o = (w_prev * op_ref[pr["rows"], pr["sl"]] + w_cur * o) / tot
            lse = top + jnp.log2(tot)
        pr["lse"] = lse
        acc_scr[c0 + pr["r"], pr["j"], pr["rows"], :] = o
    if not last:
        for r in range(n_res):
            for sub in range(n_sub):
                wc = jnp.zeros((tq, LANES), F32)
                for pr in pairs:
                    if pr["r"] == r and pr["sub"] == sub:
                        keep = jnp.logical_or(lane == pr["j"], lane == HEAD_DIM + pr["j"])
                        wc = jnp.where(keep, pr["lse"], wc)
                ml_scr[c0 + r, sub * tq:(sub + 1) * tq, :] = wc

    @pl.when(c0 + n_res == dil)
    def _():
        if last:
            for cc in range(dil):
                for j in range(n_tiles):
                    nat_scr[j, pl.ds(cc, tb, stride=dil), :] = acc_scr[cc, j]
            for j in range(n_tiles):
                o_ref[:, j * LANES:(j + 1) * LANES] = nat_scr[j].astype(o_ref.dtype)
        else:
            ratio = next_dil // dil
            rows_out = tb // ratio
            for cc in range(dil):
                for m in range(ratio):
                    cb = m * dil + cc
                    rows = pl.ds(m, rows_out, stride=ratio)
                    for j in range(n_tiles):
                        lo = cb * ATTN_WIDTH + j * LANES
                        acc_out_ref[:, lo:lo + LANES] = acc_scr[cc, j, rows, :]
                    ml_out_ref[:, cb * LANES:(cb + 1) * LANES] = ml_scr[cc, rows, :]


def _attn_pattern(qkv, prev, dil, next_dil, radius, t, t1, seq_lens):
    q, k, v = qkv
    w = ATTN_WIDTH
    first, last = prev is None, next_dil is None
    rows = t // dil
    tq, n_sub, n_res = ATTN_TILE[dil]
    tb = tq * n_sub
    per_q = tb // radius
    n_halo = rows // radius
    n_tiles = w // LANES
    wr = n_res * w
    main = lambda i, c: (i, c)
    before = lambda i, c: (jnp.maximum(i * per_q - 1, 0), c)
    after = lambda i, c: (jnp.minimum((i + 1) * per_q, n_halo - 1), c)
    whole = lambda i, c: (i, 0)
    kv_specs = [pl.BlockSpec((radius, wr), before), pl.BlockSpec((tb, wr), main), pl.BlockSpec((radius, wr), after)]
    in_specs = [pl.BlockSpec((tb, wr), main)] + kv_specs + kv_specs
    args = [q, k, k, k, v, v, v]
    if not first:
        in_specs += [pl.BlockSpec((tb, wr), main), pl.BlockSpec((tb, n_res * LANES), main)]
        args += list(prev)
    scratch = [pltpu.VMEM((dil, n_tiles, tb, LANES), F32)]
    if last:
        out_specs = [pl.BlockSpec((tb * dil, w), whole)]
        out_shape = [jax.ShapeDtypeStruct((t, w), BF16)]
        scratch.append(pltpu.VMEM((n_tiles, tb * dil, LANES), F32))
    else:
        rows_out = tb * dil // next_dil
        out_specs = [pl.BlockSpec((rows_out, next_dil * w), whole), pl.BlockSpec((rows_out, next_dil * LANES), whole)]
        out_shape = [jax.ShapeDtypeStruct((t // next_dil, next_dil * w), F32),
                     jax.ShapeDtypeStruct((t // next_dil, next_dil * LANES), F32)]
        scratch.append(pltpu.VMEM((dil, tb, LANES), F32))
    return pl.pallas_call(
        functools.partial(_attn_kernel, dil=dil, next_dil=next_dil, first=first, radius=radius, tq=tq,
                          rows1=t1 // dil, len1=seq_lens[0] // dil, len2=seq_lens[1] // dil),
        grid=(rows // tb, dil // n_res),
        in_specs=in_specs,
        out_specs=out_specs,
        out_shape=out_shape,
        scratch_shapes=scratch,
        compiler_params=_cparams(("arbitrary", "arbitrary")),
        name=f"attn_d{dil}",
    )(*args)


def _attention(qkv_views, t, t1, seq_lens):
    state = None
    n = len(DILATED_PATTERNS)
    for idx, (window, dil) in enumerate(DILATED_PATTERNS):
        next_dil = DILATED_PATTERNS[idx + 1][1] if idx + 1 < n else None
        state = _attn_pattern(qkv_views[idx], state, dil, next_dil, window // (2 * dil), t, t1, seq_lens)
    return state[0]


def _bdot(a, b):
    return jnp.dot(a.astype(BF16), b.astype(BF16), preferred_element_type=F32)


def _split3(x):
    hi = x.astype(BF16)
    r1 = x - hi.astype(F32)
    mid = r1.astype(BF16)
    lo = (r1 - mid.astype(F32)).astype(BF16)
    return hi, mid, lo


def _dn_kernel(qf_ref, kf_ref, vf_ref, gtf_ref, qb_ref, kb_ref, vb_ref, gtb_ref,
               of_ref, ob_ref, sf_ref, sb_ref, *, c, n_chunks, chunks1, len1, len2):
    n_sub = qf_ref.shape[0] // c
    i = pl.program_id(0)

    def seq_pos(ci):
        in_first = ci < chunks1
        seq_len = jnp.where(in_first, len1, len2)
        off = jnp.where(in_first, ci, ci - chunks1)
        return off % seq_len, seq_len

    pos_f, _ = seq_pos(i * n_sub)
    pos_b, len_b = seq_pos(n_chunks - 1 - i * n_sub)

    @pl.when(pos_f == 0)
    def _():
        sf_ref[...] = jnp.zeros_like(sf_ref)

    @pl.when(pos_b == len_b - 1)
    def _():
        sb_ref[...] = jnp.zeros_like(sb_ref)

    ii = lax.broadcasted_iota(jnp.int32, (c, c), 0)
    jj = lax.broadcasted_iota(jnp.int32, (c, c), 1)
    lower, lower_strict = ii >= jj, ii > jj
    upper, upper_strict = ii <= jj, ii < jj
    eye = jnp.where(ii == jj, 1.0, 0.0).astype(F32)
    tri_l = jnp.where(lower, 1.0, 0.0).astype(F32)
    tri_u = jnp.where(upper, 1.0, 0.0).astype(F32)
    level_masks = [jnp.where(ii // INV_BASE == jj // INV_BASE, 1.0, 0.0).astype(F32)]
    b = INV_BASE
    while b < c:
        same_outer = ii // (2 * b) == jj // (2 * b)
        level_masks.append(jnp.where(same_outer & (ii // b != jj // b), 1.0, 0.0).astype(F32))
        b *= 2
    ones_cd = jnp.ones((c, DN_HEAD_DIM), BF16)
    nt = (((1,), (1,)), ((), ()))

    chains = []
    for forward, (q_ref, k_ref, v_ref, gt_ref, o_ref, s_ref) in (
            (True, (qf_ref, kf_ref, vf_ref, gtf_ref, of_ref, sf_ref)),
            (False, (qb_ref, kb_ref, vb_ref, gtb_ref, ob_ref, sb_ref))):
        tri_r = tri_u if forward else tri_l
        row_rhs = jnp.concatenate([tri_r.astype(BF16), ones_cd], axis=1)
        keep, strict = (lower, lower_strict) if forward else (upper, upper_strict)
        d0 = 0 if forward else N_DN_HEADS
        for order in range(n_sub):
            sub = order if forward else n_sub - 1 - order
            rows = slice(sub * c, (sub + 1) * c)
            gt = gt_ref[:, rows]
            row = sum(jnp.dot(piece, row_rhs, preferred_element_type=F32) for piece in _split3(gt))
            cs_row, tot_row = row[:, :c], row[:, c:]
            cs_col, gates_col = jnp.transpose(cs_row), jnp.transpose(gt)
            tot_col = jnp.broadcast_to(cs_col[c - 1:c] if forward else cs_col[0:1], cs_col.shape)
            for hd in range(N_DN_HEADS):
                gi = d0 + hd
                bi = 2 * N_DN_HEADS + d0 + hd
                sl = slice(hd * DN_HEAD_DIM, (hd + 1) * DN_HEAD_DIM)
                chains.append(dict(
                    order=order, rows=rows, sl=sl, hd=hd, keep=keep, strict=strict, o_ref=o_ref, s_ref=s_ref,
                    q=q_ref[rows, sl], k=k_ref[rows, sl], v=v_ref[rows, sl],
                    gc_col=cs_col[:, gi:gi + 1], gc_row=cs_row[gi:gi + 1, :], beta=gates_col[:, bi:bi + 1],
                    tot_col=tot_col[:, gi:gi + 1], tot_row=tot_row[gi:gi + 1, :]))

    for ch in chains:
        qh, kh = ch["q"], ch["k"]
        both = lax.dot_general(jnp.concatenate([qh, kh], axis=0), kh, nt, preferred_element_type=F32)
        decay = jnp.where(ch["keep"], jnp.exp(jnp.minimum(ch["gc_col"] - ch["gc_row"], 0.0)), 0.0)
        ch["a"] = jnp.where(ch["strict"], ch["beta"] * both[c:] * decay, 0.0)
        ch["qk"] = (both[:c] * decay).astype(BF16)
    for ch in chains:
        dg = ch["a"] * level_masks[0]
        ch["x"] = eye - dg
        ch["y"] = _bdot(dg, dg)
    for ch in chains:
        ch["x"] = ch["x"] + _bdot(ch["x"], ch["y"])
    for off_mask in level_masks[1:]:
        for ch in chains:
            ch["y"] = _bdot(ch["a"] * off_mask, ch["x"])
        for ch in chains:
            ch["x"] = ch["x"] - _bdot(ch["x"], ch["y"])
    for ch in chains:
        kh = ch["k"].astype(F32)
        egc = jnp.exp(ch["gc_col"])
        rhs = jnp.concatenate([ch["v"].astype(F32) * ch["beta"], kh * (ch["beta"] * egc)], axis=1)
        ch["uw"] = _bdot(ch["x"], rhs)
        ch["lhs_q"] = ch["q"].astype(F32) * egc
        ch["k_dec_t"] = jnp.transpose(kh * jnp.exp(ch["tot_col"] - ch["gc_col"])).astype(BF16)
    states = {}
    for order in range(n_sub):
        group = [ch for ch in chains if ch["order"] == order]
        for ch in group:
            key = (id(ch["s_ref"]), ch["hd"])
            ch["state"] = states[key] if order else ch["s_ref"][ch["hd"]]
            lhs = jnp.concatenate([ch["uw"][:, DN_HEAD_DIM:], ch["lhs_q"]], axis=0)
            ch["ws"] = _bdot(lhs, ch["state"])
        for ch in group:
            v_new = (ch["uw"][:, :DN_HEAD_DIM] - ch["ws"][:c]).astype(BF16)
            ch["o_ref"][ch["rows"], ch["sl"]] = ch["ws"][c:] + jnp.dot(ch["qk"], v_new, preferred_element_type=F32)
            ch["v_new"] = v_new
        for ch in group:
            states[(id(ch["s_ref"]), ch["hd"])] = ch["state"] * jnp.exp(ch["tot_row"]) + jnp.dot(
                ch["k_dec_t"], ch["v_new"], preferred_element_type=F32)
    for ch in chains:
        if ch["order"] == n_sub - 1:
            ch["s_ref"][ch["hd"]] = states[(id(ch["s_ref"]), ch["hd"])]


def _deltanet(q, k, v, gates_t, t1, seq_lens):
    t, w = q.shape
    c = DN_CHUNK
    rows = c * DN_CHUNKS_PER_STEP
    n = t // rows
    fwd = lambda i: (i, 0)
    bwd = lambda i: (n - 1 - i, 0)
    fwd_t = lambda i: (0, i)
    bwd_t = lambda i: (0, n - 1 - i)

    def specs(row_map, col_map):
        return [pl.BlockSpec((rows, w), row_map)] * 3 + [pl.BlockSpec((N_GATES, rows), col_map)]

    state = pltpu.VMEM((N_DN_HEADS, DN_HEAD_DIM, DN_HEAD_DIM), F32)
    return pl.pallas_call(
        functools.partial(_dn_kernel, c=c, n_chunks=t // c, chunks1=t1 // c, len1=seq_lens[0] // c,
                          len2=seq_lens[1] // c),
        grid=(n,),
        in_specs=specs(fwd, fwd_t) + specs(bwd, bwd_t),
        out_specs=[pl.BlockSpec((rows, w), fwd), pl.BlockSpec((rows, w), bwd)],
        out_shape=[jax.ShapeDtypeStruct((t, w), F32)] * 2,
        scratch_shapes=[state, state],
        compiler_params=_cparams(("arbitrary",)),
        name="deltanet",
    )(q, k, v, gates_t, q, k, v, gates_t)


def kernel(x_prompt, x_sample, c_prompt, c_sample, ada_w, ada_b, norm_ffn1, ffn1_w_gate, ffn1_w_up, ffn1_w_down, norm_mix, w_in, conv_w, a_log, dt_bias, dn_norm, w_out, norm_ffn2, ffn2_w_gate, ffn2_w_up, ffn2_w_down, norm_final):
    b1, s1, d = x_prompt.shape
    b2, s2, _ = x_sample.shape
    depth = ada_w.shape[0]
    t1 = b1 * s1
    seq_lens = (s1, s2)
    assert s2 % s1 == 0 and t1 % s2 == 0, "flat layout needs nested sequence lengths"
    t = t1 + b2 * s2
    xs = [x_prompt.reshape(t1, d), x_sample.reshape(b2 * s2, d)]

    n_seq = b1 + b2
    c_all = jnp.concatenate([c_prompt, c_sample, jnp.zeros((-n_seq % SUBLANES, d), F32)], axis=0)
    mod = _modulation(c_all, ada_w, ada_b)
    blk_seq = jnp.concatenate([jnp.arange(b1), b1 + jnp.repeat(jnp.arange(b2), s2 // s1)])
    mod = mod[:, blk_seq].reshape(depth, blk_seq.shape[0], N_MOD, d)

    rope_tabs = _rope_tables(max(s1, s2))
    n_main = 3 * ATTN_WIDTH + 4 * DN_WIDTH
    gate_par_t = jnp.stack([a_log.astype(F32).reshape(depth, -1), dt_bias.astype(F32).reshape(depth, -1)], axis=2)
    gate_par_t = jnp.pad(gate_par_t, ((0, 0), (0, N_GATES - gate_par_t.shape[1]), (0, 0)))

    for l in range(depth):
        bf = lambda a: a.astype(BF16)
        x = _ffn(xs if l == 0 else [x], mod[l, :, 0:3], norm_ffn1[l], bf(ffn1_w_gate[l]), bf(ffn1_w_up[l]),
                 bf(ffn1_w_down[l]), s1, t1)
        qkv_views, (dq, dk, dv, z, gates_t) = _in_proj(
            x, mod[l, :, 3:6], norm_mix[l], bf(w_in[l][:, :n_main]), bf(w_in[l][:, n_main:]).T,
            gate_par_t[l], rope_tabs, conv_w[l], s1, seq_lens, t1)
        attn = _attention(qkv_views, t, t1, seq_lens)
        o_f, o_b = _deltanet(dq, dk, dv, gates_t, t1, seq_lens)
        mixer = (mod[l, :, 3:6], attn, o_f, o_b, z, dn_norm[l], bf(w_out[l][:ATTN_WIDTH]), bf(w_out[l][ATTN_WIDTH:]))
        is_last = l == depth - 1
        x = _ffn([x], mod[l, :, 6:9], norm_ffn2[l], bf(ffn2_w_gate[l]), bf(ffn2_w_up[l]), bf(ffn2_w_down[l]), s1,
                 t1, final_w=norm_final if is_last else None, split_out=is_last, mixer=mixer)
    return x[0].reshape(b1, s1, d), x[1].reshape(b2, s2, d)
```

```python
import functools

import jax
import jax.numpy as jnp
from jax import lax
from jax.experimental import pallas as pl
from jax.experimental.pallas import tpu as pltpu

F32 = jnp.float32
BF16 = jnp.bfloat16

N_MOD = 9
EPS = 1e-6
N_ATTN_HEADS = 8
HEAD_DIM = 64
ATTN_WIDTH = N_ATTN_HEADS * HEAD_DIM
ROT_HALF = HEAD_DIM // 8
ROPE_THETA = 500000.0
ATTN_Q_SCALE = HEAD_DIM ** -0.5 * 1.4426950408889634
DILATED_PATTERNS = ((128, 1), (512, 4), (2048, 16))
N_DN_HEADS = 4
DN_HEAD_DIM = 128
DN_WIDTH = N_DN_HEADS * DN_HEAD_DIM
CONV_K = 5
N_GATES = 4 * N_DN_HEADS

LANES = 128
SUBLANES = 8
VMEM_LIMIT_BYTES = 56 * 1024 * 1024

FFN_ROWS = 512
PROJ_ROWS = 512
ATTN_TILE = {1: (128, 4, 1), 4: (128, 1, 4), 16: (128, 1, 4)}
DN_CHUNK = 128
DN_CHUNKS_PER_STEP = 2
INV_BASE = 4
MOD_COLS = 1152
HALO_ROWS = SUBLANES
NEG_BIG = -1e30


def _cparams(sem):
    return pltpu.CompilerParams(dimension_semantics=sem, vmem_limit_bytes=VMEM_LIMIT_BYTES)


def _resident(shape):
    return pl.BlockSpec(shape, lambda *_: (0, 0), pipeline_mode=pl.Buffered(1))


def _silu(x):
    return x * jax.nn.sigmoid(x)


def _modulated_norm(x, norm_w, shift, scale):
    y = x * lax.rsqrt(jnp.mean(x * x, axis=-1, keepdims=True) + EPS) * norm_w
    return y * (1.0 + scale) + shift


def _mod_kernel(c_ref, w_ref, b_ref, o_ref):
    sc = _silu(c_ref[...]).astype(BF16)
    o_ref[0] = jnp.dot(sc, w_ref[0].astype(BF16), preferred_element_type=F32) + b_ref[0]


def _modulation(c_all, ada_w, ada_b):
    depth, d, n = ada_w.shape
    rows = c_all.shape[0]
    return pl.pallas_call(
        _mod_kernel,
        grid=(depth, n // MOD_COLS),
        in_specs=[
            pl.BlockSpec((rows, d), lambda l, j: (0, 0)),
            pl.BlockSpec((1, d, MOD_COLS), lambda l, j: (l, 0, j)),
            pl.BlockSpec((1, 1, MOD_COLS), lambda l, j: (l, 0, j)),
        ],
        out_specs=pl.BlockSpec((1, rows, MOD_COLS), lambda l, j: (l, 0, j)),
        out_shape=jax.ShapeDtypeStruct((depth, rows, n), F32),
        compiler_params=_cparams(("arbitrary", "arbitrary")),
        name="modulation",
    )(c_all, ada_w, ada_b.reshape(depth, 1, n))


def _mixer_residual(x, mod_ref, attn_ref, of_ref, ob_ref, z_ref, nw_ref, wa_ref, wd_ref):
    o = of_ref[...] + ob_ref[...]
    z = z_ref[...].astype(F32)
    nw = nw_ref[...]
    parts = []
    for hd in range(N_DN_HEADS):
        sl = slice(hd * DN_HEAD_DIM, (hd + 1) * DN_HEAD_DIM)
        oh = o[:, sl]
        parts.append(oh * lax.rsqrt(jnp.mean(oh * oh, axis=-1, keepdims=True) + EPS) * nw * _silu(z[:, sl]))
    dn = jnp.concatenate(parts, axis=1).astype(BF16)
    y = jnp.dot(attn_ref[...], wa_ref[...], preferred_element_type=F32)
    y = y + jnp.dot(dn, wd_ref[...], preferred_element_type=F32)
    return x + mod_ref[0, 2:3, :] * y


N_MIXER_REFS = 8


def _ffn_kernel(*refs, n_in, n_out, with_mixer, final_norm, rows1):
    x_refs, refs = refs[:n_in], refs[n_in:]
    if with_mixer:
        mixer_refs, refs = refs[:N_MIXER_REFS], refs[N_MIXER_REFS:]
    mod_ref, nw_ref, wg_ref, wu_ref, wd_ref = refs[:5]
    refs = refs[5:]
    if final_norm:
        fw_ref, refs = refs[0], refs[1:]
    o_refs, refs = refs[:n_out], refs[n_out:]
    tm = x_refs[0].shape[0]
    in_first = pl.program_id(0) < rows1 // tm
    if n_in == 2:
        stage_ref = refs[0]

        @pl.when(in_first)
        def _():
            stage_ref[...] = x_refs[0][...]

        @pl.when(jnp.logical_not(in_first))
        def _():
            stage_ref[...] = x_refs[1][...]

        x = stage_ref[...]
    else:
        x = x_refs[0][...]
    if with_mixer:
        x = _mixer_residual(x, *mixer_refs)
    h = _modulated_norm(x, nw_ref[...], mod_ref[0, 0:1, :], mod_ref[0, 1:2, :]).astype(BF16)
    g = jnp.dot(h, wg_ref[...], preferred_element_type=F32)
    u = jnp.dot(h, wu_ref[...], preferred_element_type=F32)
    a = (_silu(g) * u).astype(BF16)
    y = jnp.dot(a, wd_ref[...], preferred_element_type=F32)
    out = x + (0.5 * mod_ref[0, 2:3, :]) * y
    if final_norm:
        out = out * lax.rsqrt(jnp.mean(out * out, axis=-1, keepdims=True) + EPS) * fw_ref[...]
    if n_out == 2:
        @pl.when(in_first)
        def _():
            o_refs[0][...] = out

        @pl.when(jnp.logical_not(in_first))
        def _():
            o_refs[1][...] = out
    else:
        o_refs[0][...] = out


def _ffn(xs, mod, norm_w, wg, wu, wd, blk_rows, rows1, final_w=None, split_out=False, mixer=None):
    d = xs[0].shape[1]
    t = sum(x.shape[0] for x in xs)
    f = wg.shape[1]
    tm = FFN_ROWS
    per_blk = blk_rows // tm
    nb1 = rows1 // tm
    const = lambda i: (0, 0)
    row = lambda i: (i, 0)
    blk = lambda i: (i // per_blk, 0, 0)
    group1 = lambda i: (jnp.minimum(i, nb1 - 1), 0)
    group2 = lambda i: (jnp.maximum(i - nb1, 0), 0)
    in_specs = [pl.BlockSpec((tm, d), row)] if len(xs) == 1 else [pl.BlockSpec((tm, d), group1), pl.BlockSpec((tm, d), group2)]
    args = list(xs)
    if mixer is not None:
        m_mod, attn, o_f, o_b, z, dn_norm, w_attn, w_dn = mixer
        in_specs += [
            pl.BlockSpec((1, 3, d), blk),
            pl.BlockSpec((tm, ATTN_WIDTH), row),
            pl.BlockSpec((tm, DN_WIDTH), row),
            pl.BlockSpec((tm, DN_WIDTH), row),
            pl.BlockSpec((tm, DN_WIDTH), row),
            pl.BlockSpec((1, DN_HEAD_DIM), const),
            _resident((ATTN_WIDTH, d)),
            _resident((DN_WIDTH, d)),
        ]
        args += [m_mod, attn, o_f, o_b, z, dn_norm.reshape(1, DN_HEAD_DIM), w_attn, w_dn]
    in_specs += [
        pl.BlockSpec((1, 3, d), blk),
        pl.BlockSpec((1, d), const),
        _resident((d, f)),
        _resident((d, f)),
        _resident((f, d)),
    ]
    args += [mod, norm_w.reshape(1, d), wg, wu, wd]
    if final_w is not None:
        in_specs.append(pl.BlockSpec((1, d), const))
        args.append(final_w.reshape(1, d))
    if split_out:
        out_specs = [pl.BlockSpec((tm, d), group1), pl.BlockSpec((tm, d), group2)]
        out_shape = [jax.ShapeDtypeStruct((rows1, d), F32), jax.ShapeDtypeStruct((t - rows1, d), F32)]
    else:
        out_specs = [pl.BlockSpec((tm, d), row)]
        out_shape = [jax.ShapeDtypeStruct((t, d), F32)]
    outs = pl.pallas_call(
        functools.partial(_ffn_kernel, n_in=len(xs), n_out=len(out_specs), with_mixer=mixer is not None,
                          final_norm=final_w is not None, rows1=rows1),
        grid=(t // tm,),
        in_specs=in_specs,
        out_specs=out_specs,
        out_shape=out_shape,
        scratch_shapes=[pltpu.VMEM((tm, d), F32)] if len(xs) == 2 else [],
        compiler_params=_cparams(("arbitrary",)),
        name="ffn",
    )(*args)
    return outs if split_out else outs[0]


def _inproj_kernel(xp_ref, x_ref, xn_ref, mod_ref, nw_ref, w_ref, wgt_ref, gpt_ref,
                   cos_ref, sfw_ref, sbk_ref, cw_ref, *rest, rows1, len1, len2):
    n_views = len(DILATED_PATTERNS)
    qkv_refs = [rest[3 * n:3 * n + 3] for n in range(n_views)]
    dq_ref, dk_ref, dv_ref, z_ref, gbt_ref, stage_ref = rest[3 * n_views:]
    tm = x_ref.shape[0]
    nw, shift, scale = nw_ref[...], mod_ref[0, 0:1, :], mod_ref[0, 1:2, :]
    h_main = _modulated_norm(x_ref[...], nw, shift, scale)
    h_ext = jnp.concatenate([_modulated_norm(xp_ref[...], nw, shift, scale), h_main,
                             _modulated_norm(xn_ref[...], nw, shift, scale)], axis=0).astype(BF16)
    h = h_main.astype(BF16)
    cos, sfw, sbk = cos_ref[...], sfw_ref[...], sbk_ref[...]

    def rope(xp):
        return xp * cos + pltpu.roll(xp, LANES - ROT_HALF, 1) * sfw + pltpu.roll(xp, ROT_HALF, 1) * sbk

    n_tiles = ATTN_WIDTH // LANES
    o0 = 3 * ATTN_WIDTH

    def attn_epilogue(a, pa):
        for j in range(n_tiles):
            col = pa[:, j * LANES:(j + 1) * LANES]
            if a == 0:
                col = rope(col) * ATTN_Q_SCALE
            elif a == 1:
                col = rope(col)
            stage_ref[a * n_tiles + j] = col
        for n, (_, dil) in enumerate(DILATED_PATTERNS):
            rows = tm // dil
            for j in range(n_tiles):
                for c in range(dil):
                    lo = c * ATTN_WIDTH + j * LANES
                    src = stage_ref[a * n_tiles + j, pl.ds(c, rows, stride=dil), :] if dil > 1 else stage_ref[a * n_tiles + j]
                    qkv_refs[n][a][:, lo:lo + LANES] = src.astype(BF16)

    r0 = pl.program_id(0) * tm
    in_first = r0 < rows1
    seq_len = jnp.where(in_first, len1, len2)
    off = jnp.where(in_first, r0, r0 - rows1)
    keep_prev = jnp.where((off % seq_len) == 0, 0.0, 1.0).astype(F32)
    keep_next = jnp.where(((off + tm) % seq_len) == 0, 0.0, 1.0).astype(F32)
    pad = CONV_K // 2
    n_ext = tm + 2 * HALO_ROWS

    def dn_epilogue(a, pe):
        out_ref = (dq_ref, dk_ref, dv_ref)[a]
        cols = slice(a * DN_WIDTH, (a + 1) * DN_WIDTH)
        pe = jnp.concatenate([pe[0:HALO_ROWS] * keep_prev, pe[HALO_ROWS:HALO_ROWS + tm],
                              pe[HALO_ROWS + tm:] * keep_next], axis=0)
        taps = [pe * cw_ref[j:j + 1, cols] for j in range(CONV_K)]
        before, after = taps[0], taps[CONV_K - 1]
        for j in range(1, pad):
            before = taps[j] + pltpu.roll(before, 1, 0)
            after = taps[CONV_K - 1 - j] + pltpu.roll(after, n_ext - 1, 0)
        y = taps[pad] + pltpu.roll(before, 1, 0) + pltpu.roll(after, n_ext - 1, 0)
        y = _silu(y[HALO_ROWS:HALO_ROWS + tm])
        for hd in range(N_DN_HEADS):
            sl = slice(hd * DN_HEAD_DIM, (hd + 1) * DN_HEAD_DIM)
            yh = y[:, sl]
            if a == 0:
                yh = yh * lax.rsqrt(jnp.sum(yh * yh, axis=-1, keepdims=True) + EPS) * (DN_HEAD_DIM ** -0.5)
            elif a == 1:
                yh = yh * lax.rsqrt(jnp.sum(yh * yh, axis=-1, keepdims=True) + EPS)
            out_ref[:, sl] = yh.astype(BF16)

    def z_epilogue(_, pz):
        z_ref[...] = pz.astype(BF16)

    groups = [(h, a * ATTN_WIDTH, ATTN_WIDTH, attn_epilogue, a) for a in range(3)]
    groups += [(h_ext, o0 + a * DN_WIDTH, DN_WIDTH, dn_epilogue, a) for a in range(3)]
    groups.append((h, o0 + 3 * DN_WIDTH, DN_WIDTH, z_epilogue, None))

    for lhs, lo, width, epilogue, arg in groups:
        epilogue(arg, jnp.dot(lhs, w_ref[:, lo:lo + width], preferred_element_type=F32))

    abt = lax.dot_general(wgt_ref[...], h, (((1,), (1,)), ((), ())), preferred_element_type=F32)
    is_gt = lax.broadcasted_iota(jnp.int32, abt.shape, 0) < 2 * N_DN_HEADS
    gpt = gpt_ref[...]
    gbt_ref[...] = jnp.where(is_gt, -jnp.exp(gpt[:, 0:1]) * jax.nn.softplus(abt + gpt[:, 1:2]), jax.nn.sigmoid(abt))


def _in_proj(x, mod, norm_w, w_main, w_gate_t, gate_par_t, rope_tabs, conv_w, blk_rows, seq_lens, t1):
    t, d = x.shape
    tm = PROJ_ROWS
    per_blk = blk_rows // tm
    n_main = w_main.shape[1]
    nb1 = t1 // tm
    s1b, s2b = seq_lens[0] // tm, seq_lens[1] // tm
    per_halo = tm // HALO_ROWS
    n_halo = t // HALO_ROWS
    const = lambda i: (0, 0)
    row = lambda i: (i, 0)
    pos = lambda i: (jnp.where(i < nb1, i % s1b, (i - nb1) % s2b), 0)
    cos, sfw, sbk = rope_tabs
    view_specs, view_shapes = [], []
    for _, dil in DILATED_PATTERNS:
        view_specs += [pl.BlockSpec((tm // dil, dil * ATTN_WIDTH), row)] * 3
        view_shapes += [jax.ShapeDtypeStruct((t // dil, dil * ATTN_WIDTH), BF16)] * 3
    outs = pl.pallas_call(
        functools.partial(_inproj_kernel, rows1=t1, len1=seq_lens[0], len2=seq_lens[1]),
        grid=(t // tm,),
        in_specs=[
            pl.BlockSpec((HALO_ROWS, d), lambda i: (jnp.maximum(i * per_halo - 1, 0), 0)),
            pl.BlockSpec((tm, d), row),
            pl.BlockSpec((HALO_ROWS, d), lambda i: (jnp.minimum((i + 1) * per_halo, n_halo - 1), 0)),
            pl.BlockSpec((1, 3, d), lambda i: (i // per_blk, 0, 0)),
            pl.BlockSpec((1, d), const),
            _resident((d, n_main)),
            pl.BlockSpec((N_GATES, d), const),
            pl.BlockSpec((N_GATES, 2), const),
            pl.BlockSpec((tm, LANES), pos),
            pl.BlockSpec((tm, LANES), pos),
            pl.BlockSpec((tm, LANES), pos),
            pl.BlockSpec((CONV_K, 3 * DN_WIDTH), const),
        ],
        out_specs=view_specs + [pl.BlockSpec((tm, DN_WIDTH), row)] * 4 + [
            pl.BlockSpec((N_GATES, tm), lambda i: (0, i)),
        ],
        out_shape=view_shapes + [jax.ShapeDtypeStruct((t, DN_WIDTH), BF16)] * 4 + [
            jax.ShapeDtypeStruct((N_GATES, t), F32),
        ],
        scratch_shapes=[pltpu.VMEM((3 * ATTN_WIDTH // LANES, tm, LANES), F32)],
        compiler_params=_cparams(("arbitrary",)),
        name="in_proj",
    )(x, x, x, mod, norm_w.reshape(1, d), w_main, w_gate_t, gate_par_t, cos, sfw, sbk, conv_w)
    n_qkv = 3 * len(DILATED_PATTERNS)
    qkv_views = [outs[3 * n:3 * n + 3] for n in range(len(DILATED_PATTERNS))]
    return qkv_views, outs[n_qkv:]


def _rope_tables(max_len):
    half = ROT_HALF
    inv = ROPE_THETA ** (-jnp.arange(half, dtype=F32) / half)
    ang = jnp.arange(max_len, dtype=F32)[:, None] * inv[None, :]
    cos, sin = jnp.cos(ang), jnp.sin(ang)
    ones = jnp.ones((max_len, HEAD_DIM - 2 * half), F32)
    zeros_h = jnp.zeros((max_len, half), F32)
    zeros_r = jnp.zeros((max_len, HEAD_DIM - 2 * half), F32)
    cos_h = jnp.concatenate([cos, cos, ones], axis=1)
    sfw_h = jnp.concatenate([-sin, zeros_h, zeros_r], axis=1)
    sbk_h = jnp.concatenate([zeros_h, sin, zeros_r], axis=1)
    rep = LANES // HEAD_DIM
    return tuple(jnp.tile(a, (1, rep)) for a in (cos_h, sfw_h, sbk_h))


def _attn_kernel(q_ref, kp_ref, km_ref, kn_ref, vp_ref, vm_ref, vn_ref, *rest,
                 dil, next_dil, first, radius, tq, rows1, len1, len2):
    last = next_dil is None
    if not first:
        op_ref, wp_ref = rest[:2]
        rest = rest[2:]
    if last:
        o_ref, acc_scr, nat_scr = rest
    else:
        acc_out_ref, ml_out_ref, acc_scr, ml_scr = rest
    tb = q_ref.shape[0]
    n_sub = tb // tq
    tk = tq + 2 * radius
    n_tiles = ATTN_WIDTH // LANES
    n_res = q_ref.shape[1] // ATTN_WIDTH
    c0 = pl.program_id(1) * n_res
    r0 = pl.program_id(0) * tb
    in_first = r0 < rows1
    seq_len = jnp.where(in_first, len1, len2)
    off = jnp.where(in_first, r0, r0 - rows1)
    first_key = jnp.where((off % seq_len) == 0, radius, 0)
    end_key = jnp.where(((off + tb) % seq_len) == 0, tq + radius, tk)

    qi = lax.broadcasted_iota(jnp.int32, (tq, tk), 0)
    kj = lax.broadcasted_iota(jnp.int32, (tq, tk), 1)
    in_band = jnp.abs(kj - radius - qi) <= radius
    biases = []
    for sub in range(n_sub):
        valid = in_band
        if sub == 0:
            valid = valid & (kj >= first_key)
        if sub == n_sub - 1:
            valid = valid & (kj < end_key)
        bias = jnp.where(valid, 0.0, NEG_BIG).astype(F32)
        biases.append(jnp.concatenate([bias, bias], axis=0))

    lane = lax.broadcasted_iota(jnp.int32, (1, LANES), 1)
    low_half = lane < HEAD_DIM
    nt = (((1,), (1,)), ((), ()))

    def unstack(x):
        return jnp.where(low_half, x[:tq], x[tq:])

    if not first:
        e_row = lax.broadcasted_iota(jnp.int32, (LANES, ATTN_WIDTH), 0)
        e_col = lax.broadcasted_iota(jnp.int32, (LANES, ATTN_WIDTH), 1)
        spread = jnp.where(e_row == e_col // LANES + HEAD_DIM * ((e_col % LANES) // HEAD_DIM), 1.0, 0.0).astype(BF16)
        lse_old = {}
        for r in range(n_res):
            for sub in range(n_sub):
                wc = wp_ref[sub * tq:(sub + 1) * tq, r * LANES:(r + 1) * LANES]
                hi = wc.astype(BF16)
                lo = (wc - hi.astype(F32)).astype(BF16)
                ex = jnp.dot(jnp.concatenate([hi, lo], axis=0), spread, preferred_element_type=F32)
                lse_old[(r, sub)] = ex[:tq] + ex[tq:]

    pairs = []
    for r in range(n_res):
        for j in range(n_tiles):
            sl = slice(r * ATTN_WIDTH + j * LANES, r * ATTN_WIDTH + (j + 1) * LANES)
            k_all = jnp.concatenate([kp_ref[:, sl], km_ref[:, sl], kn_ref[:, sl]], axis=0)
            v_all = jnp.concatenate([vp_ref[:, sl], vm_ref[:, sl], vn_ref[:, sl]], axis=0)
            for sub in range(n_sub):
                rows = slice(sub * tq, (sub + 1) * tq)
                q2 = q_ref[rows, sl]
                zero = jnp.zeros_like(q2)
                qs = jnp.concatenate([jnp.where(low_half, q2, zero), jnp.where(low_half, zero, q2)], axis=0)
                kw = k_all[sub * tq:sub * tq + tk]
                pairs.append(dict(r=r, j=j, sub=sub, sl=sl, rows=rows, vw=v_all[sub * tq:sub * tq + tk],
                                  s=lax.dot_general(qs, kw, nt, preferred_element_type=F32) + biases[sub]))
    ones = jnp.ones((tk, LANES), BF16)
    for pr in pairs:
        m_rows = jnp.broadcast_to(jnp.max(pr["s"], axis=-1, keepdims=True), (2 * tq, LANES))
        pr["m"] = unstack(m_rows)
        pr["p"] = jnp.exp2(pr["s"] - jnp.concatenate([m_rows] * (tk // LANES), axis=1)).astype(BF16)
    for pr in pairs:
        pr["pv"] = jnp.dot(pr["p"], pr["vw"], preferred_element_type=F32)
        pr["l"] = jnp.dot(pr["p"], ones, preferred_element_type=F32)
    for pr in pairs:
        den = unstack(pr["l"])
        o = unstack(pr["pv"]) / den
        lse = pr["m"] + jnp.log2(den)
        if not first:
            lse_prev = lse_old[(pr["r"], pr["sub"])][:, pr["j"] * LANES:(pr["j"] + 1) * LANES]
            top = jnp.maximum(lse_prev, lse)
            w_prev, w_cur = jnp.exp2(lse_prev - top), jnp.exp2(lse - top)
            tot = w_prev + w_cur
            o = (w_prev * op_ref[pr["rows"], pr["sl"]] + w_cur * o) / tot
            lse = top + jnp.log2(tot)
        pr["lse"] = lse
        acc_scr[c0 + pr["r"], pr["j"], pr["rows"], :] = o
    if not last:
        for r in range(n_res):
            for sub in range(n_sub):
                wc = jnp.zeros((tq, LANES), F32)
                for pr in pairs:
                    if pr["r"] == r and pr["sub"] == sub:
                        keep = jnp.logical_or(lane == pr["j"], lane == HEAD_DIM + pr["j"])
                        wc = jnp.where(keep, pr["lse"], wc)
                ml_scr[c0 + r, sub * tq:(sub + 1) * tq, :] = wc

    @pl.when(c0 + n_res == dil)
    def _():
        if last:
            for cc in range(dil):
                for j in range(n_tiles):
                    nat_scr[j, pl.ds(cc, tb, stride=dil), :] = acc_scr[cc, j]
            for j in range(n_tiles):
                o_ref[:, j * LANES:(j + 1) * LANES] = nat_scr[j].astype(o_ref.dtype)
        else:
            ratio = next_dil // dil
            rows_out = tb // ratio
            for cc in range(dil):
                for m in range(ratio):
                    cb = m * dil + cc
                    rows = pl.ds(m, rows_out, stride=ratio)
                    for j in range(n_tiles):
                        lo = cb * ATTN_WIDTH + j * LANES
                        acc_out_ref[:, lo:lo + LANES] = acc_scr[cc, j, rows, :]
                    ml_out_ref[:, cb * LANES:(cb + 1) * LANES] = ml_scr[cc, rows, :]


def _attn_pattern(qkv, prev, dil, next_dil, radius, t, t1, seq_lens):
    q, k, v = qkv
    w = ATTN_WIDTH
    first, last = prev is None, next_dil is None
    rows = t // dil
    tq, n_sub, n_res = ATTN_TILE[dil]
    tb = tq * n_sub
    per_q = tb // radius
    n_halo = rows // radius
    n_tiles = w // LANES
    wr = n_res * w
    main = lambda i, c: (i, c)
    before = lambda i, c: (jnp.maximum(i * per_q - 1, 0), c)
    after = lambda i, c: (jnp.minimum((i + 1) * per_q, n_halo - 1), c)
    whole = lambda i, c: (i, 0)
    kv_specs = [pl.BlockSpec((radius, wr), before), pl.BlockSpec((tb, wr), main), pl.BlockSpec((radius, wr), after)]
    in_specs = [pl.BlockSpec((tb, wr), main)] + kv_specs + kv_specs
    args = [q, k, k, k, v, v, v]
    if not first:
        in_specs += [pl.BlockSpec((tb, wr), main), pl.BlockSpec((tb, n_res * LANES), main)]
        args += list(prev)
    scratch = [pltpu.VMEM((dil, n_tiles, tb, LANES), F32)]
    if last:
        out_specs = [pl.BlockSpec((tb * dil, w), whole)]
        out_shape = [jax.ShapeDtypeStruct((t, w), BF16)]
        scratch.append(pltpu.VMEM((n_tiles, tb * dil, LANES), F32))
    else:
        rows_out = tb * dil // next_dil
        out_specs = [pl.BlockSpec((rows_out, next_dil * w), whole), pl.BlockSpec((rows_out, next_dil * LANES), whole)]
        out_shape = [jax.ShapeDtypeStruct((t // next_dil, next_dil * w), F32),
                     jax.ShapeDtypeStruct((t // next_dil, next_dil * LANES), F32)]
        scratch.append(pltpu.VMEM((dil, tb, LANES), F32))
    return pl.pallas_call(
        functools.partial(_attn_kernel, dil=dil, next_dil=next_dil, first=first, radius=radius, tq=tq,
                          rows1=t1 // dil, len1=seq_lens[0] // dil, len2=seq_lens[1] // dil),
        grid=(rows // tb, dil // n_res),
        in_specs=in_specs,
        out_specs=out_specs,
        out_shape=out_shape,
        scratch_shapes=scratch,
        compiler_params=_cparams(("arbitrary", "arbitrary")),
        name=f"attn_d{dil}",
    )(*args)


def _attention(qkv_views, t, t1, seq_lens):
    state = None
    n = len(DILATED_PATTERNS)
    for idx, (window, dil) in enumerate(DILATED_PATTERNS):
        next_dil = DILATED_PATTERNS[idx + 1][1] if idx + 1 < n else None
        state = _attn_pattern(qkv_views[idx], state, dil, next_dil, window // (2 * dil), t, t1, seq_lens)
    return state[0]


def _bdot(a, b):
    return jnp.dot(a.astype(BF16), b.astype(BF16), preferred_element_type=F32)


def _split3(x):
    hi = x.astype(BF16)
    r1 = x - hi.astype(F32)
    mid = r1.astype(BF16)
    lo = (r1 - mid.astype(F32)).astype(BF16)
    return hi, mid, lo


def _dn_kernel(qf_ref, kf_ref, vf_ref, gtf_ref, qb_ref, kb_ref, vb_ref, gtb_ref,
               of_ref, ob_ref, sf_ref, sb_ref, *, c, n_chunks, chunks1, len1, len2):
    n_sub = qf_ref.shape[0] // c
    i = pl.program_id(0)

    def seq_pos(ci):
        in_first = ci < chunks1
        seq_len = jnp.where(in_first, len1, len2)
        off = jnp.where(in_first, ci, ci - chunks1)
        return off % seq_len, seq_len

    pos_f, _ = seq_pos(i * n_sub)
    pos_b, len_b = seq_pos(n_chunks - 1 - i * n_sub)

    @pl.when(pos_f == 0)
    def _():
        sf_ref[...] = jnp.zeros_like(sf_ref)

    @pl.when(pos_b == len_b - 1)
    def _():
        sb_ref[...] = jnp.zeros_like(sb_ref)

    ii = lax.broadcasted_iota(jnp.int32, (c, c), 0)
    jj = lax.broadcasted_iota(jnp.int32, (c, c), 1)
    lower, lower_strict = ii >= jj, ii > jj
    upper, upper_strict = ii <= jj, ii < jj
    eye = jnp.where(ii == jj, 1.0, 0.0).astype(F32)
    tri_l = jnp.where(lower, 1.0, 0.0).astype(F32)
    tri_u = jnp.where(upper, 1.0, 0.0).astype(F32)
    level_masks = [jnp.where(ii // INV_BASE == jj // INV_BASE, 1.0, 0.0).astype(F32)]
    b = INV_BASE
    while b < c:
        same_outer = ii // (2 * b) == jj // (2 * b)
        level_masks.append(jnp.where(same_outer & (ii // b != jj // b), 1.0, 0.0).astype(F32))
        b *= 2
    ones_cd = jnp.ones((c, DN_HEAD_DIM), BF16)
    nt = (((1,), (1,)), ((), ()))

    chains = []
    for forward, (q_ref, k_ref, v_ref, gt_ref, o_ref, s_ref) in (
            (True, (qf_ref, kf_ref, vf_ref, gtf_ref, of_ref, sf_ref)),
            (False, (qb_ref, kb_ref, vb_ref, gtb_ref, ob_ref, sb_ref))):
        tri_r = tri_u if forward else tri_l
        row_rhs = jnp.concatenate([tri_r.astype(BF16), ones_cd], axis=1)
        keep, strict = (lower, lower_strict) if forward else (upper, upper_strict)
        d0 = 0 if forward else N_DN_HEADS
        for order in range(n_sub):
            sub = order if forward else n_sub - 1 - order
            rows = slice(sub * c, (sub + 1) * c)
            gt = gt_ref[:, rows]
            row = sum(jnp.dot(piece, row_rhs, preferred_element_type=F32) for piece in _split3(gt))
            cs_row, tot_row = row[:, :c], row[:, c:]
            cs_col, gates_col = jnp.transpose(cs_row), jnp.transpose(gt)
            tot_col = jnp.broadcast_to(cs_col[c - 1:c] if forward else cs_col[0:1], cs_col.shape)
            for hd in range(N_DN_HEADS):
                gi = d0 + hd
                bi = 2 * N_DN_HEADS + d0 + hd
                sl = slice(hd * DN_HEAD_DIM, (hd + 1) * DN_HEAD_DIM)
                chains.append(dict(
                    order=order, rows=rows, sl=sl, hd=hd, keep=keep, strict=strict, o_ref=o_ref, s_ref=s_ref,
                    q=q_ref[rows, sl], k=k_ref[rows, sl], v=v_ref[rows, sl],
                    gc_col=cs_col[:, gi:gi + 1], gc_row=cs_row[gi:gi + 1, :], beta=gates_col[:, bi:bi + 1],
                    tot_col=tot_col[:, gi:gi + 1], tot_row=tot_row[gi:gi + 1, :]))

    for ch in chains:
        qh, kh = ch["q"], ch["k"]
        both = lax.dot_general(jnp.concatenate([qh, kh], axis=0), kh, nt, preferred_element_type=F32)
        decay = jnp.where(ch["keep"], jnp.exp(jnp.minimum(ch["gc_col"] - ch["gc_row"], 0.0)), 0.0)
        ch["a"] = jnp.where(ch["strict"], ch["beta"] * both[c:] * decay, 0.0)
        ch["qk"] = (both[:c] * decay).astype(BF16)
    for ch in chains:
        dg = ch["a"] * level_masks[0]
        ch["x"] = eye - dg
        ch["y"] = _bdot(dg, dg)
    for ch in chains:
        ch["x"] = ch["x"] + _bdot(ch["x"], ch["y"])
    for off_mask in level_masks[1:]:
        for ch in chains:
            ch["y"] = _bdot(ch["a"] * off_mask, ch["x"])
        for ch in chains:
            ch["x"] = ch["x"] - _bdot(ch["x"], ch["y"])
    for ch in chains:
        kh = ch["k"].astype(F32)
        egc = jnp.exp(ch["gc_col"])
        rhs = jnp.concatenate([ch["v"].astype(F32) * ch["beta"], kh * (ch["beta"] * egc)], axis=1)
        ch["uw"] = _bdot(ch["x"], rhs)
        ch["lhs_q"] = ch["q"].astype(F32) * egc
        ch["k_dec_t"] = jnp.transpose(kh * jnp.exp(ch["tot_col"] - ch["gc_col"])).astype(BF16)
    states = {}
    for order in range(n_sub):
        group = [ch for ch in chains if ch["order"] == order]
        for ch in group:
            key = (id(ch["s_ref"]), ch["hd"])
            ch["state"] = states[key] if order else ch["s_ref"][ch["hd"]]
            lhs = jnp.concatenate([ch["uw"][:, DN_HEAD_DIM:], ch["lhs_q"]], axis=0)
            ch["ws"] = _bdot(lhs, ch["state"])
        for ch in group:
            v_new = (ch["uw"][:, :DN_HEAD_DIM] - ch["ws"][:c]).astype(BF16)
            ch["o_ref"][ch["rows"], ch["sl"]] = ch["ws"][c:] + jnp.dot(ch["qk"], v_new, preferred_element_type=F32)
            ch["v_new"] = v_new
        for ch in group:
            states[(id(ch["s_ref"]), ch["hd"])] = ch["state"] * jnp.exp(ch["tot_row"]) + jnp.dot(
                ch["k_dec_t"], ch["v_new"], preferred_element_type=F32)
    for ch in chains:
        if ch["order"] == n_sub - 1:
            ch["s_ref"][ch["hd"]] = states[(id(ch["s_ref"]), ch["hd"])]


def _deltanet(q, k, v, gates_t, t1, seq_lens):
    t, w = q.shape
    c = DN_CHUNK
    rows = c * DN_CHUNKS_PER_STEP
    n = t // rows
    fwd = lambda i: (i, 0)
    bwd = lambda i: (n - 1 - i, 0)
    fwd_t = lambda i: (0, i)
    bwd_t = lambda i: (0, n - 1 - i)

    def specs(row_map, col_map):
        return [pl.BlockSpec((rows, w), row_map)] * 3 + [pl.BlockSpec((N_GATES, rows), col_map)]

    state = pltpu.VMEM((N_DN_HEADS, DN_HEAD_DIM, DN_HEAD_DIM), F32)
    return pl.pallas_call(
        functools.partial(_dn_kernel, c=c, n_chunks=t // c, chunks1=t1 // c, len1=seq_lens[0] // c,
                          len2=seq_lens[1] // c),
        grid=(n,),
        in_specs=specs(fwd, fwd_t) + specs(bwd, bwd_t),
        out_specs=[pl.BlockSpec((rows, w), fwd), pl.BlockSpec((rows, w), bwd)],
        out_shape=[jax.ShapeDtypeStruct((t, w), F32)] * 2,
        scratch_shapes=[state, state],
        compiler_params=_cparams(("arbitrary",)),
        name="deltanet",
    )(q, k, v, gates_t, q, k, v, gates_t)


def kernel(x_prompt, x_sample, c_prompt, c_sample, ada_w, ada_b, norm_ffn1, ffn1_w_gate, ffn1_w_up, ffn1_w_down, norm_mix, w_in, conv_w, a_log, dt_bias, dn_norm, w_out, norm_ffn2, ffn2_w_gate, ffn2_w_up, ffn2_w_down, norm_final):
    b1, s1, d = x_prompt.shape
    b2, s2, _ = x_sample.shape
    depth = ada_w.shape[0]
    t1 = b1 * s1
    seq_lens = (s1, s2)
    assert s2 % s1 == 0 and t1 % s2 == 0, "flat layout needs nested sequence lengths"
    t = t1 + b2 * s2
    xs = [x_prompt.reshape(t1, d), x_sample.reshape(b2 * s2, d)]

    n_seq = b1 + b2
    c_all = jnp.concatenate([c_prompt, c_sample, jnp.zeros((-n_seq % SUBLANES, d), F32)], axis=0)
    mod = _modulation(c_all, ada_w, ada_b)
    blk_seq = jnp.concatenate([jnp.arange(b1), b1 + jnp.repeat(jnp.arange(b2), s2 // s1)])
    mod = mod[:, blk_seq].reshape(depth, blk_seq.shape[0], N_MOD, d)

    rope_tabs = _rope_tables(max(s1, s2))
    n_main = 3 * ATTN_WIDTH + 4 * DN_WIDTH
    gate_par_t = jnp.stack([a_log.astype(F32).reshape(depth, -1), dt_bias.astype(F32).reshape(depth, -1)], axis=2)
    gate_par_t = jnp.pad(gate_par_t, ((0, 0), (0, N_GATES - gate_par_t.shape[1]), (0, 0)))

    for l in range(depth):
        bf = lambda a: a.astype(BF16)
        x = _ffn(xs if l == 0 else [x], mod[l, :, 0:3], norm_ffn1[l], bf(ffn1_w_gate[l]), bf(ffn1_w_up[l]),
                 bf(ffn1_w_down[l]), s1, t1)
        qkv_views, (dq, dk, dv, z, gates_t) = _in_proj(
            x, mod[l, :, 3:6], norm_mix[l], bf(w_in[l][:, :n_main]), bf(w_in[l][:, n_main:]).T,
            gate_par_t[l], rope_tabs, conv_w[l], s1, seq_lens, t1)
        attn = _attention(qkv_views, t, t1, seq_lens)
        o_f, o_b = _deltanet(dq, dk, dv, gates_t, t1, seq_lens)
        mixer = (mod[l, :, 3:6], attn, o_f, o_b, z, dn_norm[l], bf(w_out[l][:ATTN_WIDTH]), bf(w_out[l][ATTN_WIDTH:]))
        is_last = l == depth - 1
        x = _ffn([x], mod[l, :, 6:9], norm_ffn2[l], bf(ffn2_w_gate[l]), bf(ffn2_w_up[l]), bf(ffn2_w_down[l]), s1,
                 t1, final_w=norm_final if is_last else None, split_out=is_last, mixer=mixer)
    return x[0].reshape(b1, s1, d), x[1].reshape(b2, s2, d)
```

```python
import functools

import jax
import jax.numpy as jnp
from jax import lax
from jax.experimental import pallas as pl
from jax.experimental.pallas import tpu as pltpu

F32 = jnp.float32
BF16 = jnp.bfloat16

N_MOD = 9
EPS = 1e-6
N_ATTN_HEADS = 8
HEAD_DIM = 64
ATTN_WIDTH = N_ATTN_HEADS * HEAD_DIM
ROT_HALF = HEAD_DIM // 8
ROPE_THETA = 500000.0
ATTN_Q_SCALE = HEAD_DIM ** -0.5 * 1.4426950408889634
DILATED_PATTERNS = ((128, 1), (512, 4), (2048, 16))
N_DN_HEADS = 4
DN_HEAD_DIM = 128
DN_WIDTH = N_DN_HEADS * DN_HEAD_DIM
CONV_K = 5
N_GATES = 4 * N_DN_HEADS

LANES = 128
SUBLANES = 8
VMEM_LIMIT_BYTES = 56 * 1024 * 1024

FFN_ROWS = 512
PROJ_ROWS = 512
ATTN_TILE = {1: (128, 8, 1), 4: (128, 2, 4), 16: (128, 2, 4)}
DN_CHUNK = 128
DN_CHUNKS_PER_STEP = 2
INV_BASE = 4
MOD_COLS = 1152
HALO_ROWS = SUBLANES
NEG_BIG = -1e30


def _cparams(sem):
    return pltpu.CompilerParams(dimension_semantics=sem, vmem_limit_bytes=VMEM_LIMIT_BYTES)


def _resident(shape):
    return pl.BlockSpec(shape, lambda *_: (0, 0), pipeline_mode=pl.Buffered(1))


def _silu(x):
    return x * jax.nn.sigmoid(x)


def _modulated_norm(x, norm_w, shift, scale):
    inv = lax.rsqrt(jnp.mean(x * x, axis=-1, keepdims=True) + EPS)
    return x * inv * (norm_w * (1.0 + scale)) + shift


def _mod_kernel(c_ref, w_ref, b_ref, o_ref):
    sc = _silu(c_ref[...]).astype(BF16)
    o_ref[0] = jnp.dot(sc, w_ref[0].astype(BF16), preferred_element_type=F32) + b_ref[0]


def _modulation(c_all, ada_w, ada_b):
    depth, d, n = ada_w.shape
    rows = c_all.shape[0]
    return pl.pallas_call(
        _mod_kernel,
        grid=(depth, n // MOD_COLS),
        in_specs=[
            pl.BlockSpec((rows, d), lambda l, j: (0, 0)),
            pl.BlockSpec((1, d, MOD_COLS), lambda l, j: (l, 0, j)),
            pl.BlockSpec((1, 1, MOD_COLS), lambda l, j: (l, 0, j)),
        ],
        out_specs=pl.BlockSpec((1, rows, MOD_COLS), lambda l, j: (l, 0, j)),
        out_shape=jax.ShapeDtypeStruct((depth, rows, n), F32),
        compiler_params=_cparams(("arbitrary", "arbitrary")),
        name="modulation",
    )(c_all, ada_w, ada_b.reshape(depth, 1, n))


def _mixer_residual(x, mod_ref, attn_ref, of_ref, ob_ref, z_ref, nw_ref, wa_ref, wd_ref):
    o = of_ref[...] + ob_ref[...]
    z = z_ref[...].astype(F32)
    nw = nw_ref[...]
    parts = []
    for hd in range(N_DN_HEADS):
        sl = slice(hd * DN_HEAD_DIM, (hd + 1) * DN_HEAD_DIM)
        oh = o[:, sl]
        parts.append(oh * lax.rsqrt(jnp.mean(oh * oh, axis=-1, keepdims=True) + EPS) * nw * _silu(z[:, sl]))
    dn = jnp.concatenate(parts, axis=1).astype(BF16)
    y = jnp.dot(attn_ref[...], wa_ref[...], preferred_element_type=F32)
    y = y + jnp.dot(dn, wd_ref[...], preferred_element_type=F32)
    return x + mod_ref[0, 2:3, :] * y


N_MIXER_REFS = 8


def _ffn_kernel(*refs, n_in, n_out, with_mixer, final_norm, rows1):
    x_refs, refs = refs[:n_in], refs[n_in:]
    if with_mixer:
        mixer_refs, refs = refs[:N_MIXER_REFS], refs[N_MIXER_REFS:]
    mod_ref, nw_ref, wg_ref, wu_ref, wd_ref = refs[:5]
    refs = refs[5:]
    if final_norm:
        fw_ref, refs = refs[0], refs[1:]
    o_refs, refs = refs[:n_out], refs[n_out:]
    tm = x_refs[0].shape[0]
    in_first = pl.program_id(0) < rows1 // tm
    if n_in == 2:
        stage_ref = refs[0]

        @pl.when(in_first)
        def _():
            stage_ref[...] = x_refs[0][...]

        @pl.when(jnp.logical_not(in_first))
        def _():
            stage_ref[...] = x_refs[1][...]

        x = stage_ref[...]
    else:
        x = x_refs[0][...]
    if with_mixer:
        x = _mixer_residual(x, *mixer_refs)
    h = _modulated_norm(x, nw_ref[...], mod_ref[0, 0:1, :], mod_ref[0, 1:2, :]).astype(BF16)
    g = jnp.dot(h, wg_ref[...], preferred_element_type=F32)
    u = jnp.dot(h, wu_ref[...], preferred_element_type=F32)
    a = (_silu(g) * u).astype(BF16)
    y = jnp.dot(a, wd_ref[...], preferred_element_type=F32)
    out = x + (0.5 * mod_ref[0, 2:3, :]) * y
    if final_norm:
        out = out * lax.rsqrt(jnp.mean(out * out, axis=-1, keepdims=True) + EPS) * fw_ref[...]
    if n_out == 2:
        @pl.when(in_first)
        def _():
            o_refs[0][...] = out

        @pl.when(jnp.logical_not(in_first))
        def _():
            o_refs[1][...] = out
    else:
        o_refs[0][...] = out


def _ffn(xs, mod, norm_w, wg, wu, wd, blk_rows, rows1, final_w=None, split_out=False, mixer=None):
    d = xs[0].shape[1]
    t = sum(x.shape[0] for x in xs)
    f = wg.shape[1]
    tm = FFN_ROWS
    per_blk = blk_rows // tm
    nb1 = rows1 // tm
    const = lambda i: (0, 0)
    row = lambda i: (i, 0)
    blk = lambda i: (i // per_blk, 0, 0)
    group1 = lambda i: (jnp.minimum(i, nb1 - 1), 0)
    group2 = lambda i: (jnp.maximum(i - nb1, 0), 0)
    in_specs = [pl.BlockSpec((tm, d), row)] if len(xs) == 1 else [pl.BlockSpec((tm, d), group1), pl.BlockSpec((tm, d), group2)]
    args = list(xs)
    if mixer is not None:
        m_mod, attn, o_f, o_b, z, dn_norm, w_attn, w_dn = mixer
        in_specs += [
            pl.BlockSpec((1, 3, d), blk),
            pl.BlockSpec((tm, ATTN_WIDTH), row),
            pl.BlockSpec((tm, DN_WIDTH), row),
            pl.BlockSpec((tm, DN_WIDTH), row),
            pl.BlockSpec((tm, DN_WIDTH), row),
            pl.BlockSpec((1, DN_HEAD_DIM), const),
            _resident((ATTN_WIDTH, d)),
            _resident((DN_WIDTH, d)),
        ]
        args += [m_mod, attn, o_f, o_b, z, dn_norm.reshape(1, DN_HEAD_DIM), w_attn, w_dn]
    in_specs += [
        pl.BlockSpec((1, 3, d), blk),
        pl.BlockSpec((1, d), const),
        _resident((d, f)),
        _resident((d, f)),
        _resident((f, d)),
    ]
    args += [mod, norm_w.reshape(1, d), wg, wu, wd]
    if final_w is not None:
        in_specs.append(pl.BlockSpec((1, d), const))
        args.append(final_w.reshape(1, d))
    if split_out:
        out_specs = [pl.BlockSpec((tm, d), group1), pl.BlockSpec((tm, d), group2)]
        out_shape = [jax.ShapeDtypeStruct((rows1, d), F32), jax.ShapeDtypeStruct((t - rows1, d), F32)]
    else:
        out_specs = [pl.BlockSpec((tm, d), row)]
        out_shape = [jax.ShapeDtypeStruct((t, d), F32)]
    outs = pl.pallas_call(
        functools.partial(_ffn_kernel, n_in=len(xs), n_out=len(out_specs), with_mixer=mixer is not None,
                          final_norm=final_w is not None, rows1=rows1),
        grid=(t // tm,),
        in_specs=in_specs,
        out_specs=out_specs,
        out_shape=out_shape,
        scratch_shapes=[pltpu.VMEM((tm, d), F32)] if len(xs) == 2 else [],
        compiler_params=_cparams(("arbitrary",)),
        name="ffn",
    )(*args)
    return outs if split_out else outs[0]


def _inproj_kernel(xp_ref, x_ref, xn_ref, mod_ref, nw_ref, w_ref, wgt_ref, gpt_ref,
                   cos_ref, sfw_ref, sbk_ref, cw_ref, *rest, rows1, len1, len2):
    n_views = len(DILATED_PATTERNS)
    qkv_refs = [rest[3 * n:3 * n + 3] for n in range(n_views)]
    dq_ref, dk_ref, dv_ref, z_ref, gbt_ref, stage_ref = rest[3 * n_views:]
    tm = x_ref.shape[0]
    nw, shift, scale = nw_ref[...], mod_ref[0, 0:1, :], mod_ref[0, 1:2, :]
    h_main = _modulated_norm(x_ref[...], nw, shift, scale)
    h_ext = jnp.concatenate([_modulated_norm(xp_ref[...], nw, shift, scale), h_main,
                             _modulated_norm(xn_ref[...], nw, shift, scale)], axis=0).astype(BF16)
    h = h_main.astype(BF16)
    cos, sfw, sbk = cos_ref[...], sfw_ref[...], sbk_ref[...]

    def rope(xp):
        return xp * cos + pltpu.roll(xp, LANES - ROT_HALF, 1) * sfw + pltpu.roll(xp, ROT_HALF, 1) * sbk

    n_tiles = ATTN_WIDTH // LANES
    o0 = 3 * ATTN_WIDTH

    def attn_epilogue(a, pa):
        for j in range(n_tiles):
            col = pa[:, j * LANES:(j + 1) * LANES]
            if a == 0:
                col = rope(col) * ATTN_Q_SCALE
            elif a == 1:
                col = rope(col)
            stage_ref[a * n_tiles + j] = col
        for n, (_, dil) in enumerate(DILATED_PATTERNS):
            rows = tm // dil
            for j in range(n_tiles):
                for c in range(dil):
                    lo = c * ATTN_WIDTH + j * LANES
                    src = stage_ref[a * n_tiles + j, pl.ds(c, rows, stride=dil), :] if dil > 1 else stage_ref[a * n_tiles + j]
                    qkv_refs[n][a][:, lo:lo + LANES] = src.astype(BF16)

    r0 = pl.program_id(0) * tm
    in_first = r0 < rows1
    seq_len = jnp.where(in_first, len1, len2)
    off = jnp.where(in_first, r0, r0 - rows1)
    keep_prev = jnp.where((off % seq_len) == 0, 0.0, 1.0).astype(F32)
    keep_next = jnp.where(((off + tm) % seq_len) == 0, 0.0, 1.0).astype(F32)
    pad = CONV_K // 2
    n_ext = tm + 2 * HALO_ROWS

    def dn_epilogue(a, pe):
        out_ref = (dq_ref, dk_ref, dv_ref)[a]
        cols = slice(a * DN_WIDTH, (a + 1) * DN_WIDTH)
        pe = jnp.concatenate([pe[0:HALO_ROWS] * keep_prev, pe[HALO_ROWS:HALO_ROWS + tm],
                              pe[HALO_ROWS + tm:] * keep_next], axis=0)
        taps = [pe * cw_ref[j:j + 1, cols] for j in range(CONV_K)]
        before, after = taps[0], taps[CONV_K - 1]
        for j in range(1, pad):
            before = taps[j] + pltpu.roll(before, 1, 0)
            after = taps[CONV_K - 1 - j] + pltpu.roll(after, n_ext - 1, 0)
        y = taps[pad] + pltpu.roll(before, 1, 0) + pltpu.roll(after, n_ext - 1, 0)
        y = _silu(y[HALO_ROWS:HALO_ROWS + tm])
        for hd in range(N_DN_HEADS):
            sl = slice(hd * DN_HEAD_DIM, (hd + 1) * DN_HEAD_DIM)
            yh = y[:, sl]
            if a == 0:
                yh = yh * lax.rsqrt(jnp.sum(yh * yh, axis=-1, keepdims=True) + EPS) * (DN_HEAD_DIM ** -0.5)
            elif a == 1:
                yh = yh * lax.rsqrt(jnp.sum(yh * yh, axis=-1, keepdims=True) + EPS)
            out_ref[:, sl] = yh.astype(BF16)

    def z_epilogue(_, pz):
        z_ref[...] = pz.astype(BF16)

    groups = [(h, a * ATTN_WIDTH, ATTN_WIDTH, attn_epilogue, a) for a in range(3)]
    groups += [(h_ext, o0 + a * DN_WIDTH, DN_WIDTH, dn_epilogue, a) for a in range(3)]
    groups.append((h, o0 + 3 * DN_WIDTH, DN_WIDTH, z_epilogue, None))

    for lhs, lo, width, epilogue, arg in groups:
        epilogue(arg, jnp.dot(lhs, w_ref[:, lo:lo + width], preferred_element_type=F32))

    abt = lax.dot_general(wgt_ref[...], h, (((1,), (1,)), ((), ())), preferred_element_type=F32)
    is_gt = lax.broadcasted_iota(jnp.int32, abt.shape, 0) < 2 * N_DN_HEADS
    gpt = gpt_ref[...]
    gbt_ref[...] = jnp.where(is_gt, -jnp.exp(gpt[:, 0:1]) * jax.nn.softplus(abt + gpt[:, 1:2]), jax.nn.sigmoid(abt))


def _in_proj(x, mod, norm_w, w_main, w_gate_t, gate_par_t, rope_tabs, conv_w, blk_rows, seq_lens, t1):
    t, d = x.shape
    tm = PROJ_ROWS
    per_blk = blk_rows // tm
    n_main = w_main.shape[1]
    nb1 = t1 // tm
    s1b, s2b = seq_lens[0] // tm, seq_lens[1] // tm
    per_halo = tm // HALO_ROWS
    n_halo = t // HALO_ROWS
    const = lambda i: (0, 0)
    row = lambda i: (i, 0)
    pos = lambda i: (jnp.where(i < nb1, i % s1b, (i - nb1) % s2b), 0)
    cos, sfw, sbk = rope_tabs
    view_specs, view_shapes = [], []
    for _, dil in DILATED_PATTERNS:
        view_specs += [pl.BlockSpec((tm // dil, dil * ATTN_WIDTH), row)] * 3
        view_shapes += [jax.ShapeDtypeStruct((t // dil, dil * ATTN_WIDTH), BF16)] * 3
    outs = pl.pallas_call(
        functools.partial(_inproj_kernel, rows1=t1, len1=seq_lens[0], len2=seq_lens[1]),
        grid=(t // tm,),
        in_specs=[
            pl.BlockSpec((HALO_ROWS, d), lambda i: (jnp.maximum(i * per_halo - 1, 0), 0)),
            pl.BlockSpec((tm, d), row),
            pl.BlockSpec((HALO_ROWS, d), lambda i: (jnp.minimum((i + 1) * per_halo, n_halo - 1), 0)),
            pl.BlockSpec((1, 3, d), lambda i: (i // per_blk, 0, 0)),
            pl.BlockSpec((1, d), const),
            _resident((d, n_main)),
            pl.BlockSpec((N_GATES, d), const),
            pl.BlockSpec((N_GATES, 2), const),
            pl.BlockSpec((tm, LANES), pos),
            pl.BlockSpec((tm, LANES), pos),
            pl.BlockSpec((tm, LANES), pos),
            pl.BlockSpec((CONV_K, 3 * DN_WIDTH), const),
        ],
        out_specs=view_specs + [pl.BlockSpec((tm, DN_WIDTH), row)] * 4 + [
            pl.BlockSpec((N_GATES, tm), lambda i: (0, i)),
        ],
        out_shape=view_shapes + [jax.ShapeDtypeStruct((t, DN_WIDTH), BF16)] * 4 + [
            jax.ShapeDtypeStruct((N_GATES, t), F32),
        ],
        scratch_shapes=[pltpu.VMEM((3 * ATTN_WIDTH // LANES, tm, LANES), F32)],
        compiler_params=_cparams(("arbitrary",)),
        name="in_proj",
    )(x, x, x, mod, norm_w.reshape(1, d), w_main, w_gate_t, gate_par_t, cos, sfw, sbk, conv_w)
    n_qkv = 3 * len(DILATED_PATTERNS)
    qkv_views = [outs[3 * n:3 * n + 3] for n in range(len(DILATED_PATTERNS))]
    return qkv_views, outs[n_qkv:]


def _rope_tables(max_len):
    half = ROT_HALF
    inv = ROPE_THETA ** (-jnp.arange(half, dtype=F32) / half)
    ang = jnp.arange(max_len, dtype=F32)[:, None] * inv[None, :]
    cos, sin = jnp.cos(ang), jnp.sin(ang)
    ones = jnp.ones((max_len, HEAD_DIM - 2 * half), F32)
    zeros_h = jnp.zeros((max_len, half), F32)
    zeros_r = jnp.zeros((max_len, HEAD_DIM - 2 * half), F32)
    cos_h = jnp.concatenate([cos, cos, ones], axis=1)
    sfw_h = jnp.concatenate([-sin, zeros_h, zeros_r], axis=1)
    sbk_h = jnp.concatenate([zeros_h, sin, zeros_r], axis=1)
    rep = LANES // HEAD_DIM
    return tuple(jnp.tile(a, (1, rep)) for a in (cos_h, sfw_h, sbk_h))


def _attn_kernel(q_ref, kp_ref, km_ref, kn_ref, vp_ref, vm_ref, vn_ref, *rest,
                 dil, next_dil, first, radius, tq, rows1, len1, len2):
    last = next_dil is None
    if not first:
        op_ref, wp_ref = rest[:2]
        rest = rest[2:]
    if last:
        o_ref, acc_scr, nat_scr = rest
    else:
        acc_out_ref, ml_out_ref, acc_scr, ml_scr = rest
    tb = q_ref.shape[0]
    n_sub = tb // tq
    tk = tq + 2 * radius
    n_tiles = ATTN_WIDTH // LANES
    n_res = q_ref.shape[1] // ATTN_WIDTH
    c0 = pl.program_id(1) * n_res
    r0 = pl.program_id(0) * tb
    in_first = r0 < rows1
    seq_len = jnp.where(in_first, len1, len2)
    off = jnp.where(in_first, r0, r0 - rows1)
    first_key = jnp.where((off % seq_len) == 0, radius, 0)
    end_key = jnp.where(((off + tb) % seq_len) == 0, tq + radius, tk)

    qi = lax.broadcasted_iota(jnp.int32, (tq, tk), 0)
    kj = lax.broadcasted_iota(jnp.int32, (tq, tk), 1)
    in_band = jnp.abs(kj - radius - qi) <= radius
    biases = []
    for sub in range(n_sub):
        valid = in_band
        if sub == 0:
            valid = valid & (kj >= first_key)
        if sub == n_sub - 1:
            valid = valid & (kj < end_key)
        bias = jnp.where(valid, 0.0, NEG_BIG).astype(F32)
        biases.append(jnp.concatenate([bias, bias], axis=0))

    lane = lax.broadcasted_iota(jnp.int32, (1, LANES), 1)
    low_half = lane < HEAD_DIM
    nt = (((1,), (1,)), ((), ()))

    def unstack(x):
        return jnp.where(low_half, x[:tq], x[tq:])

    if not first:
        e_row = lax.broadcasted_iota(jnp.int32, (LANES, ATTN_WIDTH), 0)
        e_col = lax.broadcasted_iota(jnp.int32, (LANES, ATTN_WIDTH), 1)
        spread = jnp.where(e_row == e_col // LANES + HEAD_DIM * ((e_col % LANES) // HEAD_DIM), 1.0, 0.0).astype(BF16)
        lse_old = {}
        for r in range(n_res):
            for sub in range(n_sub):
                wc = wp_ref[sub * tq:(sub + 1) * tq, r * LANES:(r + 1) * LANES]
                hi = wc.astype(BF16)
                lo = (wc - hi.astype(F32)).astype(BF16)
                ex = jnp.dot(jnp.concatenate([hi, lo], axis=0), spread, preferred_element_type=F32)
                lse_old[(r, sub)] = ex[:tq] + ex[tq:]

    pairs = []
    for r in range(n_res):
        for j in range(n_tiles):
            sl = slice(r * ATTN_WIDTH + j * LANES, r * ATTN_WIDTH + (j + 1) * LANES)
            k_all = jnp.concatenate([kp_ref[:, sl], km_ref[:, sl], kn_ref[:, sl]], axis=0)
            v_all = jnp.concatenate([vp_ref[:, sl], vm_ref[:, sl], vn_ref[:, sl]], axis=0)
            for sub in range(n_sub):
                rows = slice(sub * tq, (sub + 1) * tq)
                q2 = q_ref[rows, sl]
                zero = jnp.zeros_like(q2)
                qs = jnp.concatenate([jnp.where(low_half, q2, zero), jnp.where(low_half, zero, q2)], axis=0)
                kw = k_all[sub * tq:sub * tq + tk]
                pairs.append(dict(r=r, j=j, sub=sub, sl=sl, rows=rows, vw=v_all[sub * tq:sub * tq + tk],
                                  s=lax.dot_general(qs, kw, nt, preferred_element_type=F32) + biases[sub]))
    ones = jnp.ones((tk, LANES), BF16)
    for pr in pairs:
        m_rows = jnp.broadcast_to(jnp.max(pr["s"], axis=-1, keepdims=True), (2 * tq, LANES))
        pr["m"] = unstack(m_rows)
        pr["p"] = jnp.exp2(pr["s"] - jnp.concatenate([m_rows] * (tk // LANES), axis=1)).astype(BF16)
    for pr in pairs:
        pr["pv"] = jnp.dot(pr["p"], pr["vw"], preferred_element_type=F32)
        pr["l"] = jnp.dot(pr["p"], ones, preferred_element_type=F32)
    for pr in pairs:
        den = unstack(pr["l"])
        o = unstack(pr["pv"]) / den
        lse = pr["m"] + jnp.log2(den)
        if not first:
            lse_prev = lse_old[(pr["r"], pr["sub"])][:, pr["j"] * LANES:(pr["j"] + 1) * LANES]
            top = jnp.maximum(lse_prev, lse)
            w_prev, w_cur = jnp.exp2(lse_prev - top), jnp.exp2(lse - top)
            tot = w_prev + w_cur
            o = (w_prev * op_ref[pr["rows"], pr["sl"]] + w_cur * o) / tot
            lse = top + jnp.log2(tot)
        pr["lse"] = lse
        acc_scr[c0 + pr["r"], pr["j"], pr["rows"], :] = o
    if not last:
        for r in range(n_res):
            for sub in range(n_sub):
                wc = jnp.zeros((tq, LANES), F32)
                for pr in pairs:
                    if pr["r"] == r and pr["sub"] == sub:
                        keep = jnp.logical_or(lane == pr["j"], lane == HEAD_DIM + pr["j"])
                        wc = jnp.where(keep, pr["lse"], wc)
                ml_scr[c0 + r, sub * tq:(sub + 1) * tq, :] = wc

    @pl.when(c0 + n_res == dil)
    def _():
        if last:
            for cc in range(dil):
                for j in range(n_tiles):
                    nat_scr[j, pl.ds(cc, tb, stride=dil), :] = acc_scr[cc, j]
            for j in range(n_tiles):
                o_ref[:, j * LANES:(j + 1) * LANES] = nat_scr[j].astype(o_ref.dtype)
        else:
            ratio = next_dil // dil
            rows_out = tb // ratio
            for cc in range(dil):
                for m in range(ratio):
                    cb = m * dil + cc
                    rows = pl.ds(m, rows_out, stride=ratio)
                    for j in range(n_tiles):
                        lo = cb * ATTN_WIDTH + j * LANES
                        acc_out_ref[:, lo:lo + LANES] = acc_scr[cc, j, rows, :]
                    ml_out_ref[:, cb * LANES:(cb + 1) * LANES] = ml_scr[cc, rows, :]


def _attn_pattern(qkv, prev, dil, next_dil, radius, t, t1, seq_lens):
    q, k, v = qkv
    w = ATTN_WIDTH
    first, last = prev is None, next_dil is None
    rows = t // dil
    tq, n_sub, n_res = ATTN_TILE[dil]
    tb = tq * n_sub
    per_q = tb // radius
    n_halo = rows // radius
    n_tiles = w // LANES
    wr = n_res * w
    main = lambda i, c: (i, c)
    before = lambda i, c: (jnp.maximum(i * per_q - 1, 0), c)
    after = lambda i, c: (jnp.minimum((i + 1) * per_q, n_halo - 1), c)
    whole = lambda i, c: (i, 0)
    kv_specs = [pl.BlockSpec((radius, wr), before), pl.BlockSpec((tb, wr), main), pl.BlockSpec((radius, wr), after)]
    in_specs = [pl.BlockSpec((tb, wr), main)] + kv_specs + kv_specs
    args = [q, k, k, k, v, v, v]
    if not first:
        in_specs += [pl.BlockSpec((tb, wr), main), pl.BlockSpec((tb, n_res * LANES), main)]
        args += list(prev)
    scratch = [pltpu.VMEM((dil, n_tiles, tb, LANES), F32)]
    if last:
        out_specs = [pl.BlockSpec((tb * dil, w), whole)]
        out_shape = [jax.ShapeDtypeStruct((t, w), BF16)]
        scratch.append(pltpu.VMEM((n_tiles, tb * dil, LANES), F32))
    else:
        rows_out = tb * dil // next_dil
        out_specs = [pl.BlockSpec((rows_out, next_dil * w), whole), pl.BlockSpec((rows_out, next_dil * LANES), whole)]
        out_shape = [jax.ShapeDtypeStruct((t // next_dil, next_dil * w), F32),
                     jax.ShapeDtypeStruct((t // next_dil, next_dil * LANES), F32)]
        scratch.append(pltpu.VMEM((dil, tb, LANES), F32))
    return pl.pallas_call(
        functools.partial(_attn_kernel, dil=dil, next_dil=next_dil, first=first, radius=radius, tq=tq,
                          rows1=t1 // dil, len1=seq_lens[0] // dil, len2=seq_lens[1] // dil),
        grid=(rows // tb, dil // n_res),
        in_specs=in_specs,
        out_specs=out_specs,
        out_shape=out_shape,
        scratch_shapes=scratch,
        compiler_params=_cparams(("arbitrary", "arbitrary")),
        name=f"attn_d{dil}",
    )(*args)


def _attention(qkv_views, t, t1, seq_lens):
    state = None
    n = len(DILATED_PATTERNS)
    for idx, (window, dil) in enumerate(DILATED_PATTERNS):
        next_dil = DILATED_PATTERNS[idx + 1][1] if idx + 1 < n else None
        state = _attn_pattern(qkv_views[idx], state, dil, next_dil, window // (2 * dil), t, t1, seq_lens)
    return state[0]


def _bdot(a, b):
    return jnp.dot(a.astype(BF16), b.astype(BF16), preferred_element_type=F32)


def _split3(x):
    hi = x.astype(BF16)
    r1 = x - hi.astype(F32)
    mid = r1.astype(BF16)
    lo = (r1 - mid.astype(F32)).astype(BF16)
    return hi, mid, lo


def _dn_kernel(qf_ref, kf_ref, vf_ref, gtf_ref, qb_ref, kb_ref, vb_ref, gtb_ref,
               of_ref, ob_ref, sf_ref, sb_ref, *, c, n_chunks, chunks1, len1, len2):
    n_sub = qf_ref.shape[0] // c
    i = pl.program_id(0)

    def seq_pos(ci):
        in_first = ci < chunks1
        seq_len = jnp.where(in_first, len1, len2)
        off = jnp.where(in_first, ci, ci - chunks1)
        return off % seq_len, seq_len

    pos_f, _ = seq_pos(i * n_sub)
    pos_b, len_b = seq_pos(n_chunks - 1 - i * n_sub)

    @pl.when(pos_f == 0)
    def _():
        sf_ref[...] = jnp.zeros_like(sf_ref)

    @pl.when(pos_b == len_b - 1)
    def _():
        sb_ref[...] = jnp.zeros_like(sb_ref)

    ii = lax.broadcasted_iota(jnp.int32, (c, c), 0)
    jj = lax.broadcasted_iota(jnp.int32, (c, c), 1)
    lower, lower_strict = ii >= jj, ii > jj
    upper, upper_strict = ii <= jj, ii < jj
    eye = jnp.where(ii == jj, 1.0, 0.0).astype(F32)
    tri_l = jnp.where(lower, 1.0, 0.0).astype(F32)
    tri_u = jnp.where(upper, 1.0, 0.0).astype(F32)
    level_masks = [jnp.where(ii // INV_BASE == jj // INV_BASE, 1.0, 0.0).astype(F32)]
    b = INV_BASE
    while b < c:
        same_outer = ii // (2 * b) == jj // (2 * b)
        level_masks.append(jnp.where(same_outer & (ii // b != jj // b), 1.0, 0.0).astype(F32))
        b *= 2
    ones_cd = jnp.ones((c, DN_HEAD_DIM), BF16)
    nt = (((1,), (1,)), ((), ()))

    chains = []
    for forward, (q_ref, k_ref, v_ref, gt_ref, o_ref, s_ref) in (
            (True, (qf_ref, kf_ref, vf_ref, gtf_ref, of_ref, sf_ref)),
            (False, (qb_ref, kb_ref, vb_ref, gtb_ref, ob_ref, sb_ref))):
        tri_r = tri_u if forward else tri_l
        row_rhs = jnp.concatenate([tri_r.astype(BF16), ones_cd], axis=1)
        keep, strict = (lower, lower_strict) if forward else (upper, upper_strict)
        d0 = 0 if forward else N_DN_HEADS
        for order in range(n_sub):
            sub = order if forward else n_sub - 1 - order
            rows = slice(sub * c, (sub + 1) * c)
            gt = gt_ref[:, rows]
            row = sum(jnp.dot(piece, row_rhs, preferred_element_type=F32) for piece in _split3(gt))
            cs_row, tot_row = row[:, :c], row[:, c:]
            cs_col, gates_col = jnp.transpose(cs_row), jnp.transpose(gt)
            tot_col = jnp.broadcast_to(cs_col[c - 1:c] if forward else cs_col[0:1], cs_col.shape)
            for hd in range(N_DN_HEADS):
                gi = d0 + hd
                bi = 2 * N_DN_HEADS + d0 + hd
                sl = slice(hd * DN_HEAD_DIM, (hd + 1) * DN_HEAD_DIM)
                chains.append(dict(
                    order=order, rows=rows, sl=sl, hd=hd, keep=keep, strict=strict, o_ref=o_ref, s_ref=s_ref,
                    q=q_ref[rows, sl], k=k_ref[rows, sl], v=v_ref[rows, sl],
                    gc_col=cs_col[:, gi:gi + 1], gc_row=cs_row[gi:gi + 1, :], beta=gates_col[:, bi:bi + 1],
                    tot_col=tot_col[:, gi:gi + 1], tot_row=tot_row[gi:gi + 1, :]))

    for ch in chains:
        qh, kh = ch["q"], ch["k"]
        both = lax.dot_general(jnp.concatenate([qh, kh], axis=0), kh, nt, preferred_element_type=F32)
        decay = jnp.where(ch["keep"], jnp.exp(jnp.minimum(ch["gc_col"] - ch["gc_row"], 0.0)), 0.0)
        ch["a"] = jnp.where(ch["strict"], ch["beta"] * both[c:] * decay, 0.0)
        ch["qk"] = (both[:c] * decay).astype(BF16)
    for ch in chains:
        dg = ch["a"] * level_masks[0]
        ch["x"] = eye - dg
        ch["y"] = _bdot(dg, dg)
    for ch in chains:
        ch["x"] = ch["x"] + _bdot(ch["x"], ch["y"])
    for off_mask in level_masks[1:]:
        for ch in chains:
            ch["y"] = _bdot(ch["a"] * off_mask, ch["x"])
        for ch in chains:
            ch["x"] = ch["x"] - _bdot(ch["x"], ch["y"])
    for ch in chains:
        kh = ch["k"].astype(F32)
        egc = jnp.exp(ch["gc_col"])
        rhs = jnp.concatenate([ch["v"].astype(F32) * ch["beta"], kh * (ch["beta"] * egc)], axis=1)
        ch["uw"] = _bdot(ch["x"], rhs)
        ch["lhs_q"] = ch["q"].astype(F32) * egc
        ch["k_dec_t"] = jnp.transpose(kh * jnp.exp(ch["tot_col"] - ch["gc_col"])).astype(BF16)
    states = {}
    for order in range(n_sub):
        group = [ch for ch in chains if ch["order"] == order]
        for ch in group:
            key = (id(ch["s_ref"]), ch["hd"])
            ch["state"] = states[key] if order else ch["s_ref"][ch["hd"]]
            lhs = jnp.concatenate([ch["uw"][:, DN_HEAD_DIM:], ch["lhs_q"]], axis=0)
            ch["ws"] = _bdot(lhs, ch["state"])
        for ch in group:
            v_new = (ch["uw"][:, :DN_HEAD_DIM] - ch["ws"][:c]).astype(BF16)
            ch["o_ref"][ch["rows"], ch["sl"]] = ch["ws"][c:] + jnp.dot(ch["qk"], v_new, preferred_element_type=F32)
            ch["v_new"] = v_new
        for ch in group:
            states[(id(ch["s_ref"]), ch["hd"])] = ch["state"] * jnp.exp(ch["tot_row"]) + jnp.dot(
                ch["k_dec_t"], ch["v_new"], preferred_element_type=F32)
    for ch in chains:
        if ch["order"] == n_sub - 1:
            ch["s_ref"][ch["hd"]] = states[(id(ch["s_ref"]), ch["hd"])]


def _deltanet(q, k, v, gates_t, t1, seq_lens):
    t, w = q.shape
    c = DN_CHUNK
    rows = c * DN_CHUNKS_PER_STEP
    n = t // rows
    fwd = lambda i: (i, 0)
    bwd = lambda i: (n - 1 - i, 0)
    fwd_t = lambda i: (0, i)
    bwd_t = lambda i: (0, n - 1 - i)

    def specs(row_map, col_map):
        return [pl.BlockSpec((rows, w), row_map)] * 3 + [pl.BlockSpec((N_GATES, rows), col_map)]

    state = pltpu.VMEM((N_DN_HEADS, DN_HEAD_DIM, DN_HEAD_DIM), F32)
    return pl.pallas_call(
        functools.partial(_dn_kernel, c=c, n_chunks=t // c, chunks1=t1 // c, len1=seq_lens[0] // c,
                          len2=seq_lens[1] // c),
        grid=(n,),
        in_specs=specs(fwd, fwd_t) + specs(bwd, bwd_t),
        out_specs=[pl.BlockSpec((rows, w), fwd), pl.BlockSpec((rows, w), bwd)],
        out_shape=[jax.ShapeDtypeStruct((t, w), F32)] * 2,
        scratch_shapes=[state, state],
        compiler_params=_cparams(("arbitrary",)),
        name="deltanet",
    )(q, k, v, gates_t, q, k, v, gates_t)


def kernel(x_prompt, x_sample, c_prompt, c_sample, ada_w, ada_b, norm_ffn1, ffn1_w_gate, ffn1_w_up, ffn1_w_down, norm_mix, w_in, conv_w, a_log, dt_bias, dn_norm, w_out, norm_ffn2, ffn2_w_gate, ffn2_w_up, ffn2_w_down, norm_final):
    b1, s1, d = x_prompt.shape
    b2, s2, _ = x_sample.shape
    depth = ada_w.shape[0]
    t1 = b1 * s1
    seq_lens = (s1, s2)
    assert s2 % s1 == 0 and t1 % s2 == 0, "flat layout needs nested sequence lengths"
    t = t1 + b2 * s2
    xs = [x_prompt.reshape(t1, d), x_sample.reshape(b2 * s2, d)]

    n_seq = b1 + b2
    c_all = jnp.concatenate([c_prompt, c_sample, jnp.zeros((-n_seq % SUBLANES, d), F32)], axis=0)
    mod = _modulation(c_all, ada_w, ada_b)
    blk_seq = jnp.concatenate([jnp.arange(b1), b1 + jnp.repeat(jnp.arange(b2), s2 // s1)])
    mod = mod[:, blk_seq].reshape(depth, blk_seq.shape[0], N_MOD, d)

    rope_tabs = _rope_tables(max(s1, s2))
    n_main = 3 * ATTN_WIDTH + 4 * DN_WIDTH
    gate_par_t = jnp.stack([a_log.astype(F32).reshape(depth, -1), dt_bias.astype(F32).reshape(depth, -1)], axis=2)
    gate_par_t = jnp.pad(gate_par_t, ((0, 0), (0, N_GATES - gate_par_t.shape[1]), (0, 0)))

    for l in range(depth):
        bf = lambda a: a.astype(BF16)
        x = _ffn(xs if l == 0 else [x], mod[l, :, 0:3], norm_ffn1[l], bf(ffn1_w_gate[l]), bf(ffn1_w_up[l]),
                 bf(ffn1_w_down[l]), s1, t1)
        qkv_views, (dq, dk, dv, z, gates_t) = _in_proj(
            x, mod[l, :, 3:6], norm_mix[l], bf(w_in[l][:, :n_main]), bf(w_in[l][:, n_main:]).T,
            gate_par_t[l], rope_tabs, conv_w[l], s1, seq_lens, t1)
        attn = _attention(qkv_views, t, t1, seq_lens)
        o_f, o_b = _deltanet(dq, dk, dv, gates_t, t1, seq_lens)
        mixer = (mod[l, :, 3:6], attn, o_f, o_b, z, dn_norm[l], bf(w_out[l][:ATTN_WIDTH]), bf(w_out[l][ATTN_WIDTH:]))
        is_last = l == depth - 1
        x = _ffn([x], mod[l, :, 6:9], norm_ffn2[l], bf(ffn2_w_gate[l]), bf(ffn2_w_up[l]), bf(ffn2_w_down[l]), s1,
                 t1, final_w=norm_final if is_last else None, split_out=is_last, mixer=mixer)
    return x[0].reshape(b1, s1, d), x[1].reshape(b2, s2, d)
```

```python
import functools

import jax
import jax.numpy as jnp
from jax import lax
from jax.experimental import pallas as pl
from jax.experimental.pallas import tpu as pltpu

F32 = jnp.float32
BF16 = jnp.bfloat16

N_MOD = 9
EPS = 1e-6
N_ATTN_HEADS = 8
HEAD_DIM = 64
ATTN_WIDTH = N_ATTN_HEADS * HEAD_DIM
ROT_HALF = HEAD_DIM // 8
ROPE_THETA = 500000.0
ATTN_Q_SCALE = HEAD_DIM ** -0.5 * 1.4426950408889634
DILATED_PATTERNS = ((128, 1), (512, 4), (2048, 16))
N_DN_HEADS = 4
DN_HEAD_DIM = 128
DN_WIDTH = N_DN_HEADS * DN_HEAD_DIM
CONV_K = 5
N_GATES = 4 * N_DN_HEADS

LANES = 128
SUBLANES = 8
VMEM_LIMIT_BYTES = 56 * 1024 * 1024

FFN_ROWS = 512
PROJ_ROWS = 512
ATTN_TILE = {1: (128, 8, 1), 4: (128, 2, 4), 16: (128, 2, 4)}
DN_CHUNK = 128
DN_CHUNKS_PER_STEP = 2
INV_BASE = 4
MOD_COLS = 1152
HALO_ROWS = SUBLANES
NEG_BIG = -1e30


def _cparams(sem):
    return pltpu.CompilerParams(dimension_semantics=sem, vmem_limit_bytes=VMEM_LIMIT_BYTES)


def _resident(shape):
    return pl.BlockSpec(shape, lambda *_: (0, 0), pipeline_mode=pl.Buffered(1))


def _silu(x):
    return x * jax.nn.sigmoid(x)


def _modulated_norm(x, norm_w, shift, scale):
    inv = lax.rsqrt(jnp.mean(x * x, axis=-1, keepdims=True) + EPS)
    return x * inv * (norm_w * (1.0 + scale)) + shift


def _mod_kernel(c_ref, w_ref, b_ref, o_ref):
    sc = _silu(c_ref[...]).astype(BF16)
    o_ref[0] = jnp.dot(sc, w_ref[0].astype(BF16), preferred_element_type=F32) + b_ref[0]


def _modulation(c_all, ada_w, ada_b):
    depth, d, n = ada_w.shape
    rows = c_all.shape[0]
    return pl.pallas_call(
        _mod_kernel,
        grid=(depth, n // MOD_COLS),
        in_specs=[
            pl.BlockSpec((rows, d), lambda l, j: (0, 0)),
            pl.BlockSpec((1, d, MOD_COLS), lambda l, j: (l, 0, j)),
            pl.BlockSpec((1, 1, MOD_COLS), lambda l, j: (l, 0, j)),
        ],
        out_specs=pl.BlockSpec((1, rows, MOD_COLS), lambda l, j: (l, 0, j)),
        out_shape=jax.ShapeDtypeStruct((depth, rows, n), F32),
        compiler_params=_cparams(("arbitrary", "arbitrary")),
        name="modulation",
    )(c_all, ada_w, ada_b.reshape(depth, 1, n))


def _mixer_residual(x, mod_ref, attn_ref, of_ref, ob_ref, z_ref, nw_ref, wa_ref, wd_ref):
    o = of_ref[...] + ob_ref[...]
    z = z_ref[...].astype(F32)
    nw = nw_ref[...]
    parts = []
    for hd in range(N_DN_HEADS):
        sl = slice(hd * DN_HEAD_DIM, (hd + 1) * DN_HEAD_DIM)
        oh = o[:, sl]
        parts.append(oh * lax.rsqrt(jnp.mean(oh * oh, axis=-1, keepdims=True) + EPS) * nw * _silu(z[:, sl]))
    dn = jnp.concatenate(parts, axis=1).astype(BF16)
    y = jnp.dot(attn_ref[...], wa_ref[...], preferred_element_type=F32)
    y = y + jnp.dot(dn, wd_ref[...], preferred_element_type=F32)
    return x + mod_ref[0, 2:3, :] * y


N_MIXER_REFS = 8


def _ffn_kernel(*refs, n_in, n_out, with_mixer, final_norm, rows1):
    x_refs, refs = refs[:n_in], refs[n_in:]
    if with_mixer:
        mixer_refs, refs = refs[:N_MIXER_REFS], refs[N_MIXER_REFS:]
    mod_ref, nw_ref, wg_ref, wu_ref, wd_ref = refs[:5]
    refs = refs[5:]
    if final_norm:
        fw_ref, refs = refs[0], refs[1:]
    o_refs, refs = refs[:n_out], refs[n_out:]
    tm = x_refs[0].shape[0]
    in_first = pl.program_id(0) < rows1 // tm
    if n_in == 2:
        stage_ref = refs[0]

        @pl.when(in_first)
        def _():
            stage_ref[...] = x_refs[0][...]

        @pl.when(jnp.logical_not(in_first))
        def _():
            stage_ref[...] = x_refs[1][...]

        x = stage_ref[...]
    else:
        x = x_refs[0][...]
    if with_mixer:
        x = _mixer_residual(x, *mixer_refs)
    h = _modulated_norm(x, nw_ref[...], mod_ref[0, 0:1, :], mod_ref[0, 1:2, :]).astype(BF16)
    g = jnp.dot(h, wg_ref[...], preferred_element_type=F32)
    u = jnp.dot(h, wu_ref[...], preferred_element_type=F32)
    a = (_silu(g) * u).astype(BF16)
    y = jnp.dot(a, wd_ref[...], preferred_element_type=F32)
    out = x + (0.5 * mod_ref[0, 2:3, :]) * y
    if final_norm:
        out = out * lax.rsqrt(jnp.mean(out * out, axis=-1, keepdims=True) + EPS) * fw_ref[...]
    if n_out == 2:
        @pl.when(in_first)
        def _():
            o_refs[0][...] = out

        @pl.when(jnp.logical_not(in_first))
        def _():
            o_refs[1][...] = out
    else:
        o_refs[0][...] = out


def _ffn(xs, mod, norm_w, wg, wu, wd, blk_rows, rows1, final_w=None, split_out=False, mixer=None):
    d = xs[0].shape[1]
    t = sum(x.shape[0] for x in xs)
    f = wg.shape[1]
    tm = FFN_ROWS
    per_blk = blk_rows // tm
    nb1 = rows1 // tm
    const = lambda i: (0, 0)
    row = lambda i: (i, 0)
    blk = lambda i: (i // per_blk, 0, 0)
    group1 = lambda i: (jnp.minimum(i, nb1 - 1), 0)
    group2 = lambda i: (jnp.maximum(i - nb1, 0), 0)
    in_specs = [pl.BlockSpec((tm, d), row)] if len(xs) == 1 else [pl.BlockSpec((tm, d), group1), pl.BlockSpec((tm, d), group2)]
    args = list(xs)
    if mixer is not None:
        m_mod, attn, o_f, o_b, z, dn_norm, w_attn, w_dn = mixer
        in_specs += [
            pl.BlockSpec((1, 3, d), blk),
            pl.BlockSpec((tm, ATTN_WIDTH), row),
            pl.BlockSpec((tm, DN_WIDTH), row),
            pl.BlockSpec((tm, DN_WIDTH), row),
            pl.BlockSpec((tm, DN_WIDTH), row),
            pl.BlockSpec((1, DN_HEAD_DIM), const),
            _resident((ATTN_WIDTH, d)),
            _resident((DN_WIDTH, d)),
        ]
        args += [m_mod, attn, o_f, o_b, z, dn_norm.reshape(1, DN_HEAD_DIM), w_attn, w_dn]
    in_specs += [
        pl.BlockSpec((1, 3, d), blk),
        pl.BlockSpec((1, d), const),
        _resident((d, f)),
        _resident((d, f)),
        _resident((f, d)),
    ]
    args += [mod, norm_w.reshape(1, d), wg, wu, wd]
    if final_w is not None:
        in_specs.append(pl.BlockSpec((1, d), const))
        args.append(final_w.reshape(1, d))
    if split_out:
        out_specs = [pl.BlockSpec((tm, d), group1), pl.BlockSpec((tm, d), group2)]
        out_shape = [jax.ShapeDtypeStruct((rows1, d), F32), jax.ShapeDtypeStruct((t - rows1, d), F32)]
    else:
        out_specs = [pl.BlockSpec((tm, d), row)]
        out_shape = [jax.ShapeDtypeStruct((t, d), F32)]
    outs = pl.pallas_call(
        functools.partial(_ffn_kernel, n_in=len(xs), n_out=len(out_specs), with_mixer=mixer is not None,
                          final_norm=final_w is not None, rows1=rows1),
        grid=(t // tm,),
        in_specs=in_specs,
        out_specs=out_specs,
        out_shape=out_shape,
        scratch_shapes=[pltpu.VMEM((tm, d), F32)] if len(xs) == 2 else [],
        compiler_params=_cparams(("arbitrary",)),
        name="ffn",
    )(*args)
    return outs if split_out else outs[0]


def _inproj_kernel(xp_ref, x_ref, xn_ref, mod_ref, nw_ref, w_ref, wgt_ref, gpt_ref,
                   cos_ref, sfw_ref, sbk_ref, cw_ref, *rest, rows1, len1, len2):
    n_views = len(DILATED_PATTERNS)
    qkv_refs = [rest[3 * n:3 * n + 3] for n in range(n_views)]
    dq_ref, dk_ref, dv_ref, z_ref, gbt_ref = rest[3 * n_views:3 * n_views + 5]
    stage_refs = rest[3 * n_views + 5:]
    stage_ref = stage_refs[0]
    tm = x_ref.shape[0]
    nw, shift, scale = nw_ref[...], mod_ref[0, 0:1, :], mod_ref[0, 1:2, :]
    h_main = _modulated_norm(x_ref[...], nw, shift, scale)
    h_ext = jnp.concatenate([_modulated_norm(xp_ref[...], nw, shift, scale), h_main,
                             _modulated_norm(xn_ref[...], nw, shift, scale)], axis=0).astype(BF16)
    h = h_main.astype(BF16)
    cos, sfw, sbk = cos_ref[...], sfw_ref[...], sbk_ref[...]

    def rope(xp):
        return xp * cos + pltpu.roll(xp, LANES - ROT_HALF, 1) * sfw + pltpu.roll(xp, ROT_HALF, 1) * sbk

    n_tiles = ATTN_WIDTH // LANES
    o0 = 3 * ATTN_WIDTH

    def attn_epilogue(a, pa):
        for j in range(n_tiles):
            col = pa[:, j * LANES:(j + 1) * LANES]
            if a == 0:
                col = rope(col) * ATTN_Q_SCALE
            elif a == 1:
                col = rope(col)
            stage_ref[a * n_tiles + j] = col
        for j in range(n_tiles):
            tile = a * n_tiles + j
            for n, (_, dil) in enumerate(DILATED_PATTERNS):
                for c in range(dil):
                    lo = c * ATTN_WIDTH + j * LANES
                    if n == 0:
                        src = stage_ref[tile]
                    else:
                        prev_dil = DILATED_PATTERNS[n - 1][1]
                        step = dil // prev_dil
                        prev = stage_refs[n - 1]
                        src = prev[tile * prev_dil + c % prev_dil, pl.ds(c // prev_dil, tm // dil, stride=step), :]
                    if 0 < n < n_views - 1:
                        stage_refs[n][tile * dil + c] = src
                    qkv_refs[n][a][:, lo:lo + LANES] = src.astype(BF16)

    r0 = pl.program_id(0) * tm
    in_first = r0 < rows1
    seq_len = jnp.where(in_first, len1, len2)
    off = jnp.where(in_first, r0, r0 - rows1)
    keep_prev = jnp.where((off % seq_len) == 0, 0.0, 1.0).astype(F32)
    keep_next = jnp.where(((off + tm) % seq_len) == 0, 0.0, 1.0).astype(F32)
    pad = CONV_K // 2
    n_ext = tm + 2 * HALO_ROWS

    def dn_epilogue(a, pe):
        out_ref = (dq_ref, dk_ref, dv_ref)[a]
        cols = slice(a * DN_WIDTH, (a + 1) * DN_WIDTH)
        pe = jnp.concatenate([pe[0:HALO_ROWS] * keep_prev, pe[HALO_ROWS:HALO_ROWS + tm],
                              pe[HALO_ROWS + tm:] * keep_next], axis=0)
        taps = [pe * cw_ref[j:j + 1, cols] for j in range(CONV_K)]
        before, after = taps[0], taps[CONV_K - 1]
        for j in range(1, pad):
            before = taps[j] + pltpu.roll(before, 1, 0)
            after = taps[CONV_K - 1 - j] + pltpu.roll(after, n_ext - 1, 0)
        y = taps[pad] + pltpu.roll(before, 1, 0) + pltpu.roll(after, n_ext - 1, 0)
        y = _silu(y[HALO_ROWS:HALO_ROWS + tm])
        for hd in range(N_DN_HEADS):
            sl = slice(hd * DN_HEAD_DIM, (hd + 1) * DN_HEAD_DIM)
            yh = y[:, sl]
            if a == 0:
                yh = yh * lax.rsqrt(jnp.sum(yh * yh, axis=-1, keepdims=True) + EPS) * (DN_HEAD_DIM ** -0.5)
            elif a == 1:
                yh = yh * lax.rsqrt(jnp.sum(yh * yh, axis=-1, keepdims=True) + EPS)
            out_ref[:, sl] = yh.astype(BF16)

    def z_epilogue(_, pz):
        z_ref[...] = pz.astype(BF16)

    groups = [(h, a * ATTN_WIDTH, ATTN_WIDTH, attn_epilogue, a) for a in range(3)]
    groups += [(h_ext, o0 + a * DN_WIDTH, DN_WIDTH, dn_epilogue, a) for a in range(3)]
    groups.append((h, o0 + 3 * DN_WIDTH, DN_WIDTH, z_epilogue, None))

    for lhs, lo, width, epilogue, arg in groups:
        epilogue(arg, jnp.dot(lhs, w_ref[:, lo:lo + width], preferred_element_type=F32))

    abt = lax.dot_general(wgt_ref[...], h, (((1,), (1,)), ((), ())), preferred_element_type=F32)
    is_gt = lax.broadcasted_iota(jnp.int32, abt.shape, 0) < 2 * N_DN_HEADS
    gpt = gpt_ref[...]
    gbt_ref[...] = jnp.where(is_gt, -jnp.exp(gpt[:, 0:1]) * jax.nn.softplus(abt + gpt[:, 1:2]), jax.nn.sigmoid(abt))


def _in_proj(x, mod, norm_w, w_main, w_gate_t, gate_par_t, rope_tabs, conv_w, blk_rows, seq_lens, t1):
    t, d = x.shape
    tm = PROJ_ROWS
    per_blk = blk_rows // tm
    n_main = w_main.shape[1]
    nb1 = t1 // tm
    s1b, s2b = seq_lens[0] // tm, seq_lens[1] // tm
    per_halo = tm // HALO_ROWS
    n_halo = t // HALO_ROWS
    const = lambda i: (0, 0)
    row = lambda i: (i, 0)
    pos = lambda i: (jnp.where(i < nb1, i % s1b, (i - nb1) % s2b), 0)
    cos, sfw, sbk = rope_tabs
    view_specs, view_shapes = [], []
    for _, dil in DILATED_PATTERNS:
        view_specs += [pl.BlockSpec((tm // dil, dil * ATTN_WIDTH), row)] * 3
        view_shapes += [jax.ShapeDtypeStruct((t // dil, dil * ATTN_WIDTH), BF16)] * 3
    outs = pl.pallas_call(
        functools.partial(_inproj_kernel, rows1=t1, len1=seq_lens[0], len2=seq_lens[1]),
        grid=(t // tm,),
        in_specs=[
            pl.BlockSpec((HALO_ROWS, d), lambda i: (jnp.maximum(i * per_halo - 1, 0), 0)),
            pl.BlockSpec((tm, d), row),
            pl.BlockSpec((HALO_ROWS, d), lambda i: (jnp.minimum((i + 1) * per_halo, n_halo - 1), 0)),
            pl.BlockSpec((1, 3, d), lambda i: (i // per_blk, 0, 0)),
            pl.BlockSpec((1, d), const),
            _resident((d, n_main)),
            pl.BlockSpec((N_GATES, d), const),
            pl.BlockSpec((N_GATES, 2), const),
            pl.BlockSpec((tm, LANES), pos),
            pl.BlockSpec((tm, LANES), pos),
            pl.BlockSpec((tm, LANES), pos),
            pl.BlockSpec((CONV_K, 3 * DN_WIDTH), const),
        ],
        out_specs=view_specs + [pl.BlockSpec((tm, DN_WIDTH), row)] * 4 + [
            pl.BlockSpec((N_GATES, tm), lambda i: (0, i)),
        ],
        out_shape=view_shapes + [jax.ShapeDtypeStruct((t, DN_WIDTH), BF16)] * 4 + [
            jax.ShapeDtypeStruct((N_GATES, t), F32),
        ],
        scratch_shapes=[pltpu.VMEM((3 * ATTN_WIDTH // LANES * dil, tm // dil, LANES), F32)
                        for _, dil in DILATED_PATTERNS[:-1]],
        compiler_params=_cparams(("arbitrary",)),
        name="in_proj",
    )(x, x, x, mod, norm_w.reshape(1, d), w_main, w_gate_t, gate_par_t, cos, sfw, sbk, conv_w)
    n_qkv = 3 * len(DILATED_PATTERNS)
    qkv_views = [outs[3 * n:3 * n + 3] for n in range(len(DILATED_PATTERNS))]
    return qkv_views, outs[n_qkv:]


def _rope_tables(max_len):
    half = ROT_HALF
    inv = ROPE_THETA ** (-jnp.arange(half, dtype=F32) / half)
    ang = jnp.arange(max_len, dtype=F32)[:, None] * inv[None, :]
    cos, sin = jnp.cos(ang), jnp.sin(ang)
    ones = jnp.ones((max_len, HEAD_DIM - 2 * half), F32)
    zeros_h = jnp.zeros((max_len, half), F32)
    zeros_r = jnp.zeros((max_len, HEAD_DIM - 2 * half), F32)
    cos_h = jnp.concatenate([cos, cos, ones], axis=1)
    sfw_h = jnp.concatenate([-sin, zeros_h, zeros_r], axis=1)
    sbk_h = jnp.concatenate([zeros_h, sin, zeros_r], axis=1)
    rep = LANES // HEAD_DIM
    return tuple(jnp.tile(a, (1, rep)) for a in (cos_h, sfw_h, sbk_h))


def _attn_kernel(q_ref, kp_ref, km_ref, kn_ref, vp_ref, vm_ref, vn_ref, *rest,
                 dil, next_dil, first, radius, tq, rows1, len1, len2):
    last = next_dil is None
    if not first:
        op_ref, wp_ref = rest[:2]
        rest = rest[2:]
    if last:
        o_ref, acc_scr, nat_scr = rest
    else:
        acc_out_ref, ml_out_ref, acc_scr, ml_scr = rest
    tb = q_ref.shape[0]
    n_sub = tb // tq
    tk = tq + 2 * radius
    n_tiles = ATTN_WIDTH // LANES
    n_res = q_ref.shape[1] // ATTN_WIDTH
    c0 = pl.program_id(1) * n_res
    r0 = pl.program_id(0) * tb
    in_first = r0 < rows1
    seq_len = jnp.where(in_first, len1, len2)
    off = jnp.where(in_first, r0, r0 - rows1)
    first_key = jnp.where((off % seq_len) == 0, radius, 0)
    end_key = jnp.where(((off + tb) % seq_len) == 0, tq + radius, tk)

    qi = lax.broadcasted_iota(jnp.int32, (tq, tk), 0)
    kj = lax.broadcasted_iota(jnp.int32, (tq, tk), 1)
    in_band = jnp.abs(kj - radius - qi) <= radius
    biases = []
    for sub in range(n_sub):
        valid = in_band
        if sub == 0:
            valid = valid & (kj >= first_key)
        if sub == n_sub - 1:
            valid = valid & (kj < end_key)
        bias = jnp.where(valid, 0.0, NEG_BIG).astype(F32)
        biases.append(jnp.concatenate([bias, bias], axis=0))

    lane = lax.broadcasted_iota(jnp.int32, (1, LANES), 1)
    low_half = lane < HEAD_DIM
    nt = (((1,), (1,)), ((), ()))

    def unstack(x):
        return jnp.where(low_half, x[:tq], x[tq:])

    if not first:
        e_row = lax.broadcasted_iota(jnp.int32, (LANES, ATTN_WIDTH), 0)
        e_col = lax.broadcasted_iota(jnp.int32, (LANES, ATTN_WIDTH), 1)
        spread = jnp.where(e_row == e_col // LANES + HEAD_DIM * ((e_col % LANES) // HEAD_DIM), 1.0, 0.0).astype(BF16)
        lse_old = {}
        for r in range(n_res):
            for sub in range(n_sub):
                wc = wp_ref[sub * tq:(sub + 1) * tq, r * LANES:(r + 1) * LANES]
                hi = wc.astype(BF16)
                lo = (wc - hi.astype(F32)).astype(BF16)
                ex = jnp.dot(jnp.concatenate([hi, lo], axis=0), spread, preferred_element_type=F32)
                lse_old[(r, sub)] = ex[:tq] + ex[tq:]

    pairs = []
    for r in range(n_res):
        for j in range(n_tiles):
            sl = slice(r * ATTN_WIDTH + j * LANES, r * ATTN_WIDTH + (j + 1) * LANES)
            k_all = jnp.concatenate([kp_ref[:, sl], km_ref[:, sl], kn_ref[:, sl]], axis=0)
            v_all = jnp.concatenate([vp_ref[:, sl], vm_ref[:, sl], vn_ref[:, sl]], axis=0)
            for sub in range(n_sub):
                rows = slice(sub * tq, (sub + 1) * tq)
                q2 = q_ref[rows, sl]
                zero = jnp.zeros_like(q2)
                qs = jnp.concatenate([jnp.where(low_half, q2, zero), jnp.where(low_half, zero, q2)], axis=0)
                kw = k_all[sub * tq:sub * tq + tk]
                pairs.append(dict(r=r, j=j, sub=sub, sl=sl, rows=rows, vw=v_all[sub * tq:sub * tq + tk],
                                  s=lax.dot_general(qs, kw, nt, preferred_element_type=F32) + biases[sub]))
    ones = jnp.ones((tk, LANES), BF16)
    for pr in pairs:
        m_rows = jnp.broadcast_to(jnp.max(pr["s"], axis=-1, keepdims=True), (2 * tq, LANES))
        pr["m"] = unstack(m_rows)
        pr["p"] = jnp.exp2(pr["s"] - jnp.concatenate([m_rows] * (tk // LANES), axis=1)).astype(BF16)
    for pr in pairs:
        pr["pv"] = jnp.dot(pr["p"], pr["vw"], preferred_element_type=F32)
        pr["l"] = jnp.dot(pr["p"], ones, preferred_element_type=F32)
    for pr in pairs:
        den = unstack(pr["l"])
        o = unstack(pr["pv"]) / den
        lse = pr["m"] + jnp.log2(den)
        if not first:
            lse_prev = lse_old[(pr["r"], pr["sub"])][:, pr["j"] * LANES:(pr["j"] + 1) * LANES]
            top = jnp.maximum(lse_prev, lse)
            w_prev, w_cur = jnp.exp2(lse_prev - top), jnp.exp2(lse - top)
            tot = w_prev + w_cur
            o = (w_prev * op_ref[pr["rows"], pr["sl"]] + w_cur * o) / tot
            lse = top + jnp.log2(tot)
        pr["lse"] = lse
        acc_scr[c0 + pr["r"], pr["j"], pr["rows"], :] = o
    if not last:
        for r in range(n_res):
            for sub in range(n_sub):
                wc = jnp.zeros((tq, LANES), F32)
                for pr in pairs:
                    if pr["r"] == r and pr["sub"] == sub:
                        keep = jnp.logical_or(lane == pr["j"], lane == HEAD_DIM + pr["j"])
                        wc = jnp.where(keep, pr["lse"], wc)
                ml_scr[c0 + r, sub * tq:(sub + 1) * tq, :] = wc

    @pl.when(c0 + n_res == dil)
    def _():
        if last:
            for cc in range(dil):
                for j in range(n_tiles):
                    nat_scr[j, pl.ds(cc, tb, stride=dil), :] = acc_scr[cc, j]
            for j in range(n_tiles):
                o_ref[:, j * LANES:(j + 1) * LANES] = nat_scr[j].astype(o_ref.dtype)
        else:
            ratio = next_dil // dil
            rows_out = tb // ratio
            for cc in range(dil):
                for m in range(ratio):
                    cb = m * dil + cc
                    rows = pl.ds(m, rows_out, stride=ratio)
                    for j in range(n_tiles):
                        lo = cb * ATTN_WIDTH + j * LANES
                        acc_out_ref[:, lo:lo + LANES] = acc_scr[cc, j, rows, :]
                    ml_out_ref[:, cb * LANES:(cb + 1) * LANES] = ml_scr[cc, rows, :]


def _attn_pattern(qkv, prev, dil, next_dil, radius, t, t1, seq_lens):
    q, k, v = qkv
    w = ATTN_WIDTH
    first, last = prev is None, next_dil is None
    rows = t // dil
    tq, n_sub, n_res = ATTN_TILE[dil]
    tb = tq * n_sub
    per_q = tb // radius
    n_halo = rows // radius
    n_tiles = w // LANES
    wr = n_res * w
    main = lambda i, c: (i, c)
    before = lambda i, c: (jnp.maximum(i * per_q - 1, 0), c)
    after = lambda i, c: (jnp.minimum((i + 1) * per_q, n_halo - 1), c)
    whole = lambda i, c: (i, 0)
    kv_specs = [pl.BlockSpec((radius, wr), before), pl.BlockSpec((tb, wr), main), pl.BlockSpec((radius, wr), after)]
    in_specs = [pl.BlockSpec((tb, wr), main)] + kv_specs + kv_specs
    args = [q, k, k, k, v, v, v]
    if not first:
        in_specs += [pl.BlockSpec((tb, wr), main), pl.BlockSpec((tb, n_res * LANES), main)]
        args += list(prev)
    scratch = [pltpu.VMEM((dil, n_tiles, tb, LANES), F32)]
    if last:
        out_specs = [pl.BlockSpec((tb * dil, w), whole)]
        out_shape = [jax.ShapeDtypeStruct((t, w), BF16)]
        scratch.append(pltpu.VMEM((n_tiles, tb * dil, LANES), F32))
    else:
        rows_out = tb * dil // next_dil
        out_specs = [pl.BlockSpec((rows_out, next_dil * w), whole), pl.BlockSpec((rows_out, next_dil * LANES), whole)]
        out_shape = [jax.ShapeDtypeStruct((t // next_dil, next_dil * w), F32),
                     jax.ShapeDtypeStruct((t // next_dil, next_dil * LANES), F32)]
        scratch.append(pltpu.VMEM((dil, tb, LANES), F32))
    return pl.pallas_call(
        functools.partial(_attn_kernel, dil=dil, next_dil=next_dil, first=first, radius=radius, tq=tq,
                          rows1=t1 // dil, len1=seq_lens[0] // dil, len2=seq_lens[1] // dil),
        grid=(rows // tb, dil // n_res),
        in_specs=in_specs,
        out_specs=out_specs,
        out_shape=out_shape,
        scratch_shapes=scratch,
        compiler_params=_cparams(("arbitrary", "arbitrary")),
        name=f"attn_d{dil}",
    )(*args)


def _attention(qkv_views, t, t1, seq_lens):
    state = None
    n = len(DILATED_PATTERNS)
    for idx, (window, dil) in enumerate(DILATED_PATTERNS):
        next_dil = DILATED_PATTERNS[idx + 1][1] if idx + 1 < n else None
        state = _attn_pattern(qkv_views[idx], state, dil, next_dil, window // (2 * dil), t, t1, seq_lens)
    return state[0]


def _bdot(a, b):
    return jnp.dot(a.astype(BF16), b.astype(BF16), preferred_element_type=F32)


def _split3(x):
    hi = x.astype(BF16)
    r1 = x - hi.astype(F32)
    mid = r1.astype(BF16)
    lo = (r1 - mid.astype(F32)).astype(BF16)
    return hi, mid, lo


def _dn_kernel(qf_ref, kf_ref, vf_ref, gtf_ref, qb_ref, kb_ref, vb_ref, gtb_ref,
               of_ref, ob_ref, sf_ref, sb_ref, *, c, n_chunks, chunks1, len1, len2):
    n_sub = qf_ref.shape[0] // c
    i = pl.program_id(0)

    def seq_pos(ci):
        in_first = ci < chunks1
        seq_len = jnp.where(in_first, len1, len2)
        off = jnp.where(in_first, ci, ci - chunks1)
        return off % seq_len, seq_len

    pos_f, _ = seq_pos(i * n_sub)
    pos_b, len_b = seq_pos(n_chunks - 1 - i * n_sub)

    @pl.when(pos_f == 0)
    def _():
        sf_ref[...] = jnp.zeros_like(sf_ref)

    @pl.when(pos_b == len_b - 1)
    def _():
        sb_ref[...] = jnp.zeros_like(sb_ref)

    ii = lax.broadcasted_iota(jnp.int32, (c, c), 0)
    jj = lax.broadcasted_iota(jnp.int32, (c, c), 1)
    lower, lower_strict = ii >= jj, ii > jj
    upper, upper_strict = ii <= jj, ii < jj
    eye = jnp.where(ii == jj, 1.0, 0.0).astype(F32)
    tri_l = jnp.where(lower, 1.0, 0.0).astype(F32)
    tri_u = jnp.where(upper, 1.0, 0.0).astype(F32)
    level_masks = [jnp.where(ii // INV_BASE == jj // INV_BASE, 1.0, 0.0).astype(F32)]
    b = INV_BASE
    while b < c:
        same_outer = ii // (2 * b) == jj // (2 * b)
        level_masks.append(jnp.where(same_outer & (ii // b != jj // b), 1.0, 0.0).astype(F32))
        b *= 2
    ones_cd = jnp.ones((c, DN_HEAD_DIM), BF16)
    nt = (((1,), (1,)), ((), ()))

    chains = []
    for forward, (q_ref, k_ref, v_ref, gt_ref, o_ref, s_ref) in (
            (True, (qf_ref, kf_ref, vf_ref, gtf_ref, of_ref, sf_ref)),
            (False, (qb_ref, kb_ref, vb_ref, gtb_ref, ob_ref, sb_ref))):
        tri_r = tri_u if forward else tri_l
        row_rhs = jnp.concatenate([tri_r.astype(BF16), ones_cd], axis=1)
        keep, strict = (lower, lower_strict) if forward else (upper, upper_strict)
        d0 = 0 if forward else N_DN_HEADS
        for order in range(n_sub):
            sub = order if forward else n_sub - 1 - order
            rows = slice(sub * c, (sub + 1) * c)
            gt = gt_ref[:, rows]
            row = sum(jnp.dot(piece, row_rhs, preferred_element_type=F32) for piece in _split3(gt))
            cs_row, tot_row = row[:, :c], row[:, c:]
            cs_col, gates_col = jnp.transpose(cs_row), jnp.transpose(gt)
            tot_col = jnp.broadcast_to(cs_col[c - 1:c] if forward else cs_col[0:1], cs_col.shape)
            for hd in range(N_DN_HEADS):
                gi = d0 + hd
                bi = 2 * N_DN_HEADS + d0 + hd
                sl = slice(hd * DN_HEAD_DIM, (hd + 1) * DN_HEAD_DIM)
                chains.append(dict(
                    order=order, rows=rows, sl=sl, hd=hd, keep=keep, strict=strict, o_ref=o_ref, s_ref=s_ref,
                    q=q_ref[rows, sl], k=k_ref[rows, sl], v=v_ref[rows, sl],
                    gc_col=cs_col[:, gi:gi + 1], gc_row=cs_row[gi:gi + 1, :], beta=gates_col[:, bi:bi + 1],
                    tot_col=tot_col[:, gi:gi + 1], tot_row=tot_row[gi:gi + 1, :]))

    for ch in chains:
        qh, kh = ch["q"], ch["k"]
        both = lax.dot_general(jnp.concatenate([qh, kh], axis=0), kh, nt, preferred_element_type=F32)
        decay = jnp.where(ch["keep"], jnp.exp(jnp.minimum(ch["gc_col"] - ch["gc_row"], 0.0)), 0.0)
        ch["a"] = jnp.where(ch["strict"], ch["beta"] * both[c:] * decay, 0.0)
        ch["qk"] = (both[:c] * decay).astype(BF16)
    for ch in chains:
        dg = ch["a"] * level_masks[0]
        ch["x"] = eye - dg
        ch["y"] = _bdot(dg, dg)
    for ch in chains:
        ch["x"] = ch["x"] + _bdot(ch["x"], ch["y"])
    for off_mask in level_masks[1:]:
        for ch in chains:
            ch["y"] = _bdot(ch["a"] * off_mask, ch["x"])
        for ch in chains:
            ch["x"] = ch["x"] - _bdot(ch["x"], ch["y"])
    for ch in chains:
        kh = ch["k"].astype(F32)
        egc = jnp.exp(ch["gc_col"])
        rhs = jnp.concatenate([ch["v"].astype(F32) * ch["beta"], kh * (ch["beta"] * egc)], axis=1)
        ch["uw"] = _bdot(ch["x"], rhs)
        ch["lhs_q"] = ch["q"].astype(F32) * egc
        ch["k_dec_t"] = jnp.transpose(kh * jnp.exp(ch["tot_col"] - ch["gc_col"])).astype(BF16)
    states = {}
    for order in range(n_sub):
        group = [ch for ch in chains if ch["order"] == order]
        for ch in group:
            key = (id(ch["s_ref"]), ch["hd"])
            ch["state"] = states[key] if order else ch["s_ref"][ch["hd"]]
            lhs = jnp.concatenate([ch["uw"][:, DN_HEAD_DIM:], ch["lhs_q"]], axis=0)
            ch["ws"] = _bdot(lhs, ch["state"])
        for ch in group:
            v_new = (ch["uw"][:, :DN_HEAD_DIM] - ch["ws"][:c]).astype(BF16)
            ch["o_ref"][ch["rows"], ch["sl"]] = ch["ws"][c:] + jnp.dot(ch["qk"], v_new, preferred_element_type=F32)
            ch["v_new"] = v_new
        for ch in group:
            states[(id(ch["s_ref"]), ch["hd"])] = ch["state"] * jnp.exp(ch["tot_row"]) + jnp.dot(
                ch["k_dec_t"], ch["v_new"], preferred_element_type=F32)
    for ch in chains:
        if ch["order"] == n_sub - 1:
            ch["s_ref"][ch["hd"]] = states[(id(ch["s_ref"]), ch["hd"])]


def _deltanet(q, k, v, gates_t, t1, seq_lens):
    t, w = q.shape
    c = DN_CHUNK
    rows = c * DN_CHUNKS_PER_STEP
    n = t // rows
    fwd = lambda i: (i, 0)
    bwd = lambda i: (n - 1 - i, 0)
    fwd_t = lambda i: (0, i)
    bwd_t = lambda i: (0, n - 1 - i)

    def specs(row_map, col_map):
        return [pl.BlockSpec((rows, w), row_map)] * 3 + [pl.BlockSpec((N_GATES, rows), col_map)]

    state = pltpu.VMEM((N_DN_HEADS, DN_HEAD_DIM, DN_HEAD_DIM), F32)
    return pl.pallas_call(
        functools.partial(_dn_kernel, c=c, n_chunks=t // c, chunks1=t1 // c, len1=seq_lens[0] // c,
                          len2=seq_lens[1] // c),
        grid=(n,),
        in_specs=specs(fwd, fwd_t) + specs(bwd, bwd_t),
        out_specs=[pl.BlockSpec((rows, w), fwd), pl.BlockSpec((rows, w), bwd)],
        out_shape=[jax.ShapeDtypeStruct((t, w), F32)] * 2,
        scratch_shapes=[state, state],
        compiler_params=_cparams(("arbitrary",)),
        name="deltanet",
    )(q, k, v, gates_t, q, k, v, gates_t)


def kernel(x_prompt, x_sample, c_prompt, c_sample, ada_w, ada_b, norm_ffn1, ffn1_w_gate, ffn1_w_up, ffn1_w_down, norm_mix, w_in, conv_w, a_log, dt_bias, dn_norm, w_out, norm_ffn2, ffn2_w_gate, ffn2_w_up, ffn2_w_down, norm_final):
    b1, s1, d = x_prompt.shape
    b2, s2, _ = x_sample.shape
    depth = ada_w.shape[0]
    t1 = b1 * s1
    seq_lens = (s1, s2)
    assert s2 % s1 == 0 and t1 % s2 == 0, "flat layout needs nested sequence lengths"
    t = t1 + b2 * s2
    xs = [x_prompt.reshape(t1, d), x_sample.reshape(b2 * s2, d)]

    n_seq = b1 + b2
    c_all = jnp.concatenate([c_prompt, c_sample, jnp.zeros((-n_seq % SUBLANES, d), F32)], axis=0)
    mod = _modulation(c_all, ada_w, ada_b)
    blk_seq = jnp.concatenate([jnp.arange(b1), b1 + jnp.repeat(jnp.arange(b2), s2 // s1)])
    mod = mod[:, blk_seq].reshape(depth, blk_seq.shape[0], N_MOD, d)

    rope_tabs = _rope_tables(max(s1, s2))
    n_main = 3 * ATTN_WIDTH + 4 * DN_WIDTH
    gate_par_t = jnp.stack([a_log.astype(F32).reshape(depth, -1), dt_bias.astype(F32).reshape(depth, -1)], axis=2)
    gate_par_t = jnp.pad(gate_par_t, ((0, 0), (0, N_GATES - gate_par_t.shape[1]), (0, 0)))

    for l in range(depth):
        bf = lambda a: a.astype(BF16)
        x = _ffn(xs if l == 0 else [x], mod[l, :, 0:3], norm_ffn1[l], bf(ffn1_w_gate[l]), bf(ffn1_w_up[l]),
                 bf(ffn1_w_down[l]), s1, t1)
        qkv_views, (dq, dk, dv, z, gates_t) = _in_proj(
            x, mod[l, :, 3:6], norm_mix[l], bf(w_in[l][:, :n_main]), bf(w_in[l][:, n_main:]).T,
            gate_par_t[l], rope_tabs, conv_w[l], s1, seq_lens, t1)
        attn = _attention(qkv_views, t, t1, seq_lens)
        o_f, o_b = _deltanet(dq, dk, dv, gates_t, t1, seq_lens)
        mixer = (mod[l, :, 3:6], attn, o_f, o_b, z, dn_norm[l], bf(w_out[l][:ATTN_WIDTH]), bf(w_out[l][ATTN_WIDTH:]))
        is_last = l == depth - 1
        x = _ffn([x], mod[l, :, 6:9], norm_ffn2[l], bf(ffn2_w_gate[l]), bf(ffn2_w_up[l]), bf(ffn2_w_down[l]), s1,
                 t1, final_w=norm_final if is_last else None, split_out=is_last, mixer=mixer)
    return x[0].reshape(b1, s1, d), x[1].reshape(b2, s2, d)
```

```python
import functools

import jax
import jax.numpy as jnp
from jax import lax
from jax.experimental import pallas as pl
from jax.experimental.pallas import tpu as pltpu

F32 = jnp.float32
BF16 = jnp.bfloat16

N_MOD = 9
EPS = 1e-6
N_ATTN_HEADS = 8
HEAD_DIM = 64
ATTN_WIDTH = N_ATTN_HEADS * HEAD_DIM
ROT_HALF = HEAD_DIM // 8
ROPE_THETA = 500000.0
ATTN_Q_SCALE = HEAD_DIM ** -0.5 * 1.4426950408889634
DILATED_PATTERNS = ((128, 1), (512, 4), (2048, 16))
assert DILATED_PATTERNS[0][1] == 1 and all(b[1] % a[1] == 0 for a, b in zip(DILATED_PATTERNS, DILATED_PATTERNS[1:]))
N_DN_HEADS = 4
DN_HEAD_DIM = 128
DN_WIDTH = N_DN_HEADS * DN_HEAD_DIM
CONV_K = 5
N_GATES = 4 * N_DN_HEADS

LANES = 128
SUBLANES = 8
VMEM_LIMIT_BYTES = 56 * 1024 * 1024

FFN_ROWS = 512
PROJ_ROWS = 512
ATTN_TILE = {1: (128, 16, 1), 4: (128, 2, 4), 16: (128, 2, 4)}
DN_CHUNK = 128
DN_CHUNKS_PER_STEP = 2
INV_BASE = 4
MOD_COLS = 1152
HALO_ROWS = SUBLANES
NEG_BIG = -1e30


def _cparams(sem):
    return pltpu.CompilerParams(dimension_semantics=sem, vmem_limit_bytes=VMEM_LIMIT_BYTES)


def _resident(shape):
    return pl.BlockSpec(shape, lambda *_: (0, 0), pipeline_mode=pl.Buffered(1))


def _silu(x):
    return x * jax.nn.sigmoid(x)


def _modulated_norm(x, norm_w, shift, scale):
    inv = lax.rsqrt(jnp.mean(x * x, axis=-1, keepdims=True) + EPS)
    return x * inv * (norm_w * (1.0 + scale)) + shift


def _mod_kernel(c_ref, w_ref, b_ref, o_ref):
    sc = _silu(c_ref[...]).astype(BF16)
    o_ref[0] = jnp.dot(sc, w_ref[0].astype(BF16), preferred_element_type=F32) + b_ref[0]


def _modulation(c_all, ada_w, ada_b):
    depth, d, n = ada_w.shape
    rows = c_all.shape[0]
    return pl.pallas_call(
        _mod_kernel,
        grid=(depth, n // MOD_COLS),
        in_specs=[
            pl.BlockSpec((rows, d), lambda l, j: (0, 0)),
            pl.BlockSpec((1, d, MOD_COLS), lambda l, j: (l, 0, j)),
            pl.BlockSpec((1, 1, MOD_COLS), lambda l, j: (l, 0, j)),
        ],
        out_specs=pl.BlockSpec((1, rows, MOD_COLS), lambda l, j: (l, 0, j)),
        out_shape=jax.ShapeDtypeStruct((depth, rows, n), F32),
        compiler_params=_cparams(("arbitrary", "arbitrary")),
        name="modulation",
    )(c_all, ada_w, ada_b.reshape(depth, 1, n))


def _mixer_residual(x, mod_ref, attn_ref, of_ref, ob_ref, z_ref, nw_ref, wa_ref, wd_ref):
    o = of_ref[...] + ob_ref[...]
    z = z_ref[...].astype(F32)
    nw = nw_ref[...]
    parts = []
    for hd in range(N_DN_HEADS):
        sl = slice(hd * DN_HEAD_DIM, (hd + 1) * DN_HEAD_DIM)
        oh = o[:, sl]
        parts.append(oh * lax.rsqrt(jnp.mean(oh * oh, axis=-1, keepdims=True) + EPS) * nw * _silu(z[:, sl]))
    dn = jnp.concatenate(parts, axis=1).astype(BF16)
    y = jnp.dot(attn_ref[...], wa_ref[...], preferred_element_type=F32)
    y = y + jnp.dot(dn, wd_ref[...], preferred_element_type=F32)
    return x + mod_ref[0, 2:3, :] * y


N_MIXER_REFS = 8


def _ffn_kernel(*refs, n_in, n_out, with_mixer, final_norm, rows1):
    x_refs, refs = refs[:n_in], refs[n_in:]
    if with_mixer:
        mixer_refs, refs = refs[:N_MIXER_REFS], refs[N_MIXER_REFS:]
    mod_ref, nw_ref, wg_ref, wu_ref, wd_ref = refs[:5]
    refs = refs[5:]
    if final_norm:
        fw_ref, refs = refs[0], refs[1:]
    o_refs, refs = refs[:n_out], refs[n_out:]
    tm = x_refs[0].shape[0]
    in_first = pl.program_id(0) < rows1 // tm
    if n_in == 2:
        stage_ref = refs[0]

        @pl.when(in_first)
        def _():
            stage_ref[...] = x_refs[0][...]

        @pl.when(jnp.logical_not(in_first))
        def _():
            stage_ref[...] = x_refs[1][...]

        x = stage_ref[...]
    else:
        x = x_refs[0][...]
    if with_mixer:
        x = _mixer_residual(x, *mixer_refs)
    h = _modulated_norm(x, nw_ref[...], mod_ref[0, 0:1, :], mod_ref[0, 1:2, :]).astype(BF16)
    g = jnp.dot(h, wg_ref[...], preferred_element_type=F32)
    u = jnp.dot(h, wu_ref[...], preferred_element_type=F32)
    a = (_silu(g) * u).astype(BF16)
    y = jnp.dot(a, wd_ref[...], preferred_element_type=F32)
    out = x + (0.5 * mod_ref[0, 2:3, :]) * y
    if final_norm:
        out = out * lax.rsqrt(jnp.mean(out * out, axis=-1, keepdims=True) + EPS) * fw_ref[...]
    if n_out == 2:
        @pl.when(in_first)
        def _():
            o_refs[0][...] = out

        @pl.when(jnp.logical_not(in_first))
        def _():
            o_refs[1][...] = out
    else:
        o_refs[0][...] = out


def _ffn(xs, mod, norm_w, wg, wu, wd, blk_rows, rows1, final_w=None, split_out=False, mixer=None):
    d = xs[0].shape[1]
    t = sum(x.shape[0] for x in xs)
    f = wg.shape[1]
    tm = FFN_ROWS
    per_blk = blk_rows // tm
    nb1 = rows1 // tm
    const = lambda i: (0, 0)
    row = lambda i: (i, 0)
    blk = lambda i: (i // per_blk, 0, 0)
    group1 = lambda i: (jnp.minimum(i, nb1 - 1), 0)
    group2 = lambda i: (jnp.maximum(i - nb1, 0), 0)
    in_specs = [pl.BlockSpec((tm, d), row)] if len(xs) == 1 else [pl.BlockSpec((tm, d), group1), pl.BlockSpec((tm, d), group2)]
    args = list(xs)
    if mixer is not None:
        m_mod, attn, o_f, o_b, z, dn_norm, w_attn, w_dn = mixer
        in_specs += [
            pl.BlockSpec((1, 3, d), blk),
            pl.BlockSpec((tm, ATTN_WIDTH), row),
            pl.BlockSpec((tm, DN_WIDTH), row),
            pl.BlockSpec((tm, DN_WIDTH), row),
            pl.BlockSpec((tm, DN_WIDTH), row),
            pl.BlockSpec((1, DN_HEAD_DIM), const),
            _resident((ATTN_WIDTH, d)),
            _resident((DN_WIDTH, d)),
        ]
        args += [m_mod, attn, o_f, o_b, z, dn_norm.reshape(1, DN_HEAD_DIM), w_attn, w_dn]
    in_specs += [
        pl.BlockSpec((1, 3, d), blk),
        pl.BlockSpec((1, d), const),
        _resident((d, f)),
        _resident((d, f)),
        _resident((f, d)),
    ]
    args += [mod, norm_w.reshape(1, d), wg, wu, wd]
    if final_w is not None:
        in_specs.append(pl.BlockSpec((1, d), const))
        args.append(final_w.reshape(1, d))
    if split_out:
        out_specs = [pl.BlockSpec((tm, d), group1), pl.BlockSpec((tm, d), group2)]
        out_shape = [jax.ShapeDtypeStruct((rows1, d), F32), jax.ShapeDtypeStruct((t - rows1, d), F32)]
    else:
        out_specs = [pl.BlockSpec((tm, d), row)]
        out_shape = [jax.ShapeDtypeStruct((t, d), F32)]
    outs = pl.pallas_call(
        functools.partial(_ffn_kernel, n_in=len(xs), n_out=len(out_specs), with_mixer=mixer is not None,
                          final_norm=final_w is not None, rows1=rows1),
        grid=(t // tm,),
        in_specs=in_specs,
        out_specs=out_specs,
        out_shape=out_shape,
        scratch_shapes=[pltpu.VMEM((tm, d), F32)] if len(xs) == 2 else [],
        compiler_params=_cparams(("arbitrary",)),
        name="ffn",
    )(*args)
    return outs if split_out else outs[0]


def _inproj_kernel(xp_ref, x_ref, xn_ref, mod_ref, nw_ref, w_ref, wgt_ref, gpt_ref,
                   cos_ref, sfw_ref, sbk_ref, cw_ref, *rest, rows1, len1, len2):
    n_views = len(DILATED_PATTERNS)
    qkv_refs = [rest[3 * n:3 * n + 3] for n in range(n_views)]
    dq_ref, dk_ref, dv_ref, z_ref, gbt_ref = rest[3 * n_views:3 * n_views + 5]
    stage_refs = rest[3 * n_views + 5:]
    stage_ref = stage_refs[0]
    tm = x_ref.shape[0]
    nw, shift, scale = nw_ref[...], mod_ref[0, 0:1, :], mod_ref[0, 1:2, :]
    h_main = _modulated_norm(x_ref[...], nw, shift, scale)
    h_ext = jnp.concatenate([_modulated_norm(xp_ref[...], nw, shift, scale), h_main,
                             _modulated_norm(xn_ref[...], nw, shift, scale)], axis=0).astype(BF16)
    h = h_main.astype(BF16)
    cos, sfw, sbk = cos_ref[...], sfw_ref[...], sbk_ref[...]

    def rope(xp):
        return xp * cos + pltpu.roll(xp, LANES - ROT_HALF, 1) * sfw + pltpu.roll(xp, ROT_HALF, 1) * sbk

    n_tiles = ATTN_WIDTH // LANES
    o0 = 3 * ATTN_WIDTH

    def attn_epilogue(a, pa):
        for j in range(n_tiles):
            col = pa[:, j * LANES:(j + 1) * LANES]
            if a == 0:
                col = rope(col) * ATTN_Q_SCALE
            elif a == 1:
                col = rope(col)
            stage_ref[a * n_tiles + j] = col
        for j in range(n_tiles):
            tile = a * n_tiles + j
            for n, (_, dil) in enumerate(DILATED_PATTERNS):
                for c in range(dil):
                    lo = c * ATTN_WIDTH + j * LANES
                    if n == 0:
                        src = stage_ref[tile]
                    else:
                        prev_dil = DILATED_PATTERNS[n - 1][1]
                        step = dil // prev_dil
                        prev = stage_refs[n - 1]
                        src = prev[tile * prev_dil + c % prev_dil, pl.ds(c // prev_dil, tm // dil, stride=step), :]
                    if 0 < n < n_views - 1:
                        stage_refs[n][tile * dil + c] = src
                    qkv_refs[n][a][:, lo:lo + LANES] = src.astype(BF16)

    r0 = pl.program_id(0) * tm
    in_first = r0 < rows1
    seq_len = jnp.where(in_first, len1, len2)
    off = jnp.where(in_first, r0, r0 - rows1)
    keep_prev = jnp.where((off % seq_len) == 0, 0.0, 1.0).astype(F32)
    keep_next = jnp.where(((off + tm) % seq_len) == 0, 0.0, 1.0).astype(F32)
    pad = CONV_K // 2
    n_ext = tm + 2 * HALO_ROWS

    def dn_epilogue(a, pe):
        out_ref = (dq_ref, dk_ref, dv_ref)[a]
        cols = slice(a * DN_WIDTH, (a + 1) * DN_WIDTH)
        pe = jnp.concatenate([pe[0:HALO_ROWS] * keep_prev, pe[HALO_ROWS:HALO_ROWS + tm],
                              pe[HALO_ROWS + tm:] * keep_next], axis=0)
        taps = [pe * cw_ref[j:j + 1, cols] for j in range(CONV_K)]
        before, after = taps[0], taps[CONV_K - 1]
        for j in range(1, pad):
            before = taps[j] + pltpu.roll(before, 1, 0)
            after = taps[CONV_K - 1 - j] + pltpu.roll(after, n_ext - 1, 0)
        y = taps[pad] + pltpu.roll(before, 1, 0) + pltpu.roll(after, n_ext - 1, 0)
        y = _silu(y[HALO_ROWS:HALO_ROWS + tm])
        for hd in range(N_DN_HEADS):
            sl = slice(hd * DN_HEAD_DIM, (hd + 1) * DN_HEAD_DIM)
            yh = y[:, sl]
            if a == 0:
                yh = yh * lax.rsqrt(jnp.sum(yh * yh, axis=-1, keepdims=True) + EPS) * (DN_HEAD_DIM ** -0.5)
            elif a == 1:
                yh = yh * lax.rsqrt(jnp.sum(yh * yh, axis=-1, keepdims=True) + EPS)
            out_ref[:, sl] = yh.astype(BF16)

    def z_epilogue(_, pz):
        z_ref[...] = pz.astype(BF16)

    groups = [(h, a * ATTN_WIDTH, ATTN_WIDTH, attn_epilogue, a) for a in range(3)]
    groups += [(h_ext, o0 + a * DN_WIDTH, DN_WIDTH, dn_epilogue, a) for a in range(3)]
    groups.append((h, o0 + 3 * DN_WIDTH, DN_WIDTH, z_epilogue, None))

    for lhs, lo, width, epilogue, arg in groups:
        epilogue(arg, jnp.dot(lhs, w_ref[:, lo:lo + width], preferred_element_type=F32))

    abt = lax.dot_general(wgt_ref[...], h, (((1,), (1,)), ((), ())), preferred_element_type=F32)
    is_gt = lax.broadcasted_iota(jnp.int32, abt.shape, 0) < 2 * N_DN_HEADS
    gpt = gpt_ref[...]
    gbt_ref[...] = jnp.where(is_gt, -jnp.exp(gpt[:, 0:1]) * jax.nn.softplus(abt + gpt[:, 1:2]), jax.nn.sigmoid(abt))


def _in_proj(x, mod, norm_w, w_main, w_gate_t, gate_par_t, rope_tabs, conv_w, blk_rows, seq_lens, t1):
    t, d = x.shape
    tm = PROJ_ROWS
    per_blk = blk_rows // tm
    n_main = w_main.shape[1]
    nb1 = t1 // tm
    s1b, s2b = seq_lens[0] // tm, seq_lens[1] // tm
    per_halo = tm // HALO_ROWS
    n_halo = t // HALO_ROWS
    const = lambda i: (0, 0)
    row = lambda i: (i, 0)
    pos = lambda i: (jnp.where(i < nb1, i % s1b, (i - nb1) % s2b), 0)
    cos, sfw, sbk = rope_tabs
    view_specs, view_shapes = [], []
    for _, dil in DILATED_PATTERNS:
        view_specs += [pl.BlockSpec((tm // dil, dil * ATTN_WIDTH), row)] * 3
        view_shapes += [jax.ShapeDtypeStruct((t // dil, dil * ATTN_WIDTH), BF16)] * 3
    outs = pl.pallas_call(
        functools.partial(_inproj_kernel, rows1=t1, len1=seq_lens[0], len2=seq_lens[1]),
        grid=(t // tm,),
        in_specs=[
            pl.BlockSpec((HALO_ROWS, d), lambda i: (jnp.maximum(i * per_halo - 1, 0), 0)),
            pl.BlockSpec((tm, d), row),
            pl.BlockSpec((HALO_ROWS, d), lambda i: (jnp.minimum((i + 1) * per_halo, n_halo - 1), 0)),
            pl.BlockSpec((1, 3, d), lambda i: (i // per_blk, 0, 0)),
            pl.BlockSpec((1, d), const),
            _resident((d, n_main)),
            pl.BlockSpec((N_GATES, d), const),
            pl.BlockSpec((N_GATES, 2), const),
            pl.BlockSpec((tm, LANES), pos),
            pl.BlockSpec((tm, LANES), pos),
            pl.BlockSpec((tm, LANES), pos),
            pl.BlockSpec((CONV_K, 3 * DN_WIDTH), const),
        ],
        out_specs=view_specs + [pl.BlockSpec((tm, DN_WIDTH), row)] * 4 + [
            pl.BlockSpec((N_GATES, tm), lambda i: (0, i)),
        ],
        out_shape=view_shapes + [jax.ShapeDtypeStruct((t, DN_WIDTH), BF16)] * 4 + [
            jax.ShapeDtypeStruct((N_GATES, t), F32),
        ],
        scratch_shapes=[pltpu.VMEM((3 * ATTN_WIDTH // LANES * dil, tm // dil, LANES), F32)
                        for _, dil in DILATED_PATTERNS[:-1]],
        compiler_params=_cparams(("arbitrary",)),
        name="in_proj",
    )(x, x, x, mod, norm_w.reshape(1, d), w_main, w_gate_t, gate_par_t, cos, sfw, sbk, conv_w)
    n_qkv = 3 * len(DILATED_PATTERNS)
    qkv_views = [outs[3 * n:3 * n + 3] for n in range(len(DILATED_PATTERNS))]
    return qkv_views, outs[n_qkv:]


def _rope_tables(max_len):
    half = ROT_HALF
    inv = ROPE_THETA ** (-jnp.arange(half, dtype=F32) / half)
    ang = jnp.arange(max_len, dtype=F32)[:, None] * inv[None, :]
    cos, sin = jnp.cos(ang), jnp.sin(ang)
    ones = jnp.ones((max_len, HEAD_DIM - 2 * half), F32)
    zeros_h = jnp.zeros((max_len, half), F32)
    zeros_r = jnp.zeros((max_len, HEAD_DIM - 2 * half), F32)
    cos_h = jnp.concatenate([cos, cos, ones], axis=1)
    sfw_h = jnp.concatenate([-sin, zeros_h, zeros_r], axis=1)
    sbk_h = jnp.concatenate([zeros_h, sin, zeros_r], axis=1)
    rep = LANES // HEAD_DIM
    return tuple(jnp.tile(a, (1, rep)) for a in (cos_h, sfw_h, sbk_h))


def _attn_kernel(q_ref, kp_ref, km_ref, kn_ref, vp_ref, vm_ref, vn_ref, *rest,
                 dil, next_dil, first, radius, tq, rows1, len1, len2):
    last = next_dil is None
    if not first:
        op_ref, wp_ref = rest[:2]
        rest = rest[2:]
    if last:
        o_ref, acc_scr, nat_scr = rest
    else:
        acc_out_ref, ml_out_ref, acc_scr, ml_scr = rest
    tb = q_ref.shape[0]
    n_sub = tb // tq
    tk = tq + 2 * radius
    n_tiles = ATTN_WIDTH // LANES
    n_res = q_ref.shape[1] // ATTN_WIDTH
    c0 = pl.program_id(1) * n_res
    r0 = pl.program_id(0) * tb
    in_first = r0 < rows1
    seq_len = jnp.where(in_first, len1, len2)
    off = jnp.where(in_first, r0, r0 - rows1)
    first_key = jnp.where((off % seq_len) == 0, radius, 0)
    end_key = jnp.where(((off + tb) % seq_len) == 0, tq + radius, tk)

    qi = lax.broadcasted_iota(jnp.int32, (tq, tk), 0)
    kj = lax.broadcasted_iota(jnp.int32, (tq, tk), 1)
    in_band = jnp.abs(kj - radius - qi) <= radius
    biases = []
    for sub in range(n_sub):
        valid = in_band
        if sub == 0:
            valid = valid & (kj >= first_key)
        if sub == n_sub - 1:
            valid = valid & (kj < end_key)
        bias = jnp.where(valid, 0.0, NEG_BIG).astype(F32)
        biases.append(jnp.concatenate([bias, bias], axis=0))

    lane = lax.broadcasted_iota(jnp.int32, (1, LANES), 1)
    low_half = lane < HEAD_DIM
    nt = (((1,), (1,)), ((), ()))

    def unstack(x):
        return jnp.where(low_half, x[:tq], x[tq:])

    if not first:
        e_row = lax.broadcasted_iota(jnp.int32, (LANES, ATTN_WIDTH), 0)
        e_col = lax.broadcasted_iota(jnp.int32, (LANES, ATTN_WIDTH), 1)
        spread = jnp.where(e_row == e_col // LANES + HEAD_DIM * ((e_col % LANES) // HEAD_DIM), 1.0, 0.0).astype(BF16)
        lse_old = {}
        for r in range(n_res):
            for sub in range(n_sub):
                wc = wp_ref[sub * tq:(sub + 1) * tq, r * LANES:(r + 1) * LANES]
                hi = wc.astype(BF16)
                lo = (wc - hi.astype(F32)).astype(BF16)
                ex = jnp.dot(jnp.concatenate([hi, lo], axis=0), spread, preferred_element_type=F32)
                lse_old[(r, sub)] = ex[:tq] + ex[tq:]

    pairs = []
    for r in range(n_res):
        for j in range(n_tiles):
            sl = slice(r * ATTN_WIDTH + j * LANES, r * ATTN_WIDTH + (j + 1) * LANES)
            k_all = jnp.concatenate([kp_ref[:, sl], km_ref[:, sl], kn_ref[:, sl]], axis=0)
            v_all = jnp.concatenate([vp_ref[:, sl], vm_ref[:, sl], vn_ref[:, sl]], axis=0)
            for sub in range(n_sub):
                rows = slice(sub * tq, (sub + 1) * tq)
                q2 = q_ref[rows, sl]
                zero = jnp.zeros_like(q2)
                qs = jnp.concatenate([jnp.where(low_half, q2, zero), jnp.where(low_half, zero, q2)], axis=0)
                kw = k_all[sub * tq:sub * tq + tk]
                pairs.append(dict(r=r, j=j, sub=sub, sl=sl, rows=rows, vw=v_all[sub * tq:sub * tq + tk],
                                  s=lax.dot_general(qs, kw, nt, preferred_element_type=F32) + biases[sub]))
    ones = jnp.ones((tk, LANES), BF16)
    for pr in pairs:
        m_rows = jnp.broadcast_to(jnp.max(pr["s"], axis=-1, keepdims=True), (2 * tq, LANES))
        pr["m"] = unstack(m_rows)
        pr["p"] = jnp.exp2(pr["s"] - jnp.concatenate([m_rows] * (tk // LANES), axis=1)).astype(BF16)
    for pr in pairs:
        pr["pv"] = jnp.dot(pr["p"], pr["vw"], preferred_element_type=F32)
        pr["l"] = jnp.dot(pr["p"], ones, preferred_element_type=F32)
    for pr in pairs:
        den = unstack(pr["l"])
        o = unstack(pr["pv"]) / den
        lse = pr["m"] + jnp.log2(den)
        if not first:
            lse_prev = lse_old[(pr["r"], pr["sub"])][:, pr["j"] * LANES:(pr["j"] + 1) * LANES]
            top = jnp.maximum(lse_prev, lse)
            w_prev, w_cur = jnp.exp2(lse_prev - top), jnp.exp2(lse - top)
            tot = w_prev + w_cur
            o = (w_prev * op_ref[pr["rows"], pr["sl"]] + w_cur * o) / tot
            lse = top + jnp.log2(tot)
        pr["lse"] = lse
        acc_scr[c0 + pr["r"], pr["j"], pr["rows"], :] = o
    if not last:
        for r in range(n_res):
            for sub in range(n_sub):
                wc = jnp.zeros((tq, LANES), F32)
                for pr in pairs:
                    if pr["r"] == r and pr["sub"] == sub:
                        keep = jnp.logical_or(lane == pr["j"], lane == HEAD_DIM + pr["j"])
                        wc = jnp.where(keep, pr["lse"], wc)
                ml_scr[c0 + r, sub * tq:(sub + 1) * tq, :] = wc

    @pl.when(c0 + n_res == dil)
    def _():
        if last:
            for cc in range(dil):
                for j in range(n_tiles):
                    nat_scr[j, pl.ds(cc, tb, stride=dil), :] = acc_scr[cc, j]
            for j in range(n_tiles):
                o_ref[:, j * LANES:(j + 1) * LANES] = nat_scr[j].astype(o_ref.dtype)
        else:
            ratio = next_dil // dil
            rows_out = tb // ratio
            for cc in range(dil):
                for m in range(ratio):
                    cb = m * dil + cc
                    rows = pl.ds(m, rows_out, stride=ratio)
                    for j in range(n_tiles):
                        lo = cb * ATTN_WIDTH + j * LANES
                        acc_out_ref[:, lo:lo + LANES] = acc_scr[cc, j, rows, :]
                    ml_out_ref[:, cb * LANES:(cb + 1) * LANES] = ml_scr[cc, rows, :]


def _attn_pattern(qkv, prev, dil, next_dil, radius, t, t1, seq_lens):
    q, k, v = qkv
    w = ATTN_WIDTH
    first, last = prev is None, next_dil is None
    rows = t // dil
    tq, n_sub, n_res = ATTN_TILE[dil]
    tb = tq * n_sub
    per_q = tb // radius
    n_halo = rows // radius
    n_tiles = w // LANES
    wr = n_res * w
    main = lambda i, c: (i, c)
    before = lambda i, c: (jnp.maximum(i * per_q - 1, 0), c)
    after = lambda i, c: (jnp.minimum((i + 1) * per_q, n_halo - 1), c)
    whole = lambda i, c: (i, 0)
    kv_specs = [pl.BlockSpec((radius, wr), before), pl.BlockSpec((tb, wr), main), pl.BlockSpec((radius, wr), after)]
    in_specs = [pl.BlockSpec((tb, wr), main)] + kv_specs + kv_specs
    args = [q, k, k, k, v, v, v]
    if not first:
        in_specs += [pl.BlockSpec((tb, wr), main), pl.BlockSpec((tb, n_res * LANES), main)]
        args += list(prev)
    scratch = [pltpu.VMEM((dil, n_tiles, tb, LANES), F32)]
    if last:
        out_specs = [pl.BlockSpec((tb * dil, w), whole)]
        out_shape = [jax.ShapeDtypeStruct((t, w), BF16)]
        scratch.append(pltpu.VMEM((n_tiles, tb * dil, LANES), F32))
    else:
        rows_out = tb * dil // next_dil
        out_specs = [pl.BlockSpec((rows_out, next_dil * w), whole), pl.BlockSpec((rows_out, next_dil * LANES), whole)]
        out_shape = [jax.ShapeDtypeStruct((t // next_dil, next_dil * w), F32),
                     jax.ShapeDtypeStruct((t // next_dil, next_dil * LANES), F32)]
        scratch.append(pltpu.VMEM((dil, tb, LANES), F32))
    return pl.pallas_call(
        functools.partial(_attn_kernel, dil=dil, next_dil=next_dil, first=first, radius=radius, tq=tq,
                          rows1=t1 // dil, len1=seq_lens[0] // dil, len2=seq_lens[1] // dil),
        grid=(rows // tb, dil // n_res),
        in_specs=in_specs,
        out_specs=out_specs,
        out_shape=out_shape,
        scratch_shapes=scratch,
        compiler_params=_cparams(("arbitrary", "arbitrary")),
        name=f"attn_d{dil}",
    )(*args)


def _attention(qkv_views, t, t1, seq_lens):
    state = None
    n = len(DILATED_PATTERNS)
    for idx, (window, dil) in enumerate(DILATED_PATTERNS):
        next_dil = DILATED_PATTERNS[idx + 1][1] if idx + 1 < n else None
        state = _attn_pattern(qkv_views[idx], state, dil, next_dil, window // (2 * dil), t, t1, seq_lens)
    return state[0]


def _bdot(a, b):
    return jnp.dot(a.astype(BF16), b.astype(BF16), preferred_element_type=F32)


def _split3(x):
    hi = x.astype(BF16)
    r1 = x - hi.astype(F32)
    mid = r1.astype(BF16)
    lo = (r1 - mid.astype(F32)).astype(BF16)
    return hi, mid, lo


def _dn_kernel(qf_ref, kf_ref, vf_ref, gtf_ref, qb_ref, kb_ref, vb_ref, gtb_ref,
               of_ref, ob_ref, sf_ref, sb_ref, *, c, n_chunks, chunks1, len1, len2):
    n_sub = qf_ref.shape[0] // c
    i = pl.program_id(0)

    def seq_pos(ci):
        in_first = ci < chunks1
        seq_len = jnp.where(in_first, len1, len2)
        off = jnp.where(in_first, ci, ci - chunks1)
        return off % seq_len, seq_len

    pos_f, _ = seq_pos(i * n_sub)
    pos_b, len_b = seq_pos(n_chunks - 1 - i * n_sub)

    @pl.when(pos_f == 0)
    def _():
        sf_ref[...] = jnp.zeros_like(sf_ref)

    @pl.when(pos_b == len_b - 1)
    def _():
        sb_ref[...] = jnp.zeros_like(sb_ref)

    ii = lax.broadcasted_iota(jnp.int32, (c, c), 0)
    jj = lax.broadcasted_iota(jnp.int32, (c, c), 1)
    lower, lower_strict = ii >= jj, ii > jj
    upper, upper_strict = ii <= jj, ii < jj
    eye = jnp.where(ii == jj, 1.0, 0.0).astype(F32)
    tri_l = jnp.where(lower, 1.0, 0.0).astype(F32)
    tri_u = jnp.where(upper, 1.0, 0.0).astype(F32)
    level_masks = [jnp.where(ii // INV_BASE == jj // INV_BASE, 1.0, 0.0).astype(F32)]
    b = INV_BASE
    while b < c:
        same_outer = ii // (2 * b) == jj // (2 * b)
        level_masks.append(jnp.where(same_outer & (ii // b != jj // b), 1.0, 0.0).astype(F32))
        b *= 2
    ones_cd = jnp.ones((c, DN_HEAD_DIM), BF16)
    nt = (((1,), (1,)), ((), ()))

    chains = []
    for forward, (q_ref, k_ref, v_ref, gt_ref, o_ref, s_ref) in (
            (True, (qf_ref, kf_ref, vf_ref, gtf_ref, of_ref, sf_ref)),
            (False, (qb_ref, kb_ref, vb_ref, gtb_ref, ob_ref, sb_ref))):
        tri_r = tri_u if forward else tri_l
        row_rhs = jnp.concatenate([tri_r.astype(BF16), ones_cd], axis=1)
        keep, strict = (lower, lower_strict) if forward else (upper, upper_strict)
        d0 = 0 if forward else N_DN_HEADS
        for order in range(n_sub):
            sub = order if forward else n_sub - 1 - order
            rows = slice(sub * c, (sub + 1) * c)
            gt = gt_ref[:, rows]
            row = sum(jnp.dot(piece, row_rhs, preferred_element_type=F32) for piece in _split3(gt))
            cs_row, tot_row = row[:, :c], row[:, c:]
            cs_col, gates_col = jnp.transpose(cs_row), jnp.transpose(gt)
            tot_col = jnp.broadcast_to(cs_col[c - 1:c] if forward else cs_col[0:1], cs_col.shape)
            for hd in range(N_DN_HEADS):
                gi = d0 + hd
                bi = 2 * N_DN_HEADS + d0 + hd
                sl = slice(hd * DN_HEAD_DIM, (hd + 1) * DN_HEAD_DIM)
                chains.append(dict(
                    order=order, rows=rows, sl=sl, hd=hd, keep=keep, strict=strict, o_ref=o_ref, s_ref=s_ref,
                    q=q_ref[rows, sl], k=k_ref[rows, sl], v=v_ref[rows, sl],
                    gc_col=cs_col[:, gi:gi + 1], gc_row=cs_row[gi:gi + 1, :], beta=gates_col[:, bi:bi + 1],
                    tot_col=tot_col[:, gi:gi + 1], tot_row=tot_row[gi:gi + 1, :]))

    for ch in chains:
        qh, kh = ch["q"], ch["k"]
        both = lax.dot_general(jnp.concatenate([qh, kh], axis=0), kh, nt, preferred_element_type=F32)
        decay = jnp.where(ch["keep"], jnp.exp(jnp.minimum(ch["gc_col"] - ch["gc_row"], 0.0)), 0.0)
        ch["a"] = jnp.where(ch["strict"], ch["beta"] * both[c:] * decay, 0.0)
        ch["qk"] = (both[:c] * decay).astype(BF16)
    for ch in chains:
        dg = ch["a"] * level_masks[0]
        ch["x"] = eye - dg
        ch["y"] = _bdot(dg, dg)
    for ch in chains:
        ch["x"] = ch["x"] + _bdot(ch["x"], ch["y"])
    for off_mask in level_masks[1:]:
        for ch in chains:
            ch["y"] = _bdot(ch["a"] * off_mask, ch["x"])
        for ch in chains:
            ch["x"] = ch["x"] - _bdot(ch["x"], ch["y"])
    for ch in chains:
        kh = ch["k"].astype(F32)
        egc = jnp.exp(ch["gc_col"])
        rhs = jnp.concatenate([ch["v"].astype(F32) * ch["beta"], kh * (ch["beta"] * egc)], axis=1)
        ch["uw"] = _bdot(ch["x"], rhs)
        ch["lhs_q"] = ch["q"].astype(F32) * egc
        ch["k_dec_t"] = jnp.transpose(kh * jnp.exp(ch["tot_col"] - ch["gc_col"])).astype(BF16)
    states = {}
    for order in range(n_sub):
        group = [ch for ch in chains if ch["order"] == order]
        for ch in group:
            key = (id(ch["s_ref"]), ch["hd"])
            ch["state"] = states[key] if order else ch["s_ref"][ch["hd"]]
            lhs = jnp.concatenate([ch["uw"][:, DN_HEAD_DIM:], ch["lhs_q"]], axis=0)
            ch["ws"] = _bdot(lhs, ch["state"])
        for ch in group:
            v_new = (ch["uw"][:, :DN_HEAD_DIM] - ch["ws"][:c]).astype(BF16)
            ch["o_ref"][ch["rows"], ch["sl"]] = ch["ws"][c:] + jnp.dot(ch["qk"], v_new, preferred_element_type=F32)
            ch["v_new"] = v_new
        for ch in group:
            states[(id(ch["s_ref"]), ch["hd"])] = ch["state"] * jnp.exp(ch["tot_row"]) + jnp.dot(
                ch["k_dec_t"], ch["v_new"], preferred_element_type=F32)
    for ch in chains:
        if ch["order"] == n_sub - 1:
            ch["s_ref"][ch["hd"]] = states[(id(ch["s_ref"]), ch["hd"])]


def _deltanet(q, k, v, gates_t, t1, seq_lens):
    t, w = q.shape
    c = DN_CHUNK
    rows = c * DN_CHUNKS_PER_STEP
    n = t // rows
    fwd = lambda i: (i, 0)
    bwd = lambda i: (n - 1 - i, 0)
    fwd_t = lambda i: (0, i)
    bwd_t = lambda i: (0, n - 1 - i)

    def specs(row_map, col_map):
        return [pl.BlockSpec((rows, w), row_map)] * 3 + [pl.BlockSpec((N_GATES, rows), col_map)]

    state = pltpu.VMEM((N_DN_HEADS, DN_HEAD_DIM, DN_HEAD_DIM), F32)
    return pl.pallas_call(
        functools.partial(_dn_kernel, c=c, n_chunks=t // c, chunks1=t1 // c, len1=seq_lens[0] // c,
                          len2=seq_lens[1] // c),
        grid=(n,),
        in_specs=specs(fwd, fwd_t) + specs(bwd, bwd_t),
        out_specs=[pl.BlockSpec((rows, w), fwd), pl.BlockSpec((rows, w), bwd)],
        out_shape=[jax.ShapeDtypeStruct((t, w), F32)] * 2,
        scratch_shapes=[state, state],
        compiler_params=_cparams(("arbitrary",)),
        name="deltanet",
    )(q, k, v, gates_t, q, k, v, gates_t)


def kernel(x_prompt, x_sample, c_prompt, c_sample, ada_w, ada_b, norm_ffn1, ffn1_w_gate, ffn1_w_up, ffn1_w_down, norm_mix, w_in, conv_w, a_log, dt_bias, dn_norm, w_out, norm_ffn2, ffn2_w_gate, ffn2_w_up, ffn2_w_down, norm_final):
    b1, s1, d = x_prompt.shape
    b2, s2, _ = x_sample.shape
    depth = ada_w.shape[0]
    t1 = b1 * s1
    seq_lens = (s1, s2)
    assert s2 % s1 == 0 and t1 % s2 == 0, "flat layout needs nested sequence lengths"
    t = t1 + b2 * s2
    xs = [x_prompt.reshape(t1, d), x_sample.reshape(b2 * s2, d)]

    n_seq = b1 + b2
    c_all = jnp.concatenate([c_prompt, c_sample, jnp.zeros((-n_seq % SUBLANES, d), F32)], axis=0)
    mod = _modulation(c_all, ada_w, ada_b)
    blk_seq = jnp.concatenate([jnp.arange(b1), b1 + jnp.repeat(jnp.arange(b2), s2 // s1)])
    mod = mod[:, blk_seq].reshape(depth, blk_seq.shape[0], N_MOD, d)

    rope_tabs = _rope_tables(max(s1, s2))
    n_main = 3 * ATTN_WIDTH + 4 * DN_WIDTH
    gate_par_t = jnp.stack([a_log.astype(F32).reshape(depth, -1), dt_bias.astype(F32).reshape(depth, -1)], axis=2)
    gate_par_t = jnp.pad(gate_par_t, ((0, 0), (0, N_GATES - gate_par_t.shape[1]), (0, 0)))

    for l in range(depth):
        bf = lambda a: a.astype(BF16)
        x = _ffn(xs if l == 0 else [x], mod[l, :, 0:3], norm_ffn1[l], bf(ffn1_w_gate[l]), bf(ffn1_w_up[l]),
                 bf(ffn1_w_down[l]), s1, t1)
        qkv_views, (dq, dk, dv, z, gates_t) = _in_proj(
            x, mod[l, :, 3:6], norm_mix[l], bf(w_in[l][:, :n_main]), bf(w_in[l][:, n_main:]).T,
            gate_par_t[l], rope_tabs, conv_w[l], s1, seq_lens, t1)
        attn = _attention(qkv_views, t, t1, seq_lens)
        o_f, o_b = _deltanet(dq, dk, dv, gates_t, t1, seq_lens)
        mixer = (mod[l, :, 3:6], attn, o_f, o_b, z, dn_norm[l], bf(w_out[l][:ATTN_WIDTH]), bf(w_out[l][ATTN_WIDTH:]))
        is_last = l == depth - 1
        x = _ffn([x], mod[l, :, 6:9], norm_ffn2[l], bf(ffn2_w_gate[l]), bf(ffn2_w_up[l]), bf(ffn2_w_down[l]), s1,
                 t1, final_w=norm_final if is_last else None, split_out=is_last, mixer=mixer)
    return x[0].reshape(b1, s1, d), x[1].reshape(b2, s2, d)
```

```python
import functools

import jax
import jax.numpy as jnp
from jax import lax
from jax.experimental import pallas as pl
from jax.experimental.pallas import tpu as pltpu

F32 = jnp.float32
BF16 = jnp.bfloat16

N_MOD = 9
EPS = 1e-6
N_ATTN_HEADS = 8
HEAD_DIM = 64
ATTN_WIDTH = N_ATTN_HEADS * HEAD_DIM
ROT_HALF = HEAD_DIM // 8
ROPE_THETA = 500000.0
ATTN_Q_SCALE = HEAD_DIM ** -0.5 * 1.4426950408889634
DILATED_PATTERNS = ((128, 1), (512, 4), (2048, 16))
assert DILATED_PATTERNS[0][1] == 1 and all(b[1] % a[1] == 0 for a, b in zip(DILATED_PATTERNS, DILATED_PATTERNS[1:]))
N_DN_HEADS = 4
DN_HEAD_DIM = 128
DN_WIDTH = N_DN_HEADS * DN_HEAD_DIM
CONV_K = 5
N_GATES = 4 * N_DN_HEADS

LANES = 128
SUBLANES = 8
VMEM_LIMIT_BYTES = 56 * 1024 * 1024

FFN_ROWS = 512
PROJ_ROWS = 512
ATTN_TILE = {1: (128, 16, 1), 4: (128, 2, 4), 16: (128, 2, 4)}
DN_CHUNK = 128
DN_CHUNKS_PER_STEP = 4
INV_BASE = 4
MOD_COLS = 1152
HALO_ROWS = SUBLANES
NEG_BIG = -1e30


def _cparams(sem):
    return pltpu.CompilerParams(dimension_semantics=sem, vmem_limit_bytes=VMEM_LIMIT_BYTES)


def _resident(shape):
    return pl.BlockSpec(shape, lambda *_: (0, 0), pipeline_mode=pl.Buffered(1))


def _silu(x):
    return x * jax.nn.sigmoid(x)


def _modulated_norm(x, norm_w, shift, scale):
    inv = lax.rsqrt(jnp.mean(x * x, axis=-1, keepdims=True) + EPS)
    return x * inv * (norm_w * (1.0 + scale)) + shift


def _mod_kernel(c_ref, w_ref, b_ref, o_ref):
    sc = _silu(c_ref[...]).astype(BF16)
    o_ref[0] = jnp.dot(sc, w_ref[0].astype(BF16), preferred_element_type=F32) + b_ref[0]


def _modulation(c_all, ada_w, ada_b):
    depth, d, n = ada_w.shape
    rows = c_all.shape[0]
    return pl.pallas_call(
        _mod_kernel,
        grid=(depth, n // MOD_COLS),
        in_specs=[
            pl.BlockSpec((rows, d), lambda l, j: (0, 0)),
            pl.BlockSpec((1, d, MOD_COLS), lambda l, j: (l, 0, j)),
            pl.BlockSpec((1, 1, MOD_COLS), lambda l, j: (l, 0, j)),
        ],
        out_specs=pl.BlockSpec((1, rows, MOD_COLS), lambda l, j: (l, 0, j)),
        out_shape=jax.ShapeDtypeStruct((depth, rows, n), F32),
        compiler_params=_cparams(("arbitrary", "arbitrary")),
        name="modulation",
    )(c_all, ada_w, ada_b.reshape(depth, 1, n))


def _mixer_residual(x, mod_ref, attn_ref, of_ref, ob_ref, z_ref, nw_ref, wa_ref, wd_ref):
    o = of_ref[...] + ob_ref[...]
    z = z_ref[...].astype(F32)
    nw = nw_ref[...]
    parts = []
    for hd in range(N_DN_HEADS):
        sl = slice(hd * DN_HEAD_DIM, (hd + 1) * DN_HEAD_DIM)
        oh = o[:, sl]
        parts.append(oh * lax.rsqrt(jnp.mean(oh * oh, axis=-1, keepdims=True) + EPS) * nw * _silu(z[:, sl]))
    dn = jnp.concatenate(parts, axis=1).astype(BF16)
    y = jnp.dot(attn_ref[...], wa_ref[...], preferred_element_type=F32)
    y = y + jnp.dot(dn, wd_ref[...], preferred_element_type=F32)
    return x + mod_ref[0, 2:3, :] * y


N_MIXER_REFS = 8


def _ffn_kernel(*refs, n_in, n_out, with_mixer, final_norm, rows1):
    x_refs, refs = refs[:n_in], refs[n_in:]
    if with_mixer:
        mixer_refs, refs = refs[:N_MIXER_REFS], refs[N_MIXER_REFS:]
    mod_ref, nw_ref, wg_ref, wu_ref, wd_ref = refs[:5]
    refs = refs[5:]
    if final_norm:
        fw_ref, refs = refs[0], refs[1:]
    o_refs, refs = refs[:n_out], refs[n_out:]
    tm = x_refs[0].shape[0]
    in_first = pl.program_id(0) < rows1 // tm
    if n_in == 2:
        stage_ref = refs[0]

        @pl.when(in_first)
        def _():
            stage_ref[...] = x_refs[0][...]

        @pl.when(jnp.logical_not(in_first))
        def _():
            stage_ref[...] = x_refs[1][...]

        x = stage_ref[...]
    else:
        x = x_refs[0][...]
    if with_mixer:
        x = _mixer_residual(x, *mixer_refs)
    h = _modulated_norm(x, nw_ref[...], mod_ref[0, 0:1, :], mod_ref[0, 1:2, :]).astype(BF16)
    g = jnp.dot(h, wg_ref[...], preferred_element_type=F32)
    u = jnp.dot(h, wu_ref[...], preferred_element_type=F32)
    a = (_silu(g) * u).astype(BF16)
    y = jnp.dot(a, wd_ref[...], preferred_element_type=F32)
    out = x + (0.5 * mod_ref[0, 2:3, :]) * y
    if final_norm:
        out = out * lax.rsqrt(jnp.mean(out * out, axis=-1, keepdims=True) + EPS) * fw_ref[...]
    if n_out == 2:
        @pl.when(in_first)
        def _():
            o_refs[0][...] = out

        @pl.when(jnp.logical_not(in_first))
        def _():
            o_refs[1][...] = out
    else:
        o_refs[0][...] = out


def _ffn(xs, mod, norm_w, wg, wu, wd, blk_rows, rows1, final_w=None, split_out=False, mixer=None):
    d = xs[0].shape[1]
    t = sum(x.shape[0] for x in xs)
    f = wg.shape[1]
    tm = FFN_ROWS
    per_blk = blk_rows // tm
    nb1 = rows1 // tm
    const = lambda i: (0, 0)
    row = lambda i: (i, 0)
    blk = lambda i: (i // per_blk, 0, 0)
    group1 = lambda i: (jnp.minimum(i, nb1 - 1), 0)
    group2 = lambda i: (jnp.maximum(i - nb1, 0), 0)
    in_specs = [pl.BlockSpec((tm, d), row)] if len(xs) == 1 else [pl.BlockSpec((tm, d), group1), pl.BlockSpec((tm, d), group2)]
    args = list(xs)
    if mixer is not None:
        m_mod, attn, o_f, o_b, z, dn_norm, w_attn, w_dn = mixer
        in_specs += [
            pl.BlockSpec((1, 3, d), blk),
            pl.BlockSpec((tm, ATTN_WIDTH), row),
            pl.BlockSpec((tm, DN_WIDTH), row),
            pl.BlockSpec((tm, DN_WIDTH), row),
            pl.BlockSpec((tm, DN_WIDTH), row),
            pl.BlockSpec((1, DN_HEAD_DIM), const),
            _resident((ATTN_WIDTH, d)),
            _resident((DN_WIDTH, d)),
        ]
        args += [m_mod, attn, o_f, o_b, z, dn_norm.reshape(1, DN_HEAD_DIM), w_attn, w_dn]
    in_specs += [
        pl.BlockSpec((1, 3, d), blk),
        pl.BlockSpec((1, d), const),
        _resident((d, f)),
        _resident((d, f)),
        _resident((f, d)),
    ]
    args += [mod, norm_w.reshape(1, d), wg, wu, wd]
    if final_w is not None:
        in_specs.append(pl.BlockSpec((1, d), const))
        args.append(final_w.reshape(1, d))
    if split_out:
        out_specs = [pl.BlockSpec((tm, d), group1), pl.BlockSpec((tm, d), group2)]
        out_shape = [jax.ShapeDtypeStruct((rows1, d), F32), jax.ShapeDtypeStruct((t - rows1, d), F32)]
    else:
        out_specs = [pl.BlockSpec((tm, d), row)]
        out_shape = [jax.ShapeDtypeStruct((t, d), F32)]
    outs = pl.pallas_call(
        functools.partial(_ffn_kernel, n_in=len(xs), n_out=len(out_specs), with_mixer=mixer is not None,
                          final_norm=final_w is not None, rows1=rows1),
        grid=(t // tm,),
        in_specs=in_specs,
        out_specs=out_specs,
        out_shape=out_shape,
        scratch_shapes=[pltpu.VMEM((tm, d), F32)] if len(xs) == 2 else [],
        compiler_params=_cparams(("arbitrary",)),
        name="ffn",
    )(*args)
    return outs if split_out else outs[0]


def _inproj_kernel(xp_ref, x_ref, xn_ref, mod_ref, nw_ref, w_ref, wgt_ref, gpt_ref,
                   cos_ref, sfw_ref, sbk_ref, cw_ref, *rest, rows1, len1, len2):
    n_views = len(DILATED_PATTERNS)
    qkv_refs = [rest[3 * n:3 * n + 3] for n in range(n_views)]
    dq_ref, dk_ref, dv_ref, z_ref, gbt_ref = rest[3 * n_views:3 * n_views + 5]
    stage_refs = rest[3 * n_views + 5:]
    stage_ref = stage_refs[0]
    tm = x_ref.shape[0]
    nw, shift, scale = nw_ref[...], mod_ref[0, 0:1, :], mod_ref[0, 1:2, :]
    h_main = _modulated_norm(x_ref[...], nw, shift, scale)
    h_ext = jnp.concatenate([_modulated_norm(xp_ref[...], nw, shift, scale), h_main,
                             _modulated_norm(xn_ref[...], nw, shift, scale)], axis=0).astype(BF16)
    h = h_main.astype(BF16)
    cos, sfw, sbk = cos_ref[...], sfw_ref[...], sbk_ref[...]

    def rope(xp):
        return xp * cos + pltpu.roll(xp, LANES - ROT_HALF, 1) * sfw + pltpu.roll(xp, ROT_HALF, 1) * sbk

    n_tiles = ATTN_WIDTH // LANES
    o0 = 3 * ATTN_WIDTH

    def attn_epilogue(a, pa):
        for j in range(n_tiles):
            col = pa[:, j * LANES:(j + 1) * LANES]
            if a == 0:
                col = rope(col) * ATTN_Q_SCALE
            elif a == 1:
                col = rope(col)
            stage_ref[a * n_tiles + j] = col
        for j in range(n_tiles):
            tile = a * n_tiles + j
            for n, (_, dil) in enumerate(DILATED_PATTERNS):
                for c in range(dil):
                    lo = c * ATTN_WIDTH + j * LANES
                    if n == 0:
                        src = stage_ref[tile]
                    else:
                        prev_dil = DILATED_PATTERNS[n - 1][1]
                        step = dil // prev_dil
                        prev = stage_refs[n - 1]
                        src = prev[tile * prev_dil + c % prev_dil, pl.ds(c // prev_dil, tm // dil, stride=step), :]
                    if 0 < n < n_views - 1:
                        stage_refs[n][tile * dil + c] = src
                    qkv_refs[n][a][:, lo:lo + LANES] = src.astype(BF16)

    r0 = pl.program_id(0) * tm
    in_first = r0 < rows1
    seq_len = jnp.where(in_first, len1, len2)
    off = jnp.where(in_first, r0, r0 - rows1)
    keep_prev = jnp.where((off % seq_len) == 0, 0.0, 1.0).astype(F32)
    keep_next = jnp.where(((off + tm) % seq_len) == 0, 0.0, 1.0).astype(F32)
    pad = CONV_K // 2
    n_ext = tm + 2 * HALO_ROWS

    def dn_epilogue(a, pe):
        out_ref = (dq_ref, dk_ref, dv_ref)[a]
        cols = slice(a * DN_WIDTH, (a + 1) * DN_WIDTH)
        pe = jnp.concatenate([pe[0:HALO_ROWS] * keep_prev, pe[HALO_ROWS:HALO_ROWS + tm],
                              pe[HALO_ROWS + tm:] * keep_next], axis=0)
        taps = [pe * cw_ref[j:j + 1, cols] for j in range(CONV_K)]
        before, after = taps[0], taps[CONV_K - 1]
        for j in range(1, pad):
            before = taps[j] + pltpu.roll(before, 1, 0)
            after = taps[CONV_K - 1 - j] + pltpu.roll(after, n_ext - 1, 0)
        y = taps[pad] + pltpu.roll(before, 1, 0) + pltpu.roll(after, n_ext - 1, 0)
        y = _silu(y[HALO_ROWS:HALO_ROWS + tm])
        for hd in range(N_DN_HEADS):
            sl = slice(hd * DN_HEAD_DIM, (hd + 1) * DN_HEAD_DIM)
            yh = y[:, sl]
            if a == 0:
                yh = yh * lax.rsqrt(jnp.sum(yh * yh, axis=-1, keepdims=True) + EPS) * (DN_HEAD_DIM ** -0.5)
            elif a == 1:
                yh = yh * lax.rsqrt(jnp.sum(yh * yh, axis=-1, keepdims=True) + EPS)
            out_ref[:, sl] = yh.astype(BF16)

    def z_epilogue(_, pz):
        z_ref[...] = pz.astype(BF16)

    groups = [(h, a * ATTN_WIDTH, ATTN_WIDTH, attn_epilogue, a) for a in range(3)]
    groups += [(h_ext, o0 + a * DN_WIDTH, DN_WIDTH, dn_epilogue, a) for a in range(3)]
    groups.append((h, o0 + 3 * DN_WIDTH, DN_WIDTH, z_epilogue, None))

    for lhs, lo, width, epilogue, arg in groups:
        epilogue(arg, jnp.dot(lhs, w_ref[:, lo:lo + width], preferred_element_type=F32))

    abt = lax.dot_general(wgt_ref[...], h, (((1,), (1,)), ((), ())), preferred_element_type=F32)
    is_gt = lax.broadcasted_iota(jnp.int32, abt.shape, 0) < 2 * N_DN_HEADS
    gpt = gpt_ref[...]
    gbt_ref[...] = jnp.where(is_gt, -jnp.exp(gpt[:, 0:1]) * jax.nn.softplus(abt + gpt[:, 1:2]), jax.nn.sigmoid(abt))


def _in_proj(x, mod, norm_w, w_main, w_gate_t, gate_par_t, rope_tabs, conv_w, blk_rows, seq_lens, t1):
    t, d = x.shape
    tm = PROJ_ROWS
    per_blk = blk_rows // tm
    n_main = w_main.shape[1]
    nb1 = t1 // tm
    s1b, s2b = seq_lens[0] // tm, seq_lens[1] // tm
    per_halo = tm // HALO_ROWS
    n_halo = t // HALO_ROWS
    const = lambda i: (0, 0)
    row = lambda i: (i, 0)
    pos = lambda i: (jnp.where(i < nb1, i % s1b, (i - nb1) % s2b), 0)
    cos, sfw, sbk = rope_tabs
    view_specs, view_shapes = [], []
    for _, dil in DILATED_PATTERNS:
        view_specs += [pl.BlockSpec((tm // dil, dil * ATTN_WIDTH), row)] * 3
        view_shapes += [jax.ShapeDtypeStruct((t // dil, dil * ATTN_WIDTH), BF16)] * 3
    outs = pl.pallas_call(
        functools.partial(_inproj_kernel, rows1=t1, len1=seq_lens[0], len2=seq_lens[1]),
        grid=(t // tm,),
        in_specs=[
            pl.BlockSpec((HALO_ROWS, d), lambda i: (jnp.maximum(i * per_halo - 1, 0), 0)),
            pl.BlockSpec((tm, d), row),
            pl.BlockSpec((HALO_ROWS, d), lambda i: (jnp.minimum((i + 1) * per_halo, n_halo - 1), 0)),
            pl.BlockSpec((1, 3, d), lambda i: (i // per_blk, 0, 0)),
            pl.BlockSpec((1, d), const),
            _resident((d, n_main)),
            pl.BlockSpec((N_GATES, d), const),
            pl.BlockSpec((N_GATES, 2), const),
            pl.BlockSpec((tm, LANES), pos),
            pl.BlockSpec((tm, LANES), pos),
            pl.BlockSpec((tm, LANES), pos),
            pl.BlockSpec((CONV_K, 3 * DN_WIDTH), const),
        ],
        out_specs=view_specs + [pl.BlockSpec((tm, DN_WIDTH), row)] * 4 + [
            pl.BlockSpec((N_GATES, tm), lambda i: (0, i)),
        ],
        out_shape=view_shapes + [jax.ShapeDtypeStruct((t, DN_WIDTH), BF16)] * 4 + [
            jax.ShapeDtypeStruct((N_GATES, t), F32),
        ],
        scratch_shapes=[pltpu.VMEM((3 * ATTN_WIDTH // LANES * dil, tm // dil, LANES), F32)
                        for _, dil in DILATED_PATTERNS[:-1]],
        compiler_params=_cparams(("arbitrary",)),
        name="in_proj",
    )(x, x, x, mod, norm_w.reshape(1, d), w_main, w_gate_t, gate_par_t, cos, sfw, sbk, conv_w)
    n_qkv = 3 * len(DILATED_PATTERNS)
    qkv_views = [outs[3 * n:3 * n + 3] for n in range(len(DILATED_PATTERNS))]
    return qkv_views, outs[n_qkv:]


def _rope_tables(max_len):
    half = ROT_HALF
    inv = ROPE_THETA ** (-jnp.arange(half, dtype=F32) / half)
    ang = jnp.arange(max_len, dtype=F32)[:, None] * inv[None, :]
    cos, sin = jnp.cos(ang), jnp.sin(ang)
    ones = jnp.ones((max_len, HEAD_DIM - 2 * half), F32)
    zeros_h = jnp.zeros((max_len, half), F32)
    zeros_r = jnp.zeros((max_len, HEAD_DIM - 2 * half), F32)
    cos_h = jnp.concatenate([cos, cos, ones], axis=1)
    sfw_h = jnp.concatenate([-sin, zeros_h, zeros_r], axis=1)
    sbk_h = jnp.concatenate([zeros_h, sin, zeros_r], axis=1)
    rep = LANES // HEAD_DIM
    return tuple(jnp.tile(a, (1, rep)) for a in (cos_h, sfw_h, sbk_h))


def _attn_kernel(q_ref, kp_ref, km_ref, kn_ref, vp_ref, vm_ref, vn_ref, *rest,
                 dil, next_dil, first, radius, tq, rows1, len1, len2):
    last = next_dil is None
    if not first:
        op_ref, wp_ref = rest[:2]
        rest = rest[2:]
    if last:
        o_ref, acc_scr, nat_scr = rest
    else:
        acc_out_ref, ml_out_ref, acc_scr, ml_scr = rest
    tb = q_ref.shape[0]
    n_sub = tb // tq
    tk = tq + 2 * radius
    n_tiles = ATTN_WIDTH // LANES
    n_res = q_ref.shape[1] // ATTN_WIDTH
    c0 = pl.program_id(1) * n_res
    r0 = pl.program_id(0) * tb
    in_first = r0 < rows1
    seq_len = jnp.where(in_first, len1, len2)
    off = jnp.where(in_first, r0, r0 - rows1)
    first_key = jnp.where((off % seq_len) == 0, radius, 0)
    end_key = jnp.where(((off + tb) % seq_len) == 0, tq + radius, tk)

    qi = lax.broadcasted_iota(jnp.int32, (tq, tk), 0)
    kj = lax.broadcasted_iota(jnp.int32, (tq, tk), 1)
    in_band = jnp.abs(kj - radius - qi) <= radius
    biases = []
    for sub in range(n_sub):
        valid = in_band
        if sub == 0:
            valid = valid & (kj >= first_key)
        if sub == n_sub - 1:
            valid = valid & (kj < end_key)
        bias = jnp.where(valid, 0.0, NEG_BIG).astype(F32)
        biases.append(jnp.concatenate([bias, bias], axis=0))

    lane = lax.broadcasted_iota(jnp.int32, (1, LANES), 1)
    low_half = lane < HEAD_DIM
    nt = (((1,), (1,)), ((), ()))

    def unstack(x):
        return jnp.where(low_half, x[:tq], x[tq:])

    if not first:
        e_row = lax.broadcasted_iota(jnp.int32, (LANES, ATTN_WIDTH), 0)
        e_col = lax.broadcasted_iota(jnp.int32, (LANES, ATTN_WIDTH), 1)
        spread = jnp.where(e_row == e_col // LANES + HEAD_DIM * ((e_col % LANES) // HEAD_DIM), 1.0, 0.0).astype(BF16)
        lse_old = {}
        for r in range(n_res):
            for sub in range(n_sub):
                wc = wp_ref[sub * tq:(sub + 1) * tq, r * LANES:(r + 1) * LANES]
                hi = wc.astype(BF16)
                lo = (wc - hi.astype(F32)).astype(BF16)
                ex = jnp.dot(jnp.concatenate([hi, lo], axis=0), spread, preferred_element_type=F32)
                lse_old[(r, sub)] = ex[:tq] + ex[tq:]

    pairs = []
    for r in range(n_res):
        for j in range(n_tiles):
            sl = slice(r * ATTN_WIDTH + j * LANES, r * ATTN_WIDTH + (j + 1) * LANES)
            k_all = jnp.concatenate([kp_ref[:, sl], km_ref[:, sl], kn_ref[:, sl]], axis=0)
            v_all = jnp.concatenate([vp_ref[:, sl], vm_ref[:, sl], vn_ref[:, sl]], axis=0)
            for sub in range(n_sub):
                rows = slice(sub * tq, (sub + 1) * tq)
                q2 = q_ref[rows, sl]
                zero = jnp.zeros_like(q2)
                qs = jnp.concatenate([jnp.where(low_half, q2, zero), jnp.where(low_half, zero, q2)], axis=0)
                kw = k_all[sub * tq:sub * tq + tk]
                pairs.append(dict(r=r, j=j, sub=sub, sl=sl, rows=rows, vw=v_all[sub * tq:sub * tq + tk],
                                  s=lax.dot_general(qs, kw, nt, preferred_element_type=F32) + biases[sub]))
    ones = jnp.ones((tk, LANES), BF16)
    for pr in pairs:
        m_rows = jnp.broadcast_to(jnp.max(pr["s"], axis=-1, keepdims=True), (2 * tq, LANES))
        pr["m"] = unstack(m_rows)
        pr["p"] = jnp.exp2(pr["s"] - jnp.concatenate([m_rows] * (tk // LANES), axis=1)).astype(BF16)
    for pr in pairs:
        pr["pv"] = jnp.dot(pr["p"], pr["vw"], preferred_element_type=F32)
        pr["l"] = jnp.dot(pr["p"], ones, preferred_element_type=F32)
    for pr in pairs:
        den = unstack(pr["l"])
        o = unstack(pr["pv"]) / den
        lse = pr["m"] + jnp.log2(den)
        if not first:
            lse_prev = lse_old[(pr["r"], pr["sub"])][:, pr["j"] * LANES:(pr["j"] + 1) * LANES]
            top = jnp.maximum(lse_prev, lse)
            w_prev, w_cur = jnp.exp2(lse_prev - top), jnp.exp2(lse - top)
            tot = w_prev + w_cur
            o = (w_prev * op_ref[pr["rows"], pr["sl"]] + w_cur * o) / tot
            lse = top + jnp.log2(tot)
        pr["lse"] = lse
        acc_scr[c0 + pr["r"], pr["j"], pr["rows"], :] = o
    if not last:
        for r in range(n_res):
            for sub in range(n_sub):
                wc = jnp.zeros((tq, LANES), F32)
                for pr in pairs:
                    if pr["r"] == r and pr["sub"] == sub:
                        keep = jnp.logical_or(lane == pr["j"], lane == HEAD_DIM + pr["j"])
                        wc = jnp.where(keep, pr["lse"], wc)
                ml_scr[c0 + r, sub * tq:(sub + 1) * tq, :] = wc

    @pl.when(c0 + n_res == dil)
    def _():
        if last:
            for cc in range(dil):
                for j in range(n_tiles):
                    nat_scr[j, pl.ds(cc, tb, stride=dil), :] = acc_scr[cc, j]
            for j in range(n_tiles):
                o_ref[:, j * LANES:(j + 1) * LANES] = nat_scr[j].astype(o_ref.dtype)
        else:
            ratio = next_dil // dil
            rows_out = tb // ratio
            for cc in range(dil):
                for m in range(ratio):
                    cb = m * dil + cc
                    rows = pl.ds(m, rows_out, stride=ratio)
                    for j in range(n_tiles):
                        lo = cb * ATTN_WIDTH + j * LANES
                        acc_out_ref[:, lo:lo + LANES] = acc_scr[cc, j, rows, :]
                    ml_out_ref[:, cb * LANES:(cb + 1) * LANES] = ml_scr[cc, rows, :]


def _attn_pattern(qkv, prev, dil, next_dil, radius, t, t1, seq_lens):
    q, k, v = qkv
    w = ATTN_WIDTH
    first, last = prev is None, next_dil is None
    rows = t // dil
    tq, n_sub, n_res = ATTN_TILE[dil]
    tb = tq * n_sub
    per_q = tb // radius
    n_halo = rows // radius
    n_tiles = w // LANES
    wr = n_res * w
    main = lambda i, c: (i, c)
    before = lambda i, c: (jnp.maximum(i * per_q - 1, 0), c)
    after = lambda i, c: (jnp.minimum((i + 1) * per_q, n_halo - 1), c)
    whole = lambda i, c: (i, 0)
    kv_specs = [pl.BlockSpec((radius, wr), before), pl.BlockSpec((tb, wr), main), pl.BlockSpec((radius, wr), after)]
    in_specs = [pl.BlockSpec((tb, wr), main)] + kv_specs + kv_specs
    args = [q, k, k, k, v, v, v]
    if not first:
        in_specs += [pl.BlockSpec((tb, wr), main), pl.BlockSpec((tb, n_res * LANES), main)]
        args += list(prev)
    scratch = [pltpu.VMEM((dil, n_tiles, tb, LANES), F32)]
    if last:
        out_specs = [pl.BlockSpec((tb * dil, w), whole)]
        out_shape = [jax.ShapeDtypeStruct((t, w), BF16)]
        scratch.append(pltpu.VMEM((n_tiles, tb * dil, LANES), F32))
    else:
        rows_out = tb * dil // next_dil
        out_specs = [pl.BlockSpec((rows_out, next_dil * w), whole), pl.BlockSpec((rows_out, next_dil * LANES), whole)]
        out_shape = [jax.ShapeDtypeStruct((t // next_dil, next_dil * w), F32),
                     jax.ShapeDtypeStruct((t // next_dil, next_dil * LANES), F32)]
        scratch.append(pltpu.VMEM((dil, tb, LANES), F32))
    return pl.pallas_call(
        functools.partial(_attn_kernel, dil=dil, next_dil=next_dil, first=first, radius=radius, tq=tq,
                          rows1=t1 // dil, len1=seq_lens[0] // dil, len2=seq_lens[1] // dil),
        grid=(rows // tb, dil // n_res),
        in_specs=in_specs,
        out_specs=out_specs,
        out_shape=out_shape,
        scratch_shapes=scratch,
        compiler_params=_cparams(("arbitrary", "arbitrary")),
        name=f"attn_d{dil}",
    )(*args)


def _attention(qkv_views, t, t1, seq_lens):
    state = None
    n = len(DILATED_PATTERNS)
    for idx, (window, dil) in enumerate(DILATED_PATTERNS):
        next_dil = DILATED_PATTERNS[idx + 1][1] if idx + 1 < n else None
        state = _attn_pattern(qkv_views[idx], state, dil, next_dil, window // (2 * dil), t, t1, seq_lens)
    return state[0]


def _bdot(a, b):
    return jnp.dot(a.astype(BF16), b.astype(BF16), preferred_element_type=F32)


def _split3(x):
    hi = x.astype(BF16)
    r1 = x - hi.astype(F32)
    mid = r1.astype(BF16)
    lo = (r1 - mid.astype(F32)).astype(BF16)
    return hi, mid, lo


def _dn_kernel(qf_ref, kf_ref, vf_ref, gtf_ref, qb_ref, kb_ref, vb_ref, gtb_ref,
               of_ref, ob_ref, sf_ref, sb_ref, *, c, n_chunks, chunks1, len1, len2):
    n_sub = qf_ref.shape[0] // c
    i = pl.program_id(0)

    def seq_pos(ci):
        in_first = ci < chunks1
        seq_len = jnp.where(in_first, len1, len2)
        off = jnp.where(in_first, ci, ci - chunks1)
        return off % seq_len, seq_len

    pos_f, _ = seq_pos(i * n_sub)
    pos_b, len_b = seq_pos(n_chunks - 1 - i * n_sub)

    @pl.when(pos_f == 0)
    def _():
        sf_ref[...] = jnp.zeros_like(sf_ref)

    @pl.when(pos_b == len_b - 1)
    def _():
        sb_ref[...] = jnp.zeros_like(sb_ref)

    ii = lax.broadcasted_iota(jnp.int32, (c, c), 0)
    jj = lax.broadcasted_iota(jnp.int32, (c, c), 1)
    lower, lower_strict = ii >= jj, ii > jj
    upper, upper_strict = ii <= jj, ii < jj
    eye = jnp.where(ii == jj, 1.0, 0.0).astype(F32)
    tri_l = jnp.where(lower, 1.0, 0.0).astype(F32)
    tri_u = jnp.where(upper, 1.0, 0.0).astype(F32)
    level_masks = [jnp.where(ii // INV_BASE == jj // INV_BASE, 1.0, 0.0).astype(F32)]
    b = INV_BASE
    while b < c:
        same_outer = ii // (2 * b) == jj // (2 * b)
        level_masks.append(jnp.where(same_outer & (ii // b != jj // b), 1.0, 0.0).astype(F32))
        b *= 2
    ones_cd = jnp.ones((c, DN_HEAD_DIM), BF16)
    nt = (((1,), (1,)), ((), ()))

    chains = []
    for forward, (q_ref, k_ref, v_ref, gt_ref, o_ref, s_ref) in (
            (True, (qf_ref, kf_ref, vf_ref, gtf_ref, of_ref, sf_ref)),
            (False, (qb_ref, kb_ref, vb_ref, gtb_ref, ob_ref, sb_ref))):
        tri_r = tri_u if forward else tri_l
        row_rhs = jnp.concatenate([tri_r.astype(BF16), ones_cd], axis=1)
        keep, strict = (lower, lower_strict) if forward else (upper, upper_strict)
        d0 = 0 if forward else N_DN_HEADS
        for order in range(n_sub):
            sub = order if forward else n_sub - 1 - order
            rows = slice(sub * c, (sub + 1) * c)
            gt = gt_ref[:, rows]
            row = sum(jnp.dot(piece, row_rhs, preferred_element_type=F32) for piece in _split3(gt))
            cs_row, tot_row = row[:, :c], row[:, c:]
            cs_col, gates_col = jnp.transpose(cs_row), jnp.transpose(gt)
            tot_col = jnp.broadcast_to(cs_col[c - 1:c] if forward else cs_col[0:1], cs_col.shape)
            for hd in range(N_DN_HEADS):
                gi = d0 + hd
                bi = 2 * N_DN_HEADS + d0 + hd
                sl = slice(hd * DN_HEAD_DIM, (hd + 1) * DN_HEAD_DIM)
                chains.append(dict(
                    order=order, rows=rows, sl=sl, hd=hd, keep=keep, strict=strict, o_ref=o_ref, s_ref=s_ref,
                    q=q_ref[rows, sl], k=k_ref[rows, sl], v=v_ref[rows, sl],
                    gc_col=cs_col[:, gi:gi + 1], gc_row=cs_row[gi:gi + 1, :], beta=gates_col[:, bi:bi + 1],
                    tot_col=tot_col[:, gi:gi + 1], tot_row=tot_row[gi:gi + 1, :]))

    for ch in chains:
        qh, kh = ch["q"], ch["k"]
        both = lax.dot_general(jnp.concatenate([qh, kh], axis=0), kh, nt, preferred_element_type=F32)
        decay = jnp.where(ch["keep"], jnp.exp(jnp.minimum(ch["gc_col"] - ch["gc_row"], 0.0)), 0.0)
        ch["a"] = jnp.where(ch["strict"], ch["beta"] * both[c:] * decay, 0.0)
        ch["qk"] = (both[:c] * decay).astype(BF16)
    for ch in chains:
        dg = ch["a"] * level_masks[0]
        ch["x"] = eye - dg
        ch["y"] = _bdot(dg, dg)
    for ch in chains:
        ch["x"] = ch["x"] + _bdot(ch["x"], ch["y"])
    def half_rows(m, b, sel):
        return m.reshape(c // (2 * b), 2, b, m.shape[1])[:, sel].reshape(c // 2, m.shape[1])

    def merge_rows(first, second, b):
        parts = [h.reshape(c // (2 * b), b, h.shape[1]) for h in (first, second)]
        return jnp.stack(parts, axis=1).reshape(c, first.shape[1])

    b = INV_BASE
    for off_mask in level_masks[1:]:
        if b < SUBLANES:
            for ch in chains:
                ch["y"] = _bdot(ch["a"] * off_mask, ch["x"])
            for ch in chains:
                ch["x"] = ch["x"] - _bdot(ch["x"], ch["y"])
        else:
            for ch in chains:
                sel = 1 if ch["keep"] is lower else 0
                y_half = _bdot(half_rows(ch["a"] * off_mask, b, sel), ch["x"])
                zero = jnp.zeros_like(y_half)
                ch["y"] = merge_rows(zero, y_half, b) if sel else merge_rows(y_half, zero, b)
            for ch in chains:
                sel = 1 if ch["keep"] is lower else 0
                changed = half_rows(ch["x"], b, sel)
                changed = changed - _bdot(changed, ch["y"])
                same = half_rows(ch["x"], b, 1 - sel)
                ch["x"] = merge_rows(same, changed, b) if sel else merge_rows(changed, same, b)
        b *= 2
    for ch in chains:
        kh = ch["k"].astype(F32)
        egc = jnp.exp(ch["gc_col"])
        rhs = jnp.concatenate([ch["v"].astype(F32) * ch["beta"], kh * (ch["beta"] * egc)], axis=1)
        ch["uw"] = _bdot(ch["x"], rhs)
        ch["lhs_q"] = ch["q"].astype(F32) * egc
        ch["k_dec_t"] = jnp.transpose(kh * jnp.exp(ch["tot_col"] - ch["gc_col"])).astype(BF16)
    states = {}
    for order in range(n_sub):
        group = [ch for ch in chains if ch["order"] == order]
        for ch in group:
            key = (id(ch["s_ref"]), ch["hd"])
            ch["state"] = states[key] if order else ch["s_ref"][ch["hd"]]
            lhs = jnp.concatenate([ch["uw"][:, DN_HEAD_DIM:], ch["lhs_q"]], axis=0)
            ch["ws"] = _bdot(lhs, ch["state"])
        for ch in group:
            v_new = (ch["uw"][:, :DN_HEAD_DIM] - ch["ws"][:c]).astype(BF16)
            ch["o_ref"][ch["rows"], ch["sl"]] = ch["ws"][c:] + jnp.dot(ch["qk"], v_new, preferred_element_type=F32)
            ch["v_new"] = v_new
        for ch in group:
            states[(id(ch["s_ref"]), ch["hd"])] = ch["state"] * jnp.exp(ch["tot_row"]) + jnp.dot(
                ch["k_dec_t"], ch["v_new"], preferred_element_type=F32)
    for ch in chains:
        if ch["order"] == n_sub - 1:
            ch["s_ref"][ch["hd"]] = states[(id(ch["s_ref"]), ch["hd"])]


def _deltanet(q, k, v, gates_t, t1, seq_lens):
    t, w = q.shape
    c = DN_CHUNK
    rows = c * DN_CHUNKS_PER_STEP
    n = t // rows
    fwd = lambda i: (i, 0)
    bwd = lambda i: (n - 1 - i, 0)
    fwd_t = lambda i: (0, i)
    bwd_t = lambda i: (0, n - 1 - i)

    def specs(row_map, col_map):
        return [pl.BlockSpec((rows, w), row_map)] * 3 + [pl.BlockSpec((N_GATES, rows), col_map)]

    state = pltpu.VMEM((N_DN_HEADS, DN_HEAD_DIM, DN_HEAD_DIM), F32)
    return pl.pallas_call(
        functools.partial(_dn_kernel, c=c, n_chunks=t // c, chunks1=t1 // c, len1=seq_lens[0] // c,
                          len2=seq_lens[1] // c),
        grid=(n,),
        in_specs=specs(fwd, fwd_t) + specs(bwd, bwd_t),
        out_specs=[pl.BlockSpec((rows, w), fwd), pl.BlockSpec((rows, w), bwd)],
        out_shape=[jax.ShapeDtypeStruct((t, w), F32)] * 2,
        scratch_shapes=[state, state],
        compiler_params=_cparams(("arbitrary",)),
        name="deltanet",
    )(q, k, v, gates_t, q, k, v, gates_t)


def kernel(x_prompt, x_sample, c_prompt, c_sample, ada_w, ada_b, norm_ffn1, ffn1_w_gate, ffn1_w_up, ffn1_w_down, norm_mix, w_in, conv_w, a_log, dt_bias, dn_norm, w_out, norm_ffn2, ffn2_w_gate, ffn2_w_up, ffn2_w_down, norm_final):
    b1, s1, d = x_prompt.shape
    b2, s2, _ = x_sample.shape
    depth = ada_w.shape[0]
    t1 = b1 * s1
    seq_lens = (s1, s2)
    assert s2 % s1 == 0 and t1 % s2 == 0, "flat layout needs nested sequence lengths"
    t = t1 + b2 * s2
    xs = [x_prompt.reshape(t1, d), x_sample.reshape(b2 * s2, d)]

    n_seq = b1 + b2
    c_all = jnp.concatenate([c_prompt, c_sample, jnp.zeros((-n_seq % SUBLANES, d), F32)], axis=0)
    mod = _modulation(c_all, ada_w, ada_b)
    blk_seq = jnp.concatenate([jnp.arange(b1), b1 + jnp.repeat(jnp.arange(b2), s2 // s1)])
    mod = mod[:, blk_seq].reshape(depth, blk_seq.shape[0], N_MOD, d)

    rope_tabs = _rope_tables(max(s1, s2))
    n_main = 3 * ATTN_WIDTH + 4 * DN_WIDTH
    gate_par_t = jnp.stack([a_log.astype(F32).reshape(depth, -1), dt_bias.astype(F32).reshape(depth, -1)], axis=2)
    gate_par_t = jnp.pad(gate_par_t, ((0, 0), (0, N_GATES - gate_par_t.shape[1]), (0, 0)))

    for l in range(depth):
        bf = lambda a: a.astype(BF16)
        x = _ffn(xs if l == 0 else [x], mod[l, :, 0:3], norm_ffn1[l], bf(ffn1_w_gate[l]), bf(ffn1_w_up[l]),
                 bf(ffn1_w_down[l]), s1, t1)
        qkv_views, (dq, dk, dv, z, gates_t) = _in_proj(
            x, mod[l, :, 3:6], norm_mix[l], bf(w_in[l][:, :n_main]), bf(w_in[l][:, n_main:]).T,
            gate_par_t[l], rope_tabs, conv_w[l], s1, seq_lens, t1)
        attn = _attention(qkv_views, t, t1, seq_lens)
        o_f, o_b = _deltanet(dq, dk, dv, gates_t, t1, seq_lens)
        mixer = (mod[l, :, 3:6], attn, o_f, o_b, z, dn_norm[l], bf(w_out[l][:ATTN_WIDTH]), bf(w_out[l][ATTN_WIDTH:]))
        is_last = l == depth - 1
        x = _ffn([x], mod[l, :, 6:9], norm_ffn2[l], bf(ffn2_w_gate[l]), bf(ffn2_w_up[l]), bf(ffn2_w_down[l]), s1,
                 t1, final_w=norm_final if is_last else None, split_out=is_last, mixer=mixer)
    return x[0].reshape(b1, s1, d), x[1].reshape(b2, s2, d)
```

```python
import functools

import jax
import jax.numpy as jnp
from jax import lax
from jax.experimental import pallas as pl
from jax.experimental.pallas import tpu as pltpu

F32 = jnp.float32
BF16 = jnp.bfloat16

N_MOD = 9
EPS = 1e-6
N_ATTN_HEADS = 8
HEAD_DIM = 64
ATTN_WIDTH = N_ATTN_HEADS * HEAD_DIM
ROT_HALF = HEAD_DIM // 8
ROPE_THETA = 500000.0
ATTN_Q_SCALE = HEAD_DIM ** -0.5 * 1.4426950408889634
DILATED_PATTERNS = ((128, 1), (512, 4), (2048, 16))
assert DILATED_PATTERNS[0][1] == 1 and all(b[1] % a[1] == 0 for a, b in zip(DILATED_PATTERNS, DILATED_PATTERNS[1:]))
N_DN_HEADS = 4
DN_HEAD_DIM = 128
DN_WIDTH = N_DN_HEADS * DN_HEAD_DIM
CONV_K = 5
N_GATES = 4 * N_DN_HEADS

LANES = 128
SUBLANES = 8
VMEM_LIMIT_BYTES = 56 * 1024 * 1024

FFN_ROWS = 512
PROJ_ROWS = 512
ATTN_TILE = {1: (128, 16, 1), 4: (64, 4, 4), 16: (64, 4, 4)}
DN_CHUNK = 128
DN_CHUNKS_PER_STEP = 4
INV_BASE = 4
MOD_COLS = 1152
HALO_ROWS = SUBLANES
NEG_BIG = -1e30


def _cparams(sem):
    return pltpu.CompilerParams(dimension_semantics=sem, vmem_limit_bytes=VMEM_LIMIT_BYTES)


def _resident(shape):
    return pl.BlockSpec(shape, lambda *_: (0, 0), pipeline_mode=pl.Buffered(1))


def _silu(x):
    return x * jax.nn.sigmoid(x)


def _modulated_norm(x, norm_w, shift, scale):
    inv = lax.rsqrt(jnp.mean(x * x, axis=-1, keepdims=True) + EPS)
    return x * inv * (norm_w * (1.0 + scale)) + shift


def _mod_kernel(c_ref, w_ref, b_ref, o_ref):
    sc = _silu(c_ref[...]).astype(BF16)
    o_ref[0] = jnp.dot(sc, w_ref[0].astype(BF16), preferred_element_type=F32) + b_ref[0]


def _modulation(c_all, ada_w, ada_b):
    depth, d, n = ada_w.shape
    rows = c_all.shape[0]
    return pl.pallas_call(
        _mod_kernel,
        grid=(depth, n // MOD_COLS),
        in_specs=[
            pl.BlockSpec((rows, d), lambda l, j: (0, 0)),
            pl.BlockSpec((1, d, MOD_COLS), lambda l, j: (l, 0, j)),
            pl.BlockSpec((1, 1, MOD_COLS), lambda l, j: (l, 0, j)),
        ],
        out_specs=pl.BlockSpec((1, rows, MOD_COLS), lambda l, j: (l, 0, j)),
        out_shape=jax.ShapeDtypeStruct((depth, rows, n), F32),
        compiler_params=_cparams(("arbitrary", "arbitrary")),
        name="modulation",
    )(c_all, ada_w, ada_b.reshape(depth, 1, n))


def _mixer_residual(x, mod_ref, attn_ref, of_ref, ob_ref, z_ref, nw_ref, wa_ref, wd_ref):
    o = of_ref[...] + ob_ref[...]
    z = z_ref[...].astype(F32)
    nw = nw_ref[...]
    parts = []
    for hd in range(N_DN_HEADS):
        sl = slice(hd * DN_HEAD_DIM, (hd + 1) * DN_HEAD_DIM)
        oh = o[:, sl]
        parts.append(oh * lax.rsqrt(jnp.mean(oh * oh, axis=-1, keepdims=True) + EPS) * nw * _silu(z[:, sl]))
    dn = jnp.concatenate(parts, axis=1).astype(BF16)
    y = jnp.dot(attn_ref[...], wa_ref[...], preferred_element_type=F32)
    y = y + jnp.dot(dn, wd_ref[...], preferred_element_type=F32)
    return x + mod_ref[0, 2:3, :] * y


N_MIXER_REFS = 8


def _ffn_kernel(*refs, n_in, n_out, with_mixer, final_norm, rows1):
    x_refs, refs = refs[:n_in], refs[n_in:]
    if with_mixer:
        mixer_refs, refs = refs[:N_MIXER_REFS], refs[N_MIXER_REFS:]
    mod_ref, nw_ref, wg_ref, wu_ref, wd_ref = refs[:5]
    refs = refs[5:]
    if final_norm:
        fw_ref, refs = refs[0], refs[1:]
    o_refs, refs = refs[:n_out], refs[n_out:]
    tm = x_refs[0].shape[0]
    in_first = pl.program_id(0) < rows1 // tm
    if n_in == 2:
        stage_ref = refs[0]

        @pl.when(in_first)
        def _():
            stage_ref[...] = x_refs[0][...]

        @pl.when(jnp.logical_not(in_first))
        def _():
            stage_ref[...] = x_refs[1][...]

        x = stage_ref[...]
    else:
        x = x_refs[0][...]
    if with_mixer:
        x = _mixer_residual(x, *mixer_refs)
    h = _modulated_norm(x, nw_ref[...], mod_ref[0, 0:1, :], mod_ref[0, 1:2, :]).astype(BF16)
    g = jnp.dot(h, wg_ref[...], preferred_element_type=F32)
    u = jnp.dot(h, wu_ref[...], preferred_element_type=F32)
    a = (_silu(g) * u).astype(BF16)
    y = jnp.dot(a, wd_ref[...], preferred_element_type=F32)
    out = x + (0.5 * mod_ref[0, 2:3, :]) * y
    if final_norm:
        out = out * lax.rsqrt(jnp.mean(out * out, axis=-1, keepdims=True) + EPS) * fw_ref[...]
    if n_out == 2:
        @pl.when(in_first)
        def _():
            o_refs[0][...] = out

        @pl.when(jnp.logical_not(in_first))
        def _():
            o_refs[1][...] = out
    else:
        o_refs[0][...] = out


def _ffn(xs, mod, norm_w, wg, wu, wd, blk_rows, rows1, final_w=None, split_out=False, mixer=None):
    d = xs[0].shape[1]
    t = sum(x.shape[0] for x in xs)
    f = wg.shape[1]
    tm = FFN_ROWS
    per_blk = blk_rows // tm
    nb1 = rows1 // tm
    const = lambda i: (0, 0)
    row = lambda i: (i, 0)
    blk = lambda i: (i // per_blk, 0, 0)
    group1 = lambda i: (jnp.minimum(i, nb1 - 1), 0)
    group2 = lambda i: (jnp.maximum(i - nb1, 0), 0)
    in_specs = [pl.BlockSpec((tm, d), row)] if len(xs) == 1 else [pl.BlockSpec((tm, d), group1), pl.BlockSpec((tm, d), group2)]
    args = list(xs)
    if mixer is not None:
        m_mod, attn, o_f, o_b, z, dn_norm, w_attn, w_dn = mixer
        in_specs += [
            pl.BlockSpec((1, 3, d), blk),
            pl.BlockSpec((tm, ATTN_WIDTH), row),
            pl.BlockSpec((tm, DN_WIDTH), row),
            pl.BlockSpec((tm, DN_WIDTH), row),
            pl.BlockSpec((tm, DN_WIDTH), row),
            pl.BlockSpec((1, DN_HEAD_DIM), const),
            _resident((ATTN_WIDTH, d)),
            _resident((DN_WIDTH, d)),
        ]
        args += [m_mod, attn, o_f, o_b, z, dn_norm.reshape(1, DN_HEAD_DIM), w_attn, w_dn]
    in_specs += [
        pl.BlockSpec((1, 3, d), blk),
        pl.BlockSpec((1, d), const),
        _resident((d, f)),
        _resident((d, f)),
        _resident((f, d)),
    ]
    args += [mod, norm_w.reshape(1, d), wg, wu, wd]
    if final_w is not None:
        in_specs.append(pl.BlockSpec((1, d), const))
        args.append(final_w.reshape(1, d))
    if split_out:
        out_specs = [pl.BlockSpec((tm, d), group1), pl.BlockSpec((tm, d), group2)]
        out_shape = [jax.ShapeDtypeStruct((rows1, d), F32), jax.ShapeDtypeStruct((t - rows1, d), F32)]
    else:
        out_specs = [pl.BlockSpec((tm, d), row)]
        out_shape = [jax.ShapeDtypeStruct((t, d), F32)]
    outs = pl.pallas_call(
        functools.partial(_ffn_kernel, n_in=len(xs), n_out=len(out_specs), with_mixer=mixer is not None,
                          final_norm=final_w is not None, rows1=rows1),
        grid=(t // tm,),
        in_specs=in_specs,
        out_specs=out_specs,
        out_shape=out_shape,
        scratch_shapes=[pltpu.VMEM((tm, d), F32)] if len(xs) == 2 else [],
        compiler_params=_cparams(("arbitrary",)),
        name="ffn",
    )(*args)
    return outs if split_out else outs[0]


def _inproj_kernel(xp_ref, x_ref, xn_ref, mod_ref, nw_ref, w_ref, wgt_ref, gpt_ref,
                   cos_ref, sfw_ref, sbk_ref, cw_ref, *rest, rows1, len1, len2):
    n_views = len(DILATED_PATTERNS)
    qkv_refs = [rest[3 * n:3 * n + 3] for n in range(n_views)]
    dq_ref, dk_ref, dv_ref, z_ref, gbt_ref = rest[3 * n_views:3 * n_views + 5]
    stage_refs = rest[3 * n_views + 5:]
    stage_ref = stage_refs[0]
    tm = x_ref.shape[0]
    nw, shift, scale = nw_ref[...], mod_ref[0, 0:1, :], mod_ref[0, 1:2, :]
    h_main = _modulated_norm(x_ref[...], nw, shift, scale)
    h_ext = jnp.concatenate([_modulated_norm(xp_ref[...], nw, shift, scale), h_main,
                             _modulated_norm(xn_ref[...], nw, shift, scale)], axis=0).astype(BF16)
    h = h_main.astype(BF16)
    cos, sfw, sbk = cos_ref[...], sfw_ref[...], sbk_ref[...]

    def rope(xp):
        return xp * cos + pltpu.roll(xp, LANES - ROT_HALF, 1) * sfw + pltpu.roll(xp, ROT_HALF, 1) * sbk

    n_tiles = ATTN_WIDTH // LANES
    o0 = 3 * ATTN_WIDTH

    def attn_epilogue(a, pa):
        for j in range(n_tiles):
            col = pa[:, j * LANES:(j + 1) * LANES]
            if a == 0:
                col = rope(col) * ATTN_Q_SCALE
            elif a == 1:
                col = rope(col)
            stage_ref[a * n_tiles + j] = col
        for j in range(n_tiles):
            tile = a * n_tiles + j
            for n, (_, dil) in enumerate(DILATED_PATTERNS):
                for c in range(dil):
                    lo = c * ATTN_WIDTH + j * LANES
                    if n == 0:
                        src = stage_ref[tile]
                    else:
                        prev_dil = DILATED_PATTERNS[n - 1][1]
                        step = dil // prev_dil
                        prev = stage_refs[n - 1]
                        src = prev[tile * prev_dil + c % prev_dil, pl.ds(c // prev_dil, tm // dil, stride=step), :]
                    if 0 < n < n_views - 1:
                        stage_refs[n][tile * dil + c] = src
                    qkv_refs[n][a][:, lo:lo + LANES] = src.astype(BF16)

    r0 = pl.program_id(0) * tm
    in_first = r0 < rows1
    seq_len = jnp.where(in_first, len1, len2)
    off = jnp.where(in_first, r0, r0 - rows1)
    keep_prev = jnp.where((off % seq_len) == 0, 0.0, 1.0).astype(F32)
    keep_next = jnp.where(((off + tm) % seq_len) == 0, 0.0, 1.0).astype(F32)
    pad = CONV_K // 2
    n_ext = tm + 2 * HALO_ROWS

    def dn_epilogue(a, pe):
        out_ref = (dq_ref, dk_ref, dv_ref)[a]
        cols = slice(a * DN_WIDTH, (a + 1) * DN_WIDTH)
        pe = jnp.concatenate([pe[0:HALO_ROWS] * keep_prev, pe[HALO_ROWS:HALO_ROWS + tm],
                              pe[HALO_ROWS + tm:] * keep_next], axis=0)
        taps = [pe * cw_ref[j:j + 1, cols] for j in range(CONV_K)]
        before, after = taps[0], taps[CONV_K - 1]
        for j in range(1, pad):
            before = taps[j] + pltpu.roll(before, 1, 0)
            after = taps[CONV_K - 1 - j] + pltpu.roll(after, n_ext - 1, 0)
        y = taps[pad] + pltpu.roll(before, 1, 0) + pltpu.roll(after, n_ext - 1, 0)
        y = _silu(y[HALO_ROWS:HALO_ROWS + tm])
        for hd in range(N_DN_HEADS):
            sl = slice(hd * DN_HEAD_DIM, (hd + 1) * DN_HEAD_DIM)
            yh = y[:, sl]
            if a == 0:
                yh = yh * lax.rsqrt(jnp.sum(yh * yh, axis=-1, keepdims=True) + EPS) * (DN_HEAD_DIM ** -0.5)
            elif a == 1:
                yh = yh * lax.rsqrt(jnp.sum(yh * yh, axis=-1, keepdims=True) + EPS)
            out_ref[:, sl] = yh.astype(BF16)

    def z_epilogue(_, pz):
        z_ref[...] = pz.astype(BF16)

    groups = [(h, a * ATTN_WIDTH, ATTN_WIDTH, attn_epilogue, a) for a in range(3)]
    groups += [(h_ext, o0 + a * DN_WIDTH, DN_WIDTH, dn_epilogue, a) for a in range(3)]
    groups.append((h, o0 + 3 * DN_WIDTH, DN_WIDTH, z_epilogue, None))

    for lhs, lo, width, epilogue, arg in groups:
        epilogue(arg, jnp.dot(lhs, w_ref[:, lo:lo + width], preferred_element_type=F32))

    abt = lax.dot_general(wgt_ref[...], h, (((1,), (1,)), ((), ())), preferred_element_type=F32)
    is_gt = lax.broadcasted_iota(jnp.int32, abt.shape, 0) < 2 * N_DN_HEADS
    gpt = gpt_ref[...]
    gbt_ref[...] = jnp.where(is_gt, -jnp.exp(gpt[:, 0:1]) * jax.nn.softplus(abt + gpt[:, 1:2]), jax.nn.sigmoid(abt))


def _in_proj(x, mod, norm_w, w_main, w_gate_t, gate_par_t, rope_tabs, conv_w, blk_rows, seq_lens, t1):
    t, d = x.shape
    tm = PROJ_ROWS
    per_blk = blk_rows // tm
    n_main = w_main.shape[1]
    nb1 = t1 // tm
    s1b, s2b = seq_lens[0] // tm, seq_lens[1] // tm
    per_halo = tm // HALO_ROWS
    n_halo = t // HALO_ROWS
    const = lambda i: (0, 0)
    row = lambda i: (i, 0)
    pos = lambda i: (jnp.where(i < nb1, i % s1b, (i - nb1) % s2b), 0)
    cos, sfw, sbk = rope_tabs
    view_specs, view_shapes = [], []
    for _, dil in DILATED_PATTERNS:
        view_specs += [pl.BlockSpec((tm // dil, dil * ATTN_WIDTH), row)] * 3
        view_shapes += [jax.ShapeDtypeStruct((t // dil, dil * ATTN_WIDTH), BF16)] * 3
    outs = pl.pallas_call(
        functools.partial(_inproj_kernel, rows1=t1, len1=seq_lens[0], len2=seq_lens[1]),
        grid=(t // tm,),
        in_specs=[
            pl.BlockSpec((HALO_ROWS, d), lambda i: (jnp.maximum(i * per_halo - 1, 0), 0)),
            pl.BlockSpec((tm, d), row),
            pl.BlockSpec((HALO_ROWS, d), lambda i: (jnp.minimum((i + 1) * per_halo, n_halo - 1), 0)),
            pl.BlockSpec((1, 3, d), lambda i: (i // per_blk, 0, 0)),
            pl.BlockSpec((1, d), const),
            _resident((d, n_main)),
            pl.BlockSpec((N_GATES, d), const),
            pl.BlockSpec((N_GATES, 2), const),
            pl.BlockSpec((tm, LANES), pos),
            pl.BlockSpec((tm, LANES), pos),
            pl.BlockSpec((tm, LANES), pos),
            pl.BlockSpec((CONV_K, 3 * DN_WIDTH), const),
        ],
        out_specs=view_specs + [pl.BlockSpec((tm, DN_WIDTH), row)] * 4 + [
            pl.BlockSpec((N_GATES, tm), lambda i: (0, i)),
        ],
        out_shape=view_shapes + [jax.ShapeDtypeStruct((t, DN_WIDTH), BF16)] * 4 + [
            jax.ShapeDtypeStruct((N_GATES, t), F32),
        ],
        scratch_shapes=[pltpu.VMEM((3 * ATTN_WIDTH // LANES * dil, tm // dil, LANES), F32)
                        for _, dil in DILATED_PATTERNS[:-1]],
        compiler_params=_cparams(("arbitrary",)),
        name="in_proj",
    )(x, x, x, mod, norm_w.reshape(1, d), w_main, w_gate_t, gate_par_t, cos, sfw, sbk, conv_w)
    n_qkv = 3 * len(DILATED_PATTERNS)
    qkv_views = [outs[3 * n:3 * n + 3] for n in range(len(DILATED_PATTERNS))]
    return qkv_views, outs[n_qkv:]


def _rope_tables(max_len):
    half = ROT_HALF
    inv = ROPE_THETA ** (-jnp.arange(half, dtype=F32) / half)
    ang = jnp.arange(max_len, dtype=F32)[:, None] * inv[None, :]
    cos, sin = jnp.cos(ang), jnp.sin(ang)
    ones = jnp.ones((max_len, HEAD_DIM - 2 * half), F32)
    zeros_h = jnp.zeros((max_len, half), F32)
    zeros_r = jnp.zeros((max_len, HEAD_DIM - 2 * half), F32)
    cos_h = jnp.concatenate([cos, cos, ones], axis=1)
    sfw_h = jnp.concatenate([-sin, zeros_h, zeros_r], axis=1)
    sbk_h = jnp.concatenate([zeros_h, sin, zeros_r], axis=1)
    rep = LANES // HEAD_DIM
    return tuple(jnp.tile(a, (1, rep)) for a in (cos_h, sfw_h, sbk_h))


def _attn_kernel(q_ref, kp_ref, km_ref, kn_ref, vp_ref, vm_ref, vn_ref, *rest,
                 dil, next_dil, first, radius, tq, rows1, len1, len2):
    last = next_dil is None
    if not first:
        op_ref, wp_ref = rest[:2]
        rest = rest[2:]
    if last:
        o_ref, acc_scr, nat_scr = rest
    else:
        acc_out_ref, ml_out_ref, acc_scr, ml_scr = rest
    tb = q_ref.shape[0]
    n_sub = tb // tq
    tk = tq + 2 * radius
    n_tiles = ATTN_WIDTH // LANES
    n_res = q_ref.shape[1] // ATTN_WIDTH
    c0 = pl.program_id(1) * n_res
    r0 = pl.program_id(0) * tb
    in_first = r0 < rows1
    seq_len = jnp.where(in_first, len1, len2)
    off = jnp.where(in_first, r0, r0 - rows1)
    first_key = jnp.where((off % seq_len) == 0, radius, 0)
    end_key = jnp.where(((off + tb) % seq_len) == 0, tq + radius, tk)

    qi = lax.broadcasted_iota(jnp.int32, (tq, tk), 0)
    kj = lax.broadcasted_iota(jnp.int32, (tq, tk), 1)
    in_band = jnp.abs(kj - radius - qi) <= radius
    biases = []
    for sub in range(n_sub):
        valid = in_band
        if sub == 0:
            valid = valid & (kj >= first_key)
        if sub == n_sub - 1:
            valid = valid & (kj < end_key)
        bias = jnp.where(valid, 0.0, NEG_BIG).astype(F32)
        biases.append(jnp.concatenate([bias, bias], axis=0))

    lane = lax.broadcasted_iota(jnp.int32, (1, LANES), 1)
    low_half = lane < HEAD_DIM
    nt = (((1,), (1,)), ((), ()))

    def unstack(x):
        return jnp.where(low_half, x[:tq], x[tq:])

    if not first:
        e_row = lax.broadcasted_iota(jnp.int32, (LANES, ATTN_WIDTH), 0)
        e_col = lax.broadcasted_iota(jnp.int32, (LANES, ATTN_WIDTH), 1)
        spread = jnp.where(e_row == e_col // LANES + HEAD_DIM * ((e_col % LANES) // HEAD_DIM), 1.0, 0.0).astype(BF16)
        lse_old = {}
        for r in range(n_res):
            for sub in range(n_sub):
                wc = wp_ref[sub * tq:(sub + 1) * tq, r * LANES:(r + 1) * LANES]
                hi = wc.astype(BF16)
                lo = (wc - hi.astype(F32)).astype(BF16)
                ex = jnp.dot(jnp.concatenate([hi, lo], axis=0), spread, preferred_element_type=F32)
                lse_old[(r, sub)] = ex[:tq] + ex[tq:]

    pairs = []
    for r in range(n_res):
        for j in range(n_tiles):
            sl = slice(r * ATTN_WIDTH + j * LANES, r * ATTN_WIDTH + (j + 1) * LANES)
            k_all = jnp.concatenate([kp_ref[:, sl], km_ref[:, sl], kn_ref[:, sl]], axis=0)
            v_all = jnp.concatenate([vp_ref[:, sl], vm_ref[:, sl], vn_ref[:, sl]], axis=0)
            for sub in range(n_sub):
                rows = slice(sub * tq, (sub + 1) * tq)
                q2 = q_ref[rows, sl]
                zero = jnp.zeros_like(q2)
                qs = jnp.concatenate([jnp.where(low_half, q2, zero), jnp.where(low_half, zero, q2)], axis=0)
                kw = k_all[sub * tq:sub * tq + tk]
                pairs.append(dict(r=r, j=j, sub=sub, sl=sl, rows=rows, vw=v_all[sub * tq:sub * tq + tk],
                                  s=lax.dot_general(qs, kw, nt, preferred_element_type=F32) + biases[sub]))
    ones = jnp.ones((tk, LANES), BF16)
    for pr in pairs:
        m_col = jnp.max(pr["s"], axis=-1, keepdims=True)
        pr["m"] = unstack(jnp.broadcast_to(m_col, (2 * tq, LANES)))
        pr["p"] = jnp.exp2(pr["s"] - m_col).astype(BF16)
    for pr in pairs:
        pr["pv"] = jnp.dot(pr["p"], pr["vw"], preferred_element_type=F32)
        pr["l"] = jnp.dot(pr["p"], ones, preferred_element_type=F32)
    for pr in pairs:
        den = unstack(pr["l"])
        o = unstack(pr["pv"]) / den
        lse = pr["m"] + jnp.log2(den)
        if not first:
            lse_prev = lse_old[(pr["r"], pr["sub"])][:, pr["j"] * LANES:(pr["j"] + 1) * LANES]
            top = jnp.maximum(lse_prev, lse)
            w_prev, w_cur = jnp.exp2(lse_prev - top), jnp.exp2(lse - top)
            tot = w_prev + w_cur
            o = (w_prev * op_ref[pr["rows"], pr["sl"]] + w_cur * o) / tot
            lse = top + jnp.log2(tot)
        pr["lse"] = lse
        acc_scr[c0 + pr["r"], pr["j"], pr["rows"], :] = o
    if not last:
        for r in range(n_res):
            for sub in range(n_sub):
                wc = jnp.zeros((tq, LANES), F32)
                for pr in pairs:
                    if pr["r"] == r and pr["sub"] == sub:
                        keep = jnp.logical_or(lane == pr["j"], lane == HEAD_DIM + pr["j"])
                        wc = jnp.where(keep, pr["lse"], wc)
                ml_scr[c0 + r, sub * tq:(sub + 1) * tq, :] = wc

    @pl.when(c0 + n_res == dil)
    def _():
        if last:
            for cc in range(dil):
                for j in range(n_tiles):
                    nat_scr[j, pl.ds(cc, tb, stride=dil), :] = acc_scr[cc, j]
            for j in range(n_tiles):
                o_ref[:, j * LANES:(j + 1) * LANES] = nat_scr[j].astype(o_ref.dtype)
        else:
            ratio = next_dil // dil
            rows_out = tb // ratio
            for cc in range(dil):
                for m in range(ratio):
                    cb = m * dil + cc
                    rows = pl.ds(m, rows_out, stride=ratio)
                    for j in range(n_tiles):
                        lo = cb * ATTN_WIDTH + j * LANES
                        acc_out_ref[:, lo:lo + LANES] = acc_scr[cc, j, rows, :]
                    ml_out_ref[:, cb * LANES:(cb + 1) * LANES] = ml_scr[cc, rows, :]


def _attn_pattern(qkv, prev, dil, next_dil, radius, t, t1, seq_lens):
    q, k, v = qkv
    w = ATTN_WIDTH
    first, last = prev is None, next_dil is None
    rows = t // dil
    tq, n_sub, n_res = ATTN_TILE[dil]
    tb = tq * n_sub
    per_q = tb // radius
    n_halo = rows // radius
    n_tiles = w // LANES
    wr = n_res * w
    main = lambda i, c: (i, c)
    before = lambda i, c: (jnp.maximum(i * per_q - 1, 0), c)
    after = lambda i, c: (jnp.minimum((i + 1) * per_q, n_halo - 1), c)
    whole = lambda i, c: (i, 0)
    kv_specs = [pl.BlockSpec((radius, wr), before), pl.BlockSpec((tb, wr), main), pl.BlockSpec((radius, wr), after)]
    in_specs = [pl.BlockSpec((tb, wr), main)] + kv_specs + kv_specs
    args = [q, k, k, k, v, v, v]
    if not first:
        in_specs += [pl.BlockSpec((tb, wr), main), pl.BlockSpec((tb, n_res * LANES), main)]
        args += list(prev)
    scratch = [pltpu.VMEM((dil, n_tiles, tb, LANES), F32)]
    if last:
        out_specs = [pl.BlockSpec((tb * dil, w), whole)]
        out_shape = [jax.ShapeDtypeStruct((t, w), BF16)]
        scratch.append(pltpu.VMEM((n_tiles, tb * dil, LANES), F32))
    else:
        rows_out = tb * dil // next_dil
        out_specs = [pl.BlockSpec((rows_out, next_dil * w), whole), pl.BlockSpec((rows_out, next_dil * LANES), whole)]
        out_shape = [jax.ShapeDtypeStruct((t // next_dil, next_dil * w), F32),
                     jax.ShapeDtypeStruct((t // next_dil, next_dil * LANES), F32)]
        scratch.append(pltpu.VMEM((dil, tb, LANES), F32))
    return pl.pallas_call(
        functools.partial(_attn_kernel, dil=dil, next_dil=next_dil, first=first, radius=radius, tq=tq,
                          rows1=t1 // dil, len1=seq_lens[0] // dil, len2=seq_lens[1] // dil),
        grid=(rows // tb, dil // n_res),
        in_specs=in_specs,
        out_specs=out_specs,
        out_shape=out_shape,
        scratch_shapes=scratch,
        compiler_params=_cparams(("arbitrary", "arbitrary")),
        name=f"attn_d{dil}",
    )(*args)


def _attention(qkv_views, t, t1, seq_lens):
    state = None
    n = len(DILATED_PATTERNS)
    for idx, (window, dil) in enumerate(DILATED_PATTERNS):
        next_dil = DILATED_PATTERNS[idx + 1][1] if idx + 1 < n else None
        state = _attn_pattern(qkv_views[idx], state, dil, next_dil, window // (2 * dil), t, t1, seq_lens)
    return state[0]


def _bdot(a, b):
    return jnp.dot(a.astype(BF16), b.astype(BF16), preferred_element_type=F32)


def _split3(x):
    hi = x.astype(BF16)
    r1 = x - hi.astype(F32)
    mid = r1.astype(BF16)
    lo = (r1 - mid.astype(F32)).astype(BF16)
    return hi, mid, lo


def _dn_kernel(qf_ref, kf_ref, vf_ref, gtf_ref, qb_ref, kb_ref, vb_ref, gtb_ref,
               of_ref, ob_ref, sf_ref, sb_ref, *, c, n_chunks, chunks1, len1, len2):
    n_sub = qf_ref.shape[0] // c
    i = pl.program_id(0)

    def seq_pos(ci):
        in_first = ci < chunks1
        seq_len = jnp.where(in_first, len1, len2)
        off = jnp.where(in_first, ci, ci - chunks1)
        return off % seq_len, seq_len

    pos_f, _ = seq_pos(i * n_sub)
    pos_b, len_b = seq_pos(n_chunks - 1 - i * n_sub)

    @pl.when(pos_f == 0)
    def _():
        sf_ref[...] = jnp.zeros_like(sf_ref)

    @pl.when(pos_b == len_b - 1)
    def _():
        sb_ref[...] = jnp.zeros_like(sb_ref)

    ii = lax.broadcasted_iota(jnp.int32, (c, c), 0)
    jj = lax.broadcasted_iota(jnp.int32, (c, c), 1)
    lower, lower_strict = ii >= jj, ii > jj
    upper, upper_strict = ii <= jj, ii < jj
    eye = jnp.where(ii == jj, 1.0, 0.0).astype(F32)
    tri_l = jnp.where(lower, 1.0, 0.0).astype(F32)
    tri_u = jnp.where(upper, 1.0, 0.0).astype(F32)
    level_masks = [jnp.where(ii // INV_BASE == jj // INV_BASE, 1.0, 0.0).astype(F32)]
    b = INV_BASE
    while b < c:
        same_outer = ii // (2 * b) == jj // (2 * b)
        level_masks.append(jnp.where(same_outer & (ii // b != jj // b), 1.0, 0.0).astype(F32))
        b *= 2
    ones_cd = jnp.ones((c, DN_HEAD_DIM), BF16)
    nt = (((1,), (1,)), ((), ()))

    chains = []
    for forward, (q_ref, k_ref, v_ref, gt_ref, o_ref, s_ref) in (
            (True, (qf_ref, kf_ref, vf_ref, gtf_ref, of_ref, sf_ref)),
            (False, (qb_ref, kb_ref, vb_ref, gtb_ref, ob_ref, sb_ref))):
        tri_r = tri_u if forward else tri_l
        row_rhs = jnp.concatenate([tri_r.astype(BF16), ones_cd], axis=1)
        keep, strict = (lower, lower_strict) if forward else (upper, upper_strict)
        d0 = 0 if forward else N_DN_HEADS
        for order in range(n_sub):
            sub = order if forward else n_sub - 1 - order
            rows = slice(sub * c, (sub + 1) * c)
            gt = gt_ref[:, rows]
            row = sum(jnp.dot(piece, row_rhs, preferred_element_type=F32) for piece in _split3(gt))
            cs_row, tot_row = row[:, :c], row[:, c:]
            cs_col, gates_col = jnp.transpose(cs_row), jnp.transpose(gt)
            tot_col = jnp.broadcast_to(cs_col[c - 1:c] if forward else cs_col[0:1], cs_col.shape)
            for hd in range(N_DN_HEADS):
                gi = d0 + hd
                bi = 2 * N_DN_HEADS + d0 + hd
                sl = slice(hd * DN_HEAD_DIM, (hd + 1) * DN_HEAD_DIM)
                chains.append(dict(
                    order=order, rows=rows, sl=sl, hd=hd, keep=keep, strict=strict, o_ref=o_ref, s_ref=s_ref,
                    q=q_ref[rows, sl], k=k_ref[rows, sl], v=v_ref[rows, sl],
                    gc_col=cs_col[:, gi:gi + 1], gc_row=cs_row[gi:gi + 1, :], beta=gates_col[:, bi:bi + 1],
                    tot_col=tot_col[:, gi:gi + 1], tot_row=tot_row[gi:gi + 1, :]))

    for ch in chains:
        qh, kh = ch["q"], ch["k"]
        both = lax.dot_general(jnp.concatenate([qh, kh], axis=0), kh, nt, preferred_element_type=F32)
        decay = jnp.where(ch["keep"], jnp.exp(jnp.minimum(ch["gc_col"] - ch["gc_row"], 0.0)), 0.0)
        ch["a"] = jnp.where(ch["strict"], ch["beta"] * both[c:] * decay, 0.0)
        ch["qk"] = (both[:c] * decay).astype(BF16)
    for ch in chains:
        dg = ch["a"] * level_masks[0]
        ch["x"] = eye - dg
        ch["y"] = _bdot(dg, dg)
    for ch in chains:
        ch["x"] = ch["x"] + _bdot(ch["x"], ch["y"])
    def half_rows(m, b, sel):
        return m.reshape(c // (2 * b), 2, b, m.shape[1])[:, sel].reshape(c // 2, m.shape[1])

    def merge_rows(first, second, b):
        parts = [h.reshape(c // (2 * b), b, h.shape[1]) for h in (first, second)]
        return jnp.stack(parts, axis=1).reshape(c, first.shape[1])

    b = INV_BASE
    for off_mask in level_masks[1:]:
        if b < SUBLANES:
            for ch in chains:
                ch["y"] = _bdot(ch["a"] * off_mask, ch["x"])
            for ch in chains:
                ch["x"] = ch["x"] - _bdot(ch["x"], ch["y"])
        else:
            for ch in chains:
                sel = 1 if ch["keep"] is lower else 0
                y_half = _bdot(half_rows(ch["a"] * off_mask, b, sel), ch["x"])
                zero = jnp.zeros_like(y_half)
                ch["y"] = merge_rows(zero, y_half, b) if sel else merge_rows(y_half, zero, b)
            for ch in chains:
                sel = 1 if ch["keep"] is lower else 0
                changed = half_rows(ch["x"], b, sel)
                changed = changed - _bdot(changed, ch["y"])
                same = half_rows(ch["x"], b, 1 - sel)
                ch["x"] = merge_rows(same, changed, b) if sel else merge_rows(changed, same, b)
        b *= 2
    for ch in chains:
        kh = ch["k"].astype(F32)
        egc = jnp.exp(ch["gc_col"])
        rhs = jnp.concatenate([ch["v"].astype(F32) * ch["beta"], kh * (ch["beta"] * egc)], axis=1)
        ch["uw"] = _bdot(ch["x"], rhs)
        ch["lhs_q"] = ch["q"].astype(F32) * egc
        ch["k_dec_t"] = jnp.transpose(kh * jnp.exp(ch["tot_col"] - ch["gc_col"])).astype(BF16)
    states = {}
    for order in range(n_sub):
        group = [ch for ch in chains if ch["order"] == order]
        for ch in group:
            key = (id(ch["s_ref"]), ch["hd"])
            ch["state"] = states[key] if order else ch["s_ref"][ch["hd"]]
            lhs = jnp.concatenate([ch["uw"][:, DN_HEAD_DIM:], ch["lhs_q"]], axis=0)
            ch["ws"] = _bdot(lhs, ch["state"])
        for ch in group:
            v_new = (ch["uw"][:, :DN_HEAD_DIM] - ch["ws"][:c]).astype(BF16)
            ch["o_ref"][ch["rows"], ch["sl"]] = ch["ws"][c:] + jnp.dot(ch["qk"], v_new, preferred_element_type=F32)
            ch["v_new"] = v_new
        for ch in group:
            states[(id(ch["s_ref"]), ch["hd"])] = ch["state"] * jnp.exp(ch["tot_row"]) + jnp.dot(
                ch["k_dec_t"], ch["v_new"], preferred_element_type=F32)
    for ch in chains:
        if ch["order"] == n_sub - 1:
            ch["s_ref"][ch["hd"]] = states[(id(ch["s_ref"]), ch["hd"])]


def _deltanet(q, k, v, gates_t, t1, seq_lens):
    t, w = q.shape
    c = DN_CHUNK
    rows = c * DN_CHUNKS_PER_STEP
    n = t // rows
    fwd = lambda i: (i, 0)
    bwd = lambda i: (n - 1 - i, 0)
    fwd_t = lambda i: (0, i)
    bwd_t = lambda i: (0, n - 1 - i)

    def specs(row_map, col_map):
        return [pl.BlockSpec((rows, w), row_map)] * 3 + [pl.BlockSpec((N_GATES, rows), col_map)]

    state = pltpu.VMEM((N_DN_HEADS, DN_HEAD_DIM, DN_HEAD_DIM), F32)
    return pl.pallas_call(
        functools.partial(_dn_kernel, c=c, n_chunks=t // c, chunks1=t1 // c, len1=seq_lens[0] // c,
                          len2=seq_lens[1] // c),
        grid=(n,),
        in_specs=specs(fwd, fwd_t) + specs(bwd, bwd_t),
        out_specs=[pl.BlockSpec((rows, w), fwd), pl.BlockSpec((rows, w), bwd)],
        out_shape=[jax.ShapeDtypeStruct((t, w), F32)] * 2,
        scratch_shapes=[state, state],
        compiler_params=_cparams(("arbitrary",)),
        name="deltanet",
    )(q, k, v, gates_t, q, k, v, gates_t)


def kernel(x_prompt, x_sample, c_prompt, c_sample, ada_w, ada_b, norm_ffn1, ffn1_w_gate, ffn1_w_up, ffn1_w_down, norm_mix, w_in, conv_w, a_log, dt_bias, dn_norm, w_out, norm_ffn2, ffn2_w_gate, ffn2_w_up, ffn2_w_down, norm_final):
    b1, s1, d = x_prompt.shape
    b2, s2, _ = x_sample.shape
    depth = ada_w.shape[0]
    t1 = b1 * s1
    seq_lens = (s1, s2)
    assert s2 % s1 == 0 and t1 % s2 == 0, "flat layout needs nested sequence lengths"
    t = t1 + b2 * s2
    xs = [x_prompt.reshape(t1, d), x_sample.reshape(b2 * s2, d)]

    n_seq = b1 + b2
    c_all = jnp.concatenate([c_prompt, c_sample, jnp.zeros((-n_seq % SUBLANES, d), F32)], axis=0)
    mod = _modulation(c_all, ada_w, ada_b)
    blk_seq = jnp.concatenate([jnp.arange(b1), b1 + jnp.repeat(jnp.arange(b2), s2 // s1)])
    mod = mod[:, blk_seq].reshape(depth, blk_seq.shape[0], N_MOD, d)

    rope_tabs = _rope_tables(max(s1, s2))
    n_main = 3 * ATTN_WIDTH + 4 * DN_WIDTH
    gate_par_t = jnp.stack([a_log.astype(F32).reshape(depth, -1), dt_bias.astype(F32).reshape(depth, -1)], axis=2)
    gate_par_t = jnp.pad(gate_par_t, ((0, 0), (0, N_GATES - gate_par_t.shape[1]), (0, 0)))

    for l in range(depth):
        bf = lambda a: a.astype(BF16)
        x = _ffn(xs if l == 0 else [x], mod[l, :, 0:3], norm_ffn1[l], bf(ffn1_w_gate[l]), bf(ffn1_w_up[l]),
                 bf(ffn1_w_down[l]), s1, t1)
        qkv_views, (dq, dk, dv, z, gates_t) = _in_proj(
            x, mod[l, :, 3:6], norm_mix[l], bf(w_in[l][:, :n_main]), bf(w_in[l][:, n_main:]).T,
            gate_par_t[l], rope_tabs, conv_w[l], s1, seq_lens, t1)
        attn = _attention(qkv_views, t, t1, seq_lens)
        o_f, o_b = _deltanet(dq, dk, dv, gates_t, t1, seq_lens)
        mixer = (mod[l, :, 3:6], attn, o_f, o_b, z, dn_norm[l], bf(w_out[l][:ATTN_WIDTH]), bf(w_out[l][ATTN_WIDTH:]))
        is_last = l == depth - 1
        x = _ffn([x], mod[l, :, 6:9], norm_ffn2[l], bf(ffn2_w_gate[l]), bf(ffn2_w_up[l]), bf(ffn2_w_down[l]), s1,
                 t1, final_w=norm_final if is_last else None, split_out=is_last, mixer=mixer)
    return x[0].reshape(b1, s1, d), x[1].reshape(b2, s2, d)
```

```python
import functools

import jax
import jax.numpy as jnp
from jax import lax
from jax.experimental import pallas as pl
from jax.experimental.pallas import tpu as pltpu

F32 = jnp.float32
BF16 = jnp.bfloat16

N_MOD = 9
EPS = 1e-6
N_ATTN_HEADS = 8
HEAD_DIM = 64
ATTN_WIDTH = N_ATTN_HEADS * HEAD_DIM
ROT_HALF = HEAD_DIM // 8
ROPE_THETA = 500000.0
ATTN_Q_SCALE = HEAD_DIM ** -0.5 * 1.4426950408889634
DILATED_PATTERNS = ((128, 1), (512, 4), (2048, 16))
assert DILATED_PATTERNS[0][1] == 1 and all(b[1] % a[1] == 0 for a, b in zip(DILATED_PATTERNS, DILATED_PATTERNS[1:]))
N_DN_HEADS = 4
DN_HEAD_DIM = 128
DN_WIDTH = N_DN_HEADS * DN_HEAD_DIM
CONV_K = 5
N_GATES = 4 * N_DN_HEADS

LANES = 128
SUBLANES = 8
VMEM_LIMIT_BYTES = 56 * 1024 * 1024

FFN_ROWS = 512
PROJ_ROWS = 512
ATTN_TILE = {1: (128, 16, 1), 4: (64, 4, 4), 16: (64, 4, 4)}
ATTN_WAVE = 32
DN_CHUNK = 128
DN_CHUNKS_PER_STEP = 4
INV_BASE = 4
MOD_COLS = 1152
HALO_ROWS = SUBLANES
NEG_BIG = -1e30


def _cparams(sem):
    return pltpu.CompilerParams(dimension_semantics=sem, vmem_limit_bytes=VMEM_LIMIT_BYTES)


def _resident(shape):
    return pl.BlockSpec(shape, lambda *_: (0, 0), pipeline_mode=pl.Buffered(1))


def _silu(x):
    return x * jax.nn.sigmoid(x)


def _modulated_norm(x, norm_w, shift, scale):
    inv = lax.rsqrt(jnp.mean(x * x, axis=-1, keepdims=True) + EPS)
    return x * inv * (norm_w * (1.0 + scale)) + shift


def _mod_kernel(c_ref, w_ref, b_ref, o_ref):
    sc = _silu(c_ref[...]).astype(BF16)
    o_ref[0] = jnp.dot(sc, w_ref[0].astype(BF16), preferred_element_type=F32) + b_ref[0]


def _modulation(c_all, ada_w, ada_b):
    depth, d, n = ada_w.shape
    rows = c_all.shape[0]
    return pl.pallas_call(
        _mod_kernel,
        grid=(depth, n // MOD_COLS),
        in_specs=[
            pl.BlockSpec((rows, d), lambda l, j: (0, 0)),
            pl.BlockSpec((1, d, MOD_COLS), lambda l, j: (l, 0, j)),
            pl.BlockSpec((1, 1, MOD_COLS), lambda l, j: (l, 0, j)),
        ],
        out_specs=pl.BlockSpec((1, rows, MOD_COLS), lambda l, j: (l, 0, j)),
        out_shape=jax.ShapeDtypeStruct((depth, rows, n), F32),
        compiler_params=_cparams(("arbitrary", "arbitrary")),
        name="modulation",
    )(c_all, ada_w, ada_b.reshape(depth, 1, n))


def _mixer_residual(x, mod_ref, attn_ref, of_ref, ob_ref, z_ref, nw_ref, wa_ref, wd_ref):
    o = of_ref[...] + ob_ref[...]
    z = z_ref[...].astype(F32)
    nw = nw_ref[...]
    parts = []
    for hd in range(N_DN_HEADS):
        sl = slice(hd * DN_HEAD_DIM, (hd + 1) * DN_HEAD_DIM)
        oh = o[:, sl]
        parts.append(oh * lax.rsqrt(jnp.mean(oh * oh, axis=-1, keepdims=True) + EPS) * nw * _silu(z[:, sl]))
    dn = jnp.concatenate(parts, axis=1).astype(BF16)
    y = jnp.dot(attn_ref[...], wa_ref[...], preferred_element_type=F32)
    y = y + jnp.dot(dn, wd_ref[...], preferred_element_type=F32)
    return x + mod_ref[0, 2:3, :] * y


N_MIXER_REFS = 8


def _ffn_kernel(*refs, n_in, n_out, with_mixer, final_norm, rows1):
    x_refs, refs = refs[:n_in], refs[n_in:]
    if with_mixer:
        mixer_refs, refs = refs[:N_MIXER_REFS], refs[N_MIXER_REFS:]
    mod_ref, nw_ref, wg_ref, wu_ref, wd_ref = refs[:5]
    refs = refs[5:]
    if final_norm:
        fw_ref, refs = refs[0], refs[1:]
    o_refs, refs = refs[:n_out], refs[n_out:]
    tm = x_refs[0].shape[0]
    in_first = pl.program_id(0) < rows1 // tm
    if n_in == 2:
        stage_ref = refs[0]

        @pl.when(in_first)
        def _():
            stage_ref[...] = x_refs[0][...]

        @pl.when(jnp.logical_not(in_first))
        def _():
            stage_ref[...] = x_refs[1][...]

        x = stage_ref[...]
    else:
        x = x_refs[0][...]
    if with_mixer:
        x = _mixer_residual(x, *mixer_refs)
    h = _modulated_norm(x, nw_ref[...], mod_ref[0, 0:1, :], mod_ref[0, 1:2, :]).astype(BF16)
    g = jnp.dot(h, wg_ref[...], preferred_element_type=F32)
    u = jnp.dot(h, wu_ref[...], preferred_element_type=F32)
    a = (_silu(g) * u).astype(BF16)
    y = jnp.dot(a, wd_ref[...], preferred_element_type=F32)
    out = x + (0.5 * mod_ref[0, 2:3, :]) * y
    if final_norm:
        out = out * lax.rsqrt(jnp.mean(out * out, axis=-1, keepdims=True) + EPS) * fw_ref[...]
    if n_out == 2:
        @pl.when(in_first)
        def _():
            o_refs[0][...] = out

        @pl.when(jnp.logical_not(in_first))
        def _():
            o_refs[1][...] = out
    else:
        o_refs[0][...] = out


def _ffn(xs, mod, norm_w, wg, wu, wd, blk_rows, rows1, final_w=None, split_out=False, mixer=None):
    d = xs[0].shape[1]
    t = sum(x.shape[0] for x in xs)
    f = wg.shape[1]
    tm = FFN_ROWS
    per_blk = blk_rows // tm
    nb1 = rows1 // tm
    const = lambda i: (0, 0)
    row = lambda i: (i, 0)
    blk = lambda i: (i // per_blk, 0, 0)
    group1 = lambda i: (jnp.minimum(i, nb1 - 1), 0)
    group2 = lambda i: (jnp.maximum(i - nb1, 0), 0)
    in_specs = [pl.BlockSpec((tm, d), row)] if len(xs) == 1 else [pl.BlockSpec((tm, d), group1), pl.BlockSpec((tm, d), group2)]
    args = list(xs)
    if mixer is not None:
        m_mod, attn, o_f, o_b, z, dn_norm, w_attn, w_dn = mixer
        in_specs += [
            pl.BlockSpec((1, 3, d), blk),
            pl.BlockSpec((tm, ATTN_WIDTH), row),
            pl.BlockSpec((tm, DN_WIDTH), row),
            pl.BlockSpec((tm, DN_WIDTH), row),
            pl.BlockSpec((tm, DN_WIDTH), row),
            pl.BlockSpec((1, DN_HEAD_DIM), const),
            _resident((ATTN_WIDTH, d)),
            _resident((DN_WIDTH, d)),
        ]
        args += [m_mod, attn, o_f, o_b, z, dn_norm.reshape(1, DN_HEAD_DIM), w_attn, w_dn]
    in_specs += [
        pl.BlockSpec((1, 3, d), blk),
        pl.BlockSpec((1, d), const),
        _resident((d, f)),
        _resident((d, f)),
        _resident((f, d)),
    ]
    args += [mod, norm_w.reshape(1, d), wg, wu, wd]
    if final_w is not None:
        in_specs.append(pl.BlockSpec((1, d), const))
        args.append(final_w.reshape(1, d))
    if split_out:
        out_specs = [pl.BlockSpec((tm, d), group1), pl.BlockSpec((tm, d), group2)]
        out_shape = [jax.ShapeDtypeStruct((rows1, d), F32), jax.ShapeDtypeStruct((t - rows1, d), F32)]
    else:
        out_specs = [pl.BlockSpec((tm, d), row)]
        out_shape = [jax.ShapeDtypeStruct((t, d), F32)]
    outs = pl.pallas_call(
        functools.partial(_ffn_kernel, n_in=len(xs), n_out=len(out_specs), with_mixer=mixer is not None,
                          final_norm=final_w is not None, rows1=rows1),
        grid=(t // tm,),
        in_specs=in_specs,
        out_specs=out_specs,
        out_shape=out_shape,
        scratch_shapes=[pltpu.VMEM((tm, d), F32)] if len(xs) == 2 else [],
        compiler_params=_cparams(("arbitrary",)),
        name="ffn",
    )(*args)
    return outs if split_out else outs[0]


def _inproj_kernel(xp_ref, x_ref, xn_ref, mod_ref, nw_ref, w_ref, wgt_ref, gpt_ref,
                   cos_ref, sfw_ref, sbk_ref, cw_ref, *rest, rows1, len1, len2):
    n_views = len(DILATED_PATTERNS)
    qkv_refs = [rest[3 * n:3 * n + 3] for n in range(n_views)]
    dq_ref, dk_ref, dv_ref, z_ref, gbt_ref = rest[3 * n_views:3 * n_views + 5]
    stage_refs = rest[3 * n_views + 5:]
    stage_ref = stage_refs[0]
    tm = x_ref.shape[0]
    nw, shift, scale = nw_ref[...], mod_ref[0, 0:1, :], mod_ref[0, 1:2, :]
    h_main = _modulated_norm(x_ref[...], nw, shift, scale)
    h_ext = jnp.concatenate([_modulated_norm(xp_ref[...], nw, shift, scale), h_main,
                             _modulated_norm(xn_ref[...], nw, shift, scale)], axis=0).astype(BF16)
    h = h_main.astype(BF16)
    cos, sfw, sbk = cos_ref[...], sfw_ref[...], sbk_ref[...]

    def rope(xp):
        return xp * cos + pltpu.roll(xp, LANES - ROT_HALF, 1) * sfw + pltpu.roll(xp, ROT_HALF, 1) * sbk

    n_tiles = ATTN_WIDTH // LANES
    o0 = 3 * ATTN_WIDTH

    def attn_epilogue(a, pa):
        for j in range(n_tiles):
            col = pa[:, j * LANES:(j + 1) * LANES]
            if a == 0:
                col = rope(col) * ATTN_Q_SCALE
            elif a == 1:
                col = rope(col)
            stage_ref[a * n_tiles + j] = col
        for j in range(n_tiles):
            tile = a * n_tiles + j
            for n, (_, dil) in enumerate(DILATED_PATTERNS):
                for c in range(dil):
                    lo = c * ATTN_WIDTH + j * LANES
                    if n == 0:
                        src = stage_ref[tile]
                    else:
                        prev_dil = DILATED_PATTERNS[n - 1][1]
                        step = dil // prev_dil
                        prev = stage_refs[n - 1]
                        src = prev[tile * prev_dil + c % prev_dil, pl.ds(c // prev_dil, tm // dil, stride=step), :]
                    if 0 < n < n_views - 1:
                        stage_refs[n][tile * dil + c] = src
                    qkv_refs[n][a][:, lo:lo + LANES] = src.astype(BF16)

    r0 = pl.program_id(0) * tm
    in_first = r0 < rows1
    seq_len = jnp.where(in_first, len1, len2)
    off = jnp.where(in_first, r0, r0 - rows1)
    keep_prev = jnp.where((off % seq_len) == 0, 0.0, 1.0).astype(F32)
    keep_next = jnp.where(((off + tm) % seq_len) == 0, 0.0, 1.0).astype(F32)
    pad = CONV_K // 2
    n_ext = tm + 2 * HALO_ROWS

    def dn_epilogue(a, pe):
        out_ref = (dq_ref, dk_ref, dv_ref)[a]
        cols = slice(a * DN_WIDTH, (a + 1) * DN_WIDTH)
        pe = jnp.concatenate([pe[0:HALO_ROWS] * keep_prev, pe[HALO_ROWS:HALO_ROWS + tm],
                              pe[HALO_ROWS + tm:] * keep_next], axis=0)
        taps = [pe * cw_ref[j:j + 1, cols] for j in range(CONV_K)]
        before, after = taps[0], taps[CONV_K - 1]
        for j in range(1, pad):
            before = taps[j] + pltpu.roll(before, 1, 0)
            after = taps[CONV_K - 1 - j] + pltpu.roll(after, n_ext - 1, 0)
        y = taps[pad] + pltpu.roll(before, 1, 0) + pltpu.roll(after, n_ext - 1, 0)
        y = _silu(y[HALO_ROWS:HALO_ROWS + tm])
        for hd in range(N_DN_HEADS):
            sl = slice(hd * DN_HEAD_DIM, (hd + 1) * DN_HEAD_DIM)
            yh = y[:, sl]
            if a == 0:
                yh = yh * lax.rsqrt(jnp.sum(yh * yh, axis=-1, keepdims=True) + EPS) * (DN_HEAD_DIM ** -0.5)
            elif a == 1:
                yh = yh * lax.rsqrt(jnp.sum(yh * yh, axis=-1, keepdims=True) + EPS)
            out_ref[:, sl] = yh.astype(BF16)

    def z_epilogue(_, pz):
        z_ref[...] = pz.astype(BF16)

    groups = [(h, a * ATTN_WIDTH, ATTN_WIDTH, attn_epilogue, a) for a in range(3)]
    groups += [(h_ext, o0 + a * DN_WIDTH, DN_WIDTH, dn_epilogue, a) for a in range(3)]
    groups.append((h, o0 + 3 * DN_WIDTH, DN_WIDTH, z_epilogue, None))

    for lhs, lo, width, epilogue, arg in groups:
        epilogue(arg, jnp.dot(lhs, w_ref[:, lo:lo + width], preferred_element_type=F32))

    abt = lax.dot_general(wgt_ref[...], h, (((1,), (1,)), ((), ())), preferred_element_type=F32)
    is_gt = lax.broadcasted_iota(jnp.int32, abt.shape, 0) < 2 * N_DN_HEADS
    gpt = gpt_ref[...]
    gbt_ref[...] = jnp.where(is_gt, -jnp.exp(gpt[:, 0:1]) * jax.nn.softplus(abt + gpt[:, 1:2]), jax.nn.sigmoid(abt))


def _in_proj(x, mod, norm_w, w_main, w_gate_t, gate_par_t, rope_tabs, conv_w, blk_rows, seq_lens, t1):
    t, d = x.shape
    tm = PROJ_ROWS
    per_blk = blk_rows // tm
    n_main = w_main.shape[1]
    nb1 = t1 // tm
    s1b, s2b = seq_lens[0] // tm, seq_lens[1] // tm
    per_halo = tm // HALO_ROWS
    n_halo = t // HALO_ROWS
    const = lambda i: (0, 0)
    row = lambda i: (i, 0)
    pos = lambda i: (jnp.where(i < nb1, i % s1b, (i - nb1) % s2b), 0)
    cos, sfw, sbk = rope_tabs
    view_specs, view_shapes = [], []
    for _, dil in DILATED_PATTERNS:
        view_specs += [pl.BlockSpec((tm // dil, dil * ATTN_WIDTH), row)] * 3
        view_shapes += [jax.ShapeDtypeStruct((t // dil, dil * ATTN_WIDTH), BF16)] * 3
    outs = pl.pallas_call(
        functools.partial(_inproj_kernel, rows1=t1, len1=seq_lens[0], len2=seq_lens[1]),
        grid=(t // tm,),
        in_specs=[
            pl.BlockSpec((HALO_ROWS, d), lambda i: (jnp.maximum(i * per_halo - 1, 0), 0)),
            pl.BlockSpec((tm, d), row),
            pl.BlockSpec((HALO_ROWS, d), lambda i: (jnp.minimum((i + 1) * per_halo, n_halo - 1), 0)),
            pl.BlockSpec((1, 3, d), lambda i: (i // per_blk, 0, 0)),
            pl.BlockSpec((1, d), const),
            _resident((d, n_main)),
            pl.BlockSpec((N_GATES, d), const),
            pl.BlockSpec((N_GATES, 2), const),
            pl.BlockSpec((tm, LANES), pos),
            pl.BlockSpec((tm, LANES), pos),
            pl.BlockSpec((tm, LANES), pos),
            pl.BlockSpec((CONV_K, 3 * DN_WIDTH), const),
        ],
        out_specs=view_specs + [pl.BlockSpec((tm, DN_WIDTH), row)] * 4 + [
            pl.BlockSpec((N_GATES, tm), lambda i: (0, i)),
        ],
        out_shape=view_shapes + [jax.ShapeDtypeStruct((t, DN_WIDTH), BF16)] * 4 + [
            jax.ShapeDtypeStruct((N_GATES, t), F32),
        ],
        scratch_shapes=[pltpu.VMEM((3 * ATTN_WIDTH // LANES * dil, tm // dil, LANES), F32)
                        for _, dil in DILATED_PATTERNS[:-1]],
        compiler_params=_cparams(("arbitrary",)),
        name="in_proj",
    )(x, x, x, mod, norm_w.reshape(1, d), w_main, w_gate_t, gate_par_t, cos, sfw, sbk, conv_w)
    n_qkv = 3 * len(DILATED_PATTERNS)
    qkv_views = [outs[3 * n:3 * n + 3] for n in range(len(DILATED_PATTERNS))]
    return qkv_views, outs[n_qkv:]


def _rope_tables(max_len):
    half = ROT_HALF
    inv = ROPE_THETA ** (-jnp.arange(half, dtype=F32) / half)
    ang = jnp.arange(max_len, dtype=F32)[:, None] * inv[None, :]
    cos, sin = jnp.cos(ang), jnp.sin(ang)
    ones = jnp.ones((max_len, HEAD_DIM - 2 * half), F32)
    zeros_h = jnp.zeros((max_len, half), F32)
    zeros_r = jnp.zeros((max_len, HEAD_DIM - 2 * half), F32)
    cos_h = jnp.concatenate([cos, cos, ones], axis=1)
    sfw_h = jnp.concatenate([-sin, zeros_h, zeros_r], axis=1)
    sbk_h = jnp.concatenate([zeros_h, sin, zeros_r], axis=1)
    rep = LANES // HEAD_DIM
    return tuple(jnp.tile(a, (1, rep)) for a in (cos_h, sfw_h, sbk_h))


def _attn_kernel(q_ref, kp_ref, km_ref, kn_ref, vp_ref, vm_ref, vn_ref, *rest,
                 dil, next_dil, first, radius, tq, rows1, len1, len2):
    last = next_dil is None
    if not first:
        op_ref, wp_ref = rest[:2]
        rest = rest[2:]
    if last:
        o_ref, acc_scr, nat_scr = rest
    else:
        acc_out_ref, ml_out_ref, acc_scr, ml_scr = rest
    tb = q_ref.shape[0]
    n_sub = tb // tq
    tk = tq + 2 * radius
    n_tiles = ATTN_WIDTH // LANES
    n_res = q_ref.shape[1] // ATTN_WIDTH
    c0 = pl.program_id(1) * n_res
    r0 = pl.program_id(0) * tb
    in_first = r0 < rows1
    seq_len = jnp.where(in_first, len1, len2)
    off = jnp.where(in_first, r0, r0 - rows1)
    first_key = jnp.where((off % seq_len) == 0, radius, 0)
    end_key = jnp.where(((off + tb) % seq_len) == 0, tq + radius, tk)

    qi = lax.broadcasted_iota(jnp.int32, (tq, tk), 0)
    kj = lax.broadcasted_iota(jnp.int32, (tq, tk), 1)
    in_band = jnp.abs(kj - radius - qi) <= radius
    biases = []
    for sub in range(n_sub):
        valid = in_band
        if sub == 0:
            valid = valid & (kj >= first_key)
        if sub == n_sub - 1:
            valid = valid & (kj < end_key)
        bias = jnp.where(valid, 0.0, NEG_BIG).astype(F32)
        biases.append(jnp.concatenate([bias, bias], axis=0))

    lane = lax.broadcasted_iota(jnp.int32, (1, LANES), 1)
    low_half = lane < HEAD_DIM
    nt = (((1,), (1,)), ((), ()))

    def unstack(x):
        return jnp.where(low_half, x[:tq], x[tq:])

    if not first:
        e_row = lax.broadcasted_iota(jnp.int32, (LANES, ATTN_WIDTH), 0)
        e_col = lax.broadcasted_iota(jnp.int32, (LANES, ATTN_WIDTH), 1)
        spread = jnp.where(e_row == e_col // LANES + HEAD_DIM * ((e_col % LANES) // HEAD_DIM), 1.0, 0.0).astype(BF16)
        lse_old = {}
        for r in range(n_res):
            for sub in range(n_sub):
                wc = wp_ref[sub * tq:(sub + 1) * tq, r * LANES:(r + 1) * LANES]
                hi = wc.astype(BF16)
                lo = (wc - hi.astype(F32)).astype(BF16)
                ex = jnp.dot(jnp.concatenate([hi, lo], axis=0), spread, preferred_element_type=F32)
                lse_old[(r, sub)] = ex[:tq] + ex[tq:]

    pairs = [dict(r=r, j=j, sub=sub, rows=slice(sub * tq, (sub + 1) * tq),
                  sl=slice(r * ATTN_WIDTH + j * LANES, r * ATTN_WIDTH + (j + 1) * LANES))
             for r in range(n_res) for j in range(n_tiles) for sub in range(n_sub)]
    ones = jnp.ones((tk, LANES), BF16)
    for start in range(0, len(pairs), ATTN_WAVE):
        wave = pairs[start:start + ATTN_WAVE]
        for pr in wave:
            sl, sub = pr["sl"], pr["sub"]
            q2 = q_ref[pr["rows"], sl]
            zero = jnp.zeros_like(q2)
            qs = jnp.concatenate([jnp.where(low_half, q2, zero), jnp.where(low_half, zero, q2)], axis=0)
            k_all = jnp.concatenate([kp_ref[:, sl], km_ref[:, sl], kn_ref[:, sl]], axis=0)
            v_all = jnp.concatenate([vp_ref[:, sl], vm_ref[:, sl], vn_ref[:, sl]], axis=0)
            pr["vw"] = v_all[sub * tq:sub * tq + tk]
            pr["s"] = lax.dot_general(qs, k_all[sub * tq:sub * tq + tk], nt, preferred_element_type=F32) + biases[sub]
        for pr in wave:
            m_col = jnp.max(pr["s"], axis=-1, keepdims=True)
            pr["m"] = unstack(jnp.broadcast_to(m_col, (2 * tq, LANES)))
            pr["p"] = jnp.exp2(pr["s"] - m_col).astype(BF16)
        for pr in wave:
            pr["pv"] = jnp.dot(pr["p"], pr["vw"], preferred_element_type=F32)
            pr["l"] = jnp.dot(pr["p"], ones, preferred_element_type=F32)
        for pr in wave:
            den = unstack(pr["l"])
            o = unstack(pr["pv"]) / den
            lse = pr["m"] + jnp.log2(den)
            if not first:
                lse_prev = lse_old[(pr["r"], pr["sub"])][:, pr["j"] * LANES:(pr["j"] + 1) * LANES]
                top = jnp.maximum(lse_prev, lse)
                w_prev, w_cur = jnp.exp2(lse_prev - top), jnp.exp2(lse - top)
                tot = w_prev + w_cur
                o = (w_prev * op_ref[pr["rows"], pr["sl"]] + w_cur * o) / tot
                lse = top + jnp.log2(tot)
            pr["lse"] = lse
            acc_scr[c0 + pr["r"], pr["j"], pr["rows"], :] = o
            for key in ("s", "p", "pv", "l", "m", "vw"):
                del pr[key]
    if not last:
        for r in range(n_res):
            for sub in range(n_sub):
                wc = jnp.zeros((tq, LANES), F32)
                for pr in pairs:
                    if pr["r"] == r and pr["sub"] == sub:
                        keep = jnp.logical_or(lane == pr["j"], lane == HEAD_DIM + pr["j"])
                        wc = jnp.where(keep, pr["lse"], wc)
                ml_scr[c0 + r, sub * tq:(sub + 1) * tq, :] = wc

    @pl.when(c0 + n_res == dil)
    def _():
        if last:
            for cc in range(dil):
                for j in range(n_tiles):
                    nat_scr[j, pl.ds(cc, tb, stride=dil), :] = acc_scr[cc, j]
            for j in range(n_tiles):
                o_ref[:, j * LANES:(j + 1) * LANES] = nat_scr[j].astype(o_ref.dtype)
        else:
            ratio = next_dil // dil
            rows_out = tb // ratio
            for cc in range(dil):
                for m in range(ratio):
                    cb = m * dil + cc
                    rows = pl.ds(m, rows_out, stride=ratio)
                    for j in range(n_tiles):
                        lo = cb * ATTN_WIDTH + j * LANES
                        acc_out_ref[:, lo:lo + LANES] = acc_scr[cc, j, rows, :]
                    ml_out_ref[:, cb * LANES:(cb + 1) * LANES] = ml_scr[cc, rows, :]


def _attn_pattern(qkv, prev, dil, next_dil, radius, t, t1, seq_lens):
    q, k, v = qkv
    w = ATTN_WIDTH
    first, last = prev is None, next_dil is None
    rows = t // dil
    tq, n_sub, n_res = ATTN_TILE[dil]
    tb = tq * n_sub
    per_q = tb // radius
    n_halo = rows // radius
    n_tiles = w // LANES
    wr = n_res * w
    main = lambda i, c: (i, c)
    before = lambda i, c: (jnp.maximum(i * per_q - 1, 0), c)
    after = lambda i, c: (jnp.minimum((i + 1) * per_q, n_halo - 1), c)
    whole = lambda i, c: (i, 0)
    kv_specs = [pl.BlockSpec((radius, wr), before), pl.BlockSpec((tb, wr), main), pl.BlockSpec((radius, wr), after)]
    in_specs = [pl.BlockSpec((tb, wr), main)] + kv_specs + kv_specs
    args = [q, k, k, k, v, v, v]
    if not first:
        in_specs += [pl.BlockSpec((tb, wr), main), pl.BlockSpec((tb, n_res * LANES), main)]
        args += list(prev)
    scratch = [pltpu.VMEM((dil, n_tiles, tb, LANES), F32)]
    if last:
        out_specs = [pl.BlockSpec((tb * dil, w), whole)]
        out_shape = [jax.ShapeDtypeStruct((t, w), BF16)]
        scratch.append(pltpu.VMEM((n_tiles, tb * dil, LANES), F32))
    else:
        rows_out = tb * dil // next_dil
        out_specs = [pl.BlockSpec((rows_out, next_dil * w), whole), pl.BlockSpec((rows_out, next_dil * LANES), whole)]
        out_shape = [jax.ShapeDtypeStruct((t // next_dil, next_dil * w), F32),
                     jax.ShapeDtypeStruct((t // next_dil, next_dil * LANES), F32)]
        scratch.append(pltpu.VMEM((dil, tb, LANES), F32))
    return pl.pallas_call(
        functools.partial(_attn_kernel, dil=dil, next_dil=next_dil, first=first, radius=radius, tq=tq,
                          rows1=t1 // dil, len1=seq_lens[0] // dil, len2=seq_lens[1] // dil),
        grid=(rows // tb, dil // n_res),
        in_specs=in_specs,
        out_specs=out_specs,
        out_shape=out_shape,
        scratch_shapes=scratch,
        compiler_params=_cparams(("arbitrary", "arbitrary")),
        name=f"attn_d{dil}",
    )(*args)


def _attention(qkv_views, t, t1, seq_lens):
    state = None
    n = len(DILATED_PATTERNS)
    for idx, (window, dil) in enumerate(DILATED_PATTERNS):
        next_dil = DILATED_PATTERNS[idx + 1][1] if idx + 1 < n else None
        state = _attn_pattern(qkv_views[idx], state, dil, next_dil, window // (2 * dil), t, t1, seq_lens)
    return state[0]


def _bdot(a, b):
    return jnp.dot(a.astype(BF16), b.astype(BF16), preferred_element_type=F32)


def _split3(x):
    hi = x.astype(BF16)
    r1 = x - hi.astype(F32)
    mid = r1.astype(BF16)
    lo = (r1 - mid.astype(F32)).astype(BF16)
    return hi, mid, lo


def _dn_kernel(qf_ref, kf_ref, vf_ref, gtf_ref, qb_ref, kb_ref, vb_ref, gtb_ref,
               of_ref, ob_ref, sf_ref, sb_ref, *, c, n_chunks, chunks1, len1, len2):
    n_sub = qf_ref.shape[0] // c
    i = pl.program_id(0)

    def seq_pos(ci):
        in_first = ci < chunks1
        seq_len = jnp.where(in_first, len1, len2)
        off = jnp.where(in_first, ci, ci - chunks1)
        return off % seq_len, seq_len

    pos_f, _ = seq_pos(i * n_sub)
    pos_b, len_b = seq_pos(n_chunks - 1 - i * n_sub)

    @pl.when(pos_f == 0)
    def _():
        sf_ref[...] = jnp.zeros_like(sf_ref)

    @pl.when(pos_b == len_b - 1)
    def _():
        sb_ref[...] = jnp.zeros_like(sb_ref)

    ii = lax.broadcasted_iota(jnp.int32, (c, c), 0)
    jj = lax.broadcasted_iota(jnp.int32, (c, c), 1)
    lower, lower_strict = ii >= jj, ii > jj
    upper, upper_strict = ii <= jj, ii < jj
    eye = jnp.where(ii == jj, 1.0, 0.0).astype(F32)
    tri_l = jnp.where(lower, 1.0, 0.0).astype(F32)
    tri_u = jnp.where(upper, 1.0, 0.0).astype(F32)
    level_masks = [jnp.where(ii // INV_BASE == jj // INV_BASE, 1.0, 0.0).astype(F32)]
    b = INV_BASE
    while b < c:
        same_outer = ii // (2 * b) == jj // (2 * b)
        level_masks.append(jnp.where(same_outer & (ii // b != jj // b), 1.0, 0.0).astype(F32))
        b *= 2
    ones_cd = jnp.ones((c, DN_HEAD_DIM), BF16)
    nt = (((1,), (1,)), ((), ()))

    chains = []
    for forward, (q_ref, k_ref, v_ref, gt_ref, o_ref, s_ref) in (
            (True, (qf_ref, kf_ref, vf_ref, gtf_ref, of_ref, sf_ref)),
            (False, (qb_ref, kb_ref, vb_ref, gtb_ref, ob_ref, sb_ref))):
        tri_r = tri_u if forward else tri_l
        row_rhs = jnp.concatenate([tri_r.astype(BF16), ones_cd], axis=1)
        keep, strict = (lower, lower_strict) if forward else (upper, upper_strict)
        d0 = 0 if forward else N_DN_HEADS
        for order in range(n_sub):
            sub = order if forward else n_sub - 1 - order
            rows = slice(sub * c, (sub + 1) * c)
            gt = gt_ref[:, rows]
            row = sum(jnp.dot(piece, row_rhs, preferred_element_type=F32) for piece in _split3(gt))
            cs_row, tot_row = row[:, :c], row[:, c:]
            cs_col, gates_col = jnp.transpose(cs_row), jnp.transpose(gt)
            tot_col = jnp.broadcast_to(cs_col[c - 1:c] if forward else cs_col[0:1], cs_col.shape)
            for hd in range(N_DN_HEADS):
                gi = d0 + hd
                bi = 2 * N_DN_HEADS + d0 + hd
                sl = slice(hd * DN_HEAD_DIM, (hd + 1) * DN_HEAD_DIM)
                chains.append(dict(
                    order=order, rows=rows, sl=sl, hd=hd, keep=keep, strict=strict, o_ref=o_ref, s_ref=s_ref,
                    q=q_ref[rows, sl], k=k_ref[rows, sl], v=v_ref[rows, sl],
                    gc_col=cs_col[:, gi:gi + 1], gc_row=cs_row[gi:gi + 1, :], beta=gates_col[:, bi:bi + 1],
                    tot_col=tot_col[:, gi:gi + 1], tot_row=tot_row[gi:gi + 1, :]))

    for ch in chains:
        qh, kh = ch["q"], ch["k"]
        both = lax.dot_general(jnp.concatenate([qh, kh], axis=0), kh, nt, preferred_element_type=F32)
        decay = jnp.where(ch["keep"], jnp.exp(jnp.minimum(ch["gc_col"] - ch["gc_row"], 0.0)), 0.0)
        ch["a"] = jnp.where(ch["strict"], ch["beta"] * both[c:] * decay, 0.0)
        ch["qk"] = (both[:c] * decay).astype(BF16)
    for ch in chains:
        dg = ch["a"] * level_masks[0]
        ch["x"] = eye - dg
        ch["y"] = _bdot(dg, dg)
    for ch in chains:
        ch["x"] = ch["x"] + _bdot(ch["x"], ch["y"])
    def half_rows(m, b, sel):
        return m.reshape(c // (2 * b), 2, b, m.shape[1])[:, sel].reshape(c // 2, m.shape[1])

    def merge_rows(first, second, b):
        parts = [h.reshape(c // (2 * b), b, h.shape[1]) for h in (first, second)]
        return jnp.stack(parts, axis=1).reshape(c, first.shape[1])

    b = INV_BASE
    for off_mask in level_masks[1:]:
        if b < SUBLANES:
            for ch in chains:
                ch["y"] = _bdot(ch["a"] * off_mask, ch["x"])
            for ch in chains:
                ch["x"] = ch["x"] - _bdot(ch["x"], ch["y"])
        else:
            for ch in chains:
                sel = 1 if ch["keep"] is lower else 0
                y_half = _bdot(half_rows(ch["a"] * off_mask, b, sel), ch["x"])
                zero = jnp.zeros_like(y_half)
                ch["y"] = merge_rows(zero, y_half, b) if sel else merge_rows(y_half, zero, b)
            for ch in chains:
                sel = 1 if ch["keep"] is lower else 0
                changed = half_rows(ch["x"], b, sel)
                changed = changed - _bdot(changed, ch["y"])
                same = half_rows(ch["x"], b, 1 - sel)
                ch["x"] = merge_rows(same, changed, b) if sel else merge_rows(changed, same, b)
        b *= 2
    for ch in chains:
        kh = ch["k"].astype(F32)
        egc = jnp.exp(ch["gc_col"])
        rhs = jnp.concatenate([ch["v"].astype(F32) * ch["beta"], kh * (ch["beta"] * egc)], axis=1)
        ch["uw"] = _bdot(ch["x"], rhs)
        ch["lhs_q"] = ch["q"].astype(F32) * egc
        ch["k_dec_t"] = jnp.transpose(kh * jnp.exp(ch["tot_col"] - ch["gc_col"])).astype(BF16)
    states = {}
    for order in range(n_sub):
        group = [ch for ch in chains if ch["order"] == order]
        for ch in group:
            key = (id(ch["s_ref"]), ch["hd"])
            ch["state"] = states[key] if order else ch["s_ref"][ch["hd"]]
            lhs = jnp.concatenate([ch["uw"][:, DN_HEAD_DIM:], ch["lhs_q"]], axis=0)
            ch["ws"] = _bdot(lhs, ch["state"])
        for ch in group:
            v_new = (ch["uw"][:, :DN_HEAD_DIM] - ch["ws"][:c]).astype(BF16)
            ch["o_ref"][ch["rows"], ch["sl"]] = ch["ws"][c:] + jnp.dot(ch["qk"], v_new, preferred_element_type=F32)
            ch["v_new"] = v_new
        for ch in group:
            states[(id(ch["s_ref"]), ch["hd"])] = ch["state"] * jnp.exp(ch["tot_row"]) + jnp.dot(
                ch["k_dec_t"], ch["v_new"], preferred_element_type=F32)
    for ch in chains:
        if ch["order"] == n_sub - 1:
            ch["s_ref"][ch["hd"]] = states[(id(ch["s_ref"]), ch["hd"])]


def _deltanet(q, k, v, gates_t, t1, seq_lens):
    t, w = q.shape
    c = DN_CHUNK
    rows = c * DN_CHUNKS_PER_STEP
    n = t // rows
    fwd = lambda i: (i, 0)
    bwd = lambda i: (n - 1 - i, 0)
    fwd_t = lambda i: (0, i)
    bwd_t = lambda i: (0, n - 1 - i)

    def specs(row_map, col_map):
        return [pl.BlockSpec((rows, w), row_map)] * 3 + [pl.BlockSpec((N_GATES, rows), col_map)]

    state = pltpu.VMEM((N_DN_HEADS, DN_HEAD_DIM, DN_HEAD_DIM), F32)
    return pl.pallas_call(
        functools.partial(_dn_kernel, c=c, n_chunks=t // c, chunks1=t1 // c, len1=seq_lens[0] // c,
                          len2=seq_lens[1] // c),
        grid=(n,),
        in_specs=specs(fwd, fwd_t) + specs(bwd, bwd_t),
        out_specs=[pl.BlockSpec((rows, w), fwd), pl.BlockSpec((rows, w), bwd)],
        out_shape=[jax.ShapeDtypeStruct((t, w), F32)] * 2,
        scratch_shapes=[state, state],
        compiler_params=_cparams(("arbitrary",)),
        name="deltanet",
    )(q, k, v, gates_t, q, k, v, gates_t)


def kernel(x_prompt, x_sample, c_prompt, c_sample, ada_w, ada_b, norm_ffn1, ffn1_w_gate, ffn1_w_up, ffn1_w_down, norm_mix, w_in, conv_w, a_log, dt_bias, dn_norm, w_out, norm_ffn2, ffn2_w_gate, ffn2_w_up, ffn2_w_down, norm_final):
    b1, s1, d = x_prompt.shape
    b2, s2, _ = x_sample.shape
    depth = ada_w.shape[0]
    t1 = b1 * s1
    seq_lens = (s1, s2)
    assert s2 % s1 == 0 and t1 % s2 == 0, "flat layout needs nested sequence lengths"
    t = t1 + b2 * s2
    xs = [x_prompt.reshape(t1, d), x_sample.reshape(b2 * s2, d)]

    n_seq = b1 + b2
    c_all = jnp.concatenate([c_prompt, c_sample, jnp.zeros((-n_seq % SUBLANES, d), F32)], axis=0)
    mod = _modulation(c_all, ada_w, ada_b)
    blk_seq = jnp.concatenate([jnp.arange(b1), b1 + jnp.repeat(jnp.arange(b2), s2 // s1)])
    mod = mod[:, blk_seq].reshape(depth, blk_seq.shape[0], N_MOD, d)

    rope_tabs = _rope_tables(max(s1, s2))
    n_main = 3 * ATTN_WIDTH + 4 * DN_WIDTH
    gate_par_t = jnp.stack([a_log.astype(F32).reshape(depth, -1), dt_bias.astype(F32).reshape(depth, -1)], axis=2)
    gate_par_t = jnp.pad(gate_par_t, ((0, 0), (0, N_GATES - gate_par_t.shape[1]), (0, 0)))

    for l in range(depth):
        bf = lambda a: a.astype(BF16)
        x = _ffn(xs if l == 0 else [x], mod[l, :, 0:3], norm_ffn1[l], bf(ffn1_w_gate[l]), bf(ffn1_w_up[l]),
                 bf(ffn1_w_down[l]), s1, t1)
        qkv_views, (dq, dk, dv, z, gates_t) = _in_proj(
            x, mod[l, :, 3:6], norm_mix[l], bf(w_in[l][:, :n_main]), bf(w_in[l][:, n_main:]).T,
            gate_par_t[l], rope_tabs, conv_w[l], s1, seq_lens, t1)
        attn = _attention(qkv_views, t, t1, seq_lens)
        o_f, o_b = _deltanet(dq, dk, dv, gates_t, t1, seq_lens)
        mixer = (mod[l, :, 3:6], attn, o_f, o_b, z, dn_norm[l], bf(w_out[l][:ATTN_WIDTH]), bf(w_out[l][ATTN_WIDTH:]))
        is_last = l == depth - 1
        x = _ffn([x], mod[l, :, 6:9], norm_ffn2[l], bf(ffn2_w_gate[l]), bf(ffn2_w_up[l]), bf(ffn2_w_down[l]), s1,
                 t1, final_w=norm_final if is_last else None, split_out=is_last, mixer=mixer)
    return x[0].reshape(b1, s1, d), x[1].reshape(b2, s2, d)
```

```python
import functools

import jax
import jax.numpy as jnp
from jax import lax
from jax.experimental import pallas as pl
from jax.experimental.pallas import tpu as pltpu

F32 = jnp.float32
BF16 = jnp.bfloat16

N_MOD = 9
EPS = 1e-6
N_ATTN_HEADS = 8
HEAD_DIM = 64
ATTN_WIDTH = N_ATTN_HEADS * HEAD_DIM
ROT_HALF = HEAD_DIM // 8
ROPE_THETA = 500000.0
ATTN_Q_SCALE = HEAD_DIM ** -0.5 * 1.4426950408889634
DILATED_PATTERNS = ((128, 1), (512, 4), (2048, 16))
assert DILATED_PATTERNS[0][1] == 1 and all(b[1] % a[1] == 0 for a, b in zip(DILATED_PATTERNS, DILATED_PATTERNS[1:]))
N_DN_HEADS = 4
DN_HEAD_DIM = 128
DN_WIDTH = N_DN_HEADS * DN_HEAD_DIM
CONV_K = 5
N_GATES = 4 * N_DN_HEADS

LANES = 128
SUBLANES = 8
VMEM_LIMIT_BYTES = 56 * 1024 * 1024

FFN_ROWS = 512
PROJ_ROWS = 512
ATTN_TILE = {1: (128, 16, 1), 4: (64, 4, 4), 16: (64, 4, 4)}
DN_CHUNK = 128
DN_CHUNKS_PER_STEP = 4
INV_BASE = 4
MOD_COLS = 1152
HALO_ROWS = SUBLANES
NEG_BIG = -1e30


def _cparams(sem):
    return pltpu.CompilerParams(dimension_semantics=sem, vmem_limit_bytes=VMEM_LIMIT_BYTES)


def _resident(shape):
    return pl.BlockSpec(shape, lambda *_: (0, 0), pipeline_mode=pl.Buffered(1))


def _silu(x):
    return x * jax.nn.sigmoid(x)


def _modulated_norm(x, norm_w, shift, scale):
    inv = lax.rsqrt(jnp.mean(x * x, axis=-1, keepdims=True) + EPS)
    return x * inv * (norm_w * (1.0 + scale)) + shift


def _mod_kernel(c_ref, w_ref, b_ref, o_ref):
    sc = _silu(c_ref[...]).astype(BF16)
    o_ref[0] = jnp.dot(sc, w_ref[0].astype(BF16), preferred_element_type=F32) + b_ref[0]


def _modulation(c_all, ada_w, ada_b):
    depth, d, n = ada_w.shape
    rows = c_all.shape[0]
    return pl.pallas_call(
        _mod_kernel,
        grid=(depth, n // MOD_COLS),
        in_specs=[
            pl.BlockSpec((rows, d), lambda l, j: (0, 0)),
            pl.BlockSpec((1, d, MOD_COLS), lambda l, j: (l, 0, j)),
            pl.BlockSpec((1, 1, MOD_COLS), lambda l, j: (l, 0, j)),
        ],
        out_specs=pl.BlockSpec((1, rows, MOD_COLS), lambda l, j: (l, 0, j)),
        out_shape=jax.ShapeDtypeStruct((depth, rows, n), F32),
        compiler_params=_cparams(("arbitrary", "arbitrary")),
        name="modulation",
    )(c_all, ada_w, ada_b.reshape(depth, 1, n))


def _mixer_residual(x, mod_ref, attn_ref, of_ref, ob_ref, z_ref, nw_ref, wa_ref, wd_ref):
    o = of_ref[...] + ob_ref[...]
    z = z_ref[...].astype(F32)
    nw = nw_ref[...]
    parts = []
    for hd in range(N_DN_HEADS):
        sl = slice(hd * DN_HEAD_DIM, (hd + 1) * DN_HEAD_DIM)
        oh = o[:, sl]
        parts.append(oh * lax.rsqrt(jnp.mean(oh * oh, axis=-1, keepdims=True) + EPS) * nw * _silu(z[:, sl]))
    dn = jnp.concatenate(parts, axis=1).astype(BF16)
    y = jnp.dot(attn_ref[...], wa_ref[...], preferred_element_type=F32)
    y = y + jnp.dot(dn, wd_ref[...], preferred_element_type=F32)
    return x + mod_ref[0, 2:3, :] * y


N_MIXER_REFS = 8


def _ffn_kernel(*refs, n_in, n_out, with_mixer, final_norm, rows1):
    x_refs, refs = refs[:n_in], refs[n_in:]
    if with_mixer:
        mixer_refs, refs = refs[:N_MIXER_REFS], refs[N_MIXER_REFS:]
    mod_ref, nw_ref, wg_ref, wu_ref, wd_ref = refs[:5]
    refs = refs[5:]
    if final_norm:
        fw_ref, refs = refs[0], refs[1:]
    o_refs, refs = refs[:n_out], refs[n_out:]
    tm = x_refs[0].shape[0]
    in_first = pl.program_id(0) < rows1 // tm
    if n_in == 2:
        stage_ref = refs[0]

        @pl.when(in_first)
        def _():
            stage_ref[...] = x_refs[0][...]

        @pl.when(jnp.logical_not(in_first))
        def _():
            stage_ref[...] = x_refs[1][...]

        x = stage_ref[...]
    else:
        x = x_refs[0][...]
    if with_mixer:
        x = _mixer_residual(x, *mixer_refs)
    h = _modulated_norm(x, nw_ref[...], mod_ref[0, 0:1, :], mod_ref[0, 1:2, :]).astype(BF16)
    g = jnp.dot(h, wg_ref[...], preferred_element_type=F32)
    u = jnp.dot(h, wu_ref[...], preferred_element_type=F32)
    a = (_silu(g) * u).astype(BF16)
    y = jnp.dot(a, wd_ref[...], preferred_element_type=F32)
    out = x + (0.5 * mod_ref[0, 2:3, :]) * y
    if final_norm:
        out = out * lax.rsqrt(jnp.mean(out * out, axis=-1, keepdims=True) + EPS) * fw_ref[...]
    if n_out == 2:
        @pl.when(in_first)
        def _():
            o_refs[0][...] = out

        @pl.when(jnp.logical_not(in_first))
        def _():
            o_refs[1][...] = out
    else:
        o_refs[0][...] = out


def _ffn(xs, mod, norm_w, wg, wu, wd, blk_rows, rows1, final_w=None, split_out=False, mixer=None):
    d = xs[0].shape[1]
    t = sum(x.shape[0] for x in xs)
    f = wg.shape[1]
    tm = FFN_ROWS
    per_blk = blk_rows // tm
    nb1 = rows1 // tm
    const = lambda i: (0, 0)
    row = lambda i: (i, 0)
    blk = lambda i: (i // per_blk, 0, 0)
    group1 = lambda i: (jnp.minimum(i, nb1 - 1), 0)
    group2 = lambda i: (jnp.maximum(i - nb1, 0), 0)
    in_specs = [pl.BlockSpec((tm, d), row)] if len(xs) == 1 else [pl.BlockSpec((tm, d), group1), pl.BlockSpec((tm, d), group2)]
    args = list(xs)
    if mixer is not None:
        m_mod, attn, o_f, o_b, z, dn_norm, w_attn, w_dn = mixer
        in_specs += [
            pl.BlockSpec((1, 3, d), blk),
            pl.BlockSpec((tm, ATTN_WIDTH), row),
            pl.BlockSpec((tm, DN_WIDTH), row),
            pl.BlockSpec((tm, DN_WIDTH), row),
            pl.BlockSpec((tm, DN_WIDTH), row),
            pl.BlockSpec((1, DN_HEAD_DIM), const),
            _resident((ATTN_WIDTH, d)),
            _resident((DN_WIDTH, d)),
        ]
        args += [m_mod, attn, o_f, o_b, z, dn_norm.reshape(1, DN_HEAD_DIM), w_attn, w_dn]
    in_specs += [
        pl.BlockSpec((1, 3, d), blk),
        pl.BlockSpec((1, d), const),
        _resident((d, f)),
        _resident((d, f)),
        _resident((f, d)),
    ]
    args += [mod, norm_w.reshape(1, d), wg, wu, wd]
    if final_w is not None:
        in_specs.append(pl.BlockSpec((1, d), const))
        args.append(final_w.reshape(1, d))
    if split_out:
        out_specs = [pl.BlockSpec((tm, d), group1), pl.BlockSpec((tm, d), group2)]
        out_shape = [jax.ShapeDtypeStruct((rows1, d), F32), jax.ShapeDtypeStruct((t - rows1, d), F32)]
    else:
        out_specs = [pl.BlockSpec((tm, d), row)]
        out_shape = [jax.ShapeDtypeStruct((t, d), F32)]
    outs = pl.pallas_call(
        functools.partial(_ffn_kernel, n_in=len(xs), n_out=len(out_specs), with_mixer=mixer is not None,
                          final_norm=final_w is not None, rows1=rows1),
        grid=(t // tm,),
        in_specs=in_specs,
        out_specs=out_specs,
        out_shape=out_shape,
        scratch_shapes=[pltpu.VMEM((tm, d), F32)] if len(xs) == 2 else [],
        compiler_params=_cparams(("arbitrary",)),
        name="ffn",
    )(*args)
    return outs if split_out else outs[0]


def _inproj_kernel(xp_ref, x_ref, xn_ref, mod_ref, nw_ref, w_ref, wgt_ref, gpt_ref,
                   cos_ref, sfw_ref, sbk_ref, cw_ref, *rest, rows1, len1, len2):
    n_views = len(DILATED_PATTERNS)
    qkv_refs = [rest[3 * n:3 * n + 3] for n in range(n_views)]
    dq_ref, dk_ref, dv_ref, z_ref, gbt_ref = rest[3 * n_views:3 * n_views + 5]
    stage_refs = rest[3 * n_views + 5:]
    stage_ref = stage_refs[0]
    tm = x_ref.shape[0]
    nw, shift, scale = nw_ref[...], mod_ref[0, 0:1, :], mod_ref[0, 1:2, :]
    h_main = _modulated_norm(x_ref[...], nw, shift, scale)
    h_ext = jnp.concatenate([_modulated_norm(xp_ref[...], nw, shift, scale), h_main,
                             _modulated_norm(xn_ref[...], nw, shift, scale)], axis=0).astype(BF16)
    h = h_main.astype(BF16)
    cos, sfw, sbk = cos_ref[...], sfw_ref[...], sbk_ref[...]

    def rope(xp):
        return xp * cos + pltpu.roll(xp, LANES - ROT_HALF, 1) * sfw + pltpu.roll(xp, ROT_HALF, 1) * sbk

    n_tiles = ATTN_WIDTH // LANES
    o0 = 3 * ATTN_WIDTH

    def attn_epilogue(a, pa):
        for j in range(n_tiles):
            col = pa[:, j * LANES:(j + 1) * LANES]
            if a == 0:
                col = rope(col) * ATTN_Q_SCALE
            elif a == 1:
                col = rope(col)
            stage_ref[a * n_tiles + j] = col
        for j in range(n_tiles):
            tile = a * n_tiles + j
            for n, (_, dil) in enumerate(DILATED_PATTERNS):
                for c in range(dil):
                    lo = c * ATTN_WIDTH + j * LANES
                    if n == 0:
                        src = stage_ref[tile]
                    else:
                        prev_dil = DILATED_PATTERNS[n - 1][1]
                        step = dil // prev_dil
                        prev = stage_refs[n - 1]
                        src = prev[tile * prev_dil + c % prev_dil, pl.ds(c // prev_dil, tm // dil, stride=step), :]
                    if 0 < n < n_views - 1:
                        stage_refs[n][tile * dil + c] = src
                    qkv_refs[n][a][:, lo:lo + LANES] = src.astype(BF16)

    r0 = pl.program_id(0) * tm
    in_first = r0 < rows1
    seq_len = jnp.where(in_first, len1, len2)
    off = jnp.where(in_first, r0, r0 - rows1)
    keep_prev = jnp.where((off % seq_len) == 0, 0.0, 1.0).astype(F32)
    keep_next = jnp.where(((off + tm) % seq_len) == 0, 0.0, 1.0).astype(F32)
    pad = CONV_K // 2
    n_ext = tm + 2 * HALO_ROWS

    def dn_epilogue(part, pe):
        a, half = part
        out_ref = (dq_ref, dk_ref, dv_ref)[a]
        width = DN_WIDTH // 2
        cols = slice(a * DN_WIDTH + half * width, a * DN_WIDTH + (half + 1) * width)
        pe = jnp.concatenate([pe[0:HALO_ROWS] * keep_prev, pe[HALO_ROWS:HALO_ROWS + tm],
                              pe[HALO_ROWS + tm:] * keep_next], axis=0)
        taps = [pe * cw_ref[j:j + 1, cols] for j in range(CONV_K)]
        before, after = taps[0], taps[CONV_K - 1]
        for j in range(1, pad):
            before = taps[j] + pltpu.roll(before, 1, 0)
            after = taps[CONV_K - 1 - j] + pltpu.roll(after, n_ext - 1, 0)
        y = taps[pad] + pltpu.roll(before, 1, 0) + pltpu.roll(after, n_ext - 1, 0)
        y = _silu(y[HALO_ROWS:HALO_ROWS + tm])
        for hd in range(N_DN_HEADS // 2):
            yh = y[:, hd * DN_HEAD_DIM:(hd + 1) * DN_HEAD_DIM]
            sl = slice(half * width + hd * DN_HEAD_DIM, half * width + (hd + 1) * DN_HEAD_DIM)
            if a == 0:
                yh = yh * lax.rsqrt(jnp.sum(yh * yh, axis=-1, keepdims=True) + EPS) * (DN_HEAD_DIM ** -0.5)
            elif a == 1:
                yh = yh * lax.rsqrt(jnp.sum(yh * yh, axis=-1, keepdims=True) + EPS)
            out_ref[:, sl] = yh.astype(BF16)

    def z_epilogue(_, pz):
        z_ref[...] = pz.astype(BF16)

    groups = [(h, a * ATTN_WIDTH, ATTN_WIDTH, attn_epilogue, a) for a in range(3)]
    groups += [(h_ext, o0 + a * DN_WIDTH + half * (DN_WIDTH // 2), DN_WIDTH // 2, dn_epilogue, (a, half))
               for a in range(3) for half in range(2)]
    groups.append((h, o0 + 3 * DN_WIDTH, DN_WIDTH, z_epilogue, None))

    for lhs, lo, width, epilogue, arg in groups:
        epilogue(arg, jnp.dot(lhs, w_ref[:, lo:lo + width], preferred_element_type=F32))

    abt = lax.dot_general(wgt_ref[...], h, (((1,), (1,)), ((), ())), preferred_element_type=F32)
    is_gt = lax.broadcasted_iota(jnp.int32, abt.shape, 0) < 2 * N_DN_HEADS
    gpt = gpt_ref[...]
    gbt_ref[...] = jnp.where(is_gt, -jnp.exp(gpt[:, 0:1]) * jax.nn.softplus(abt + gpt[:, 1:2]), jax.nn.sigmoid(abt))


def _in_proj(x, mod, norm_w, w_main, w_gate_t, gate_par_t, rope_tabs, conv_w, blk_rows, seq_lens, t1):
    t, d = x.shape
    tm = PROJ_ROWS
    per_blk = blk_rows // tm
    n_main = w_main.shape[1]
    nb1 = t1 // tm
    s1b, s2b = seq_lens[0] // tm, seq_lens[1] // tm
    per_halo = tm // HALO_ROWS
    n_halo = t // HALO_ROWS
    const = lambda i: (0, 0)
    row = lambda i: (i, 0)
    pos = lambda i: (jnp.where(i < nb1, i % s1b, (i - nb1) % s2b), 0)
    cos, sfw, sbk = rope_tabs
    view_specs, view_shapes = [], []
    for _, dil in DILATED_PATTERNS:
        view_specs += [pl.BlockSpec((tm // dil, dil * ATTN_WIDTH), row)] * 3
        view_shapes += [jax.ShapeDtypeStruct((t // dil, dil * ATTN_WIDTH), BF16)] * 3
    outs = pl.pallas_call(
        functools.partial(_inproj_kernel, rows1=t1, len1=seq_lens[0], len2=seq_lens[1]),
        grid=(t // tm,),
        in_specs=[
            pl.BlockSpec((HALO_ROWS, d), lambda i: (jnp.maximum(i * per_halo - 1, 0), 0)),
            pl.BlockSpec((tm, d), row),
            pl.BlockSpec((HALO_ROWS, d), lambda i: (jnp.minimum((i + 1) * per_halo, n_halo - 1), 0)),
            pl.BlockSpec((1, 3, d), lambda i: (i // per_blk, 0, 0)),
            pl.BlockSpec((1, d), const),
            _resident((d, n_main)),
            pl.BlockSpec((N_GATES, d), const),
            pl.BlockSpec((N_GATES, 2), const),
            pl.BlockSpec((tm, LANES), pos),
            pl.BlockSpec((tm, LANES), pos),
            pl.BlockSpec((tm, LANES), pos),
            pl.BlockSpec((CONV_K, 3 * DN_WIDTH), const),
        ],
        out_specs=view_specs + [pl.BlockSpec((tm, DN_WIDTH), row)] * 4 + [
            pl.BlockSpec((N_GATES, tm), lambda i: (0, i)),
        ],
        out_shape=view_shapes + [jax.ShapeDtypeStruct((t, DN_WIDTH), BF16)] * 4 + [
            jax.ShapeDtypeStruct((N_GATES, t), F32),
        ],
        scratch_shapes=[pltpu.VMEM((3 * ATTN_WIDTH // LANES * dil, tm // dil, LANES), F32)
                        for _, dil in DILATED_PATTERNS[:-1]],
        compiler_params=_cparams(("arbitrary",)),
        name="in_proj",
    )(x, x, x, mod, norm_w.reshape(1, d), w_main, w_gate_t, gate_par_t, cos, sfw, sbk, conv_w)
    n_qkv = 3 * len(DILATED_PATTERNS)
    qkv_views = [outs[3 * n:3 * n + 3] for n in range(len(DILATED_PATTERNS))]
    return qkv_views, outs[n_qkv:]


def _rope_tables(max_len):
    half = ROT_HALF
    inv = ROPE_THETA ** (-jnp.arange(half, dtype=F32) / half)
    ang = jnp.arange(max_len, dtype=F32)[:, None] * inv[None, :]
    cos, sin = jnp.cos(ang), jnp.sin(ang)
    ones = jnp.ones((max_len, HEAD_DIM - 2 * half), F32)
    zeros_h = jnp.zeros((max_len, half), F32)
    zeros_r = jnp.zeros((max_len, HEAD_DIM - 2 * half), F32)
    cos_h = jnp.concatenate([cos, cos, ones], axis=1)
    sfw_h = jnp.concatenate([-sin, zeros_h, zeros_r], axis=1)
    sbk_h = jnp.concatenate([zeros_h, sin, zeros_r], axis=1)
    rep = LANES // HEAD_DIM
    return tuple(jnp.tile(a, (1, rep)) for a in (cos_h, sfw_h, sbk_h))


def _attn_kernel(q_ref, kp_ref, km_ref, kn_ref, vp_ref, vm_ref, vn_ref, *rest,
                 dil, next_dil, first, radius, tq, rows1, len1, len2):
    last = next_dil is None
    if not first:
        op_ref, wp_ref = rest[:2]
        rest = rest[2:]
    if last:
        o_ref, acc_scr, nat_scr = rest
    else:
        acc_out_ref, ml_out_ref, acc_scr, ml_scr = rest
    tb = q_ref.shape[0]
    n_sub = tb // tq
    tk = tq + 2 * radius
    n_tiles = ATTN_WIDTH // LANES
    n_res = q_ref.shape[1] // ATTN_WIDTH
    c0 = pl.program_id(1) * n_res
    r0 = pl.program_id(0) * tb
    in_first = r0 < rows1
    seq_len = jnp.where(in_first, len1, len2)
    off = jnp.where(in_first, r0, r0 - rows1)
    first_key = jnp.where((off % seq_len) == 0, radius, 0)
    end_key = jnp.where(((off + tb) % seq_len) == 0, tq + radius, tk)

    qi = lax.broadcasted_iota(jnp.int32, (tq, tk), 0)
    kj = lax.broadcasted_iota(jnp.int32, (tq, tk), 1)
    in_band = jnp.abs(kj - radius - qi) <= radius
    biases = []
    for sub in range(n_sub):
        valid = in_band
        if sub == 0:
            valid = valid & (kj >= first_key)
        if sub == n_sub - 1:
            valid = valid & (kj < end_key)
        bias = jnp.where(valid, 0.0, NEG_BIG).astype(F32)
        biases.append(jnp.concatenate([bias, bias], axis=0))

    lane = lax.broadcasted_iota(jnp.int32, (1, LANES), 1)
    low_half = lane < HEAD_DIM
    nt = (((1,), (1,)), ((), ()))

    def unstack(x):
        return jnp.where(low_half, x[:tq], x[tq:])

    if not first:
        e_row = lax.broadcasted_iota(jnp.int32, (LANES, ATTN_WIDTH), 0)
        e_col = lax.broadcasted_iota(jnp.int32, (LANES, ATTN_WIDTH), 1)
        spread = jnp.where(e_row == e_col // LANES + HEAD_DIM * ((e_col % LANES) // HEAD_DIM), 1.0, 0.0).astype(BF16)
        lse_old = {}
        for r in range(n_res):
            for sub in range(n_sub):
                wc = wp_ref[sub * tq:(sub + 1) * tq, r * LANES:(r + 1) * LANES]
                hi = wc.astype(BF16)
                lo = (wc - hi.astype(F32)).astype(BF16)
                ex = jnp.dot(jnp.concatenate([hi, lo], axis=0), spread, preferred_element_type=F32)
                lse_old[(r, sub)] = ex[:tq] + ex[tq:]

    pairs = []
    for r in range(n_res):
        for j in range(n_tiles):
            sl = slice(r * ATTN_WIDTH + j * LANES, r * ATTN_WIDTH + (j + 1) * LANES)
            k_all = jnp.concatenate([kp_ref[:, sl], km_ref[:, sl], kn_ref[:, sl]], axis=0)
            v_all = jnp.concatenate([vp_ref[:, sl], vm_ref[:, sl], vn_ref[:, sl]], axis=0)
            for sub in range(n_sub):
                rows = slice(sub * tq, (sub + 1) * tq)
                q2 = q_ref[rows, sl]
                zero = jnp.zeros_like(q2)
                qs = jnp.concatenate([jnp.where(low_half, q2, zero), jnp.where(low_half, zero, q2)], axis=0)
                kw = k_all[sub * tq:sub * tq + tk]
                pairs.append(dict(r=r, j=j, sub=sub, sl=sl, rows=rows, vw=v_all[sub * tq:sub * tq + tk],
                                  s=lax.dot_general(qs, kw, nt, preferred_element_type=F32) + biases[sub]))
    ones = jnp.ones((tk, LANES), BF16)
    for pr in pairs:
        m_col = jnp.max(pr["s"], axis=-1, keepdims=True)
        pr["m"] = unstack(jnp.broadcast_to(m_col, (2 * tq, LANES)))
        pr["p"] = jnp.exp2(pr["s"] - m_col).astype(BF16)
    for pr in pairs:
        pr["pv"] = jnp.dot(pr["p"], pr["vw"], preferred_element_type=F32)
        pr["l"] = jnp.dot(pr["p"], ones, preferred_element_type=F32)
    for pr in pairs:
        den = unstack(pr["l"])
        o = unstack(pr["pv"]) / den
        lse = pr["m"] + jnp.log2(den)
        if not first:
            lse_prev = lse_old[(pr["r"], pr["sub"])][:, pr["j"] * LANES:(pr["j"] + 1) * LANES]
            top = jnp.maximum(lse_prev, lse)
            w_prev, w_cur = jnp.exp2(lse_prev - top), jnp.exp2(lse - top)
            tot = w_prev + w_cur
            o = (w_prev * op_ref[pr["rows"], pr["sl"]] + w_cur * o) / tot
            lse = top + jnp.log2(tot)
        pr["lse"] = lse
        acc_scr[c0 + pr["r"], pr["j"], pr["rows"], :] = o
    if not last:
        for r in range(n_res):
            for sub in range(n_sub):
                wc = jnp.zeros((tq, LANES), F32)
                for pr in pairs:
                    if pr["r"] == r and pr["sub"] == sub:
                        keep = jnp.logical_or(lane == pr["j"], lane == HEAD_DIM + pr["j"])
                        wc = jnp.where(keep, pr["lse"], wc)
                ml_scr[c0 + r, sub * tq:(sub + 1) * tq, :] = wc

    @pl.when(c0 + n_res == dil)
    def _():
        if last:
            for cc in range(dil):
                for j in range(n_tiles):
                    nat_scr[j, pl.ds(cc, tb, stride=dil), :] = acc_scr[cc, j]
            for j in range(n_tiles):
                o_ref[:, j * LANES:(j + 1) * LANES] = nat_scr[j].astype(o_ref.dtype)
        else:
            ratio = next_dil // dil
            rows_out = tb // ratio
            for cc in range(dil):
                for m in range(ratio):
                    cb = m * dil + cc
                    rows = pl.ds(m, rows_out, stride=ratio)
                    for j in range(n_tiles):
                        lo = cb * ATTN_WIDTH + j * LANES
                        acc_out_ref[:, lo:lo + LANES] = acc_scr[cc, j, rows, :]
                    ml_out_ref[:, cb * LANES:(cb + 1) * LANES] = ml_scr[cc, rows, :]


def _attn_pattern(qkv, prev, dil, next_dil, radius, t, t1, seq_lens):
    q, k, v = qkv
    w = ATTN_WIDTH
    first, last = prev is None, next_dil is None
    rows = t // dil
    tq, n_sub, n_res = ATTN_TILE[dil]
    tb = tq * n_sub
    per_q = tb // radius
    n_halo = rows // radius
    n_tiles = w // LANES
    wr = n_res * w
    main = lambda i, c: (i, c)
    before = lambda i, c: (jnp.maximum(i * per_q - 1, 0), c)
    after = lambda i, c: (jnp.minimum((i + 1) * per_q, n_halo - 1), c)
    whole = lambda i, c: (i, 0)
    kv_specs = [pl.BlockSpec((radius, wr), before), pl.BlockSpec((tb, wr), main), pl.BlockSpec((radius, wr), after)]
    in_specs = [pl.BlockSpec((tb, wr), main)] + kv_specs + kv_specs
    args = [q, k, k, k, v, v, v]
    if not first:
        in_specs += [pl.BlockSpec((tb, wr), main), pl.BlockSpec((tb, n_res * LANES), main)]
        args += list(prev)
    scratch = [pltpu.VMEM((dil, n_tiles, tb, LANES), F32)]
    if last:
        out_specs = [pl.BlockSpec((tb * dil, w), whole)]
        out_shape = [jax.ShapeDtypeStruct((t, w), BF16)]
        scratch.append(pltpu.VMEM((n_tiles, tb * dil, LANES), F32))
    else:
        rows_out = tb * dil // next_dil
        out_specs = [pl.BlockSpec((rows_out, next_dil * w), whole), pl.BlockSpec((rows_out, next_dil * LANES), whole)]
        out_shape = [jax.ShapeDtypeStruct((t // next_dil, next_dil * w), F32),
                     jax.ShapeDtypeStruct((t // next_dil, next_dil * LANES), F32)]
        scratch.append(pltpu.VMEM((dil, tb, LANES), F32))
    return pl.pallas_call(
        functools.partial(_attn_kernel, dil=dil, next_dil=next_dil, first=first, radius=radius, tq=tq,
                          rows1=t1 // dil, len1=seq_lens[0] // dil, len2=seq_lens[1] // dil),
        grid=(rows // tb, dil // n_res),
        in_specs=in_specs,
        out_specs=out_specs,
        out_shape=out_shape,
        scratch_shapes=scratch,
        compiler_params=_cparams(("arbitrary", "arbitrary")),
        name=f"attn_d{dil}",
    )(*args)


def _attention(qkv_views, t, t1, seq_lens):
    state = None
    n = len(DILATED_PATTERNS)
    for idx, (window, dil) in enumerate(DILATED_PATTERNS):
        next_dil = DILATED_PATTERNS[idx + 1][1] if idx + 1 < n else None
        state = _attn_pattern(qkv_views[idx], state, dil, next_dil, window // (2 * dil), t, t1, seq_lens)
    return state[0]


def _bdot(a, b):
    return jnp.dot(a.astype(BF16), b.astype(BF16), preferred_element_type=F32)


def _split3(x):
    hi = x.astype(BF16)
    r1 = x - hi.astype(F32)
    mid = r1.astype(BF16)
    lo = (r1 - mid.astype(F32)).astype(BF16)
    return hi, mid, lo


def _dn_kernel(qf_ref, kf_ref, vf_ref, gtf_ref, qb_ref, kb_ref, vb_ref, gtb_ref,
               of_ref, ob_ref, sf_ref, sb_ref, *, c, n_chunks, chunks1, len1, len2):
    n_sub = qf_ref.shape[0] // c
    i = pl.program_id(0)

    def seq_pos(ci):
        in_first = ci < chunks1
        seq_len = jnp.where(in_first, len1, len2)
        off = jnp.where(in_first, ci, ci - chunks1)
        return off % seq_len, seq_len

    pos_f, _ = seq_pos(i * n_sub)
    pos_b, len_b = seq_pos(n_chunks - 1 - i * n_sub)

    @pl.when(pos_f == 0)
    def _():
        sf_ref[...] = jnp.zeros_like(sf_ref)

    @pl.when(pos_b == len_b - 1)
    def _():
        sb_ref[...] = jnp.zeros_like(sb_ref)

    ii = lax.broadcasted_iota(jnp.int32, (c, c), 0)
    jj = lax.broadcasted_iota(jnp.int32, (c, c), 1)
    lower, lower_strict = ii >= jj, ii > jj
    upper, upper_strict = ii <= jj, ii < jj
    eye = jnp.where(ii == jj, 1.0, 0.0).astype(F32)
    tri_l = jnp.where(lower, 1.0, 0.0).astype(F32)
    tri_u = jnp.where(upper, 1.0, 0.0).astype(F32)
    level_masks = [jnp.where(ii // INV_BASE == jj // INV_BASE, 1.0, 0.0).astype(F32)]
    b = INV_BASE
    while b < c:
        same_outer = ii // (2 * b) == jj // (2 * b)
        level_masks.append(jnp.where(same_outer & (ii // b != jj // b), 1.0, 0.0).astype(F32))
        b *= 2
    ones_cd = jnp.ones((c, DN_HEAD_DIM), BF16)
    nt = (((1,), (1,)), ((), ()))

    chains = []
    for forward, (q_ref, k_ref, v_ref, gt_ref, o_ref, s_ref) in (
            (True, (qf_ref, kf_ref, vf_ref, gtf_ref, of_ref, sf_ref)),
            (False, (qb_ref, kb_ref, vb_ref, gtb_ref, ob_ref, sb_ref))):
        tri_r = tri_u if forward else tri_l
        row_rhs = jnp.concatenate([tri_r.astype(BF16), ones_cd], axis=1)
        keep, strict = (lower, lower_strict) if forward else (upper, upper_strict)
        d0 = 0 if forward else N_DN_HEADS
        for order in range(n_sub):
            sub = order if forward else n_sub - 1 - order
            rows = slice(sub * c, (sub + 1) * c)
            gt = gt_ref[:, rows]
            row = sum(jnp.dot(piece, row_rhs, preferred_element_type=F32) for piece in _split3(gt))
            cs_row, tot_row = row[:, :c], row[:, c:]
            cs_col, gates_col = jnp.transpose(cs_row), jnp.transpose(gt)
            tot_col = jnp.broadcast_to(cs_col[c - 1:c] if forward else cs_col[0:1], cs_col.shape)
            for hd in range(N_DN_HEADS):
                gi = d0 + hd
                bi = 2 * N_DN_HEADS + d0 + hd
                sl = slice(hd * DN_HEAD_DIM, (hd + 1) * DN_HEAD_DIM)
                chains.append(dict(
                    order=order, rows=rows, sl=sl, hd=hd, keep=keep, strict=strict, o_ref=o_ref, s_ref=s_ref,
                    q=q_ref[rows, sl], k=k_ref[rows, sl], v=v_ref[rows, sl],
                    gc_col=cs_col[:, gi:gi + 1], gc_row=cs_row[gi:gi + 1, :], beta=gates_col[:, bi:bi + 1],
                    tot_col=tot_col[:, gi:gi + 1], tot_row=tot_row[gi:gi + 1, :]))

    for ch in chains:
        qh, kh = ch["q"], ch["k"]
        both = lax.dot_general(jnp.concatenate([qh, kh], axis=0), kh, nt, preferred_element_type=F32)
        decay = jnp.where(ch["keep"], jnp.exp(jnp.minimum(ch["gc_col"] - ch["gc_row"], 0.0)), 0.0)
        ch["a"] = jnp.where(ch["strict"], ch["beta"] * both[c:] * decay, 0.0)
        ch["qk"] = (both[:c] * decay).astype(BF16)
    for ch in chains:
        dg = ch["a"] * level_masks[0]
        ch["x"] = eye - dg
        ch["y"] = _bdot(dg, dg)
    for ch in chains:
        ch["x"] = ch["x"] + _bdot(ch["x"], ch["y"])
    def half_rows(m, b, sel):
        return m.reshape(c // (2 * b), 2, b, m.shape[1])[:, sel].reshape(c // 2, m.shape[1])

    def merge_rows(first, second, b):
        parts = [h.reshape(c // (2 * b), b, h.shape[1]) for h in (first, second)]
        return jnp.stack(parts, axis=1).reshape(c, first.shape[1])

    b = INV_BASE
    for off_mask in level_masks[1:]:
        if b < SUBLANES:
            for ch in chains:
                ch["y"] = _bdot(ch["a"] * off_mask, ch["x"])
            for ch in chains:
                ch["x"] = ch["x"] - _bdot(ch["x"], ch["y"])
        else:
            for ch in chains:
                sel = 1 if ch["keep"] is lower else 0
                y_half = _bdot(half_rows(ch["a"] * off_mask, b, sel), ch["x"])
                zero = jnp.zeros_like(y_half)
                ch["y"] = merge_rows(zero, y_half, b) if sel else merge_rows(y_half, zero, b)
            for ch in chains:
                sel = 1 if ch["keep"] is lower else 0
                changed = half_rows(ch["x"], b, sel)
                changed = changed - _bdot(changed, ch["y"])
                same = half_rows(ch["x"], b, 1 - sel)
                ch["x"] = merge_rows(same, changed, b) if sel else merge_rows(changed, same, b)
        b *= 2
    for ch in chains:
        kh = ch["k"].astype(F32)
        egc = jnp.exp(ch["gc_col"])
        rhs = jnp.concatenate([ch["v"].astype(F32) * ch["beta"], kh * (ch["beta"] * egc)], axis=1)
        ch["uw"] = _bdot(ch["x"], rhs)
        ch["lhs_q"] = ch["q"].astype(F32) * egc
        ch["k_dec_t"] = jnp.transpose(kh * jnp.exp(ch["tot_col"] - ch["gc_col"])).astype(BF16)
    states = {}
    for order in range(n_sub):
        group = [ch for ch in chains if ch["order"] == order]
        for ch in group:
            key = (id(ch["s_ref"]), ch["hd"])
            ch["state"] = states[key] if order else ch["s_ref"][ch["hd"]]
            lhs = jnp.concatenate([ch["uw"][:, DN_HEAD_DIM:], ch["lhs_q"]], axis=0)
            ch["ws"] = _bdot(lhs, ch["state"])
        for ch in group:
            v_new = (ch["uw"][:, :DN_HEAD_DIM] - ch["ws"][:c]).astype(BF16)
            ch["o_ref"][ch["rows"], ch["sl"]] = ch["ws"][c:] + jnp.dot(ch["qk"], v_new, preferred_element_type=F32)
            ch["v_new"] = v_new
        for ch in group:
            states[(id(ch["s_ref"]), ch["hd"])] = ch["state"] * jnp.exp(ch["tot_row"]) + jnp.dot(
                ch["k_dec_t"], ch["v_new"], preferred_element_type=F32)
    for ch in chains:
        if ch["order"] == n_sub - 1:
            ch["s_ref"][ch["hd"]] = states[(id(ch["s_ref"]), ch["hd"])]


def _deltanet(q, k, v, gates_t, t1, seq_lens):
    t, w = q.shape
    c = DN_CHUNK
    rows = c * DN_CHUNKS_PER_STEP
    n = t // rows
    fwd = lambda i: (i, 0)
    bwd = lambda i: (n - 1 - i, 0)
    fwd_t = lambda i: (0, i)
    bwd_t = lambda i: (0, n - 1 - i)

    def specs(row_map, col_map):
        return [pl.BlockSpec((rows, w), row_map)] * 3 + [pl.BlockSpec((N_GATES, rows), col_map)]

    state = pltpu.VMEM((N_DN_HEADS, DN_HEAD_DIM, DN_HEAD_DIM), F32)
    return pl.pallas_call(
        functools.partial(_dn_kernel, c=c, n_chunks=t // c, chunks1=t1 // c, len1=seq_lens[0] // c,
                          len2=seq_lens[1] // c),
        grid=(n,),
        in_specs=specs(fwd, fwd_t) + specs(bwd, bwd_t),
        out_specs=[pl.BlockSpec((rows, w), fwd), pl.BlockSpec((rows, w), bwd)],
        out_shape=[jax.ShapeDtypeStruct((t, w), F32)] * 2,
        scratch_shapes=[state, state],
        compiler_params=_cparams(("arbitrary",)),
        name="deltanet",
    )(q, k, v, gates_t, q, k, v, gates_t)


def kernel(x_prompt, x_sample, c_prompt, c_sample, ada_w, ada_b, norm_ffn1, ffn1_w_gate, ffn1_w_up, ffn1_w_down, norm_mix, w_in, conv_w, a_log, dt_bias, dn_norm, w_out, norm_ffn2, ffn2_w_gate, ffn2_w_up, ffn2_w_down, norm_final):
    b1, s1, d = x_prompt.shape
    b2, s2, _ = x_sample.shape
    depth = ada_w.shape[0]
    t1 = b1 * s1
    seq_lens = (s1, s2)
    assert s2 % s1 == 0 and t1 % s2 == 0, "flat layout needs nested sequence lengths"
    t = t1 + b2 * s2
    xs = [x_prompt.reshape(t1, d), x_sample.reshape(b2 * s2, d)]

    n_seq = b1 + b2
    c_all = jnp.concatenate([c_prompt, c_sample, jnp.zeros((-n_seq % SUBLANES, d), F32)], axis=0)
    mod = _modulation(c_all, ada_w, ada_b)
    blk_seq = jnp.concatenate([jnp.arange(b1), b1 + jnp.repeat(jnp.arange(b2), s2 // s1)])
    mod = mod[:, blk_seq].reshape(depth, blk_seq.shape[0], N_MOD, d)

    rope_tabs = _rope_tables(max(s1, s2))
    n_main = 3 * ATTN_WIDTH + 4 * DN_WIDTH
    gate_par_t = jnp.stack([a_log.astype(F32).reshape(depth, -1), dt_bias.astype(F32).reshape(depth, -1)], axis=2)
    gate_par_t = jnp.pad(gate_par_t, ((0, 0), (0, N_GATES - gate_par_t.shape[1]), (0, 0)))

    for l in range(depth):
        bf = lambda a: a.astype(BF16)
        x = _ffn(xs if l == 0 else [x], mod[l, :, 0:3], norm_ffn1[l], bf(ffn1_w_gate[l]), bf(ffn1_w_up[l]),
                 bf(ffn1_w_down[l]), s1, t1)
        qkv_views, (dq, dk, dv, z, gates_t) = _in_proj(
            x, mod[l, :, 3:6], norm_mix[l], bf(w_in[l][:, :n_main]), bf(w_in[l][:, n_main:]).T,
            gate_par_t[l], rope_tabs, conv_w[l], s1, seq_lens, t1)
        attn = _attention(qkv_views, t, t1, seq_lens)
        o_f, o_b = _deltanet(dq, dk, dv, gates_t, t1, seq_lens)
        mixer = (mod[l, :, 3:6], attn, o_f, o_b, z, dn_norm[l], bf(w_out[l][:ATTN_WIDTH]), bf(w_out[l][ATTN_WIDTH:]))
        is_last = l == depth - 1
        x = _ffn([x], mod[l, :, 6:9], norm_ffn2[l], bf(ffn2_w_gate[l]), bf(ffn2_w_up[l]), bf(ffn2_w_down[l]), s1,
                 t1, final_w=norm_final if is_last else None, split_out=is_last, mixer=mixer)
    return x[0].reshape(b1, s1, d), x[1].reshape(b2, s2, d)
```
